```python
import math
import jax, jax.numpy as jnp
from jax import lax
import numpy as np

D_MODEL = 1024
BATCH = 16
SEQ = 2048
DEPTH = 4

CTX_LEN = 256
GRID_W = 64
FN_GROUPS = 4
FN_GROUP_DIM = 128
FN_WIDTH = FN_GROUPS * FN_GROUP_DIM
DA_HEADS = 4
DA_HEAD_DIM = 64
DA_V_DIM = 2 * DA_HEAD_DIM
DA_QK_WIDTH = DA_HEADS * 2 * DA_HEAD_DIM
DA_V_WIDTH = DA_HEADS * DA_V_DIM
DA_Q_BLOCK = 128
NA_HEADS = 8
NA_HEAD_DIM = 64
NA_WIDTH = NA_HEADS * NA_HEAD_DIM
NA_ROWS = 8
NA_COLS = 16
NA_Q_BLOCK_W = 16
NA_K_BLOCK_W = 32
ROPE_THETA = 10000.0
ROPE_AXIS_DIM = DA_HEAD_DIM // 2
N_BRANCH = 3
PROJ_SPLITS = (
    FN_WIDTH,
    FN_WIDTH + DA_QK_WIDTH,
    FN_WIDTH + 2 * DA_QK_WIDTH,
    FN_WIDTH + 2 * DA_QK_WIDTH + DA_V_WIDTH,
    FN_WIDTH + 2 * DA_QK_WIDTH + DA_V_WIDTH + NA_WIDTH,
    FN_WIDTH + 2 * DA_QK_WIDTH + DA_V_WIDTH + 2 * NA_WIDTH,
    FN_WIDTH + 2 * DA_QK_WIDTH + DA_V_WIDTH + 3 * NA_WIDTH,
)
PROJ_WIDTH = PROJ_SPLITS[-1] + N_BRANCH * D_MODEL
D_FF = 2816
CONV_W = 3
N_MOD = 6
NORM_EPS = 1e-6
SUBLN_EPS = 1e-5
NEG_INF = -1e30

kernel_name = "hybrid_fnet_diffattn_natten_dit_block"


def rms_norm(x, g, eps=NORM_EPS):
    xf = x.astype(jnp.float32)
    y = xf * lax.rsqrt(jnp.mean(xf * xf, axis=-1, keepdims=True) + eps)
    return (y * g.astype(jnp.float32)).astype(x.dtype)


def modulate(h, shift, scale):
    return h * (1.0 + scale) + shift


def axial_rope_tables(n_tokens):
    t = jnp.arange(n_tokens, dtype=jnp.int32)
    row = (t // GRID_W).astype(jnp.float32)
    col = (t % GRID_W).astype(jnp.float32)
    n_freq = ROPE_AXIS_DIM // 2
    inv = ROPE_THETA ** (-jnp.arange(n_freq, dtype=jnp.float32) / n_freq)
    ang = jnp.stack([row[:, None] * inv, col[:, None] * inv], axis=1)
    return jnp.cos(ang), jnp.sin(ang)


def apply_axial_rope(x, cos, sin):
    shp = x.shape
    xr = x.astype(jnp.float32).reshape(shp[:-1] + (2, 2, ROPE_AXIS_DIM // 2))
    a, b = xr[..., 0, :], xr[..., 1, :]
    cb = cos[None, :, None, None]
    sb = sin[None, :, None, None]
    out = jnp.stack([a * cb - b * sb, a * sb + b * cb], axis=-2)
    return out.reshape(shp).astype(x.dtype)


def dwconv3(x, w, b):
    xp = jnp.pad(x, ((0, 0), (1, 1), (0, 0)))
    return xp[:, :-2] * w[0] + xp[:, 1:-1] * w[1] + xp[:, 2:] * w[2] + b


def fourier_mix(u):
    bsz, n, _ = u.shape
    ug = u.astype(jnp.float32).reshape(bsz, n, FN_GROUPS, FN_GROUP_DIM)
    f = jnp.fft.fft2(ug, axes=(1, 3), norm="ortho").real
    return f.reshape(bsz, n, FN_WIDTH).astype(u.dtype)


def diff_attend(q, k, v, lam):
    s = jnp.einsum('bqhmd,bkhmd->bhmqk', q, k).astype(jnp.float32) * (DA_HEAD_DIM ** -0.5)
    p = jax.nn.softmax(s, axis=-1)
    w = p[:, :, 0] - lam * p[:, :, 1]
    return jnp.einsum('bhqk,bkhe->bqhe', w.astype(v.dtype), v)


def diff_attention_latent(q, k_all, v_all, lam):
    bsz, n, h, m, d = q.shape
    nb = n // DA_Q_BLOCK
    qb = q.reshape(bsz, nb, DA_Q_BLOCK, h, m, d).swapaxes(0, 1)
    out = lax.map(lambda qi: diff_attend(qi, k_all, v_all, lam), qb)
    return out.swapaxes(0, 1).reshape(bsz, n, h, DA_V_DIM)


def softmax_attend(q, k, v):
    s = jnp.einsum('bqhd,bkhd->bhqk', q, k).astype(jnp.float32) * (q.shape[-1] ** -0.5)
    p = jax.nn.softmax(s, axis=-1).astype(v.dtype)
    return jnp.einsum('bhqk,bkhe->bqhe', p, v)


def neighbourhood_attention_latent(q, k, v, k_ctx, v_ctx, rpb):
    bsz, n, h, d = q.shape
    rows = n // GRID_W
    kh = min(NA_ROWS, rows)
    ncb = GRID_W // NA_Q_BLOCK_W
    qg = (q * (d ** -0.5)).reshape(bsz, rows, ncb, NA_Q_BLOCK_W, h, d)
    kg = k.reshape(bsz, rows, GRID_W, h, d)
    vg = v.reshape(bsz, rows, GRID_W, h, d)
    qcol = np.arange(GRID_W).reshape(ncb, NA_Q_BLOCK_W)
    c0 = np.clip(qcol - NA_COLS // 2, 0, GRID_W - NA_COLS)
    kc0 = np.clip(np.arange(ncb) * NA_Q_BLOCK_W - NA_COLS // 2, 0, GRID_W - NA_K_BLOCK_W)
    kcol = kc0[:, None] + np.arange(NA_K_BLOCK_W)
    col_ok = (kcol[:, None, :] >= c0[:, :, None]) & (kcol[:, None, :] < c0[:, :, None] + NA_COLS)
    dc = np.clip(kcol[:, None, :] - qcol[:, :, None] + NA_COLS - 1, 0, 2 * NA_COLS - 2)
    bias_c = rpb.astype(jnp.float32)[:, :, dc]
    bias_c = jnp.where(jnp.asarray(col_ok)[None, None], bias_c, NEG_INF)
    bias_c = jnp.transpose(bias_c, (0, 2, 3, 1, 4))
    kcol_j = jnp.asarray(kcol, dtype=jnp.int32)
    n_lat = kh * NA_K_BLOCK_W

    def one_row(r):
        r0 = jnp.clip(r - kh // 2, 0, rows - kh)
        q_r = lax.dynamic_index_in_dim(qg, r, axis=1, keepdims=False)
        k_rows = lax.dynamic_slice_in_dim(kg, r0, kh, axis=1)
        v_rows = lax.dynamic_slice_in_dim(vg, r0, kh, axis=1)
        k_blk = k_rows[:, :, kcol_j]
        v_blk = v_rows[:, :, kcol_j]
        bias = lax.dynamic_slice_in_dim(bias_c, r0 - r + NA_ROWS - 1, kh, axis=3)
        s_lat = jnp.einsum('bjqhd,bkjwhd->bhjqkw', q_r, k_blk).astype(jnp.float32) + bias
        s_ctx = jnp.einsum('bjqhd,bchd->bhjqc', q_r, k_ctx).astype(jnp.float32)
        s = jnp.concatenate([s_lat.reshape(s_lat.shape[:4] + (n_lat,)), s_ctx], axis=-1)
        p = jax.nn.softmax(s, axis=-1).astype(v.dtype)
        p_lat = p[..., :n_lat].reshape(s_lat.shape)
        p_ctx = p[..., n_lat:]
        o = (jnp.einsum('bhjqkw,bkjwhe->bjqhe', p_lat, v_blk)
             + jnp.einsum('bhjqc,bche->bjqhe', p_ctx, v_ctx))
        return o.reshape(bsz, GRID_W, h * d)

    out = lax.map(one_row, jnp.arange(rows, dtype=jnp.int32))
    return out.swapaxes(0, 1).reshape(bsz, n, h * d)


def mixer(h_lat, h_ctx, w_in, b_gate, w_a, lam_vec, subln_g, w_b, rpb, w_c, w_out,
          lam_init, cos, sin, with_ctx_out):
    bsz = h_lat.shape[0]
    w_fa, w_bq, w_bk, w_bv, w_cq, w_ck, w_cv, w_g = jnp.split(w_in, PROJ_SPLITS, axis=1)
    lv = lam_vec.astype(jnp.float32)
    lam = jnp.exp(jnp.sum(lv[0] * lv[1])) - jnp.exp(jnp.sum(lv[2] * lv[3])) + lam_init

    def da_qk(h, w):
        return (h @ w).reshape(h.shape[0], h.shape[1], DA_HEADS, 2, DA_HEAD_DIM)

    def heads(h, w, nh, dh):
        return (h @ w).reshape(h.shape[0], h.shape[1], nh, dh)

    def diff_post(o):
        return (rms_norm(o, subln_g, SUBLN_EPS) * (1.0 - lam_init)).reshape(bsz, o.shape[1], DA_V_WIDTH)

    def merge(h, fa, db, nc):
        gates = jax.nn.sigmoid(h @ w_g + b_gate)
        g_a, g_b, g_c = jnp.split(gates, N_BRANCH, axis=-1)
        return (g_a * (fa @ w_a) + g_b * (db @ w_b) + g_c * (nc @ w_c)) @ w_out

    kb_c = da_qk(h_ctx, w_bk)
    vb_c = heads(h_ctx, w_bv, DA_HEADS, DA_V_DIM)
    kc_c = heads(h_ctx, w_ck, NA_HEADS, NA_HEAD_DIM)
    vc_c = heads(h_ctx, w_cv, NA_HEADS, NA_HEAD_DIM)

    fa = fourier_mix(h_lat @ w_fa)
    qb = apply_axial_rope(da_qk(h_lat, w_bq), cos, sin)
    kb = apply_axial_rope(da_qk(h_lat, w_bk), cos, sin)
    vb = heads(h_lat, w_bv, DA_HEADS, DA_V_DIM)
    db = diff_post(diff_attention_latent(qb, jnp.concatenate([kb, kb_c], axis=1),
                                         jnp.concatenate([vb, vb_c], axis=1), lam))
    nc = neighbourhood_attention_latent(heads(h_lat, w_cq, NA_HEADS, NA_HEAD_DIM),
                                        heads(h_lat, w_ck, NA_HEADS, NA_HEAD_DIM),
                                        heads(h_lat, w_cv, NA_HEADS, NA_HEAD_DIM),
                                        kc_c, vc_c, rpb)
    y_lat = merge(h_lat, fa, db, nc)
    if not with_ctx_out:
        return y_lat, None

    n_ctx = h_ctx.shape[1]
    fa_c = fourier_mix(h_ctx @ w_fa)
    db_c = diff_post(diff_attend(da_qk(h_ctx, w_bq), kb_c, vb_c, lam))
    nc_c = softmax_attend(heads(h_ctx, w_cq, NA_HEADS, NA_HEAD_DIM), kc_c, vc_c).reshape(bsz, n_ctx, NA_WIDTH)
    y_ctx = merge(h_ctx, fa_c, db_c, nc_c)
    return y_lat, y_ctx


def conv_ffn(h, w_up, conv_w, conv_b, w_down):
    u = dwconv3(h @ w_up, conv_w, conv_b)
    a, b = jnp.split(u, 2, axis=-1)
    return (jax.nn.silu(a) * b) @ w_down


def setup_inputs(seed: int = 0) -> dict:
    key = jax.random.key(seed)
    ks = jax.random.split(key, 24)
    f32 = jnp.float32

    def nrm(k, shape, s):
        return s * jax.random.normal(k, shape, f32)

    L, D = DEPTH, D_MODEL
    return {
        "x": nrm(ks[0], (BATCH, SEQ, D), 1.0),
        "c": nrm(ks[1], (BATCH, D), 1.0),
        "ctx": nrm(ks[2], (BATCH, CTX_LEN, D), 1.0),
        "c_ctx": nrm(ks[3], (D,), 1.0),
        "w_ada": nrm(ks[4], (L, D, N_MOD * D), 0.5 * D ** -0.5),
        "b_ada": nrm(ks[5], (L, N_MOD * D), 0.02),
        "g_mix": 1.0 + nrm(ks[6], (L, D), 0.02),
        "g_ffn": 1.0 + nrm(ks[7], (L, D), 0.02),
        "w_in": nrm(ks[8], (L, D, PROJ_WIDTH), D ** -0.5),
        "b_gate": nrm(ks[9], (L, N_BRANCH * D), 0.02),
        "w_a": nrm(ks[10], (L, FN_WIDTH, D), FN_WIDTH ** -0.5),
        "lam": nrm(ks[11], (L, 4, DA_HEAD_DIM), 0.1),
        "subln_g": 1.0 + nrm(ks[12], (L, DA_V_DIM), 0.02),
        "w_b": nrm(ks[13], (L, DA_V_WIDTH, D), DA_V_WIDTH ** -0.5),
        "rpb": nrm(ks[14], (L, NA_HEADS, 2 * NA_ROWS - 1, 2 * NA_COLS - 1), 0.1),
        "w_c": nrm(ks[15], (L, NA_WIDTH, D), NA_WIDTH ** -0.5),
        "w_out": nrm(ks[16], (L, D, D), D ** -0.5),
        "w_up": nrm(ks[17], (L, D, 2 * D_FF), D ** -0.5),
        "conv_w": nrm(ks[18], (L, CONV_W, 2 * D_FF), CONV_W ** -0.5),
        "conv_b": nrm(ks[19], (L, 2 * D_FF), 0.02),
        "w_down": nrm(ks[20], (L, D_FF, D), D_FF ** -0.5),
        "g_final": 1.0 + nrm(ks[21], (D,), 0.02),
    }


def reference(x, c, ctx, c_ctx, w_ada, b_ada, g_mix, g_ffn, w_in, b_gate, w_a, lam,
              subln_g, w_b, rpb, w_c, w_out, w_up, conv_w, conv_b, w_down, g_final):
    n_tok = x.shape[1]
    cos, sin = axial_rope_tables(n_tok)
    silu_c = jax.nn.silu(c)
    silu_cc = jax.nn.silu(c_ctx)
    h_x = x
    h_c = ctx
    for l in range(DEPTH):
        last = l == DEPTH - 1
        lam_init = 0.8 - 0.6 * math.exp(-0.3 * l)
        mod = (silu_c @ w_ada[l] + b_ada[l])[:, None, :]
        mod_c = silu_cc @ w_ada[l] + b_ada[l]
        sh1, sc1, gt1, sh2, sc2, gt2 = jnp.split(mod, N_MOD, axis=-1)
        csh1, csc1, cgt1, csh2, csc2, cgt2 = jnp.split(mod_c, N_MOD, axis=-1)

        a_x = modulate(rms_norm(h_x, g_mix[l]), sh1, sc1)
        a_c = modulate(rms_norm(h_c, g_mix[l]), csh1, csc1)
        y_x, y_c = mixer(a_x, a_c, w_in[l], b_gate[l], w_a[l], lam[l], subln_g[l], w_b[l],
                         rpb[l], w_c[l], w_out[l], lam_init, cos, sin, not last)
        h_x = h_x + gt1 * y_x
        f_x = modulate(rms_norm(h_x, g_ffn[l]), sh2, sc2)
        h_x = h_x + gt2 * conv_ffn(f_x, w_up[l], conv_w[l], conv_b[l], w_down[l])
        if not last:
            h_c = h_c + cgt1 * y_c
            f_c = modulate(rms_norm(h_c, g_ffn[l]), csh2, csc2)
            h_c = h_c + cgt2 * conv_ffn(f_c, w_up[l], conv_w[l], conv_b[l], w_down[l])
    return rms_norm(h_x, g_final)
```

```python
import functools
import math

import jax
import jax.numpy as jnp
from jax import lax
from jax.experimental import pallas as pl
from jax.experimental.pallas import tpu as pltpu

D_MODEL = 1024
DEPTH = 4
GRID_W = 64
FN_GROUPS = 4
FN_GROUP_DIM = 128
FN_WIDTH = FN_GROUPS * FN_GROUP_DIM
DA_HEADS = 4
DA_HEAD_DIM = 64
DA_V_DIM = 2 * DA_HEAD_DIM
NA_HEADS = 8
NA_HEAD_DIM = 64
NA_ROWS = 8
NA_COLS = 16
BRANCH_W = 512
N_BRANCH = 3
ROPE_THETA = 10000.0
ROPE_AXIS_DIM = DA_HEAD_DIM // 2
D_FF = 2816
N_MOD = 6
NORM_EPS = 1e-6
SUBLN_EPS = 1e-5
NEG_INF = -1e30

LANES = 128
SUBLANES = 8
TOK_TILE = 256
FF_CHUNK = 256
VMEM_LIMIT = 56 * 1024 * 1024

BF16 = jnp.bfloat16
F32 = jnp.float32


def _params(n_grid_dims):
    return pltpu.CompilerParams(dimension_semantics=("arbitrary",) * n_grid_dims,
                                vmem_limit_bytes=VMEM_LIMIT)


def _resident(shape):
    return pl.BlockSpec(shape, lambda *_: (0,) * len(shape), pipeline_mode=pl.Buffered(1))


def _nt_dot(a, b):
    return lax.dot_general(a, b, (((1,), (1,)), ((), ())), preferred_element_type=F32)


def _dot(a, b):
    return jnp.dot(a, b, preferred_element_type=F32)


def _norm_modulate(x, g, shift, scale):
    ms = jnp.mean(x * x, axis=-1, keepdims=True)
    return (x * lax.rsqrt(ms + NORM_EPS) * g) * (1.0 + scale) + shift


def _split_bf16(v):
    hi = v.astype(BF16)
    lo = (v - hi.astype(F32)).astype(BF16)
    return hi, lo


def _mod_kernel(c_ref, w_ref, b_ref, o_ref):
    c = c_ref[...]
    s = c / (1.0 + jnp.exp(-c))
    s_hi, s_lo = _split_bf16(s)
    w_hi, w_lo = _split_bf16(w_ref[0])
    acc = _dot(s_hi, w_hi) + (_dot(s_hi, w_lo) + _dot(s_lo, w_hi))
    o_ref[0] = acc + b_ref[0]


def _modulation(cvec, w_ada, b_ada):
    n_rows = cvec.shape[0]
    n_out = N_MOD * D_MODEL
    tn = 1536
    return pl.pallas_call(
        _mod_kernel,
        grid=(DEPTH, n_out // tn),
        in_specs=[
            pl.BlockSpec((n_rows, D_MODEL), lambda l, j: (0, 0)),
            pl.BlockSpec((1, D_MODEL, tn), lambda l, j: (l, 0, j)),
            pl.BlockSpec((1, 1, tn), lambda l, j: (l, 0, j)),
        ],
        out_specs=pl.BlockSpec((1, n_rows, tn), lambda l, j: (l, 0, j)),
        out_shape=jax.ShapeDtypeStruct((DEPTH, n_rows, n_out), F32),
        compiler_params=_params(2),
        name="adaln_modulation",
    )(cvec, w_ada, b_ada.reshape(DEPTH, 1, n_out))


def _rope(p, cos, sin_a, sin_b):
    outs = []
    for k in range(p.shape[1] // LANES):
        xs = p[:, k * LANES:(k + 1) * LANES]
        from_hi = pltpu.roll(xs, LANES - ROPE_AXIS_DIM // 2, axis=1)
        from_lo = pltpu.roll(xs, ROPE_AXIS_DIM // 2, axis=1)
        outs.append(xs * cos + from_hi * sin_a + from_lo * sin_b)
    return jnp.concatenate(outs, axis=1)


def _inproj_kernel(x_ref, mod_ref, g_ref, w_ref, bg_ref, cos_ref, sa_ref, sb_ref,
                   fa_ref, qb_ref, kb_ref, vb_ref, qc_ref, kc_ref, vc_ref, gate_ref):
    a = _norm_modulate(x_ref[0], g_ref[...], mod_ref[0:1, :], mod_ref[1:2, :]).astype(BF16)
    cos, sin_a, sin_b = cos_ref[...], sa_ref[...], sb_ref[...]

    def proj(seg):
        return _dot(a, w_ref[:, seg * BRANCH_W:(seg + 1) * BRANCH_W])

    fa_ref[0] = proj(0).astype(BF16)
    qb_ref[0] = (_rope(proj(1), cos, sin_a, sin_b) * (DA_HEAD_DIM ** -0.5)).astype(BF16)
    kb_ref[0] = _rope(proj(2), cos, sin_a, sin_b).astype(BF16)
    vb_ref[0] = proj(3).astype(BF16)
    qc_ref[0] = (proj(4) * (NA_HEAD_DIM ** -0.5)).astype(BF16)
    kc_ref[0] = proj(5).astype(BF16)
    vc_ref[0] = proj(6).astype(BF16)
    for j in range(N_BRANCH * D_MODEL // BRANCH_W):
        z = proj(7 + j) + bg_ref[:, j * BRANCH_W:(j + 1) * BRANCH_W]
        gate_ref[0, :, j * BRANCH_W:(j + 1) * BRANCH_W] = (1.0 / (1.0 + jnp.exp(-z))).astype(BF16)


def _in_projection(h, mod, g_mix, w_in, b_gate, rope_tabs, n_lat):
    bsz, n_tok, _ = h.shape
    n_tiles = n_tok // TOK_TILE
    lat_tiles = n_lat // TOK_TILE
    proj_w = w_in.shape[1]
    tok = lambda w: pl.BlockSpec((1, TOK_TILE, w), lambda b, t: (b, t, 0))
    tab = pl.BlockSpec((TOK_TILE, LANES), lambda b, t: (t, 0))
    branch = jax.ShapeDtypeStruct((bsz, n_tok, BRANCH_W), BF16)
    return pl.pallas_call(
        _inproj_kernel,
        grid=(bsz, n_tiles),
        in_specs=[
            tok(D_MODEL),
            pl.BlockSpec((None, None, N_MOD, D_MODEL), lambda b, t: (b, t // lat_tiles, 0, 0)),
            _resident((1, D_MODEL)),
            _resident((D_MODEL, proj_w)),
            _resident((1, N_BRANCH * D_MODEL)),
            tab, tab, tab,
        ],
        out_specs=[tok(BRANCH_W)] * 7 + [tok(N_BRANCH * D_MODEL)],
        out_shape=[branch] * 7 + [jax.ShapeDtypeStruct((bsz, n_tok, N_BRANCH * D_MODEL), BF16)],
        compiler_params=_params(2),
        name="in_projection",
    )(h, mod, g_mix.reshape(1, D_MODEL), w_in, b_gate.reshape(1, -1), *rope_tabs)


def _dft_kernel(u_ref, cl_ref, sl_ref, cc_ref, sc_ref, cg_ref, sg_ref, o_ref, *, n_lat, n_ctx):
    t = pl.program_id(1)
    lat_tiles = n_lat // TOK_TILE

    def finish(p, q):
        p = p.astype(BF16)
        q = q.astype(BF16)
        for g in range(FN_GROUPS):
            sl = slice(g * FN_GROUP_DIM, (g + 1) * FN_GROUP_DIM)
            f = _dot(p[:, sl], cg_ref[...]) - _dot(q[:, sl], sg_ref[...])
            o_ref[0, :, sl] = f.astype(BF16)

    @pl.when(t < lat_tiles)
    def _():
        u = u_ref[0, 0:n_lat, :]
        finish(_dot(cl_ref[...], u), _dot(sl_ref[...], u))

    @pl.when(t >= lat_tiles)
    def _():
        u = u_ref[0, n_lat:n_lat + n_ctx, :]
        finish(_dot(cc_ref[...], u), _dot(sc_ref[...], u))


def _fourier_mix(fa_in, tabs, n_lat, n_ctx, with_ctx):
    bsz, n_tok, _ = fa_in.shape
    lat_tiles = n_lat // TOK_TILE
    n_tiles = n_tok // TOK_TILE if with_ctx else lat_tiles
    cl, sl, cc, sc, cg, sg = tabs
    lat_tab = pl.BlockSpec((TOK_TILE, n_lat), lambda b, t: (jnp.minimum(t, lat_tiles - 1), 0))
    return pl.pallas_call(
        functools.partial(_dft_kernel, n_lat=n_lat, n_ctx=n_ctx),
        grid=(bsz, n_tiles),
        in_specs=[
            pl.BlockSpec((1, n_tok, FN_WIDTH), lambda b, t: (b, 0, 0)),
            lat_tab, lat_tab,
            _resident((n_ctx, n_ctx)), _resident((n_ctx, n_ctx)),
            _resident((FN_GROUP_DIM, FN_GROUP_DIM)), _resident((FN_GROUP_DIM, FN_GROUP_DIM)),
        ],
        out_specs=pl.BlockSpec((1, TOK_TILE, FN_WIDTH), lambda b, t: (b, t, 0)),
        out_shape=jax.ShapeDtypeStruct((bsz, n_tok, FN_WIDTH), BF16),
        compiler_params=_params(2),
        name="fourier_mix",
    )(fa_in, cl, sl, cc, sc, cg, sg)


def _dft_tables(n, scale):
    k = jnp.arange(n, dtype=jnp.int32)
    ang = ((k[:, None] * k[None, :]) % n).astype(F32) * (2.0 * math.pi / n)
    return (jnp.cos(ang) * scale).astype(BF16), (jnp.sin(ang) * scale).astype(BF16)


def _softmax_pv(s, v):
    m = jnp.max(s, axis=-1, keepdims=True)
    e = jnp.exp(s - m)
    return _dot(e.astype(BF16), v) / jnp.sum(e, axis=-1, keepdims=True)


def _diffattn_kernel(q_ref, k_ref, v_ref, lam_ref, g_ref, o_ref, *, n_lat, n_ctx, lam_init):
    t = pl.program_id(1)
    lat_tiles = n_lat // TOK_TILE
    lv = lam_ref[...]
    lam = (jnp.exp(jnp.sum(lv[0:1] * lv[1:2], axis=-1, keepdims=True))
           - jnp.exp(jnp.sum(lv[2:3] * lv[3:4], axis=-1, keepdims=True)) + lam_init)
    first_half = lax.broadcasted_iota(jnp.int32, (1, LANES), 1) < DA_HEAD_DIM

    def attend(k_lo, k_len):
        for h in range(DA_HEADS):
            sl = slice(h * DA_V_DIM, (h + 1) * DA_V_DIM)
            q = q_ref[0, :, sl]
            k = k_ref[0, k_lo:k_lo + k_len, sl]
            v = v_ref[0, k_lo:k_lo + k_len, sl]
            q1 = jnp.where(first_half, q, jnp.zeros_like(q))
            q2 = jnp.where(first_half, jnp.zeros_like(q), q)
            o = _softmax_pv(_nt_dot(q1, k), v) - lam * _softmax_pv(_nt_dot(q2, k), v)
            ms = jnp.mean(o * o, axis=-1, keepdims=True)
            o = o * lax.rsqrt(ms + SUBLN_EPS) * g_ref[...] * (1.0 - lam_init)
            o_ref[0, :, sl] = o.astype(BF16)

    @pl.when(t < lat_tiles)
    def _():
        attend(0, n_lat + n_ctx)

    @pl.when(t >= lat_tiles)
    def _():
        attend(n_lat, n_ctx)


def _diff_attention(qb, kb, vb, lam_vec, subln_g, lam_init, n_lat, n_ctx, with_ctx):
    bsz, n_tok, _ = qb.shape
    n_tiles = (n_tok if with_ctx else n_lat) // TOK_TILE
    full = pl.BlockSpec((1, n_tok, BRANCH_W), lambda b, t: (b, 0, 0))
    tile = pl.BlockSpec((1, TOK_TILE, BRANCH_W), lambda b, t: (b, t, 0))
    return pl.pallas_call(
        functools.partial(_diffattn_kernel, n_lat=n_lat, n_ctx=n_ctx, lam_init=lam_init),
        grid=(bsz, n_tiles),
        in_specs=[tile, full, full, _resident((4, DA_HEAD_DIM)), _resident((1, DA_V_DIM))],
        out_specs=tile,
        out_shape=jax.ShapeDtypeStruct((bsz, n_tok, BRANCH_W), BF16),
        compiler_params=_params(2),
        name="diff_attention",
    )(qb, kb, vb, lam_vec, subln_g.reshape(1, DA_V_DIM))


def _natten_kernel(q_ref, k_ref, v_ref, bias_ref, o_ref, *, n_lat, n_ctx):
    s_idx = pl.program_id(1)
    rows = n_lat // GRID_W
    kh = min(NA_ROWS, rows)
    n_win = kh * GRID_W
    first_half = lax.broadcasted_iota(jnp.int32, (1, LANES), 1) < NA_HEAD_DIM

    def run(window):
        r0 = jnp.clip(s_idx - kh // 2, 0, rows - kh)
        k_lo = pl.multiple_of(r0 * GRID_W, GRID_W)
        for pair in range(NA_HEADS // 2):
            sl = slice(pair * LANES, (pair + 1) * LANES)
            q = q_ref[0, :, sl]
            k_ctx = k_ref[0, n_lat:n_lat + n_ctx, sl]
            v_ctx = v_ref[0, n_lat:n_lat + n_ctx, sl]
            if window:
                k_win = k_ref[0, pl.ds(k_lo, n_win), sl]
                v_win = v_ref[0, pl.ds(k_lo, n_win), sl]
            halves = []
            for sub in range(2):
                keep = first_half if sub == 0 else jnp.logical_not(first_half)
                qm = jnp.where(keep, q, jnp.zeros_like(q))
                s_ctx = _nt_dot(qm, k_ctx)
                m = jnp.max(s_ctx, axis=-1, keepdims=True)
                if window:
                    s_win = _nt_dot(qm, k_win) + bias_ref[0, 2 * pair + sub]
                    m = jnp.maximum(m, jnp.max(s_win, axis=-1, keepdims=True))
                e_ctx = jnp.exp(s_ctx - m)
                den = jnp.sum(e_ctx, axis=-1, keepdims=True)
                num = _dot(e_ctx.astype(BF16), v_ctx)
                if window:
                    e_win = jnp.exp(s_win - m)
                    den = den + jnp.sum(e_win, axis=-1, keepdims=True)
                    num = num + _dot(e_win.astype(BF16), v_win)
                halves.append(num / den)
            o_ref[0, :, sl] = jnp.where(first_half, halves[0], halves[1]).astype(BF16)

    @pl.when(s_idx < rows)
    def _():
        run(True)

    @pl.when(s_idx >= rows)
    def _():
        run(False)


def _row_offset_index(s, rows, kh):
    return jnp.minimum(s - jnp.clip(s - kh // 2, 0, rows - kh), kh - 1)


def _neighbourhood_attention(qc, kc, vc, bias, n_lat, n_ctx, with_ctx):
    bsz, n_tok, _ = qc.shape
    rows = n_lat // GRID_W
    kh = min(NA_ROWS, rows)
    n_steps = (n_tok if with_ctx else n_lat) // GRID_W
    full = pl.BlockSpec((1, n_tok, BRANCH_W), lambda b, s: (b, 0, 0))
    tile = pl.BlockSpec((1, GRID_W, BRANCH_W), lambda b, s: (b, s, 0))
    return pl.pallas_call(
        functools.partial(_natten_kernel, n_lat=n_lat, n_ctx=n_ctx),
        grid=(bsz, n_steps),
        in_specs=[
            tile, full, full,
            pl.BlockSpec((1, NA_HEADS, GRID_W, kh * GRID_W),
                         lambda b, s: (_row_offset_index(s, rows, kh), 0, 0, 0)),
        ],
        out_specs=tile,
        out_shape=jax.ShapeDtypeStruct((bsz, n_tok, BRANCH_W), BF16),
        compiler_params=_params(2),
        name="neighbourhood_attention",
    )(qc, kc, vc, bias)


def _natten_bias(rpb, rows):
    kh = min(NA_ROWS, rows)
    c = jnp.arange(GRID_W, dtype=jnp.int32)
    c0 = jnp.clip(c - NA_COLS // 2, 0, GRID_W - NA_COLS)
    ok = (c[None, :] >= c0[:, None]) & (c[None, :] < c0[:, None] + NA_COLS)
    dc = jnp.clip(c[None, :] - c[:, None] + NA_COLS - 1, 0, 2 * NA_COLS - 2)
    o = jnp.arange(kh, dtype=jnp.int32)
    dr = o[None, :] - o[:, None] + NA_ROWS - 1
    b = rpb.astype(F32)[:, dr][:, :, :, dc]
    b = jnp.where(ok[None, None, None], b, NEG_INF)
    return jnp.transpose(b, (1, 0, 3, 2, 4)).reshape(kh, NA_HEADS, GRID_W, kh * GRID_W)


def _merge_kernel(x_ref, mod_ref, fa_ref, db_ref, nc_ref, gate_ref, wa_ref, wb_ref, wc_ref,
                  wo_ref, o_ref):
    def gate(j):
        return gate_ref[0, :, j * D_MODEL:(j + 1) * D_MODEL].astype(F32)

    y = (gate(0) * _dot(fa_ref[0], wa_ref[...]) + gate(1) * _dot(db_ref[0], wb_ref[...])
         + gate(2) * _dot(nc_ref[0], wc_ref[...]))
    y = _dot(y.astype(BF16), wo_ref[...])
    o_ref[0] = x_ref[0] + mod_ref[2:3, :] * y


def _merge(h, mod, fa, db, nc, gates, w_a, w_b, w_c, w_out, n_lat, with_ctx):
    bsz, n_tok, _ = h.shape
    lat_tiles = n_lat // TOK_TILE
    n_tiles = (n_tok if with_ctx else n_lat) // TOK_TILE
    tok = lambda w: pl.BlockSpec((1, TOK_TILE, w), lambda b, t: (b, t, 0))
    return pl.pallas_call(
        _merge_kernel,
        grid=(bsz, n_tiles),
        in_specs=[
            tok(D_MODEL),
            pl.BlockSpec((None, None, N_MOD, D_MODEL), lambda b, t: (b, t // lat_tiles, 0, 0)),
            tok(BRANCH_W), tok(BRANCH_W), tok(BRANCH_W), tok(N_BRANCH * D_MODEL),
            _resident((BRANCH_W, D_MODEL)), _resident((BRANCH_W, D_MODEL)),
            _resident((BRANCH_W, D_MODEL)), _resident((D_MODEL, D_MODEL)),
        ],
        out_specs=tok(D_MODEL),
        out_shape=jax.ShapeDtypeStruct((bsz, n_tiles * TOK_TILE, D_MODEL), F32),
        compiler_params=_params(2),
        name="merge_out_projection",
    )(h, mod, fa, db, nc, gates, w_a, w_b, w_c, w_out)


def _convffn_kernel(x_ref, prev_ref, next_ref, mod_ref, g_ref, wu_ref, cw_ref, cb_ref, wd_ref,
                    gf_ref, o_ref, lhs_ref, *, lat_tiles, n_tiles, final_norm):
    t = pl.program_id(1)
    g = g_ref[...]
    shift, scale = mod_ref[3:4, :], mod_ref[4:5, :]
    x = x_ref[0]
    has_prev = jnp.logical_and(t != 0, t != lat_tiles)
    has_next = jnp.logical_and(t != lat_tiles - 1, t != n_tiles - 1)
    prev = _norm_modulate(prev_ref[0], g, shift, scale) * has_prev.astype(F32)
    nxt = _norm_modulate(next_ref[0], g, shift, scale) * has_next.astype(F32)
    lhs_ref[0:SUBLANES, :] = prev
    lhs_ref[SUBLANES:SUBLANES + TOK_TILE, :] = _norm_modulate(x, g, shift, scale)
    lhs_ref[SUBLANES + TOK_TILE:, :] = nxt
    lhs = lhs_ref[...].astype(BF16)
    n_rows = TOK_TILE + 2 * SUBLANES

    def conv(u, col):
        w = cw_ref[:, col:col + FF_CHUNK]
        before = pltpu.roll(u, 1, axis=0)
        after = pltpu.roll(u, n_rows - 1, axis=0)
        v = before * w[0:1] + u * w[1:2] + after * w[2:3] + cb_ref[:, col:col + FF_CHUNK]
        return v[SUBLANES:SUBLANES + TOK_TILE]

    acc = jnp.zeros((TOK_TILE, D_MODEL), F32)
    for j in range(D_FF // FF_CHUNK):
        col_a, col_b = j * FF_CHUNK, D_FF + j * FF_CHUNK
        a = conv(_dot(lhs, wu_ref[:, col_a:col_a + FF_CHUNK]), col_a)
        b = conv(_dot(lhs, wu_ref[:, col_b:col_b + FF_CHUNK]), col_b)
        act = (a / (1.0 + jnp.exp(-a))) * b
        acc = acc + _dot(act.astype(BF16), wd_ref[col_a:col_a + FF_CHUNK, :])
    y = x + mod_ref[5:6, :] * acc
    if final_norm:
        ms = jnp.mean(y * y, axis=-1, keepdims=True)
        y = y * lax.rsqrt(ms + NORM_EPS) * gf_ref[...]
    o_ref[0] = y


def _conv_ffn(h, mod, g_ffn, w_up, conv_w, conv_b, w_down, g_final, n_lat, with_ctx, final_norm):
    bsz, n_rows_in, _ = h.shape
    lat_tiles = n_lat // TOK_TILE
    n_tiles = n_rows_in // TOK_TILE if with_ctx else lat_tiles
    blocks_per_tile = TOK_TILE // SUBLANES
    last_block = n_rows_in // SUBLANES - 1
    return pl.pallas_call(
        functools.partial(_convffn_kernel, lat_tiles=lat_tiles, n_tiles=n_tiles,
                          final_norm=final_norm),
        grid=(bsz, n_tiles),
        in_specs=[
            pl.BlockSpec((1, TOK_TILE, D_MODEL), lambda b, t: (b, t, 0)),
            pl.BlockSpec((1, SUBLANES, D_MODEL),
                         lambda b, t: (b, jnp.maximum(t * blocks_per_tile - 1, 0), 0)),
            pl.BlockSpec((1, SUBLANES, D_MODEL),
                         lambda b, t: (b, jnp.minimum((t + 1) * blocks_per_tile, last_block), 0)),
            pl.BlockSpec((None, None, N_MOD, D_MODEL), lambda b, t: (b, t // lat_tiles, 0, 0)),
            _resident((1, D_MODEL)),
            _resident((D_MODEL, 2 * D_FF)),
            _resident((3, 2 * D_FF)),
            _resident((1, 2 * D_FF)),
            _resident((D_FF, D_MODEL)),
            _resident((1, D_MODEL)),
        ],
        out_specs=pl.BlockSpec((1, TOK_TILE, D_MODEL), lambda b, t: (b, t, 0)),
        out_shape=jax.ShapeDtypeStruct((bsz, n_tiles * TOK_TILE, D_MODEL), F32),
        scratch_shapes=[pltpu.VMEM((TOK_TILE + 2 * SUBLANES, D_MODEL), F32)],
        compiler_params=_params(2),
        name="conv_ffn",
    )(h, h, h, mod, g_ffn.reshape(1, D_MODEL), w_up, conv_w, conv_b.reshape(1, -1), w_down,
      g_final.reshape(1, D_MODEL))


def _rope_tables(n_lat, n_ctx):
    t = jnp.arange(n_lat, dtype=jnp.int32)
    pos = jnp.stack([(t // GRID_W).astype(F32), (t % GRID_W).astype(F32)], axis=1)
    n_freq = ROPE_AXIS_DIM // 2
    inv = ROPE_THETA ** (-jnp.arange(n_freq, dtype=F32) / n_freq)
    ang = pos[:, :, None] * inv
    lane = jnp.arange(LANES, dtype=jnp.int32)
    axis = (lane % DA_HEAD_DIM) // ROPE_AXIS_DIM
    freq = lane % n_freq
    second_half = (lane % ROPE_AXIS_DIM) >= n_freq
    ang_l = ang[:, axis, freq]
    cos, sin = jnp.cos(ang_l), jnp.sin(ang_l)
    sin_a = jnp.where(second_half, 0.0, -sin)
    sin_b = jnp.where(second_half, sin, 0.0)
    pad = lambda a, v: jnp.concatenate([a, jnp.full((n_ctx, LANES), v, F32)], axis=0)
    return pad(cos, 1.0), pad(sin_a, 0.0), pad(sin_b, 0.0)


def kernel(x, c, ctx, c_ctx, w_ada, b_ada, g_mix, g_ffn, w_in, b_gate, w_a, lam, subln_g, w_b,
           rpb, w_c, w_out, w_up, conv_w, conv_b, w_down, g_final):
    bsz, n_lat, _ = x.shape
    n_ctx = ctx.shape[1]
    assert n_lat % TOK_TILE == 0 and n_ctx == TOK_TILE and n_lat % GRID_W == 0

    rope_tabs = _rope_tables(n_lat, n_ctx)
    cl, sl = _dft_tables(n_lat, (n_lat * FN_GROUP_DIM) ** -0.5)
    cc, sc = _dft_tables(n_ctx, (n_ctx * FN_GROUP_DIM) ** -0.5)
    cg, sg = _dft_tables(FN_GROUP_DIM, 1.0)

    n_mod_rows = 2 * SUBLANES * (-(-(bsz + 1) // (2 * SUBLANES)))
    cvec = jnp.zeros((n_mod_rows, D_MODEL), F32).at[:bsz].set(c).at[bsz].set(c_ctx)
    mods = _modulation(cvec, w_ada, b_ada).reshape(DEPTH, n_mod_rows, N_MOD, D_MODEL)

    h = jnp.concatenate([x, ctx], axis=1)
    for l in range(DEPTH):
        with_ctx = l != DEPTH - 1
        lam_init = 0.8 - 0.6 * math.exp(-0.3 * l)
        mod = jnp.stack([mods[l, :bsz],
                         jnp.broadcast_to(mods[l, bsz], (bsz, N_MOD, D_MODEL))], axis=1)
        fa_in, qb, kb, vb, qc, kc, vc, gates = _in_projection(
            h, mod, g_mix[l], w_in[l].astype(BF16), b_gate[l], rope_tabs, n_lat)
        fa = _fourier_mix(fa_in, (cl, sl, cc, sc, cg, sg), n_lat, n_ctx, with_ctx)
        db = _diff_attention(qb, kb, vb, lam[l], subln_g[l], lam_init, n_lat, n_ctx, with_ctx)
        nc = _neighbourhood_attention(qc, kc, vc, _natten_bias(rpb[l], n_lat // GRID_W),
                                      n_lat, n_ctx, with_ctx)
        h = _merge(h, mod, fa, db, nc, gates, w_a[l].astype(BF16), w_b[l].astype(BF16),
                   w_c[l].astype(BF16), w_out[l].astype(BF16), n_lat, with_ctx)
        h = _conv_ffn(h, mod, g_ffn[l], w_up[l].astype(BF16), conv_w[l], conv_b[l],
                      w_down[l].astype(BF16), g_final, n_lat, with_ctx, final_norm=not with_ctx)
    return h
```

```python
import functools
import math

import jax
import jax.numpy as jnp
from jax import lax
from jax.experimental import pallas as pl
from jax.experimental.pallas import tpu as pltpu

D_MODEL = 1024
DEPTH = 4
GRID_W = 64
FN_GROUPS = 4
FN_GROUP_DIM = 128
FN_WIDTH = FN_GROUPS * FN_GROUP_DIM
DA_HEADS = 4
DA_HEAD_DIM = 64
DA_V_DIM = 2 * DA_HEAD_DIM
NA_HEADS = 8
NA_HEAD_DIM = 64
NA_ROWS = 8
NA_COLS = 16
BRANCH_W = 512
N_BRANCH = 3
ROPE_THETA = 10000.0
ROPE_AXIS_DIM = DA_HEAD_DIM // 2
D_FF = 2816
N_MOD = 6
NORM_EPS = 1e-6
SUBLN_EPS = 1e-5
NEG_INF = -1e30

LANES = 128
SUBLANES = 8
TOK_TILE = 256
FF_CHUNK = 256
NA_STEP_ROWS = 2
NA_SLAB_ROWS = 10
VMEM_LIMIT = 56 * 1024 * 1024
ONES_ROWS = 16
LOG2_E = math.log2(math.e)

BF16 = jnp.bfloat16
F32 = jnp.float32


def _params(n_grid_dims):
    return pltpu.CompilerParams(dimension_semantics=("arbitrary",) * n_grid_dims,
                                vmem_limit_bytes=VMEM_LIMIT)


def _resident(shape):
    return pl.BlockSpec(shape, lambda *_: (0,) * len(shape), pipeline_mode=pl.Buffered(1))


def _nt_dot(a, b):
    return lax.dot_general(a, b, (((1,), (1,)), ((), ())), preferred_element_type=F32)


def _dot(a, b):
    return jnp.dot(a, b, preferred_element_type=F32)


def _norm_modulate(x, g, shift, scale):
    ms = jnp.mean(x * x, axis=-1, keepdims=True)
    return (x * lax.rsqrt(ms + NORM_EPS) * g) * (1.0 + scale) + shift


def _split_bf16(v):
    hi = v.astype(BF16)
    lo = (v - hi.astype(F32)).astype(BF16)
    return hi, lo


def _mod_kernel(c_ref, w_ref, b_ref, o_ref):
    c = c_ref[...]
    s = c / (1.0 + jnp.exp(-c))
    s_hi, s_lo = _split_bf16(s)
    w_hi, w_lo = _split_bf16(w_ref[0])
    acc = _dot(s_hi, w_hi) + (_dot(s_hi, w_lo) + _dot(s_lo, w_hi))
    o_ref[0] = acc + b_ref[0]


def _modulation(cvec, w_ada, b_ada):
    n_rows = cvec.shape[0]
    n_out = N_MOD * D_MODEL
    tn = 1536
    return pl.pallas_call(
        _mod_kernel,
        grid=(DEPTH, n_out // tn),
        in_specs=[
            pl.BlockSpec((n_rows, D_MODEL), lambda l, j: (0, 0)),
            pl.BlockSpec((1, D_MODEL, tn), lambda l, j: (l, 0, j)),
            pl.BlockSpec((1, 1, tn), lambda l, j: (l, 0, j)),
        ],
        out_specs=pl.BlockSpec((1, n_rows, tn), lambda l, j: (l, 0, j)),
        out_shape=jax.ShapeDtypeStruct((DEPTH, n_rows, n_out), F32),
        compiler_params=_params(2),
        name="adaln_modulation",
    )(cvec, w_ada, b_ada.reshape(DEPTH, 1, n_out))


def _rope(p, cos, sin_a, sin_b):
    outs = []
    for k in range(p.shape[1] // LANES):
        xs = p[:, k * LANES:(k + 1) * LANES]
        from_hi = pltpu.roll(xs, LANES - ROPE_AXIS_DIM // 2, axis=1)
        from_lo = pltpu.roll(xs, ROPE_AXIS_DIM // 2, axis=1)
        outs.append(xs * cos + from_hi * sin_a + from_lo * sin_b)
    return jnp.concatenate(outs, axis=1)


def _inproj_kernel(x_ref, mod_ref, g_ref, w_ref, bg_ref, cos_ref, sa_ref, sb_ref,
                   fa_ref, qb_ref, kb_ref, vbt_ref, qc_ref, kc_ref, vct_ref, gate_ref):
    a = _norm_modulate(x_ref[0], g_ref[...], mod_ref[0:1, :], mod_ref[1:2, :]).astype(BF16)
    cos, sin_a, sin_b = cos_ref[...], sa_ref[...], sb_ref[...]

    def proj(seg):
        return _dot(a, w_ref[:, seg * BRANCH_W:(seg + 1) * BRANCH_W])

    fa_ref[0] = proj(0).astype(BF16)
    qb_ref[0] = (_rope(proj(1), cos, sin_a, sin_b) * (DA_HEAD_DIM ** -0.5 * LOG2_E)).astype(BF16)
    kb_ref[0] = _rope(proj(2), cos, sin_a, sin_b).astype(BF16)
    vbt_ref[0] = proj(3).T.astype(BF16)
    qc_ref[0] = (proj(4) * (NA_HEAD_DIM ** -0.5 * LOG2_E)).astype(BF16)
    kc_ref[0] = proj(5).astype(BF16)
    vct_ref[0] = proj(6).T.astype(BF16)
    for j in range(N_BRANCH * D_MODEL // BRANCH_W):
        z = proj(7 + j) + bg_ref[:, j * BRANCH_W:(j + 1) * BRANCH_W]
        gate_ref[0, :, j * BRANCH_W:(j + 1) * BRANCH_W] = (1.0 / (1.0 + jnp.exp(-z))).astype(BF16)


def _in_projection(h, mod, g_mix, w_in, b_gate, rope_tabs, n_lat):
    bsz, n_tok, _ = h.shape
    n_tiles = n_tok // TOK_TILE
    lat_tiles = n_lat // TOK_TILE
    proj_w = w_in.shape[1]
    tok = lambda w: pl.BlockSpec((1, TOK_TILE, w), lambda b, t: (b, t, 0))
    tab = pl.BlockSpec((TOK_TILE, LANES), lambda b, t: (t, 0))
    branch = jax.ShapeDtypeStruct((bsz, n_tok, BRANCH_W), BF16)
    tok_t = pl.BlockSpec((1, BRANCH_W, TOK_TILE), lambda b, t: (b, 0, t))
    branch_t = jax.ShapeDtypeStruct((bsz, BRANCH_W, n_tok), BF16)
    return pl.pallas_call(
        _inproj_kernel,
        grid=(bsz, n_tiles),
        in_specs=[
            tok(D_MODEL),
            pl.BlockSpec((None, None, N_MOD, D_MODEL), lambda b, t: (b, t // lat_tiles, 0, 0)),
            _resident((1, D_MODEL)),
            _resident((D_MODEL, proj_w)),
            _resident((1, N_BRANCH * D_MODEL)),
            tab, tab, tab,
        ],
        out_specs=[tok(BRANCH_W)] * 3 + [tok_t] + [tok(BRANCH_W)] * 2 + [tok_t]
                  + [tok(N_BRANCH * D_MODEL)],
        out_shape=[branch] * 3 + [branch_t] + [branch] * 2 + [branch_t]
                  + [jax.ShapeDtypeStruct((bsz, n_tok, N_BRANCH * D_MODEL), BF16)],
        compiler_params=_params(2),
        name="in_projection",
    )(h, mod, g_mix.reshape(1, D_MODEL), w_in, b_gate.reshape(1, -1), *rope_tabs)


def _dft_kernel(u_ref, cl_ref, sl_ref, cc_ref, sc_ref, cg_ref, sg_ref, o_ref, *, n_lat, n_ctx):
    t = pl.program_id(1)
    lat_tiles = n_lat // TOK_TILE

    def finish(p, q):
        p = p.astype(BF16)
        q = q.astype(BF16)
        for g in range(FN_GROUPS):
            sl = slice(g * FN_GROUP_DIM, (g + 1) * FN_GROUP_DIM)
            f = _dot(p[:, sl], cg_ref[...]) - _dot(q[:, sl], sg_ref[...])
            o_ref[0, :, sl] = f.astype(BF16)

    @pl.when(t < lat_tiles)
    def _():
        u = u_ref[0, 0:n_lat, :]
        finish(_dot(cl_ref[...], u), _dot(sl_ref[...], u))

    @pl.when(t >= lat_tiles)
    def _():
        u = u_ref[0, n_lat:n_lat + n_ctx, :]
        finish(_dot(cc_ref[...], u), _dot(sc_ref[...], u))


def _fourier_mix(fa_in, tabs, n_lat, n_ctx, with_ctx):
    bsz, n_tok, _ = fa_in.shape
    lat_tiles = n_lat // TOK_TILE
    n_tiles = n_tok // TOK_TILE if with_ctx else lat_tiles
    cl, sl, cc, sc, cg, sg = tabs
    lat_tab = pl.BlockSpec((TOK_TILE, n_lat), lambda b, t: (jnp.minimum(t, lat_tiles - 1), 0))
    return pl.pallas_call(
        functools.partial(_dft_kernel, n_lat=n_lat, n_ctx=n_ctx),
        grid=(bsz, n_tiles),
        in_specs=[
            pl.BlockSpec((1, n_tok, FN_WIDTH), lambda b, t: (b, 0, 0)),
            lat_tab, lat_tab,
            _resident((n_ctx, n_ctx)), _resident((n_ctx, n_ctx)),
            _resident((FN_GROUP_DIM, FN_GROUP_DIM)), _resident((FN_GROUP_DIM, FN_GROUP_DIM)),
        ],
        out_specs=pl.BlockSpec((1, TOK_TILE, FN_WIDTH), lambda b, t: (b, t, 0)),
        out_shape=jax.ShapeDtypeStruct((bsz, n_tok, FN_WIDTH), BF16),
        compiler_params=_params(2),
        name="fourier_mix",
    )(fa_in, cl, sl, cc, sc, cg, sg)


def _dft_tables(n, scale):
    k = jnp.arange(n, dtype=jnp.int32)
    ang = ((k[:, None] * k[None, :]) % n).astype(F32) * (2.0 * math.pi / n)
    return (jnp.cos(ang) * scale).astype(BF16), (jnp.sin(ang) * scale).astype(BF16)


def _stack_sub_heads(q, first_half):
    zero = jnp.zeros_like(q)
    return jnp.concatenate([jnp.where(first_half, q, zero), jnp.where(first_half, zero, q)], axis=0)


def _diffattn_kernel(q_ref, k_ref, vt_ref, lam_ref, g_ref, o_ref, *, n_lat, n_ctx, lam_init):
    t = pl.program_id(1)
    lat_tiles = n_lat // TOK_TILE
    lv = lam_ref[...]
    lam = (jnp.exp(jnp.sum(lv[0:1] * lv[1:2], axis=-1, keepdims=True))
           - jnp.exp(jnp.sum(lv[2:3] * lv[3:4], axis=-1, keepdims=True)) + lam_init)
    first_half = lax.broadcasted_iota(jnp.int32, (1, LANES), 1) < DA_HEAD_DIM

    def attend(k_lo, k_len):
        ones = jnp.ones((ONES_ROWS, k_len), BF16)

        def scores(h):
            sl = slice(h * DA_V_DIM, (h + 1) * DA_V_DIM)
            return _nt_dot(k_ref[0, k_lo:k_lo + k_len, sl],
                           _stack_sub_heads(q_ref[0, :, sl], first_half))

        s_next = scores(0)
        for h in range(DA_HEADS):
            sl = slice(h * DA_V_DIM, (h + 1) * DA_V_DIM)
            s_t = s_next
            if h + 1 < DA_HEADS:
                s_next = scores(h + 1)
            e = jnp.exp2(s_t - jnp.max(s_t, axis=0, keepdims=True)).astype(BF16)
            o_t = _dot(jnp.concatenate([vt_ref[0, sl, k_lo:k_lo + k_len], ones], axis=0), e)
            o12 = (o_t[0:DA_V_DIM] * (1.0 / o_t[DA_V_DIM:DA_V_DIM + 1])).T
            o = o12[0:TOK_TILE] - lam * o12[TOK_TILE:2 * TOK_TILE]
            ms = jnp.mean(o * o, axis=-1, keepdims=True)
            o = o * lax.rsqrt(ms + SUBLN_EPS) * g_ref[...] * (1.0 - lam_init)
            o_ref[0, :, sl] = o.astype(BF16)

    @pl.when(t < lat_tiles)
    def _():
        attend(0, n_lat + n_ctx)

    @pl.when(t >= lat_tiles)
    def _():
        attend(n_lat, n_ctx)


def _diff_attention(qb, kb, vbt, lam_vec, subln_g, lam_init, n_lat, n_ctx, with_ctx):
    bsz, n_tok, _ = qb.shape
    n_tiles = (n_tok if with_ctx else n_lat) // TOK_TILE
    full = pl.BlockSpec((1, n_tok, BRANCH_W), lambda b, t: (b, 0, 0))
    full_t = pl.BlockSpec((1, BRANCH_W, n_tok), lambda b, t: (b, 0, 0))
    tile = pl.BlockSpec((1, TOK_TILE, BRANCH_W), lambda b, t: (b, t, 0))
    return pl.pallas_call(
        functools.partial(_diffattn_kernel, n_lat=n_lat, n_ctx=n_ctx, lam_init=lam_init),
        grid=(bsz, n_tiles),
        in_specs=[tile, full, full_t, _resident((4, DA_HEAD_DIM)), _resident((1, DA_V_DIM))],
        out_specs=tile,
        out_shape=jax.ShapeDtypeStruct((bsz, n_tok, BRANCH_W), BF16),
        compiler_params=_params(2),
        name="diff_attention",
    )(qb, kb, vbt, lam_vec, subln_g.reshape(1, DA_V_DIM))


def _natten_kernel(q_ref, k_ref, vt_ref, bias_ref, o_ref, *, n_lat, n_ctx):
    s_idx = pl.program_id(1)
    rows = n_lat // GRID_W
    kh = min(NA_ROWS, rows)
    n_slab = NA_SLAB_ROWS * GRID_W
    n_q = NA_STEP_ROWS * GRID_W
    first_half = lax.broadcasted_iota(jnp.int32, (1, LANES), 1) < NA_HEAD_DIM

    def run(window):
        first_row = jnp.clip(NA_STEP_ROWS * s_idx - kh // 2, 0, rows - kh)
        k_lo = pl.multiple_of(first_row * GRID_W, LANES)
        pairs = [slice(p * LANES, (p + 1) * LANES) for p in range(NA_HEADS // 2)]
        qs = [_stack_sub_heads(q_ref[0, :, sl], first_half) for sl in pairs]
        s_ctx = [_nt_dot(k_ref[0, n_lat:n_lat + n_ctx, sl], q) for sl, q in zip(pairs, qs)]
        if window:
            s_win = [_nt_dot(k_ref[0, pl.ds(k_lo, n_slab), sl], q) + bias_ref[0, p]
                     for p, (sl, q) in enumerate(zip(pairs, qs))]
        for p, sl in enumerate(pairs):
            m = jnp.max(s_ctx[p], axis=0, keepdims=True)
            if window:
                m = jnp.maximum(m, jnp.max(s_win[p], axis=0, keepdims=True))
            e_ctx = jnp.exp2(s_ctx[p] - m).astype(BF16)
            ones = jnp.ones((ONES_ROWS, n_ctx), BF16)
            o_t = _dot(jnp.concatenate([vt_ref[0, sl, n_lat:n_lat + n_ctx], ones], axis=0), e_ctx)
            if window:
                e_win = jnp.exp2(s_win[p] - m).astype(BF16)
                ones = jnp.ones((ONES_ROWS, n_slab), BF16)
                o_t = o_t + _dot(
                    jnp.concatenate([vt_ref[0, sl, pl.ds(k_lo, n_slab)], ones], axis=0), e_win)
            o2 = (o_t[0:LANES] * (1.0 / o_t[LANES:LANES + 1])).T
            o_ref[0, :, sl] = jnp.where(first_half, o2[0:n_q], o2[n_q:2 * n_q]).astype(BF16)

    @pl.when(s_idx < rows // NA_STEP_ROWS)
    def _():
        run(True)

    @pl.when(s_idx >= rows // NA_STEP_ROWS)
    def _():
        run(False)


def _natten_step_classes(rows):
    kh = min(NA_ROWS, rows)
    classes, first_steps = [], []
    for s in range(rows // NA_STEP_ROWS):
        slab = min(max(NA_STEP_ROWS * s - kh // 2, 0), rows - kh)
        geom = tuple((r - slab, min(max(r - kh // 2, 0), rows - kh) - slab)
                     for r in range(NA_STEP_ROWS * s, NA_STEP_ROWS * (s + 1)))
        geom = geom + (min(NA_SLAB_ROWS, rows - slab),)
        if not classes or classes[-1] != geom:
            assert geom not in classes
            classes.append(geom)
            first_steps.append(s)
    return classes, first_steps


def _natten_bias(rpb, rows):
    kh = min(NA_ROWS, rows)
    classes, _ = _natten_step_classes(rows)
    c = jnp.arange(GRID_W, dtype=jnp.int32)
    c0 = jnp.clip(c - NA_COLS // 2, 0, GRID_W - NA_COLS)
    col_ok = (c[None, :] >= c0[:, None]) & (c[None, :] < c0[:, None] + NA_COLS)
    dc = jnp.clip(c[None, :] - c[:, None] + NA_COLS - 1, 0, 2 * NA_COLS - 2)
    kr = jnp.arange(NA_SLAB_ROWS, dtype=jnp.int32)
    tables = []
    for geom in classes:
        n_latent_rows = geom[-1]
        q_off = jnp.array([g[0] for g in geom[:-1]], jnp.int32)
        w_off = jnp.array([g[1] for g in geom[:-1]], jnp.int32)
        row_ok = ((kr[None, :] >= w_off[:, None]) & (kr[None, :] < w_off[:, None] + kh)
                  & (kr[None, :] < n_latent_rows))
        dr = jnp.clip(kr[None, :] - q_off[:, None] + NA_ROWS - 1, 0, 2 * NA_ROWS - 2)
        b = rpb.astype(F32)[:, dr][:, :, :, dc]
        b = jnp.where(row_ok[None, :, :, None, None] & col_ok[None, None, None], b * LOG2_E, NEG_INF)
        b = b.reshape(NA_HEADS // 2, 2, NA_STEP_ROWS, NA_SLAB_ROWS, GRID_W, GRID_W)
        b = jnp.transpose(b, (0, 3, 5, 1, 2, 4))
        tables.append(b.reshape(NA_HEADS // 2, NA_SLAB_ROWS * GRID_W, 2 * NA_STEP_ROWS * GRID_W))
    return jnp.stack(tables)


def _neighbourhood_attention(qc, kc, vct, bias, n_lat, n_ctx, with_ctx):
    bsz, n_tok, _ = qc.shape
    rows = n_lat // GRID_W
    kh = min(NA_ROWS, rows)
    n_q = NA_STEP_ROWS * GRID_W
    assert (kh // 2) % NA_STEP_ROWS == 0 and (rows - kh) % NA_STEP_ROWS == 0 and n_q == LANES
    assert (rows - kh + NA_SLAB_ROWS) * GRID_W <= n_tok and kh + NA_STEP_ROWS <= NA_SLAB_ROWS
    _, first_steps = _natten_step_classes(rows)
    n_steps = (n_tok if with_ctx else n_lat) // n_q

    def bias_class(s):
        return sum((s >= f).astype(jnp.int32) for f in first_steps[1:])

    full = pl.BlockSpec((1, n_tok, BRANCH_W), lambda b, s: (b, 0, 0))
    full_t = pl.BlockSpec((1, BRANCH_W, n_tok), lambda b, s: (b, 0, 0))
    tile = pl.BlockSpec((1, n_q, BRANCH_W), lambda b, s: (b, s, 0))
    return pl.pallas_call(
        functools.partial(_natten_kernel, n_lat=n_lat, n_ctx=n_ctx),
        grid=(bsz, n_steps),
        in_specs=[
            tile, full, full_t,
            pl.BlockSpec((1,) + bias.shape[1:], lambda b, s: (bias_class(s), 0, 0, 0)),
        ],
        out_specs=tile,
        out_shape=jax.ShapeDtypeStruct((bsz, n_tok, BRANCH_W), BF16),
        compiler_params=_params(2),
        name="neighbourhood_attention",
    )(qc, kc, vct, bias)


def _merge_kernel(x_ref, mod_ref, fa_ref, db_ref, nc_ref, gate_ref, wa_ref, wb_ref, wc_ref,
                  wo_ref, o_ref):
    def gate(j):
        return gate_ref[0, :, j * D_MODEL:(j + 1) * D_MODEL].astype(F32)

    y = (gate(0) * _dot(fa_ref[0], wa_ref[...]) + gate(1) * _dot(db_ref[0], wb_ref[...])
         + gate(2) * _dot(nc_ref[0], wc_ref[...]))
    y = _dot(y.astype(BF16), wo_ref[...])
    o_ref[0] = x_ref[0] + mod_ref[2:3, :] * y


def _merge(h, mod, fa, db, nc, gates, w_a, w_b, w_c, w_out, n_lat, with_ctx):
    bsz, n_tok, _ = h.shape
    lat_tiles = n_lat // TOK_TILE
    n_tiles = (n_tok if with_ctx else n_lat) // TOK_TILE
    tok = lambda w: pl.BlockSpec((1, TOK_TILE, w), lambda b, t: (b, t, 0))
    return pl.pallas_call(
        _merge_kernel,
        grid=(bsz, n_tiles),
        in_specs=[
            tok(D_MODEL),
            pl.BlockSpec((None, None, N_MOD, D_MODEL), lambda b, t: (b, t // lat_tiles, 0, 0)),
            tok(BRANCH_W), tok(BRANCH_W), tok(BRANCH_W), tok(N_BRANCH * D_MODEL),
            _resident((BRANCH_W, D_MODEL)), _resident((BRANCH_W, D_MODEL)),
            _resident((BRANCH_W, D_MODEL)), _resident((D_MODEL, D_MODEL)),
        ],
        out_specs=tok(D_MODEL),
        out_shape=jax.ShapeDtypeStruct((bsz, n_tiles * TOK_TILE, D_MODEL), F32),
        compiler_params=_params(2),
        name="merge_out_projection",
    )(h, mod, fa, db, nc, gates, w_a, w_b, w_c, w_out)


def _convffn_kernel(x_ref, prev_ref, next_ref, mod_ref, g_ref, wu_ref, cw_ref, cb_ref, wd_ref,
                    gf_ref, o_ref, lhs_ref, *, lat_tiles, n_tiles, final_norm):
    t = pl.program_id(1)
    g = g_ref[...]
    shift, scale = mod_ref[3:4, :], mod_ref[4:5, :]
    x = x_ref[0]
    has_prev = jnp.logical_and(t != 0, t != lat_tiles)
    has_next = jnp.logical_and(t != lat_tiles - 1, t != n_tiles - 1)
    prev = _norm_modulate(prev_ref[0], g, shift, scale) * has_prev.astype(F32)
    nxt = _norm_modulate(next_ref[0], g, shift, scale) * has_next.astype(F32)
    lhs_ref[0:SUBLANES, :] = prev
    lhs_ref[SUBLANES:SUBLANES + TOK_TILE, :] = _norm_modulate(x, g, shift, scale)
    lhs_ref[SUBLANES + TOK_TILE:, :] = nxt
    lhs = lhs_ref[...].astype(BF16)
    n_rows = TOK_TILE + 2 * SUBLANES

    def conv(u, col):
        w = cw_ref[:, col:col + FF_CHUNK]
        before = pltpu.roll(u, 1, axis=0)
        after = pltpu.roll(u, n_rows - 1, axis=0)
        v = before * w[0:1] + u * w[1:2] + after * w[2:3] + cb_ref[:, col:col + FF_CHUNK]
        return v[SUBLANES:SUBLANES + TOK_TILE]

    acc = jnp.zeros((TOK_TILE, D_MODEL), F32)
    for j in range(D_FF // FF_CHUNK):
        col_a, col_b = j * FF_CHUNK, D_FF + j * FF_CHUNK
        a = conv(_dot(lhs, wu_ref[:, col_a:col_a + FF_CHUNK]), col_a)
        b = conv(_dot(lhs, wu_ref[:, col_b:col_b + FF_CHUNK]), col_b)
        act = (a / (1.0 + jnp.exp(-a))) * b
        acc = acc + _dot(act.astype(BF16), wd_ref[col_a:col_a + FF_CHUNK, :])
    y = x + mod_ref[5:6, :] * acc
    if final_norm:
        ms = jnp.mean(y * y, axis=-1, keepdims=True)
        y = y * lax.rsqrt(ms + NORM_EPS) * gf_ref[...]
    o_ref[0] = y


def _conv_ffn(h, mod, g_ffn, w_up, conv_w, conv_b, w_down, g_final, n_lat, with_ctx, final_norm):
    bsz, n_rows_in, _ = h.shape
    lat_tiles = n_lat // TOK_TILE
    n_tiles = n_rows_in // TOK_TILE if with_ctx else lat_tiles
    blocks_per_tile = TOK_TILE // SUBLANES
    last_block = n_rows_in // SUBLANES - 1
    return pl.pallas_call(
        functools.partial(_convffn_kernel, lat_tiles=lat_tiles, n_tiles=n_tiles,
                          final_norm=final_norm),
        grid=(bsz, n_tiles),
        in_specs=[
            pl.BlockSpec((1, TOK_TILE, D_MODEL), lambda b, t: (b, t, 0)),
            pl.BlockSpec((1, SUBLANES, D_MODEL),
                         lambda b, t: (b, jnp.maximum(t * blocks_per_tile - 1, 0), 0)),
            pl.BlockSpec((1, SUBLANES, D_MODEL),
                         lambda b, t: (b, jnp.minimum((t + 1) * blocks_per_tile, last_block), 0)),
            pl.BlockSpec((None, None, N_MOD, D_MODEL), lambda b, t: (b, t // lat_tiles, 0, 0)),
            _resident((1, D_MODEL)),
            _resident((D_MODEL, 2 * D_FF)),
            _resident((3, 2 * D_FF)),
            _resident((1, 2 * D_FF)),
            _resident((D_FF, D_MODEL)),
            _resident((1, D_MODEL)),
        ],
        out_specs=pl.BlockSpec((1, TOK_TILE, D_MODEL), lambda b, t: (b, t, 0)),
        out_shape=jax.ShapeDtypeStruct((bsz, n_tiles * TOK_TILE, D_MODEL), F32),
        scratch_shapes=[pltpu.VMEM((TOK_TILE + 2 * SUBLANES, D_MODEL), F32)],
        compiler_params=_params(2),
        name="conv_ffn",
    )(h, h, h, mod, g_ffn.reshape(1, D_MODEL), w_up, conv_w, conv_b.reshape(1, -1), w_down,
      g_final.reshape(1, D_MODEL))


def _rope_tables(n_lat, n_ctx):
    t = jnp.arange(n_lat, dtype=jnp.int32)
    pos = jnp.stack([(t // GRID_W).astype(F32), (t % GRID_W).astype(F32)], axis=1)
    n_freq = ROPE_AXIS_DIM // 2
    inv = ROPE_THETA ** (-jnp.arange(n_freq, dtype=F32) / n_freq)
    ang = pos[:, :, None] * inv
    lane = jnp.arange(LANES, dtype=jnp.int32)
    axis = (lane % DA_HEAD_DIM) // ROPE_AXIS_DIM
    freq = lane % n_freq
    second_half = (lane % ROPE_AXIS_DIM) >= n_freq
    ang_l = ang[:, axis, freq]
    cos, sin = jnp.cos(ang_l), jnp.sin(ang_l)
    sin_a = jnp.where(second_half, 0.0, -sin)
    sin_b = jnp.where(second_half, sin, 0.0)
    pad = lambda a, v: jnp.concatenate([a, jnp.full((n_ctx, LANES), v, F32)], axis=0)
    return pad(cos, 1.0), pad(sin_a, 0.0), pad(sin_b, 0.0)


def kernel(x, c, ctx, c_ctx, w_ada, b_ada, g_mix, g_ffn, w_in, b_gate, w_a, lam, subln_g, w_b,
           rpb, w_c, w_out, w_up, conv_w, conv_b, w_down, g_final):
    bsz, n_lat, _ = x.shape
    n_ctx = ctx.shape[1]
    assert n_lat % TOK_TILE == 0 and n_ctx == TOK_TILE and n_lat % GRID_W == 0

    rope_tabs = _rope_tables(n_lat, n_ctx)
    cl, sl = _dft_tables(n_lat, (n_lat * FN_GROUP_DIM) ** -0.5)
    cc, sc = _dft_tables(n_ctx, (n_ctx * FN_GROUP_DIM) ** -0.5)
    cg, sg = _dft_tables(FN_GROUP_DIM, 1.0)

    n_mod_rows = 2 * SUBLANES * (-(-(bsz + 1) // (2 * SUBLANES)))
    cvec = jnp.zeros((n_mod_rows, D_MODEL), F32).at[:bsz].set(c).at[bsz].set(c_ctx)
    mods = _modulation(cvec, w_ada, b_ada).reshape(DEPTH, n_mod_rows, N_MOD, D_MODEL)

    h = jnp.concatenate([x, ctx], axis=1)
    for l in range(DEPTH):
        with_ctx = l != DEPTH - 1
        lam_init = 0.8 - 0.6 * math.exp(-0.3 * l)
        mod = jnp.stack([mods[l, :bsz],
                         jnp.broadcast_to(mods[l, bsz], (bsz, N_MOD, D_MODEL))], axis=1)
        fa_in, qb, kb, vbt, qc, kc, vct, gates = _in_projection(
            h, mod, g_mix[l], w_in[l].astype(BF16), b_gate[l], rope_tabs, n_lat)
        fa = _fourier_mix(fa_in, (cl, sl, cc, sc, cg, sg), n_lat, n_ctx, with_ctx)
        db = _diff_attention(qb, kb, vbt, lam[l], subln_g[l], lam_init, n_lat, n_ctx, with_ctx)
        nc = _neighbourhood_attention(qc, kc, vct, _natten_bias(rpb[l], n_lat // GRID_W),
                                      n_lat, n_ctx, with_ctx)
        h = _merge(h, mod, fa, db, nc, gates, w_a[l].astype(BF16), w_b[l].astype(BF16),
                   w_c[l].astype(BF16), w_out[l].astype(BF16), n_lat, with_ctx)
        h = _conv_ffn(h, mod, g_ffn[l], w_up[l].astype(BF16), conv_w[l], conv_b[l],
                      w_down[l].astype(BF16), g_final, n_lat, with_ctx, final_norm=not with_ctx)
    return h
```

```python
import functools
import math

import jax
import jax.numpy as jnp
from jax import lax
from jax.experimental import pallas as pl
from jax.experimental.pallas import tpu as pltpu

D_MODEL = 1024
DEPTH = 4
GRID_W = 64
FN_GROUPS = 4
FN_GROUP_DIM = 128
FN_WIDTH = FN_GROUPS * FN_GROUP_DIM
DA_HEADS = 4
DA_HEAD_DIM = 64
DA_V_DIM = 2 * DA_HEAD_DIM
NA_HEADS = 8
NA_HEAD_DIM = 64
NA_ROWS = 8
NA_COLS = 16
BRANCH_W = 512
N_BRANCH = 3
ROPE_THETA = 10000.0
ROPE_AXIS_DIM = DA_HEAD_DIM // 2
D_FF = 2816
N_MOD = 6
NORM_EPS = 1e-6
SUBLN_EPS = 1e-5
NEG_INF = -1e30

LANES = 128
SUBLANES = 8
TOK_TILE = 256
FF_CHUNK = 256
NA_STEP_ROWS = 2
NA_SLAB_ROWS = 10
VMEM_LIMIT = 56 * 1024 * 1024
ONES_ROWS = 16
LOG2_E = math.log2(math.e)

BF16 = jnp.bfloat16
F32 = jnp.float32


def _params(n_grid_dims):
    return pltpu.CompilerParams(dimension_semantics=("arbitrary",) * n_grid_dims,
                                vmem_limit_bytes=VMEM_LIMIT)


def _resident(shape):
    return pl.BlockSpec(shape, lambda *_: (0,) * len(shape), pipeline_mode=pl.Buffered(1))


def _nt_dot(a, b):
    return lax.dot_general(a, b, (((1,), (1,)), ((), ())), preferred_element_type=F32)


def _dot(a, b):
    return jnp.dot(a, b, preferred_element_type=F32)


def _norm_modulate(x, g, shift, scale):
    ms = jnp.mean(x * x, axis=-1, keepdims=True)
    return (x * lax.rsqrt(ms + NORM_EPS) * g) * (1.0 + scale) + shift


def _split_bf16(v):
    hi = v.astype(BF16)
    lo = (v - hi.astype(F32)).astype(BF16)
    return hi, lo


def _mod_kernel(c_ref, w_ref, b_ref, o_ref):
    c = c_ref[...]
    s = c / (1.0 + jnp.exp(-c))
    s_hi, s_lo = _split_bf16(s)
    w_hi, w_lo = _split_bf16(w_ref[0])
    acc = _dot(s_hi, w_hi) + (_dot(s_hi, w_lo) + _dot(s_lo, w_hi))
    o_ref[0] = acc + b_ref[0]


def _modulation(cvec, w_ada, b_ada):
    n_rows = cvec.shape[0]
    n_out = N_MOD * D_MODEL
    tn = 1536
    return pl.pallas_call(
        _mod_kernel,
        grid=(DEPTH, n_out // tn),
        in_specs=[
            pl.BlockSpec((n_rows, D_MODEL), lambda l, j: (0, 0)),
            pl.BlockSpec((1, D_MODEL, tn), lambda l, j: (l, 0, j)),
            pl.BlockSpec((1, 1, tn), lambda l, j: (l, 0, j)),
        ],
        out_specs=pl.BlockSpec((1, n_rows, tn), lambda l, j: (l, 0, j)),
        out_shape=jax.ShapeDtypeStruct((DEPTH, n_rows, n_out), F32),
        compiler_params=_params(2),
        name="adaln_modulation",
    )(cvec, w_ada, b_ada.reshape(DEPTH, 1, n_out))


def _rope(p, cos, sin_a, sin_b):
    outs = []
    for k in range(p.shape[1] // LANES):
        xs = p[:, k * LANES:(k + 1) * LANES]
        from_hi = pltpu.roll(xs, LANES - ROPE_AXIS_DIM // 2, axis=1)
        from_lo = pltpu.roll(xs, ROPE_AXIS_DIM // 2, axis=1)
        outs.append(xs * cos + from_hi * sin_a + from_lo * sin_b)
    return jnp.concatenate(outs, axis=1)


def _inproj_kernel(x_ref, mod_ref, g_ref, w_ref, bg_ref, cos_ref, sa_ref, sb_ref,
                   fa_ref, qb_ref, kb_ref, vbt_ref, qc_ref, kc_ref, vct_ref, gate_ref):
    a = _norm_modulate(x_ref[0], g_ref[...], mod_ref[0:1, :], mod_ref[1:2, :]).astype(BF16)
    cos, sin_a, sin_b = cos_ref[...], sa_ref[...], sb_ref[...]

    def proj(seg):
        return _dot(a, w_ref[:, seg * BRANCH_W:(seg + 1) * BRANCH_W])

    fa_ref[0] = proj(0).astype(BF16)
    qb_ref[0] = (_rope(proj(1), cos, sin_a, sin_b) * (DA_HEAD_DIM ** -0.5 * LOG2_E)).astype(BF16)
    kb_ref[0] = _rope(proj(2), cos, sin_a, sin_b).astype(BF16)
    vbt_ref[0] = proj(3).T.astype(BF16)
    qc_ref[0] = (proj(4) * (NA_HEAD_DIM ** -0.5 * LOG2_E)).astype(BF16)
    kc_ref[0] = proj(5).astype(BF16)
    vct_ref[0] = proj(6).T.astype(BF16)
    for j in range(N_BRANCH * D_MODEL // BRANCH_W):
        z = proj(7 + j) + bg_ref[:, j * BRANCH_W:(j + 1) * BRANCH_W]
        gate_ref[0, :, j * BRANCH_W:(j + 1) * BRANCH_W] = (1.0 / (1.0 + jnp.exp(-z))).astype(BF16)


def _in_projection(h, mod, g_mix, w_in, b_gate, rope_tabs, n_lat):
    bsz, n_tok, _ = h.shape
    n_tiles = n_tok // TOK_TILE
    lat_tiles = n_lat // TOK_TILE
    proj_w = w_in.shape[1]
    tok = lambda w: pl.BlockSpec((1, TOK_TILE, w), lambda b, t: (b, t, 0))
    tab = pl.BlockSpec((TOK_TILE, LANES), lambda b, t: (t, 0))
    branch = jax.ShapeDtypeStruct((bsz, n_tok, BRANCH_W), BF16)
    tok_t = pl.BlockSpec((1, BRANCH_W, TOK_TILE), lambda b, t: (b, 0, t))
    branch_t = jax.ShapeDtypeStruct((bsz, BRANCH_W, n_tok), BF16)
    return pl.pallas_call(
        _inproj_kernel,
        grid=(bsz, n_tiles),
        in_specs=[
            tok(D_MODEL),
            pl.BlockSpec((None, None, N_MOD, D_MODEL), lambda b, t: (b, t // lat_tiles, 0, 0)),
            _resident((1, D_MODEL)),
            _resident((D_MODEL, proj_w)),
            _resident((1, N_BRANCH * D_MODEL)),
            tab, tab, tab,
        ],
        out_specs=[tok(BRANCH_W)] * 3 + [tok_t] + [tok(BRANCH_W)] * 2 + [tok_t]
                  + [tok(N_BRANCH * D_MODEL)],
        out_shape=[branch] * 3 + [branch_t] + [branch] * 2 + [branch_t]
                  + [jax.ShapeDtypeStruct((bsz, n_tok, N_BRANCH * D_MODEL), BF16)],
        compiler_params=_params(2),
        name="in_projection",
    )(h, mod, g_mix.reshape(1, D_MODEL), w_in, b_gate.reshape(1, -1), *rope_tabs)


def _dft_kernel(u_ref, cl_ref, sl_ref, cc_ref, sc_ref, cg_ref, sg_ref, o_ref, *, n_lat, n_ctx):
    t = pl.program_id(1)
    lat_tiles = n_lat // TOK_TILE

    def finish(p, q):
        p = p.astype(BF16)
        q = q.astype(BF16)
        for g in range(FN_GROUPS):
            sl = slice(g * FN_GROUP_DIM, (g + 1) * FN_GROUP_DIM)
            f = _dot(p[:, sl], cg_ref[...]) - _dot(q[:, sl], sg_ref[...])
            o_ref[0, :, sl] = f.astype(BF16)

    @pl.when(t < lat_tiles)
    def _():
        u = u_ref[0, 0:n_lat, :]
        finish(_dot(cl_ref[...], u), _dot(sl_ref[...], u))

    @pl.when(t >= lat_tiles)
    def _():
        u = u_ref[0, n_lat:n_lat + n_ctx, :]
        finish(_dot(cc_ref[...], u), _dot(sc_ref[...], u))


def _fourier_mix(fa_in, tabs, n_lat, n_ctx, with_ctx):
    bsz, n_tok, _ = fa_in.shape
    lat_tiles = n_lat // TOK_TILE
    n_tiles = n_tok // TOK_TILE if with_ctx else lat_tiles
    cl, sl, cc, sc, cg, sg = tabs
    lat_tab = pl.BlockSpec((TOK_TILE, n_lat), lambda b, t: (jnp.minimum(t, lat_tiles - 1), 0))
    return pl.pallas_call(
        functools.partial(_dft_kernel, n_lat=n_lat, n_ctx=n_ctx),
        grid=(bsz, n_tiles),
        in_specs=[
            pl.BlockSpec((1, n_tok, FN_WIDTH), lambda b, t: (b, 0, 0)),
            lat_tab, lat_tab,
            _resident((n_ctx, n_ctx)), _resident((n_ctx, n_ctx)),
            _resident((FN_GROUP_DIM, FN_GROUP_DIM)), _resident((FN_GROUP_DIM, FN_GROUP_DIM)),
        ],
        out_specs=pl.BlockSpec((1, TOK_TILE, FN_WIDTH), lambda b, t: (b, t, 0)),
        out_shape=jax.ShapeDtypeStruct((bsz, n_tok, FN_WIDTH), BF16),
        compiler_params=_params(2),
        name="fourier_mix",
    )(fa_in, cl, sl, cc, sc, cg, sg)


def _dft_tables(n, scale):
    k = jnp.arange(n, dtype=jnp.int32)
    ang = ((k[:, None] * k[None, :]) % n).astype(F32) * (2.0 * math.pi / n)
    return (jnp.cos(ang) * scale).astype(BF16), (jnp.sin(ang) * scale).astype(BF16)


def _stack_sub_heads(q, first_half):
    zero = jnp.zeros_like(q)
    return jnp.concatenate([jnp.where(first_half, q, zero), jnp.where(first_half, zero, q)], axis=0)


def _diffattn_kernel(q_ref, k_ref, vt_ref, lam_ref, g_ref, o_ref, *, n_lat, n_ctx, lam_init):
    t = pl.program_id(1)
    lat_tiles = n_lat // TOK_TILE
    lv = lam_ref[...]
    lam = (jnp.exp(jnp.sum(lv[0:1] * lv[1:2], axis=-1, keepdims=True))
           - jnp.exp(jnp.sum(lv[2:3] * lv[3:4], axis=-1, keepdims=True)) + lam_init)
    first_half = lax.broadcasted_iota(jnp.int32, (1, LANES), 1) < DA_HEAD_DIM

    def attend(k_lo, k_len):
        ones = jnp.ones((ONES_ROWS, k_len), BF16)

        def scores(h):
            sl = slice(h * DA_V_DIM, (h + 1) * DA_V_DIM)
            return _nt_dot(k_ref[0, k_lo:k_lo + k_len, sl],
                           _stack_sub_heads(q_ref[0, :, sl], first_half))

        s_next = scores(0)
        for h in range(DA_HEADS):
            sl = slice(h * DA_V_DIM, (h + 1) * DA_V_DIM)
            s_t = s_next
            if h + 1 < DA_HEADS:
                s_next = scores(h + 1)
            e = jnp.exp2(s_t - jnp.max(s_t, axis=0, keepdims=True)).astype(BF16)
            o_t = _dot(jnp.concatenate([vt_ref[0, sl, k_lo:k_lo + k_len], ones], axis=0), e)
            o12 = (o_t[0:DA_V_DIM] * (1.0 / o_t[DA_V_DIM:DA_V_DIM + 1])).T
            o = o12[0:TOK_TILE] - lam * o12[TOK_TILE:2 * TOK_TILE]
            ms = jnp.mean(o * o, axis=-1, keepdims=True)
            o = o * lax.rsqrt(ms + SUBLN_EPS) * g_ref[...] * (1.0 - lam_init)
            o_ref[0, :, sl] = o.astype(BF16)

    @pl.when(t < lat_tiles)
    def _():
        attend(0, n_lat + n_ctx)

    @pl.when(t >= lat_tiles)
    def _():
        attend(n_lat, n_ctx)


def _diff_attention(qb, kb, vbt, lam_vec, subln_g, lam_init, n_lat, n_ctx, with_ctx):
    bsz, n_tok, _ = qb.shape
    n_tiles = (n_tok if with_ctx else n_lat) // TOK_TILE
    full = pl.BlockSpec((1, n_tok, BRANCH_W), lambda b, t: (b, 0, 0))
    full_t = pl.BlockSpec((1, BRANCH_W, n_tok), lambda b, t: (b, 0, 0))
    tile = pl.BlockSpec((1, TOK_TILE, BRANCH_W), lambda b, t: (b, t, 0))
    return pl.pallas_call(
        functools.partial(_diffattn_kernel, n_lat=n_lat, n_ctx=n_ctx, lam_init=lam_init),
        grid=(bsz, n_tiles),
        in_specs=[tile, full, full_t, _resident((4, DA_HEAD_DIM)), _resident((1, DA_V_DIM))],
        out_specs=tile,
        out_shape=jax.ShapeDtypeStruct((bsz, n_tok, BRANCH_W), BF16),
        compiler_params=_params(2),
        name="diff_attention",
    )(qb, kb, vbt, lam_vec, subln_g.reshape(1, DA_V_DIM))


def _natten_kernel(q_ref, k_ref, vt_ref, bias_ref, o_ref, *, n_lat, n_ctx):
    s_idx = pl.program_id(1)
    rows = n_lat // GRID_W
    kh = min(NA_ROWS, rows)
    n_slab = NA_SLAB_ROWS * GRID_W
    n_q = NA_STEP_ROWS * GRID_W
    first_half = lax.broadcasted_iota(jnp.int32, (1, LANES), 1) < NA_HEAD_DIM

    def run(window):
        first_row = jnp.clip(NA_STEP_ROWS * s_idx - kh // 2, 0, rows - kh)
        k_lo = pl.multiple_of(first_row * GRID_W, LANES)
        pairs = [slice(p * LANES, (p + 1) * LANES) for p in range(NA_HEADS // 2)]
        qs = [_stack_sub_heads(q_ref[0, :, sl], first_half) for sl in pairs]
        s_ctx = [_nt_dot(k_ref[0, n_lat:n_lat + n_ctx, sl], q) for sl, q in zip(pairs, qs)]
        if window:
            s_win = [_nt_dot(k_ref[0, pl.ds(k_lo, n_slab), sl], q) + bias_ref[0, 0, p]
                     for p, (sl, q) in enumerate(zip(pairs, qs))]
        for p, sl in enumerate(pairs):
            m = jnp.max(s_ctx[p], axis=0, keepdims=True)
            if window:
                m = jnp.maximum(m, jnp.max(s_win[p], axis=0, keepdims=True))
            e_ctx = jnp.exp2(s_ctx[p] - m).astype(BF16)
            ones = jnp.ones((ONES_ROWS, n_ctx), BF16)
            o_t = _dot(jnp.concatenate([vt_ref[0, sl, n_lat:n_lat + n_ctx], ones], axis=0), e_ctx)
            if window:
                e_win = jnp.exp2(s_win[p] - m).astype(BF16)
                ones = jnp.ones((ONES_ROWS, n_slab), BF16)
                o_t = o_t + _dot(
                    jnp.concatenate([vt_ref[0, sl, pl.ds(k_lo, n_slab)], ones], axis=0), e_win)
            o2 = (o_t[0:LANES] * (1.0 / o_t[LANES:LANES + 1])).T
            o_ref[0, :, sl] = jnp.where(first_half, o2[0:n_q], o2[n_q:2 * n_q]).astype(BF16)

    @pl.when(s_idx < rows // NA_STEP_ROWS)
    def _():
        run(True)

    @pl.when(s_idx >= rows // NA_STEP_ROWS)
    def _():
        run(False)


def _natten_step_classes(rows):
    kh = min(NA_ROWS, rows)
    classes, first_steps = [], []
    for s in range(rows // NA_STEP_ROWS):
        slab = min(max(NA_STEP_ROWS * s - kh // 2, 0), rows - kh)
        geom = tuple((r - slab, min(max(r - kh // 2, 0), rows - kh) - slab)
                     for r in range(NA_STEP_ROWS * s, NA_STEP_ROWS * (s + 1)))
        geom = geom + (min(NA_SLAB_ROWS, rows - slab),)
        if not classes or classes[-1] != geom:
            assert geom not in classes
            classes.append(geom)
            first_steps.append(s)
    return classes, first_steps


def _natten_bias_kernel(r_ref, o_ref, *, classes, kh):
    cls = pl.program_id(1)
    kc = lax.broadcasted_iota(jnp.int32, (GRID_W, LANES), 0)
    lane = lax.broadcasted_iota(jnp.int32, (GRID_W, LANES), 1)
    second = lane >= GRID_W
    c = jnp.where(second, lane - GRID_W, lane)
    c0 = jnp.clip(c - NA_COLS // 2, 0, GRID_W - NA_COLS)
    col_ok = jnp.logical_and(kc >= c0, kc < c0 + NA_COLS)
    neg = jnp.full((GRID_W, LANES), NEG_INF, F32)

    def fill(geom):
        for h in range(NA_HEADS):
            for kr in range(NA_SLAB_ROWS):
                halves = []
                for j, (q_off, w_off) in enumerate(geom[:-1]):
                    if w_off <= kr < w_off + kh and kr < geom[-1]:
                        dr = kr - q_off + NA_ROWS - 1
                        row = jnp.broadcast_to(r_ref[0, h, dr:dr + 1, :], (GRID_W, LANES))
                        halves.append(pltpu.roll(row, (j * GRID_W - (NA_COLS - 1)) % LANES, axis=1,
                                                 stride=1, stride_axis=0))
                    else:
                        halves.append(neg)
                val = jnp.where(col_ok, jnp.where(second, halves[1], halves[0]), neg)
                o_ref[0, 0, h // 2, kr * GRID_W:(kr + 1) * GRID_W,
                      (h % 2) * LANES:(h % 2 + 1) * LANES] = val

    for ci, geom in enumerate(classes):
        pl.when(cls == ci)(functools.partial(fill, geom))


def _natten_bias(rpb, rows):
    kh = min(NA_ROWS, rows)
    classes, _ = _natten_step_classes(rows)
    n_dr, n_dc = 2 * NA_ROWS - 1, 2 * NA_COLS - 1
    assert NA_STEP_ROWS == 2 and n_dc <= GRID_W
    r = jnp.pad(rpb[..., ::-1].astype(F32) * LOG2_E,
                ((0, 0), (0, 0), (0, 2 * SUBLANES - n_dr), (0, LANES - n_dc)))
    return pl.pallas_call(
        functools.partial(_natten_bias_kernel, classes=classes, kh=kh),
        grid=(DEPTH, len(classes)),
        in_specs=[pl.BlockSpec((1, NA_HEADS, 2 * SUBLANES, LANES), lambda l, k: (l, 0, 0, 0))],
        out_specs=pl.BlockSpec((1, 1, NA_HEADS // 2, NA_SLAB_ROWS * GRID_W, 2 * LANES),
                               lambda l, k: (l, k, 0, 0, 0)),
        out_shape=jax.ShapeDtypeStruct(
            (DEPTH, len(classes), NA_HEADS // 2, NA_SLAB_ROWS * GRID_W, 2 * LANES), F32),
        compiler_params=_params(2),
        name="natten_bias_tables",
    )(r)


def _neighbourhood_attention(qc, kc, vct, bias, layer, n_lat, n_ctx, with_ctx):
    bsz, n_tok, _ = qc.shape
    rows = n_lat // GRID_W
    kh = min(NA_ROWS, rows)
    n_q = NA_STEP_ROWS * GRID_W
    assert (kh // 2) % NA_STEP_ROWS == 0 and (rows - kh) % NA_STEP_ROWS == 0 and n_q == LANES
    assert (rows - kh + NA_SLAB_ROWS) * GRID_W <= n_tok and kh + NA_STEP_ROWS <= NA_SLAB_ROWS
    _, first_steps = _natten_step_classes(rows)
    n_steps = (n_tok if with_ctx else n_lat) // n_q

    def bias_class(s):
        return sum((s >= f).astype(jnp.int32) for f in first_steps[1:])

    full = pl.BlockSpec((1, n_tok, BRANCH_W), lambda b, s: (b, 0, 0))
    full_t = pl.BlockSpec((1, BRANCH_W, n_tok), lambda b, s: (b, 0, 0))
    tile = pl.BlockSpec((1, n_q, BRANCH_W), lambda b, s: (b, s, 0))
    return pl.pallas_call(
        functools.partial(_natten_kernel, n_lat=n_lat, n_ctx=n_ctx),
        grid=(bsz, n_steps),
        in_specs=[
            tile, full, full_t,
            pl.BlockSpec((1, 1) + bias.shape[2:], lambda b, s: (layer, bias_class(s), 0, 0, 0)),
        ],
        out_specs=tile,
        out_shape=jax.ShapeDtypeStruct((bsz, n_tok, BRANCH_W), BF16),
        compiler_params=_params(2),
        name="neighbourhood_attention",
    )(qc, kc, vct, bias)


def _merge_kernel(x_ref, mod_ref, fa_ref, db_ref, nc_ref, gate_ref, wa_ref, wb_ref, wc_ref,
                  wo_ref, o_ref):
    def gate(j):
        return gate_ref[0, :, j * D_MODEL:(j + 1) * D_MODEL].astype(F32)

    y = (gate(0) * _dot(fa_ref[0], wa_ref[...]) + gate(1) * _dot(db_ref[0], wb_ref[...])
         + gate(2) * _dot(nc_ref[0], wc_ref[...]))
    y = _dot(y.astype(BF16), wo_ref[...])
    o_ref[0] = x_ref[0] + mod_ref[2:3, :] * y


def _merge(h, mod, fa, db, nc, gates, w_a, w_b, w_c, w_out, n_lat, with_ctx):
    bsz, n_tok, _ = h.shape
    lat_tiles = n_lat // TOK_TILE
    n_tiles = (n_tok if with_ctx else n_lat) // TOK_TILE
    tok = lambda w: pl.BlockSpec((1, TOK_TILE, w), lambda b, t: (b, t, 0))
    return pl.pallas_call(
        _merge_kernel,
        grid=(bsz, n_tiles),
        in_specs=[
            tok(D_MODEL),
            pl.BlockSpec((None, None, N_MOD, D_MODEL), lambda b, t: (b, t // lat_tiles, 0, 0)),
            tok(BRANCH_W), tok(BRANCH_W), tok(BRANCH_W), tok(N_BRANCH * D_MODEL),
            _resident((BRANCH_W, D_MODEL)), _resident((BRANCH_W, D_MODEL)),
            _resident((BRANCH_W, D_MODEL)), _resident((D_MODEL, D_MODEL)),
        ],
        out_specs=tok(D_MODEL),
        out_shape=jax.ShapeDtypeStruct((bsz, n_tiles * TOK_TILE, D_MODEL), F32),
        compiler_params=_params(2),
        name="merge_out_projection",
    )(h, mod, fa, db, nc, gates, w_a, w_b, w_c, w_out)


def _convffn_kernel(x_ref, prev_ref, next_ref, mod_ref, g_ref, wu_ref, cw_ref, cb_ref, wd_ref,
                    gf_ref, o_ref, lhs_ref, *, lat_tiles, n_tiles, final_norm):
    t = pl.program_id(1)
    g = g_ref[...]
    shift, scale = mod_ref[3:4, :], mod_ref[4:5, :]
    x = x_ref[0]
    has_prev = jnp.logical_and(t != 0, t != lat_tiles)
    has_next = jnp.logical_and(t != lat_tiles - 1, t != n_tiles - 1)
    prev = _norm_modulate(prev_ref[0], g, shift, scale) * has_prev.astype(F32)
    nxt = _norm_modulate(next_ref[0], g, shift, scale) * has_next.astype(F32)
    lhs_ref[0:SUBLANES, :] = prev
    lhs_ref[SUBLANES:SUBLANES + TOK_TILE, :] = _norm_modulate(x, g, shift, scale)
    lhs_ref[SUBLANES + TOK_TILE:, :] = nxt
    lhs = lhs_ref[...].astype(BF16)
    n_rows = TOK_TILE + 2 * SUBLANES

    def conv(u, col):
        w = cw_ref[:, col:col + FF_CHUNK]
        before = pltpu.roll(u, 1, axis=0)
        after = pltpu.roll(u, n_rows - 1, axis=0)
        v = before * w[0:1] + u * w[1:2] + after * w[2:3] + cb_ref[:, col:col + FF_CHUNK]
        return v[SUBLANES:SUBLANES + TOK_TILE]

    def up(j):
        col_a, col_b = j * FF_CHUNK, D_FF + j * FF_CHUNK
        return (_dot(lhs, wu_ref[:, col_a:col_a + FF_CHUNK]),
                _dot(lhs, wu_ref[:, col_b:col_b + FF_CHUNK]))

    n_chunks = D_FF // FF_CHUNK
    acc = jnp.zeros((TOK_TILE, D_MODEL), F32)
    u_next = up(0)
    for j in range(n_chunks):
        col_a, col_b = j * FF_CHUNK, D_FF + j * FF_CHUNK
        u_a, u_b = u_next
        if j + 1 < n_chunks:
            u_next = up(j + 1)
        a = conv(u_a, col_a)
        b = conv(u_b, col_b)
        act = (a / (1.0 + jnp.exp(-a))) * b
        acc = acc + _dot(act.astype(BF16), wd_ref[col_a:col_a + FF_CHUNK, :])
    y = x + mod_ref[5:6, :] * acc
    if final_norm:
        ms = jnp.mean(y * y, axis=-1, keepdims=True)
        y = y * lax.rsqrt(ms + NORM_EPS) * gf_ref[...]
    o_ref[0] = y


def _conv_ffn(h, mod, g_ffn, w_up, conv_w, conv_b, w_down, g_final, n_lat, with_ctx, final_norm):
    bsz, n_rows_in, _ = h.shape
    lat_tiles = n_lat // TOK_TILE
    n_tiles = n_rows_in // TOK_TILE if with_ctx else lat_tiles
    blocks_per_tile = TOK_TILE // SUBLANES
    last_block = n_rows_in // SUBLANES - 1
    return pl.pallas_call(
        functools.partial(_convffn_kernel, lat_tiles=lat_tiles, n_tiles=n_tiles,
                          final_norm=final_norm),
        grid=(bsz, n_tiles),
        in_specs=[
            pl.BlockSpec((1, TOK_TILE, D_MODEL), lambda b, t: (b, t, 0)),
            pl.BlockSpec((1, SUBLANES, D_MODEL),
                         lambda b, t: (b, jnp.maximum(t * blocks_per_tile - 1, 0), 0)),
            pl.BlockSpec((1, SUBLANES, D_MODEL),
                         lambda b, t: (b, jnp.minimum((t + 1) * blocks_per_tile, last_block), 0)),
            pl.BlockSpec((None, None, N_MOD, D_MODEL), lambda b, t: (b, t // lat_tiles, 0, 0)),
            _resident((1, D_MODEL)),
            _resident((D_MODEL, 2 * D_FF)),
            _resident((3, 2 * D_FF)),
            _resident((1, 2 * D_FF)),
            _resident((D_FF, D_MODEL)),
            _resident((1, D_MODEL)),
        ],
        out_specs=pl.BlockSpec((1, TOK_TILE, D_MODEL), lambda b, t: (b, t, 0)),
        out_shape=jax.ShapeDtypeStruct((bsz, n_tiles * TOK_TILE, D_MODEL), F32),
        scratch_shapes=[pltpu.VMEM((TOK_TILE + 2 * SUBLANES, D_MODEL), F32)],
        compiler_params=_params(2),
        name="conv_ffn",
    )(h, h, h, mod, g_ffn.reshape(1, D_MODEL), w_up, conv_w, conv_b.reshape(1, -1), w_down,
      g_final.reshape(1, D_MODEL))


def _rope_tables(n_lat, n_ctx):
    t = jnp.arange(n_lat, dtype=jnp.int32)
    pos = jnp.stack([(t // GRID_W).astype(F32), (t % GRID_W).astype(F32)], axis=1)
    n_freq = ROPE_AXIS_DIM // 2
    inv = ROPE_THETA ** (-jnp.arange(n_freq, dtype=F32) / n_freq)
    ang = pos[:, :, None] * inv
    lane = jnp.arange(LANES, dtype=jnp.int32)
    axis = (lane % DA_HEAD_DIM) // ROPE_AXIS_DIM
    freq = lane % n_freq
    second_half = (lane % ROPE_AXIS_DIM) >= n_freq
    ang_l = ang[:, axis, freq]
    cos, sin = jnp.cos(ang_l), jnp.sin(ang_l)
    sin_a = jnp.where(second_half, 0.0, -sin)
    sin_b = jnp.where(second_half, sin, 0.0)
    pad = lambda a, v: jnp.concatenate([a, jnp.full((n_ctx, LANES), v, F32)], axis=0)
    return pad(cos, 1.0), pad(sin_a, 0.0), pad(sin_b, 0.0)


def kernel(x, c, ctx, c_ctx, w_ada, b_ada, g_mix, g_ffn, w_in, b_gate, w_a, lam, subln_g, w_b,
           rpb, w_c, w_out, w_up, conv_w, conv_b, w_down, g_final):
    bsz, n_lat, _ = x.shape
    n_ctx = ctx.shape[1]
    assert n_lat % TOK_TILE == 0 and n_ctx == TOK_TILE and n_lat % GRID_W == 0

    rope_tabs = _rope_tables(n_lat, n_ctx)
    cl, sl = _dft_tables(n_lat, (n_lat * FN_GROUP_DIM) ** -0.5)
    cc, sc = _dft_tables(n_ctx, (n_ctx * FN_GROUP_DIM) ** -0.5)
    cg, sg = _dft_tables(FN_GROUP_DIM, 1.0)

    n_mod_rows = 2 * SUBLANES * (-(-(bsz + 1) // (2 * SUBLANES)))
    cvec = jnp.zeros((n_mod_rows, D_MODEL), F32).at[:bsz].set(c).at[bsz].set(c_ctx)
    mods = _modulation(cvec, w_ada, b_ada).reshape(DEPTH, n_mod_rows, N_MOD, D_MODEL)

    na_bias = _natten_bias(rpb, n_lat // GRID_W)

    h = jnp.concatenate([x, ctx], axis=1)
    for l in range(DEPTH):
        with_ctx = l != DEPTH - 1
        lam_init = 0.8 - 0.6 * math.exp(-0.3 * l)
        mod = jnp.stack([mods[l, :bsz],
                         jnp.broadcast_to(mods[l, bsz], (bsz, N_MOD, D_MODEL))], axis=1)
        fa_in, qb, kb, vbt, qc, kc, vct, gates = _in_projection(
            h, mod, g_mix[l], w_in[l].astype(BF16), b_gate[l], rope_tabs, n_lat)
        fa = _fourier_mix(fa_in, (cl, sl, cc, sc, cg, sg), n_lat, n_ctx, with_ctx)
        db = _diff_attention(qb, kb, vbt, lam[l], subln_g[l], lam_init, n_lat, n_ctx, with_ctx)
        nc = _neighbourhood_attention(qc, kc, vct, na_bias, l, n_lat, n_ctx, with_ctx)
        h = _merge(h, mod, fa, db, nc, gates, w_a[l].astype(BF16), w_b[l].astype(BF16),
                   w_c[l].astype(BF16), w_out[l].astype(BF16), n_lat, with_ctx)
        h = _conv_ffn(h, mod, g_ffn[l], w_up[l].astype(BF16), conv_w[l], conv_b[l],
                      w_down[l].astype(BF16), g_final, n_lat, with_ctx, final_norm=not with_ctx)
    return h
```

```python
import functools
import math

import jax
import jax.numpy as jnp
from jax import lax
from jax.experimental import pallas as pl
from jax.experimental.pallas import tpu as pltpu

D_MODEL = 1024
DEPTH = 4
GRID_W = 64
FN_GROUPS = 4
FN_GROUP_DIM = 128
FN_WIDTH = FN_GROUPS * FN_GROUP_DIM
DA_HEADS = 4
DA_HEAD_DIM = 64
DA_V_DIM = 2 * DA_HEAD_DIM
NA_HEADS = 8
NA_HEAD_DIM = 64
NA_ROWS = 8
NA_COLS = 16
BRANCH_W = 512
N_BRANCH = 3
ROPE_THETA = 10000.0
ROPE_AXIS_DIM = DA_HEAD_DIM // 2
D_FF = 2816
N_MOD = 6
NORM_EPS = 1e-6
SUBLN_EPS = 1e-5
NEG_INF = -1e30

LANES = 128
SUBLANES = 8
TOK_TILE = 256
FF_CHUNK = 256
NA_STEP_ROWS = 2
NA_SLAB_ROWS = 10
VMEM_LIMIT = 56 * 1024 * 1024
ONES_ROWS = 16
SOFTMAX_CHUNK = 64
PV_CHUNK = 256
LOG2_E = math.log2(math.e)

BF16 = jnp.bfloat16
F32 = jnp.float32


def _params(n_grid_dims):
    return pltpu.CompilerParams(dimension_semantics=("arbitrary",) * n_grid_dims,
                                vmem_limit_bytes=VMEM_LIMIT)


def _resident(shape):
    return pl.BlockSpec(shape, lambda *_: (0,) * len(shape), pipeline_mode=pl.Buffered(1))


def _nt_dot(a, b):
    return lax.dot_general(a, b, (((1,), (1,)), ((), ())), preferred_element_type=F32)


def _dot(a, b):
    return jnp.dot(a, b, preferred_element_type=F32)


def _norm_modulate(x, g, shift, scale):
    ms = jnp.mean(x * x, axis=-1, keepdims=True)
    return (x * lax.rsqrt(ms + NORM_EPS) * g) * (1.0 + scale) + shift


def _split_bf16(v):
    hi = v.astype(BF16)
    lo = (v - hi.astype(F32)).astype(BF16)
    return hi, lo


def _mod_kernel(c_ref, w_ref, b_ref, o_ref):
    c = c_ref[...]
    s = c / (1.0 + jnp.exp(-c))
    s_hi, s_lo = _split_bf16(s)
    w_hi, w_lo = _split_bf16(w_ref[0])
    acc = _dot(s_hi, w_hi) + (_dot(s_hi, w_lo) + _dot(s_lo, w_hi))
    o_ref[0] = acc + b_ref[0]


def _modulation(cvec, w_ada, b_ada):
    n_rows = cvec.shape[0]
    n_out = N_MOD * D_MODEL
    tn = 1536
    return pl.pallas_call(
        _mod_kernel,
        grid=(DEPTH, n_out // tn),
        in_specs=[
            pl.BlockSpec((n_rows, D_MODEL), lambda l, j: (0, 0)),
            pl.BlockSpec((1, D_MODEL, tn), lambda l, j: (l, 0, j)),
            pl.BlockSpec((1, 1, tn), lambda l, j: (l, 0, j)),
        ],
        out_specs=pl.BlockSpec((1, n_rows, tn), lambda l, j: (l, 0, j)),
        out_shape=jax.ShapeDtypeStruct((DEPTH, n_rows, n_out), F32),
        compiler_params=_params(2),
        name="adaln_modulation",
    )(cvec, w_ada, b_ada.reshape(DEPTH, 1, n_out))


def _rope(p, cos, sin_a, sin_b):
    outs = []
    for k in range(p.shape[1] // LANES):
        xs = p[:, k * LANES:(k + 1) * LANES]
        from_hi = pltpu.roll(xs, LANES - ROPE_AXIS_DIM // 2, axis=1)
        from_lo = pltpu.roll(xs, ROPE_AXIS_DIM // 2, axis=1)
        outs.append(xs * cos + from_hi * sin_a + from_lo * sin_b)
    return jnp.concatenate(outs, axis=1)


def _inproj_kernel(x_ref, mod_ref, g_ref, w_ref, bg_ref, cos_ref, sa_ref, sb_ref,
                   fa_ref, qb_ref, kb_ref, vbt_ref, qc_ref, kc_ref, vct_ref, gate_ref):
    a = _norm_modulate(x_ref[0], g_ref[...], mod_ref[0:1, :], mod_ref[1:2, :]).astype(BF16)
    cos, sin_a, sin_b = cos_ref[...], sa_ref[...], sb_ref[...]

    def proj(seg):
        return _dot(a, w_ref[:, seg * BRANCH_W:(seg + 1) * BRANCH_W])

    fa_ref[0] = proj(0).astype(BF16)
    qb_ref[0] = (_rope(proj(1), cos, sin_a, sin_b) * (DA_HEAD_DIM ** -0.5 * LOG2_E)).astype(BF16)
    kb_ref[0] = _rope(proj(2), cos, sin_a, sin_b).astype(BF16)
    vbt_ref[0] = proj(3).T.astype(BF16)
    qc_ref[0] = (proj(4) * (NA_HEAD_DIM ** -0.5 * LOG2_E)).astype(BF16)
    kc_ref[0] = proj(5).astype(BF16)
    vct_ref[0] = proj(6).T.astype(BF16)
    for j in range(N_BRANCH * D_MODEL // BRANCH_W):
        z = proj(7 + j) + bg_ref[:, j * BRANCH_W:(j + 1) * BRANCH_W]
        gate_ref[0, :, j * BRANCH_W:(j + 1) * BRANCH_W] = (1.0 / (1.0 + jnp.exp(-z))).astype(BF16)


def _in_projection(h, mod, g_mix, w_in, b_gate, rope_tabs, n_lat):
    bsz, n_tok, _ = h.shape
    n_tiles = n_tok // TOK_TILE
    lat_tiles = n_lat // TOK_TILE
    proj_w = w_in.shape[1]
    tok = lambda w: pl.BlockSpec((1, TOK_TILE, w), lambda b, t: (b, t, 0))
    tab = pl.BlockSpec((TOK_TILE, LANES), lambda b, t: (t, 0))
    branch = jax.ShapeDtypeStruct((bsz, n_tok, BRANCH_W), BF16)
    tok_t = pl.BlockSpec((1, BRANCH_W, TOK_TILE), lambda b, t: (b, 0, t))
    branch_t = jax.ShapeDtypeStruct((bsz, BRANCH_W, n_tok), BF16)
    return pl.pallas_call(
        _inproj_kernel,
        grid=(bsz, n_tiles),
        in_specs=[
            tok(D_MODEL),
            pl.BlockSpec((None, None, N_MOD, D_MODEL), lambda b, t: (b, t // lat_tiles, 0, 0)),
            _resident((1, D_MODEL)),
            _resident((D_MODEL, proj_w)),
            _resident((1, N_BRANCH * D_MODEL)),
            tab, tab, tab,
        ],
        out_specs=[tok(BRANCH_W)] * 3 + [tok_t] + [tok(BRANCH_W)] * 2 + [tok_t]
                  + [tok(N_BRANCH * D_MODEL)],
        out_shape=[branch] * 3 + [branch_t] + [branch] * 2 + [branch_t]
                  + [jax.ShapeDtypeStruct((bsz, n_tok, N_BRANCH * D_MODEL), BF16)],
        compiler_params=_params(2),
        name="in_projection",
    )(h, mod, g_mix.reshape(1, D_MODEL), w_in, b_gate.reshape(1, -1), *rope_tabs)


def _dft_kernel(u_ref, cl_ref, sl_ref, cc_ref, sc_ref, cg_ref, sg_ref, o_ref, *, n_lat, n_ctx):
    t = pl.program_id(1)
    lat_tiles = n_lat // TOK_TILE

    def finish(p, q):
        p = p.astype(BF16)
        q = q.astype(BF16)
        for g in range(FN_GROUPS):
            sl = slice(g * FN_GROUP_DIM, (g + 1) * FN_GROUP_DIM)
            f = _dot(p[:, sl], cg_ref[...]) - _dot(q[:, sl], sg_ref[...])
            o_ref[0, :, sl] = f.astype(BF16)

    @pl.when(t < lat_tiles)
    def _():
        u = u_ref[0, 0:n_lat, :]
        finish(_dot(cl_ref[...], u), _dot(sl_ref[...], u))

    @pl.when(t >= lat_tiles)
    def _():
        u = u_ref[0, n_lat:n_lat + n_ctx, :]
        finish(_dot(cc_ref[...], u), _dot(sc_ref[...], u))


def _fourier_mix(fa_in, tabs, n_lat, n_ctx, with_ctx):
    bsz, n_tok, _ = fa_in.shape
    lat_tiles = n_lat // TOK_TILE
    n_tiles = n_tok // TOK_TILE if with_ctx else lat_tiles
    cl, sl, cc, sc, cg, sg = tabs
    lat_tab = pl.BlockSpec((TOK_TILE, n_lat), lambda b, t: (jnp.minimum(t, lat_tiles - 1), 0))
    return pl.pallas_call(
        functools.partial(_dft_kernel, n_lat=n_lat, n_ctx=n_ctx),
        grid=(bsz, n_tiles),
        in_specs=[
            pl.BlockSpec((1, n_tok, FN_WIDTH), lambda b, t: (b, 0, 0)),
            lat_tab, lat_tab,
            _resident((n_ctx, n_ctx)), _resident((n_ctx, n_ctx)),
            _resident((FN_GROUP_DIM, FN_GROUP_DIM)), _resident((FN_GROUP_DIM, FN_GROUP_DIM)),
        ],
        out_specs=pl.BlockSpec((1, TOK_TILE, FN_WIDTH), lambda b, t: (b, t, 0)),
        out_shape=jax.ShapeDtypeStruct((bsz, n_tok, FN_WIDTH), BF16),
        compiler_params=_params(2),
        name="fourier_mix",
    )(fa_in, cl, sl, cc, sc, cg, sg)


def _dft_tables(n, scale):
    k = jnp.arange(n, dtype=jnp.int32)
    ang = ((k[:, None] * k[None, :]) % n).astype(F32) * (2.0 * math.pi / n)
    return (jnp.cos(ang) * scale).astype(BF16), (jnp.sin(ang) * scale).astype(BF16)


def _stack_sub_heads(q, first_half):
    zero = jnp.zeros_like(q)
    return jnp.concatenate([jnp.where(first_half, q, zero), jnp.where(first_half, zero, q)], axis=0)


def _column_max(s_ref, n_rows):
    m = s_ref[0:SOFTMAX_CHUNK, :].reshape(SOFTMAX_CHUNK // SUBLANES, SUBLANES, -1).max(axis=0)
    for r in range(SOFTMAX_CHUNK, n_rows, SOFTMAX_CHUNK):
        c = s_ref[r:r + SOFTMAX_CHUNK, :].reshape(SOFTMAX_CHUNK // SUBLANES, SUBLANES, -1)
        m = jnp.maximum(m, c.max(axis=0))
    return jnp.max(m, axis=0, keepdims=True)


def _softmax_times_values(s_ref, n_rows, values_t):
    m = _column_max(s_ref, n_rows)
    acc = None
    for r in range(0, n_rows, PV_CHUNK):
        n = min(PV_CHUNK, n_rows - r)
        e = jnp.exp2(s_ref[r:r + n, :] - m).astype(BF16)
        v = values_t(r, n)
        part = _dot(jnp.concatenate([v, jnp.ones((ONES_ROWS, n), BF16)], axis=0), e)
        acc = part if acc is None else acc + part
    n_ch = acc.shape[0] - ONES_ROWS
    return acc[0:n_ch] * (1.0 / acc[n_ch:n_ch + 1])


def _diffattn_kernel(q_ref, k_ref, vt_ref, lam_ref, g_ref, o_ref, s0_ref, s1_ref,
                     *, n_lat, n_ctx, lam_init):
    t = pl.program_id(1)
    lat_tiles = n_lat // TOK_TILE
    lv = lam_ref[...]
    lam = (jnp.exp(jnp.sum(lv[0:1] * lv[1:2], axis=-1, keepdims=True))
           - jnp.exp(jnp.sum(lv[2:3] * lv[3:4], axis=-1, keepdims=True)) + lam_init)
    first_half = lax.broadcasted_iota(jnp.int32, (1, LANES), 1) < DA_HEAD_DIM
    s_refs = (s0_ref, s1_ref)

    def attend(k_lo, k_len):
        def scores(h):
            sl = slice(h * DA_V_DIM, (h + 1) * DA_V_DIM)
            s_refs[h % 2][0:k_len, :] = _nt_dot(k_ref[0, k_lo:k_lo + k_len, sl],
                                                _stack_sub_heads(q_ref[0, :, sl], first_half))

        scores(0)
        for h in range(DA_HEADS):
            sl = slice(h * DA_V_DIM, (h + 1) * DA_V_DIM)
            if h + 1 < DA_HEADS:
                scores(h + 1)
            o12 = _softmax_times_values(
                s_refs[h % 2], k_len, lambda r, n: vt_ref[0, sl, k_lo + r:k_lo + r + n]).T
            o = o12[0:TOK_TILE] - lam * o12[TOK_TILE:2 * TOK_TILE]
            ms = jnp.mean(o * o, axis=-1, keepdims=True)
            o = o * lax.rsqrt(ms + SUBLN_EPS) * g_ref[...] * (1.0 - lam_init)
            o_ref[0, :, sl] = o.astype(BF16)

    @pl.when(t < lat_tiles)
    def _():
        attend(0, n_lat + n_ctx)

    @pl.when(t >= lat_tiles)
    def _():
        attend(n_lat, n_ctx)


def _diff_attention(qb, kb, vbt, lam_vec, subln_g, lam_init, n_lat, n_ctx, with_ctx):
    bsz, n_tok, _ = qb.shape
    n_tiles = (n_tok if with_ctx else n_lat) // TOK_TILE
    full = pl.BlockSpec((1, n_tok, BRANCH_W), lambda b, t: (b, 0, 0))
    full_t = pl.BlockSpec((1, BRANCH_W, n_tok), lambda b, t: (b, 0, 0))
    tile = pl.BlockSpec((1, TOK_TILE, BRANCH_W), lambda b, t: (b, t, 0))
    return pl.pallas_call(
        functools.partial(_diffattn_kernel, n_lat=n_lat, n_ctx=n_ctx, lam_init=lam_init),
        grid=(bsz, n_tiles),
        in_specs=[tile, full, full_t, _resident((4, DA_HEAD_DIM)), _resident((1, DA_V_DIM))],
        out_specs=tile,
        out_shape=jax.ShapeDtypeStruct((bsz, n_tok, BRANCH_W), BF16),
        scratch_shapes=[pltpu.VMEM((n_tok, 2 * TOK_TILE), F32)] * 2,
        compiler_params=_params(2),
        name="diff_attention",
    )(qb, kb, vbt, lam_vec, subln_g.reshape(1, DA_V_DIM))


def _natten_kernel(q_ref, k_ref, vt_ref, bias_ref, o_ref, *s_refs, n_lat, n_ctx):
    s_idx = pl.program_id(1)
    rows = n_lat // GRID_W
    kh = min(NA_ROWS, rows)
    n_slab = NA_SLAB_ROWS * GRID_W
    n_q = NA_STEP_ROWS * GRID_W
    first_half = lax.broadcasted_iota(jnp.int32, (1, LANES), 1) < NA_HEAD_DIM

    def run(window):
        first_row = jnp.clip(NA_STEP_ROWS * s_idx - kh // 2, 0, rows - kh)
        k_lo = pl.multiple_of(first_row * GRID_W, LANES)
        n_keys = n_ctx + (n_slab if window else 0)
        pairs = [slice(p * LANES, (p + 1) * LANES) for p in range(NA_HEADS // 2)]
        for p, sl in enumerate(pairs):
            qs = _stack_sub_heads(q_ref[0, :, sl], first_half)
            s_refs[p][0:n_ctx, :] = _nt_dot(k_ref[0, n_lat:n_lat + n_ctx, sl], qs)
            if window:
                s_refs[p][n_ctx:n_keys, :] = (_nt_dot(k_ref[0, pl.ds(k_lo, n_slab), sl], qs)
                                              + bias_ref[0, 0, p])
        for p, sl in enumerate(pairs):
            def values_t(r, n):
                if r < n_ctx:
                    return vt_ref[0, sl, n_lat + r:n_lat + r + n]
                return vt_ref[0, sl, pl.ds(pl.multiple_of(k_lo + (r - n_ctx), LANES), n)]

            o2 = _softmax_times_values(s_refs[p], n_keys, values_t).T
            o_ref[0, :, sl] = jnp.where(first_half, o2[0:n_q], o2[n_q:2 * n_q]).astype(BF16)

    @pl.when(s_idx < rows // NA_STEP_ROWS)
    def _():
        run(True)

    @pl.when(s_idx >= rows // NA_STEP_ROWS)
    def _():
        run(False)


def _natten_step_classes(rows):
    kh = min(NA_ROWS, rows)
    classes, first_steps = [], []
    for s in range(rows // NA_STEP_ROWS):
        slab = min(max(NA_STEP_ROWS * s - kh // 2, 0), rows - kh)
        geom = tuple((r - slab, min(max(r - kh // 2, 0), rows - kh) - slab)
                     for r in range(NA_STEP_ROWS * s, NA_STEP_ROWS * (s + 1)))
        geom = geom + (min(NA_SLAB_ROWS, rows - slab),)
        if not classes or classes[-1] != geom:
            assert geom not in classes
            classes.append(geom)
            first_steps.append(s)
    return classes, first_steps


def _natten_bias_kernel(r_ref, o_ref, *, classes, kh):
    cls = pl.program_id(1)
    kc = lax.broadcasted_iota(jnp.int32, (GRID_W, LANES), 0)
    lane = lax.broadcasted_iota(jnp.int32, (GRID_W, LANES), 1)
    second = lane >= GRID_W
    c = jnp.where(second, lane - GRID_W, lane)
    c0 = jnp.clip(c - NA_COLS // 2, 0, GRID_W - NA_COLS)
    col_ok = jnp.logical_and(kc >= c0, kc < c0 + NA_COLS)
    neg = jnp.full((GRID_W, LANES), NEG_INF, F32)

    def fill(geom):
        for h in range(NA_HEADS):
            for kr in range(NA_SLAB_ROWS):
                halves = []
                for j, (q_off, w_off) in enumerate(geom[:-1]):
                    if w_off <= kr < w_off + kh and kr < geom[-1]:
                        dr = kr - q_off + NA_ROWS - 1
                        row = jnp.broadcast_to(r_ref[0, h, dr:dr + 1, :], (GRID_W, LANES))
                        halves.append(pltpu.roll(row, (j * GRID_W - (NA_COLS - 1)) % LANES, axis=1,
                                                 stride=1, stride_axis=0))
                    else:
                        halves.append(neg)
                val = jnp.where(col_ok, jnp.where(second, halves[1], halves[0]), neg)
                o_ref[0, 0, h // 2, kr * GRID_W:(kr + 1) * GRID_W,
                      (h % 2) * LANES:(h % 2 + 1) * LANES] = val

    for ci, geom in enumerate(classes):
        pl.when(cls == ci)(functools.partial(fill, geom))


def _natten_bias(rpb, rows):
    kh = min(NA_ROWS, rows)
    classes, _ = _natten_step_classes(rows)
    n_dr, n_dc = 2 * NA_ROWS - 1, 2 * NA_COLS - 1
    assert NA_STEP_ROWS == 2 and n_dc <= GRID_W
    r = jnp.pad(rpb[..., ::-1].astype(F32) * LOG2_E,
                ((0, 0), (0, 0), (0, 2 * SUBLANES - n_dr), (0, LANES - n_dc)))
    return pl.pallas_call(
        functools.partial(_natten_bias_kernel, classes=classes, kh=kh),
        grid=(DEPTH, len(classes)),
        in_specs=[pl.BlockSpec((1, NA_HEADS, 2 * SUBLANES, LANES), lambda l, k: (l, 0, 0, 0))],
        out_specs=pl.BlockSpec((1, 1, NA_HEADS // 2, NA_SLAB_ROWS * GRID_W, 2 * LANES),
                               lambda l, k: (l, k, 0, 0, 0)),
        out_shape=jax.ShapeDtypeStruct(
            (DEPTH, len(classes), NA_HEADS // 2, NA_SLAB_ROWS * GRID_W, 2 * LANES), F32),
        compiler_params=_params(2),
        name="natten_bias_tables",
    )(r)


def _neighbourhood_attention(qc, kc, vct, bias, layer, n_lat, n_ctx, with_ctx):
    bsz, n_tok, _ = qc.shape
    rows = n_lat // GRID_W
    kh = min(NA_ROWS, rows)
    n_q = NA_STEP_ROWS * GRID_W
    assert (kh // 2) % NA_STEP_ROWS == 0 and (rows - kh) % NA_STEP_ROWS == 0 and n_q == LANES
    assert (rows - kh + NA_SLAB_ROWS) * GRID_W <= n_tok and kh + NA_STEP_ROWS <= NA_SLAB_ROWS
    assert n_ctx % PV_CHUNK == 0 and (NA_SLAB_ROWS * GRID_W) % LANES == 0
    _, first_steps = _natten_step_classes(rows)
    n_steps = (n_tok if with_ctx else n_lat) // n_q

    def bias_class(s):
        return sum((s >= f).astype(jnp.int32) for f in first_steps[1:])

    full = pl.BlockSpec((1, n_tok, BRANCH_W), lambda b, s: (b, 0, 0))
    full_t = pl.BlockSpec((1, BRANCH_W, n_tok), lambda b, s: (b, 0, 0))
    tile = pl.BlockSpec((1, n_q, BRANCH_W), lambda b, s: (b, s, 0))
    return pl.pallas_call(
        functools.partial(_natten_kernel, n_lat=n_lat, n_ctx=n_ctx),
        grid=(bsz, n_steps),
        in_specs=[
            tile, full, full_t,
            pl.BlockSpec((1, 1) + bias.shape[2:], lambda b, s: (layer, bias_class(s), 0, 0, 0)),
        ],
        out_specs=tile,
        out_shape=jax.ShapeDtypeStruct((bsz, n_tok, BRANCH_W), BF16),
        scratch_shapes=[pltpu.VMEM((n_ctx + NA_SLAB_ROWS * GRID_W, 2 * n_q), F32)]
                       * (NA_HEADS // 2),
        compiler_params=_params(2),
        name="neighbourhood_attention",
    )(qc, kc, vct, bias)


def _merge_kernel(x_ref, mod_ref, fa_ref, db_ref, nc_ref, gate_ref, wa_ref, wb_ref, wc_ref,
                  wo_ref, o_ref):
    def gate(j):
        return gate_ref[0, :, j * D_MODEL:(j + 1) * D_MODEL].astype(F32)

    y = (gate(0) * _dot(fa_ref[0], wa_ref[...]) + gate(1) * _dot(db_ref[0], wb_ref[...])
         + gate(2) * _dot(nc_ref[0], wc_ref[...]))
    y = _dot(y.astype(BF16), wo_ref[...])
    o_ref[0] = x_ref[0] + mod_ref[2:3, :] * y


def _merge(h, mod, fa, db, nc, gates, w_a, w_b, w_c, w_out, n_lat, with_ctx):
    bsz, n_tok, _ = h.shape
    lat_tiles = n_lat // TOK_TILE
    n_tiles = (n_tok if with_ctx else n_lat) // TOK_TILE
    tok = lambda w: pl.BlockSpec((1, TOK_TILE, w), lambda b, t: (b, t, 0))
    return pl.pallas_call(
        _merge_kernel,
        grid=(bsz, n_tiles),
        in_specs=[
            tok(D_MODEL),
            pl.BlockSpec((None, None, N_MOD, D_MODEL), lambda b, t: (b, t // lat_tiles, 0, 0)),
            tok(BRANCH_W), tok(BRANCH_W), tok(BRANCH_W), tok(N_BRANCH * D_MODEL),
            _resident((BRANCH_W, D_MODEL)), _resident((BRANCH_W, D_MODEL)),
            _resident((BRANCH_W, D_MODEL)), _resident((D_MODEL, D_MODEL)),
        ],
        out_specs=tok(D_MODEL),
        out_shape=jax.ShapeDtypeStruct((bsz, n_tiles * TOK_TILE, D_MODEL), F32),
        compiler_params=_params(2),
        name="merge_out_projection",
    )(h, mod, fa, db, nc, gates, w_a, w_b, w_c, w_out)


def _convffn_kernel(x_ref, prev_ref, next_ref, mod_ref, g_ref, wu_ref, cw_ref, cb_ref, wd_ref,
                    gf_ref, o_ref, lhs_ref, *, lat_tiles, n_tiles, final_norm):
    t = pl.program_id(1)
    g = g_ref[...]
    shift, scale = mod_ref[3:4, :], mod_ref[4:5, :]
    x = x_ref[0]
    has_prev = jnp.logical_and(t != 0, t != lat_tiles)
    has_next = jnp.logical_and(t != lat_tiles - 1, t != n_tiles - 1)
    prev = _norm_modulate(prev_ref[0], g, shift, scale) * has_prev.astype(F32)
    nxt = _norm_modulate(next_ref[0], g, shift, scale) * has_next.astype(F32)
    lhs_ref[0:SUBLANES, :] = prev
    lhs_ref[SUBLANES:SUBLANES + TOK_TILE, :] = _norm_modulate(x, g, shift, scale)
    lhs_ref[SUBLANES + TOK_TILE:, :] = nxt
    lhs = lhs_ref[...].astype(BF16)
    n_rows = TOK_TILE + 2 * SUBLANES

    def conv(u, col):
        w = cw_ref[:, col:col + FF_CHUNK]
        before = pltpu.roll(u, 1, axis=0)
        after = pltpu.roll(u, n_rows - 1, axis=0)
        v = before * w[0:1] + u * w[1:2] + after * w[2:3] + cb_ref[:, col:col + FF_CHUNK]
        return v[SUBLANES:SUBLANES + TOK_TILE]

    def up(j):
        col_a, col_b = j * FF_CHUNK, D_FF + j * FF_CHUNK
        return (_dot(lhs, wu_ref[:, col_a:col_a + FF_CHUNK]),
                _dot(lhs, wu_ref[:, col_b:col_b + FF_CHUNK]))

    n_chunks = D_FF // FF_CHUNK
    acc = jnp.zeros((TOK_TILE, D_MODEL), F32)
    u_next = up(0)
    for j in range(n_chunks):
        col_a, col_b = j * FF_CHUNK, D_FF + j * FF_CHUNK
        u_a, u_b = u_next
        if j + 1 < n_chunks:
            u_next = up(j + 1)
        a = conv(u_a, col_a)
        b = conv(u_b, col_b)
        act = (a / (1.0 + jnp.exp(-a))) * b
        acc = acc + _dot(act.astype(BF16), wd_ref[col_a:col_a + FF_CHUNK, :])
    y = x + mod_ref[5:6, :] * acc
    if final_norm:
        ms = jnp.mean(y * y, axis=-1, keepdims=True)
        y = y * lax.rsqrt(ms + NORM_EPS) * gf_ref[...]
    o_ref[0] = y


def _conv_ffn(h, mod, g_ffn, w_up, conv_w, conv_b, w_down, g_final, n_lat, with_ctx, final_norm):
    bsz, n_rows_in, _ = h.shape
    lat_tiles = n_lat // TOK_TILE
    n_tiles = n_rows_in // TOK_TILE if with_ctx else lat_tiles
    blocks_per_tile = TOK_TILE // SUBLANES
    last_block = n_rows_in // SUBLANES - 1
    return pl.pallas_call(
        functools.partial(_convffn_kernel, lat_tiles=lat_tiles, n_tiles=n_tiles,
                          final_norm=final_norm),
        grid=(bsz, n_tiles),
        in_specs=[
            pl.BlockSpec((1, TOK_TILE, D_MODEL), lambda b, t: (b, t, 0)),
            pl.BlockSpec((1, SUBLANES, D_MODEL),
                         lambda b, t: (b, jnp.maximum(t * blocks_per_tile - 1, 0), 0)),
            pl.BlockSpec((1, SUBLANES, D_MODEL),
                         lambda b, t: (b, jnp.minimum((t + 1) * blocks_per_tile, last_block), 0)),
            pl.BlockSpec((None, None, N_MOD, D_MODEL), lambda b, t: (b, t // lat_tiles, 0, 0)),
            _resident((1, D_MODEL)),
            _resident((D_MODEL, 2 * D_FF)),
            _resident((3, 2 * D_FF)),
            _resident((1, 2 * D_FF)),
            _resident((D_FF, D_MODEL)),
            _resident((1, D_MODEL)),
        ],
        out_specs=pl.BlockSpec((1, TOK_TILE, D_MODEL), lambda b, t: (b, t, 0)),
        out_shape=jax.ShapeDtypeStruct((bsz, n_tiles * TOK_TILE, D_MODEL), F32),
        scratch_shapes=[pltpu.VMEM((TOK_TILE + 2 * SUBLANES, D_MODEL), F32)],
        compiler_params=_params(2),
        name="conv_ffn",
    )(h, h, h, mod, g_ffn.reshape(1, D_MODEL), w_up, conv_w, conv_b.reshape(1, -1), w_down,
      g_final.reshape(1, D_MODEL))


def _rope_tables(n_lat, n_ctx):
    t = jnp.arange(n_lat, dtype=jnp.int32)
    pos = jnp.stack([(t // GRID_W).astype(F32), (t % GRID_W).astype(F32)], axis=1)
    n_freq = ROPE_AXIS_DIM // 2
    inv = ROPE_THETA ** (-jnp.arange(n_freq, dtype=F32) / n_freq)
    ang = pos[:, :, None] * inv
    lane = jnp.arange(LANES, dtype=jnp.int32)
    axis = (lane % DA_HEAD_DIM) // ROPE_AXIS_DIM
    freq = lane % n_freq
    second_half = (lane % ROPE_AXIS_DIM) >= n_freq
    ang_l = ang[:, axis, freq]
    cos, sin = jnp.cos(ang_l), jnp.sin(ang_l)
    sin_a = jnp.where(second_half, 0.0, -sin)
    sin_b = jnp.where(second_half, sin, 0.0)
    pad = lambda a, v: jnp.concatenate([a, jnp.full((n_ctx, LANES), v, F32)], axis=0)
    return pad(cos, 1.0), pad(sin_a, 0.0), pad(sin_b, 0.0)


def kernel(x, c, ctx, c_ctx, w_ada, b_ada, g_mix, g_ffn, w_in, b_gate, w_a, lam, subln_g, w_b,
           rpb, w_c, w_out, w_up, conv_w, conv_b, w_down, g_final):
    bsz, n_lat, _ = x.shape
    n_ctx = ctx.shape[1]
    assert n_lat % TOK_TILE == 0 and n_ctx == TOK_TILE and n_lat % GRID_W == 0

    rope_tabs = _rope_tables(n_lat, n_ctx)
    cl, sl = _dft_tables(n_lat, (n_lat * FN_GROUP_DIM) ** -0.5)
    cc, sc = _dft_tables(n_ctx, (n_ctx * FN_GROUP_DIM) ** -0.5)
    cg, sg = _dft_tables(FN_GROUP_DIM, 1.0)

    n_mod_rows = 2 * SUBLANES * (-(-(bsz + 1) // (2 * SUBLANES)))
    cvec = jnp.zeros((n_mod_rows, D_MODEL), F32).at[:bsz].set(c).at[bsz].set(c_ctx)
    mods = _modulation(cvec, w_ada, b_ada).reshape(DEPTH, n_mod_rows, N_MOD, D_MODEL)

    na_bias = _natten_bias(rpb, n_lat // GRID_W)

    h = jnp.concatenate([x, ctx], axis=1)
    for l in range(DEPTH):
        with_ctx = l != DEPTH - 1
        lam_init = 0.8 - 0.6 * math.exp(-0.3 * l)
        mod = jnp.stack([mods[l, :bsz],
                         jnp.broadcast_to(mods[l, bsz], (bsz, N_MOD, D_MODEL))], axis=1)
        fa_in, qb, kb, vbt, qc, kc, vct, gates = _in_projection(
            h, mod, g_mix[l], w_in[l].astype(BF16), b_gate[l], rope_tabs, n_lat)
        fa = _fourier_mix(fa_in, (cl, sl, cc, sc, cg, sg), n_lat, n_ctx, with_ctx)
        db = _diff_attention(qb, kb, vbt, lam[l], subln_g[l], lam_init, n_lat, n_ctx, with_ctx)
        nc = _neighbourhood_attention(qc, kc, vct, na_bias, l, n_lat, n_ctx, with_ctx)
        h = _merge(h, mod, fa, db, nc, gates, w_a[l].astype(BF16), w_b[l].astype(BF16),
                   w_c[l].astype(BF16), w_out[l].astype(BF16), n_lat, with_ctx)
        h = _conv_ffn(h, mod, g_ffn[l], w_up[l].astype(BF16), conv_w[l], conv_b[l],
                      w_down[l].astype(BF16), g_final, n_lat, with_ctx, final_norm=not with_ctx)
    return h
```

```python
import functools
import math

import jax
import jax.numpy as jnp
from jax import lax
from jax.experimental import pallas as pl
from jax.experimental.pallas import tpu as pltpu

D_MODEL = 1024
DEPTH = 4
GRID_W = 64
FN_GROUPS = 4
FN_GROUP_DIM = 128
FN_WIDTH = FN_GROUPS * FN_GROUP_DIM
DA_HEADS = 4
DA_HEAD_DIM = 64
DA_V_DIM = 2 * DA_HEAD_DIM
NA_HEADS = 8
NA_HEAD_DIM = 64
NA_ROWS = 8
NA_COLS = 16
BRANCH_W = 512
N_BRANCH = 3
ROPE_THETA = 10000.0
ROPE_AXIS_DIM = DA_HEAD_DIM // 2
D_FF = 2816
N_MOD = 6
NORM_EPS = 1e-6
SUBLN_EPS = 1e-5
NEG_INF = -1e30

LANES = 128
SUBLANES = 8
TOK_TILE = 256
FF_CHUNK = 256
FF_LOOKAHEAD = 2
NA_STEP_ROWS = 2
NA_SLAB_ROWS = 10
VMEM_LIMIT = 56 * 1024 * 1024
ONES_ROWS = 16
PV_CHUNK = 256
LOG2_E = math.log2(math.e)

BF16 = jnp.bfloat16
F32 = jnp.float32


def _params(n_grid_dims):
    return pltpu.CompilerParams(dimension_semantics=("arbitrary",) * n_grid_dims,
                                vmem_limit_bytes=VMEM_LIMIT)


def _resident(shape):
    return pl.BlockSpec(shape, lambda *_: (0,) * len(shape), pipeline_mode=pl.Buffered(1))


def _nt_dot(a, b):
    return lax.dot_general(a, b, (((1,), (1,)), ((), ())), preferred_element_type=F32)


def _dot(a, b):
    return jnp.dot(a, b, preferred_element_type=F32)


def _norm_modulate(x, g, shift, scale):
    ms = jnp.mean(x * x, axis=-1, keepdims=True)
    return (x * lax.rsqrt(ms + NORM_EPS) * g) * (1.0 + scale) + shift


def _split_bf16(v):
    hi = v.astype(BF16)
    lo = (v - hi.astype(F32)).astype(BF16)
    return hi, lo


def _mod_kernel(c_ref, w_ref, b_ref, o_ref):
    c = c_ref[...]
    s = c / (1.0 + jnp.exp(-c))
    s_hi, s_lo = _split_bf16(s)
    w_hi, w_lo = _split_bf16(w_ref[0])
    acc = _dot(s_hi, w_hi) + (_dot(s_hi, w_lo) + _dot(s_lo, w_hi))
    o_ref[0] = acc + b_ref[0]


def _modulation(cvec, w_ada, b_ada):
    n_rows = cvec.shape[0]
    n_out = N_MOD * D_MODEL
    tn = 1536
    return pl.pallas_call(
        _mod_kernel,
        grid=(DEPTH, n_out // tn),
        in_specs=[
            pl.BlockSpec((n_rows, D_MODEL), lambda l, j: (0, 0)),
            pl.BlockSpec((1, D_MODEL, tn), lambda l, j: (l, 0, j)),
            pl.BlockSpec((1, 1, tn), lambda l, j: (l, 0, j)),
        ],
        out_specs=pl.BlockSpec((1, n_rows, tn), lambda l, j: (l, 0, j)),
        out_shape=jax.ShapeDtypeStruct((DEPTH, n_rows, n_out), F32),
        compiler_params=_params(2),
        name="adaln_modulation",
    )(cvec, w_ada, b_ada.reshape(DEPTH, 1, n_out))


def _rope(p, cos, sin_a, sin_b):
    outs = []
    for k in range(p.shape[1] // LANES):
        xs = p[:, k * LANES:(k + 1) * LANES]
        from_hi = pltpu.roll(xs, LANES - ROPE_AXIS_DIM // 2, axis=1)
        from_lo = pltpu.roll(xs, ROPE_AXIS_DIM // 2, axis=1)
        outs.append(xs * cos + from_hi * sin_a + from_lo * sin_b)
    return jnp.concatenate(outs, axis=1)


def _inproj_kernel(x_ref, mod_ref, g_ref, w_ref, bg_ref, cos_ref, sa_ref, sb_ref,
                   fa_ref, qb_ref, kb_ref, vbt_ref, qc_ref, kc_ref, vct_ref, gate_ref):
    a = _norm_modulate(x_ref[0], g_ref[...], mod_ref[0:1, :], mod_ref[1:2, :]).astype(BF16)
    cos, sin_a, sin_b = cos_ref[...], sa_ref[...], sb_ref[...]

    def proj(seg):
        return _dot(a, w_ref[:, seg * BRANCH_W:(seg + 1) * BRANCH_W])

    fa_ref[0] = proj(0).astype(BF16)
    qb_ref[0] = (_rope(proj(1), cos, sin_a, sin_b) * (DA_HEAD_DIM ** -0.5 * LOG2_E)).astype(BF16)
    kb_ref[0] = _rope(proj(2), cos, sin_a, sin_b).astype(BF16)
    vbt_ref[0] = proj(3).T.astype(BF16)
    qc_ref[0] = (proj(4) * (NA_HEAD_DIM ** -0.5 * LOG2_E)).astype(BF16)
    kc_ref[0] = proj(5).astype(BF16)
    vct_ref[0] = proj(6).T.astype(BF16)
    for j in range(N_BRANCH * D_MODEL // BRANCH_W):
        z = proj(7 + j) + bg_ref[:, j * BRANCH_W:(j + 1) * BRANCH_W]
        gate_ref[0, :, j * BRANCH_W:(j + 1) * BRANCH_W] = (1.0 / (1.0 + jnp.exp(-z))).astype(BF16)


def _in_projection(h, mod, g_mix, w_in, b_gate, rope_tabs, n_lat):
    bsz, n_tok, _ = h.shape
    n_tiles = n_tok // TOK_TILE
    lat_tiles = n_lat // TOK_TILE
    proj_w = w_in.shape[1]
    tok = lambda w: pl.BlockSpec((1, TOK_TILE, w), lambda b, t: (b, t, 0))
    tab = pl.BlockSpec((TOK_TILE, LANES), lambda b, t: (t, 0))
    branch = jax.ShapeDtypeStruct((bsz, n_tok, BRANCH_W), BF16)
    tok_t = pl.BlockSpec((1, BRANCH_W, TOK_TILE), lambda b, t: (b, 0, t))
    branch_t = jax.ShapeDtypeStruct((bsz, BRANCH_W, n_tok), BF16)
    return pl.pallas_call(
        _inproj_kernel,
        grid=(bsz, n_tiles),
        in_specs=[
            tok(D_MODEL),
            pl.BlockSpec((None, None, N_MOD, D_MODEL), lambda b, t: (b, t // lat_tiles, 0, 0)),
            _resident((1, D_MODEL)),
            _resident((D_MODEL, proj_w)),
            _resident((1, N_BRANCH * D_MODEL)),
            tab, tab, tab,
        ],
        out_specs=[tok(BRANCH_W)] * 3 + [tok_t] + [tok(BRANCH_W)] * 2 + [tok_t]
                  + [tok(N_BRANCH * D_MODEL)],
        out_shape=[branch] * 3 + [branch_t] + [branch] * 2 + [branch_t]
                  + [jax.ShapeDtypeStruct((bsz, n_tok, N_BRANCH * D_MODEL), BF16)],
        compiler_params=_params(2),
        name="in_projection",
    )(h, mod, g_mix.reshape(1, D_MODEL), w_in, b_gate.reshape(1, -1), *rope_tabs)


def _dft_kernel(u_ref, cl_ref, sl_ref, cc_ref, sc_ref, cg_ref, sg_ref, o_ref, *, n_lat, n_ctx):
    t = pl.program_id(1)
    lat_tiles = n_lat // TOK_TILE

    def finish(p, q):
        p = p.astype(BF16)
        q = q.astype(BF16)
        for g in range(FN_GROUPS):
            sl = slice(g * FN_GROUP_DIM, (g + 1) * FN_GROUP_DIM)
            f = _dot(p[:, sl], cg_ref[...]) - _dot(q[:, sl], sg_ref[...])
            o_ref[0, :, sl] = f.astype(BF16)

    @pl.when(t < lat_tiles)
    def _():
        u = u_ref[0, 0:n_lat, :]
        finish(_dot(cl_ref[...], u), _dot(sl_ref[...], u))

    @pl.when(t >= lat_tiles)
    def _():
        u = u_ref[0, n_lat:n_lat + n_ctx, :]
        finish(_dot(cc_ref[...], u), _dot(sc_ref[...], u))


def _fourier_mix(fa_in, tabs, n_lat, n_ctx, with_ctx):
    bsz, n_tok, _ = fa_in.shape
    lat_tiles = n_lat // TOK_TILE
    n_tiles = n_tok // TOK_TILE if with_ctx else lat_tiles
    cl, sl, cc, sc, cg, sg = tabs
    lat_tab = pl.BlockSpec((TOK_TILE, n_lat), lambda b, t: (jnp.minimum(t, lat_tiles - 1), 0))
    return pl.pallas_call(
        functools.partial(_dft_kernel, n_lat=n_lat, n_ctx=n_ctx),
        grid=(bsz, n_tiles),
        in_specs=[
            pl.BlockSpec((1, n_tok, FN_WIDTH), lambda b, t: (b, 0, 0)),
            lat_tab, lat_tab,
            _resident((n_ctx, n_ctx)), _resident((n_ctx, n_ctx)),
            _resident((FN_GROUP_DIM, FN_GROUP_DIM)), _resident((FN_GROUP_DIM, FN_GROUP_DIM)),
        ],
        out_specs=pl.BlockSpec((1, TOK_TILE, FN_WIDTH), lambda b, t: (b, t, 0)),
        out_shape=jax.ShapeDtypeStruct((bsz, n_tok, FN_WIDTH), BF16),
        compiler_params=_params(2),
        name="fourier_mix",
    )(fa_in, cl, sl, cc, sc, cg, sg)


def _dft_tables(n, scale):
    k = jnp.arange(n, dtype=jnp.int32)
    ang = ((k[:, None] * k[None, :]) % n).astype(F32) * (2.0 * math.pi / n)
    return (jnp.cos(ang) * scale).astype(BF16), (jnp.sin(ang) * scale).astype(BF16)


def _stack_sub_heads(q, first_half):
    zero = jnp.zeros_like(q)
    return jnp.concatenate([jnp.where(first_half, q, zero), jnp.where(first_half, zero, q)], axis=0)


def _store_scores(s_ref, row, s):
    n = s.shape[0]
    s_ref[row:row + n, :] = s
    return jnp.max(s.reshape(n // SUBLANES, SUBLANES, -1).max(axis=0), axis=0, keepdims=True)


def _softmax_times_values(s_ref, n_rows, m, values_t):
    acc = None
    for r in range(0, n_rows, PV_CHUNK):
        n = min(PV_CHUNK, n_rows - r)
        e = jnp.exp2(s_ref[r:r + n, :] - m).astype(BF16)
        v = values_t(r, n)
        part = _dot(jnp.concatenate([v, jnp.ones((ONES_ROWS, n), BF16)], axis=0), e)
        acc = part if acc is None else acc + part
    n_ch = acc.shape[0] - ONES_ROWS
    return acc[0:n_ch] * (1.0 / acc[n_ch:n_ch + 1])


def _diffattn_kernel(q_ref, k_ref, vt_ref, lam_ref, g_ref, o_ref, s0_ref, s1_ref,
                     *, n_lat, n_ctx, lam_init):
    t = pl.program_id(1)
    lat_tiles = n_lat // TOK_TILE
    lv = lam_ref[...]
    lam = (jnp.exp(jnp.sum(lv[0:1] * lv[1:2], axis=-1, keepdims=True))
           - jnp.exp(jnp.sum(lv[2:3] * lv[3:4], axis=-1, keepdims=True)) + lam_init)
    first_half = lax.broadcasted_iota(jnp.int32, (1, LANES), 1) < DA_HEAD_DIM
    s_refs = (s0_ref, s1_ref)

    def attend(k_lo, k_len):
        def scores(h):
            sl = slice(h * DA_V_DIM, (h + 1) * DA_V_DIM)
            s = _nt_dot(k_ref[0, k_lo:k_lo + k_len, sl],
                        _stack_sub_heads(q_ref[0, :, sl], first_half))
            return _store_scores(s_refs[h % 2], 0, s)

        m_next = scores(0)
        for h in range(DA_HEADS):
            sl = slice(h * DA_V_DIM, (h + 1) * DA_V_DIM)
            m = m_next
            if h + 1 < DA_HEADS:
                m_next = scores(h + 1)
            o12 = _softmax_times_values(
                s_refs[h % 2], k_len, m, lambda r, n: vt_ref[0, sl, k_lo + r:k_lo + r + n]).T
            o = o12[0:TOK_TILE] - lam * o12[TOK_TILE:2 * TOK_TILE]
            ms = jnp.mean(o * o, axis=-1, keepdims=True)
            o = o * lax.rsqrt(ms + SUBLN_EPS) * g_ref[...] * (1.0 - lam_init)
            o_ref[0, :, sl] = o.astype(BF16)

    @pl.when(t < lat_tiles)
    def _():
        attend(0, n_lat + n_ctx)

    @pl.when(t >= lat_tiles)
    def _():
        attend(n_lat, n_ctx)


def _diff_attention(qb, kb, vbt, lam_vec, subln_g, lam_init, n_lat, n_ctx, with_ctx):
    bsz, n_tok, _ = qb.shape
    n_tiles = (n_tok if with_ctx else n_lat) // TOK_TILE
    full = pl.BlockSpec((1, n_tok, BRANCH_W), lambda b, t: (b, 0, 0))
    full_t = pl.BlockSpec((1, BRANCH_W, n_tok), lambda b, t: (b, 0, 0))
    tile = pl.BlockSpec((1, TOK_TILE, BRANCH_W), lambda b, t: (b, t, 0))
    return pl.pallas_call(
        functools.partial(_diffattn_kernel, n_lat=n_lat, n_ctx=n_ctx, lam_init=lam_init),
        grid=(bsz, n_tiles),
        in_specs=[tile, full, full_t, _resident((4, DA_HEAD_DIM)), _resident((1, DA_V_DIM))],
        out_specs=tile,
        out_shape=jax.ShapeDtypeStruct((bsz, n_tok, BRANCH_W), BF16),
        scratch_shapes=[pltpu.VMEM((n_tok, 2 * TOK_TILE), F32)] * 2,
        compiler_params=_params(2),
        name="diff_attention",
    )(qb, kb, vbt, lam_vec, subln_g.reshape(1, DA_V_DIM))


def _natten_kernel(q_ref, k_ref, vt_ref, bias_ref, o_ref, *s_refs, n_lat, n_ctx):
    s_idx = pl.program_id(1)
    rows = n_lat // GRID_W
    kh = min(NA_ROWS, rows)
    n_slab = NA_SLAB_ROWS * GRID_W
    n_q = NA_STEP_ROWS * GRID_W
    first_half = lax.broadcasted_iota(jnp.int32, (1, LANES), 1) < NA_HEAD_DIM

    def run(window):
        first_row = jnp.clip(NA_STEP_ROWS * s_idx - kh // 2, 0, rows - kh)
        k_lo = pl.multiple_of(first_row * GRID_W, LANES)
        n_keys = n_ctx + (n_slab if window else 0)
        pairs = [slice(p * LANES, (p + 1) * LANES) for p in range(NA_HEADS // 2)]
        maxes = []
        for p, sl in enumerate(pairs):
            qs = _stack_sub_heads(q_ref[0, :, sl], first_half)
            m = _store_scores(s_refs[p], 0, _nt_dot(k_ref[0, n_lat:n_lat + n_ctx, sl], qs))
            if window:
                s_win = _nt_dot(k_ref[0, pl.ds(k_lo, n_slab), sl], qs) + bias_ref[0, 0, p]
                m = jnp.maximum(m, _store_scores(s_refs[p], n_ctx, s_win))
            maxes.append(m)
        for p, sl in enumerate(pairs):
            def values_t(r, n):
                if r < n_ctx:
                    return vt_ref[0, sl, n_lat + r:n_lat + r + n]
                return vt_ref[0, sl, pl.ds(pl.multiple_of(k_lo + (r - n_ctx), LANES), n)]

            o2 = _softmax_times_values(s_refs[p], n_keys, maxes[p], values_t).T
            o_ref[0, :, sl] = jnp.where(first_half, o2[0:n_q], o2[n_q:2 * n_q]).astype(BF16)

    @pl.when(s_idx < rows // NA_STEP_ROWS)
    def _():
        run(True)

    @pl.when(s_idx >= rows // NA_STEP_ROWS)
    def _():
        run(False)


def _natten_step_classes(rows):
    kh = min(NA_ROWS, rows)
    classes, first_steps = [], []
    for s in range(rows // NA_STEP_ROWS):
        slab = min(max(NA_STEP_ROWS * s - kh // 2, 0), rows - kh)
        geom = tuple((r - slab, min(max(r - kh // 2, 0), rows - kh) - slab)
                     for r in range(NA_STEP_ROWS * s, NA_STEP_ROWS * (s + 1)))
        geom = geom + (min(NA_SLAB_ROWS, rows - slab),)
        if not classes or classes[-1] != geom:
            assert geom not in classes
            classes.append(geom)
            first_steps.append(s)
    return classes, first_steps


def _natten_bias_kernel(r_ref, o_ref, *, classes, kh):
    cls = pl.program_id(1)
    kc = lax.broadcasted_iota(jnp.int32, (GRID_W, LANES), 0)
    lane = lax.broadcasted_iota(jnp.int32, (GRID_W, LANES), 1)
    second = lane >= GRID_W
    c = jnp.where(second, lane - GRID_W, lane)
    c0 = jnp.clip(c - NA_COLS // 2, 0, GRID_W - NA_COLS)
    col_ok = jnp.logical_and(kc >= c0, kc < c0 + NA_COLS)
    neg = jnp.full((GRID_W, LANES), NEG_INF, F32)

    def fill(geom):
        for h in range(NA_HEADS):
            for kr in range(NA_SLAB_ROWS):
                halves = []
                for j, (q_off, w_off) in enumerate(geom[:-1]):
                    if w_off <= kr < w_off + kh and kr < geom[-1]:
                        dr = kr - q_off + NA_ROWS - 1
                        row = jnp.broadcast_to(r_ref[0, h, dr:dr + 1, :], (GRID_W, LANES))
                        halves.append(pltpu.roll(row, (j * GRID_W - (NA_COLS - 1)) % LANES, axis=1,
                                                 stride=1, stride_axis=0))
                    else:
                        halves.append(neg)
                val = jnp.where(col_ok, jnp.where(second, halves[1], halves[0]), neg)
                o_ref[0, 0, h // 2, kr * GRID_W:(kr + 1) * GRID_W,
                      (h % 2) * LANES:(h % 2 + 1) * LANES] = val

    for ci, geom in enumerate(classes):
        pl.when(cls == ci)(functools.partial(fill, geom))


def _natten_bias(rpb, rows):
    kh = min(NA_ROWS, rows)
    classes, _ = _natten_step_classes(rows)
    n_dr, n_dc = 2 * NA_ROWS - 1, 2 * NA_COLS - 1
    assert NA_STEP_ROWS == 2 and n_dc <= GRID_W
    r = jnp.pad(rpb[..., ::-1].astype(F32) * LOG2_E,
                ((0, 0), (0, 0), (0, 2 * SUBLANES - n_dr), (0, LANES - n_dc)))
    return pl.pallas_call(
        functools.partial(_natten_bias_kernel, classes=classes, kh=kh),
        grid=(DEPTH, len(classes)),
        in_specs=[pl.BlockSpec((1, NA_HEADS, 2 * SUBLANES, LANES), lambda l, k: (l, 0, 0, 0))],
        out_specs=pl.BlockSpec((1, 1, NA_HEADS // 2, NA_SLAB_ROWS * GRID_W, 2 * LANES),
                               lambda l, k: (l, k, 0, 0, 0)),
        out_shape=jax.ShapeDtypeStruct(
            (DEPTH, len(classes), NA_HEADS // 2, NA_SLAB_ROWS * GRID_W, 2 * LANES), F32),
        compiler_params=_params(2),
        name="natten_bias_tables",
    )(r)


def _neighbourhood_attention(qc, kc, vct, bias, layer, n_lat, n_ctx, with_ctx):
    bsz, n_tok, _ = qc.shape
    rows = n_lat // GRID_W
    kh = min(NA_ROWS, rows)
    n_q = NA_STEP_ROWS * GRID_W
    assert (kh // 2) % NA_STEP_ROWS == 0 and (rows - kh) % NA_STEP_ROWS == 0 and n_q == LANES
    assert (rows - kh + NA_SLAB_ROWS) * GRID_W <= n_tok and kh + NA_STEP_ROWS <= NA_SLAB_ROWS
    assert n_ctx % PV_CHUNK == 0 and (NA_SLAB_ROWS * GRID_W) % LANES == 0
    _, first_steps = _natten_step_classes(rows)
    n_steps = (n_tok if with_ctx else n_lat) // n_q

    def bias_class(s):
        return sum((s >= f).astype(jnp.int32) for f in first_steps[1:])

    full = pl.BlockSpec((1, n_tok, BRANCH_W), lambda b, s: (b, 0, 0))
    full_t = pl.BlockSpec((1, BRANCH_W, n_tok), lambda b, s: (b, 0, 0))
    tile = pl.BlockSpec((1, n_q, BRANCH_W), lambda b, s: (b, s, 0))
    return pl.pallas_call(
        functools.partial(_natten_kernel, n_lat=n_lat, n_ctx=n_ctx),
        grid=(bsz, n_steps),
        in_specs=[
            tile, full, full_t,
            pl.BlockSpec((1, 1) + bias.shape[2:], lambda b, s: (layer, bias_class(s), 0, 0, 0)),
        ],
        out_specs=tile,
        out_shape=jax.ShapeDtypeStruct((bsz, n_tok, BRANCH_W), BF16),
        scratch_shapes=[pltpu.VMEM((n_ctx + NA_SLAB_ROWS * GRID_W, 2 * n_q), F32)]
                       * (NA_HEADS // 2),
        compiler_params=_params(2),
        name="neighbourhood_attention",
    )(qc, kc, vct, bias)


def _merge_kernel(x_ref, mod_ref, fa_ref, db_ref, nc_ref, gate_ref, wa_ref, wb_ref, wc_ref,
                  wo_ref, o_ref):
    def gate(j):
        return gate_ref[0, :, j * D_MODEL:(j + 1) * D_MODEL].astype(F32)

    y = (gate(0) * _dot(fa_ref[0], wa_ref[...]) + gate(1) * _dot(db_ref[0], wb_ref[...])
         + gate(2) * _dot(nc_ref[0], wc_ref[...]))
    y = _dot(y.astype(BF16), wo_ref[...])
    o_ref[0] = x_ref[0] + mod_ref[2:3, :] * y


def _merge(h, mod, fa, db, nc, gates, w_a, w_b, w_c, w_out, n_lat, with_ctx):
    bsz, n_tok, _ = h.shape
    lat_tiles = n_lat // TOK_TILE
    n_tiles = (n_tok if with_ctx else n_lat) // TOK_TILE
    tok = lambda w: pl.BlockSpec((1, TOK_TILE, w), lambda b, t: (b, t, 0))
    return pl.pallas_call(
        _merge_kernel,
        grid=(bsz, n_tiles),
        in_specs=[
            tok(D_MODEL),
            pl.BlockSpec((None, None, N_MOD, D_MODEL), lambda b, t: (b, t // lat_tiles, 0, 0)),
            tok(BRANCH_W), tok(BRANCH_W), tok(BRANCH_W), tok(N_BRANCH * D_MODEL),
            _resident((BRANCH_W, D_MODEL)), _resident((BRANCH_W, D_MODEL)),
            _resident((BRANCH_W, D_MODEL)), _resident((D_MODEL, D_MODEL)),
        ],
        out_specs=tok(D_MODEL),
        out_shape=jax.ShapeDtypeStruct((bsz, n_tiles * TOK_TILE, D_MODEL), F32),
        compiler_params=_params(2),
        name="merge_out_projection",
    )(h, mod, fa, db, nc, gates, w_a, w_b, w_c, w_out)


def _convffn_kernel(x_ref, prev_ref, next_ref, mod_ref, g_ref, wu_ref, cw_ref, cb_ref, wd_ref,
                    gf_ref, o_ref, lhs_ref, *, lat_tiles, n_tiles, final_norm):
    t = pl.program_id(1)
    g = g_ref[...]
    shift, scale = mod_ref[3:4, :], mod_ref[4:5, :]
    x = x_ref[0]
    has_prev = jnp.logical_and(t != 0, t != lat_tiles)
    has_next = jnp.logical_and(t != lat_tiles - 1, t != n_tiles - 1)
    prev = _norm_modulate(prev_ref[0], g, shift, scale) * has_prev.astype(F32)
    nxt = _norm_modulate(next_ref[0], g, shift, scale) * has_next.astype(F32)
    lhs_ref[0:SUBLANES, :] = prev
    lhs_ref[SUBLANES:SUBLANES + TOK_TILE, :] = _norm_modulate(x, g, shift, scale)
    lhs_ref[SUBLANES + TOK_TILE:, :] = nxt
    lhs = lhs_ref[...].astype(BF16)
    n_rows = TOK_TILE + 2 * SUBLANES

    def conv(u, col):
        w = cw_ref[:, col:col + FF_CHUNK]
        before = pltpu.roll(u, 1, axis=0)
        after = pltpu.roll(u, n_rows - 1, axis=0)
        v = before * w[0:1] + u * w[1:2] + after * w[2:3] + cb_ref[:, col:col + FF_CHUNK]
        return v[SUBLANES:SUBLANES + TOK_TILE]

    def up(j):
        col_a, col_b = j * FF_CHUNK, D_FF + j * FF_CHUNK
        return (_dot(lhs, wu_ref[:, col_a:col_a + FF_CHUNK]),
                _dot(lhs, wu_ref[:, col_b:col_b + FF_CHUNK]))

    n_chunks = D_FF // FF_CHUNK
    acc = jnp.zeros((TOK_TILE, D_MODEL), F32)
    ahead = [up(j) for j in range(FF_LOOKAHEAD)]
    for j in range(n_chunks):
        col_a, col_b = j * FF_CHUNK, D_FF + j * FF_CHUNK
        u_a, u_b = ahead.pop(0)
        if j + FF_LOOKAHEAD < n_chunks:
            ahead.append(up(j + FF_LOOKAHEAD))
        a = conv(u_a, col_a)
        b = conv(u_b, col_b)
        act = (a / (1.0 + jnp.exp(-a))) * b
        acc = acc + _dot(act.astype(BF16), wd_ref[col_a:col_a + FF_CHUNK, :])
    y = x + mod_ref[5:6, :] * acc
    if final_norm:
        ms = jnp.mean(y * y, axis=-1, keepdims=True)
        y = y * lax.rsqrt(ms + NORM_EPS) * gf_ref[...]
    o_ref[0] = y


def _conv_ffn(h, mod, g_ffn, w_up, conv_w, conv_b, w_down, g_final, n_lat, with_ctx, final_norm):
    bsz, n_rows_in, _ = h.shape
    lat_tiles = n_lat // TOK_TILE
    n_tiles = n_rows_in // TOK_TILE if with_ctx else lat_tiles
    blocks_per_tile = TOK_TILE // SUBLANES
    last_block = n_rows_in // SUBLANES - 1
    return pl.pallas_call(
        functools.partial(_convffn_kernel, lat_tiles=lat_tiles, n_tiles=n_tiles,
                          final_norm=final_norm),
        grid=(bsz, n_tiles),
        in_specs=[
            pl.BlockSpec((1, TOK_TILE, D_MODEL), lambda b, t: (b, t, 0)),
            pl.BlockSpec((1, SUBLANES, D_MODEL),
                         lambda b, t: (b, jnp.maximum(t * blocks_per_tile - 1, 0), 0)),
            pl.BlockSpec((1, SUBLANES, D_MODEL),
                         lambda b, t: (b, jnp.minimum((t + 1) * blocks_per_tile, last_block), 0)),
            pl.BlockSpec((None, None, N_MOD, D_MODEL), lambda b, t: (b, t // lat_tiles, 0, 0)),
            _resident((1, D_MODEL)),
            _resident((D_MODEL, 2 * D_FF)),
            _resident((3, 2 * D_FF)),
            _resident((1, 2 * D_FF)),
            _resident((D_FF, D_MODEL)),
            _resident((1, D_MODEL)),
        ],
        out_specs=pl.BlockSpec((1, TOK_TILE, D_MODEL), lambda b, t: (b, t, 0)),
        out_shape=jax.ShapeDtypeStruct((bsz, n_tiles * TOK_TILE, D_MODEL), F32),
        scratch_shapes=[pltpu.VMEM((TOK_TILE + 2 * SUBLANES, D_MODEL), F32)],
        compiler_params=_params(2),
        name="conv_ffn",
    )(h, h, h, mod, g_ffn.reshape(1, D_MODEL), w_up, conv_w, conv_b.reshape(1, -1), w_down,
      g_final.reshape(1, D_MODEL))


def _rope_tables(n_lat, n_ctx):
    t = jnp.arange(n_lat, dtype=jnp.int32)
    pos = jnp.stack([(t // GRID_W).astype(F32), (t % GRID_W).astype(F32)], axis=1)
    n_freq = ROPE_AXIS_DIM // 2
    inv = ROPE_THETA ** (-jnp.arange(n_freq, dtype=F32) / n_freq)
    ang = pos[:, :, None] * inv
    lane = jnp.arange(LANES, dtype=jnp.int32)
    axis = (lane % DA_HEAD_DIM) // ROPE_AXIS_DIM
    freq = lane % n_freq
    second_half = (lane % ROPE_AXIS_DIM) >= n_freq
    ang_l = ang[:, axis, freq]
    cos, sin = jnp.cos(ang_l), jnp.sin(ang_l)
    sin_a = jnp.where(second_half, 0.0, -sin)
    sin_b = jnp.where(second_half, sin, 0.0)
    pad = lambda a, v: jnp.concatenate([a, jnp.full((n_ctx, LANES), v, F32)], axis=0)
    return pad(cos, 1.0), pad(sin_a, 0.0), pad(sin_b, 0.0)


def kernel(x, c, ctx, c_ctx, w_ada, b_ada, g_mix, g_ffn, w_in, b_gate, w_a, lam, subln_g, w_b,
           rpb, w_c, w_out, w_up, conv_w, conv_b, w_down, g_final):
    bsz, n_lat, _ = x.shape
    n_ctx = ctx.shape[1]
    assert n_lat % TOK_TILE == 0 and n_ctx == TOK_TILE and n_lat % GRID_W == 0

    rope_tabs = _rope_tables(n_lat, n_ctx)
    cl, sl = _dft_tables(n_lat, (n_lat * FN_GROUP_DIM) ** -0.5)
    cc, sc = _dft_tables(n_ctx, (n_ctx * FN_GROUP_DIM) ** -0.5)
    cg, sg = _dft_tables(FN_GROUP_DIM, 1.0)

    n_mod_rows = 2 * SUBLANES * (-(-(bsz + 1) // (2 * SUBLANES)))
    cvec = jnp.zeros((n_mod_rows, D_MODEL), F32).at[:bsz].set(c).at[bsz].set(c_ctx)
    mods = _modulation(cvec, w_ada, b_ada).reshape(DEPTH, n_mod_rows, N_MOD, D_MODEL)

    na_bias = _natten_bias(rpb, n_lat // GRID_W)

    h = jnp.concatenate([x, ctx], axis=1)
    for l in range(DEPTH):
        with_ctx = l != DEPTH - 1
        lam_init = 0.8 - 0.6 * math.exp(-0.3 * l)
        mod = jnp.stack([mods[l, :bsz],
                         jnp.broadcast_to(mods[l, bsz], (bsz, N_MOD, D_MODEL))], axis=1)
        fa_in, qb, kb, vbt, qc, kc, vct, gates = _in_projection(
            h, mod, g_mix[l], w_in[l].astype(BF16), b_gate[l], rope_tabs, n_lat)
        fa = _fourier_mix(fa_in, (cl, sl, cc, sc, cg, sg), n_lat, n_ctx, with_ctx)
        db = _diff_attention(qb, kb, vbt, lam[l], subln_g[l], lam_init, n_lat, n_ctx, with_ctx)
        nc = _neighbourhood_attention(qc, kc, vct, na_bias, l, n_lat, n_ctx, with_ctx)
        h = _merge(h, mod, fa, db, nc, gates, w_a[l].astype(BF16), w_b[l].astype(BF16),
                   w_c[l].astype(BF16), w_out[l].astype(BF16), n_lat, with_ctx)
        h = _conv_ffn(h, mod, g_ffn[l], w_up[l].astype(BF16), conv_w[l], conv_b[l],
                      w_down[l].astype(BF16), g_final, n_lat, with_ctx, final_norm=not with_ctx)
    return h
```

```python
import functools
import math

import jax
import jax.numpy as jnp
from jax import lax
from jax.experimental import pallas as pl
from jax.experimental.pallas import tpu as pltpu

D_MODEL = 1024
DEPTH = 4
GRID_W = 64
FN_GROUPS = 4
FN_GROUP_DIM = 128
FN_WIDTH = FN_GROUPS * FN_GROUP_DIM
DA_HEADS = 4
DA_HEAD_DIM = 64
DA_V_DIM = 2 * DA_HEAD_DIM
NA_HEADS = 8
NA_HEAD_DIM = 64
NA_ROWS = 8
NA_COLS = 16
BRANCH_W = 512
N_BRANCH = 3
ROPE_THETA = 10000.0
ROPE_AXIS_DIM = DA_HEAD_DIM // 2
D_FF = 2816
N_MOD = 6
NORM_EPS = 1e-6
SUBLN_EPS = 1e-5
NEG_INF = -1e30

LANES = 128
SUBLANES = 8
TOK_TILE = 256
FF_CHUNK = 256
FF_LOOKAHEAD = 2
NA_STEP_ROWS = 2
NA_GROUPS = 2
NA_SLAB_ROWS = 10
VMEM_LIMIT = 56 * 1024 * 1024
ONES_ROWS = 16
PV_CHUNK = 256
LOG2_E = math.log2(math.e)

BF16 = jnp.bfloat16
F32 = jnp.float32


def _params(n_grid_dims):
    return pltpu.CompilerParams(dimension_semantics=("arbitrary",) * n_grid_dims,
                                vmem_limit_bytes=VMEM_LIMIT)


def _resident(shape):
    return pl.BlockSpec(shape, lambda *_: (0,) * len(shape), pipeline_mode=pl.Buffered(1))


def _resident_layer(shape, layer):
    return pl.BlockSpec((None,) + tuple(shape), lambda *_: (layer,) + (0,) * len(shape),
                        pipeline_mode=pl.Buffered(1))


def _nt_dot(a, b):
    return lax.dot_general(a, b, (((1,), (1,)), ((), ())), preferred_element_type=F32)


def _dot(a, b):
    return jnp.dot(a, b, preferred_element_type=F32)


def _norm_modulate(x, g, shift, scale):
    ms = jnp.mean(x * x, axis=-1, keepdims=True)
    return (x * lax.rsqrt(ms + NORM_EPS) * g) * (1.0 + scale) + shift


def _split_bf16(v):
    hi = v.astype(BF16)
    lo = (v - hi.astype(F32)).astype(BF16)
    return hi, lo


def _mod_kernel(c_ref, w_ref, b_ref, o_ref):
    c = c_ref[...]
    s = c / (1.0 + jnp.exp(-c))
    s_hi, s_lo = _split_bf16(s)
    w_hi, w_lo = _split_bf16(w_ref[0])
    acc = _dot(s_hi, w_hi) + (_dot(s_hi, w_lo) + _dot(s_lo, w_hi))
    o_ref[0] = acc + b_ref[0]


def _modulation(cvec, w_ada, b_ada):
    n_rows = cvec.shape[0]
    n_out = N_MOD * D_MODEL
    tn = 1536
    return pl.pallas_call(
        _mod_kernel,
        grid=(DEPTH, n_out // tn),
        in_specs=[
            pl.BlockSpec((n_rows, D_MODEL), lambda l, j: (0, 0)),
            pl.BlockSpec((1, D_MODEL, tn), lambda l, j: (l, 0, j)),
            pl.BlockSpec((1, 1, tn), lambda l, j: (l, 0, j)),
        ],
        out_specs=pl.BlockSpec((1, n_rows, tn), lambda l, j: (l, 0, j)),
        out_shape=jax.ShapeDtypeStruct((DEPTH, n_rows, n_out), F32),
        compiler_params=_params(2),
        name="adaln_modulation",
    )(cvec, w_ada, b_ada.reshape(DEPTH, 1, n_out))


def _rope(p, cos, sin_a, sin_b):
    outs = []
    for k in range(p.shape[1] // LANES):
        xs = p[:, k * LANES:(k + 1) * LANES]
        from_hi = pltpu.roll(xs, LANES - ROPE_AXIS_DIM // 2, axis=1)
        from_lo = pltpu.roll(xs, ROPE_AXIS_DIM // 2, axis=1)
        outs.append(xs * cos + from_hi * sin_a + from_lo * sin_b)
    return jnp.concatenate(outs, axis=1)


def _inproj_kernel(x_ref, mod_ref, g_ref, w_ref, bg_ref, cos_ref, sa_ref, sb_ref,
                   fa_ref, qb_ref, kb_ref, vbt_ref, qc_ref, kc_ref, vct_ref, gate_ref):
    a = _norm_modulate(x_ref[0], g_ref[...], mod_ref[0:1, :], mod_ref[1:2, :]).astype(BF16)
    cos, sin_a, sin_b = cos_ref[...], sa_ref[...], sb_ref[...]

    def proj(seg):
        return _dot(a, w_ref[:, seg * BRANCH_W:(seg + 1) * BRANCH_W])

    fa_ref[0] = proj(0).astype(BF16)
    qb_ref[0] = (_rope(proj(1), cos, sin_a, sin_b) * (DA_HEAD_DIM ** -0.5 * LOG2_E)).astype(BF16)
    kb_ref[0] = _rope(proj(2), cos, sin_a, sin_b).astype(BF16)
    vbt_ref[0] = proj(3).T.astype(BF16)
    qc_ref[0] = (proj(4) * (NA_HEAD_DIM ** -0.5 * LOG2_E)).astype(BF16)
    kc_ref[0] = proj(5).astype(BF16)
    vct_ref[0] = proj(6).T.astype(BF16)
    for j in range(N_BRANCH * D_MODEL // BRANCH_W):
        z = proj(7 + j) + bg_ref[:, j * BRANCH_W:(j + 1) * BRANCH_W]
        gate_ref[0, :, j * BRANCH_W:(j + 1) * BRANCH_W] = (1.0 / (1.0 + jnp.exp(-z))).astype(BF16)


def _in_projection(h, mod, g_mix, w_in, layer, b_gate, rope_tabs, n_lat):
    bsz, n_tok, _ = h.shape
    n_tiles = n_tok // TOK_TILE
    lat_tiles = n_lat // TOK_TILE
    proj_w = w_in.shape[-1]
    tok = lambda w: pl.BlockSpec((1, TOK_TILE, w), lambda b, t: (b, t, 0))
    tab = pl.BlockSpec((TOK_TILE, LANES), lambda b, t: (t, 0))
    branch = jax.ShapeDtypeStruct((bsz, n_tok, BRANCH_W), BF16)
    tok_t = pl.BlockSpec((1, BRANCH_W, TOK_TILE), lambda b, t: (b, 0, t))
    branch_t = jax.ShapeDtypeStruct((bsz, BRANCH_W, n_tok), BF16)
    return pl.pallas_call(
        _inproj_kernel,
        grid=(bsz, n_tiles),
        in_specs=[
            tok(D_MODEL),
            pl.BlockSpec((None, None, N_MOD, D_MODEL), lambda b, t: (b, t // lat_tiles, 0, 0)),
            _resident((1, D_MODEL)),
            _resident_layer((D_MODEL, proj_w), layer),
            _resident((1, N_BRANCH * D_MODEL)),
            tab, tab, tab,
        ],
        out_specs=[tok(BRANCH_W)] * 3 + [tok_t] + [tok(BRANCH_W)] * 2 + [tok_t]
                  + [tok(N_BRANCH * D_MODEL)],
        out_shape=[branch] * 3 + [branch_t] + [branch] * 2 + [branch_t]
                  + [jax.ShapeDtypeStruct((bsz, n_tok, N_BRANCH * D_MODEL), BF16)],
        compiler_params=_params(2),
        name="in_projection",
    )(h, mod, g_mix.reshape(1, D_MODEL), w_in, b_gate.reshape(1, -1), *rope_tabs)


def _dft_kernel(u_ref, cl_ref, sl_ref, cc_ref, sc_ref, cg_ref, sg_ref, o_ref, *, n_lat, n_ctx):
    t = pl.program_id(1)
    lat_tiles = n_lat // TOK_TILE

    def finish(p, q):
        p = p.astype(BF16)
        q = q.astype(BF16)
        for g in range(FN_GROUPS):
            sl = slice(g * FN_GROUP_DIM, (g + 1) * FN_GROUP_DIM)
            f = _dot(p[:, sl], cg_ref[...]) - _dot(q[:, sl], sg_ref[...])
            o_ref[0, :, sl] = f.astype(BF16)

    @pl.when(t < lat_tiles)
    def _():
        u = u_ref[0, 0:n_lat, :]
        finish(_dot(cl_ref[...], u), _dot(sl_ref[...], u))

    @pl.when(t >= lat_tiles)
    def _():
        u = u_ref[0, n_lat:n_lat + n_ctx, :]
        finish(_dot(cc_ref[...], u), _dot(sc_ref[...], u))


def _fourier_mix(fa_in, tabs, n_lat, n_ctx, with_ctx):
    bsz, n_tok, _ = fa_in.shape
    lat_tiles = n_lat // TOK_TILE
    n_tiles = n_tok // TOK_TILE if with_ctx else lat_tiles
    cl, sl, cc, sc, cg, sg = tabs
    lat_tab = pl.BlockSpec((TOK_TILE, n_lat), lambda b, t: (jnp.minimum(t, lat_tiles - 1), 0))
    return pl.pallas_call(
        functools.partial(_dft_kernel, n_lat=n_lat, n_ctx=n_ctx),
        grid=(bsz, n_tiles),
        in_specs=[
            pl.BlockSpec((1, n_tok, FN_WIDTH), lambda b, t: (b, 0, 0)),
            lat_tab, lat_tab,
            _resident((n_ctx, n_ctx)), _resident((n_ctx, n_ctx)),
            _resident((FN_GROUP_DIM, FN_GROUP_DIM)), _resident((FN_GROUP_DIM, FN_GROUP_DIM)),
        ],
        out_specs=pl.BlockSpec((1, TOK_TILE, FN_WIDTH), lambda b, t: (b, t, 0)),
        out_shape=jax.ShapeDtypeStruct((bsz, n_tok, FN_WIDTH), BF16),
        compiler_params=_params(2),
        name="fourier_mix",
    )(fa_in, cl, sl, cc, sc, cg, sg)


def _dft_tables(n, scale):
    k = jnp.arange(n, dtype=jnp.int32)
    ang = ((k[:, None] * k[None, :]) % n).astype(F32) * (2.0 * math.pi / n)
    return (jnp.cos(ang) * scale).astype(BF16), (jnp.sin(ang) * scale).astype(BF16)


def _stack_sub_heads(q, first_half):
    zero = jnp.zeros_like(q)
    return jnp.concatenate([jnp.where(first_half, q, zero), jnp.where(first_half, zero, q)], axis=0)


def _store_scores(s_ref, row, s):
    n = s.shape[0]
    s_ref[row:row + n, :] = s
    return jnp.max(s.reshape(n // SUBLANES, SUBLANES, -1).max(axis=0), axis=0, keepdims=True)


def _softmax_times_values(s_ref, n_rows, m, values_t):
    acc = None
    for r in range(0, n_rows, PV_CHUNK):
        n = min(PV_CHUNK, n_rows - r)
        e = jnp.exp2(s_ref[r:r + n, :] - m).astype(BF16)
        v = values_t(r, n)
        part = _dot(jnp.concatenate([v, jnp.ones((ONES_ROWS, n), BF16)], axis=0), e)
        acc = part if acc is None else acc + part
    n_ch = acc.shape[0] - ONES_ROWS
    return acc[0:n_ch] * (1.0 / acc[n_ch:n_ch + 1])


def _diffattn_kernel(q_ref, k_ref, vt_ref, lam_ref, g_ref, o_ref, s0_ref, s1_ref,
                     *, n_lat, n_ctx, lam_init):
    t = pl.program_id(1)
    lat_tiles = n_lat // TOK_TILE
    lv = lam_ref[...]
    lam = (jnp.exp(jnp.sum(lv[0:1] * lv[1:2], axis=-1, keepdims=True))
           - jnp.exp(jnp.sum(lv[2:3] * lv[3:4], axis=-1, keepdims=True)) + lam_init)
    first_half = lax.broadcasted_iota(jnp.int32, (1, LANES), 1) < DA_HEAD_DIM
    s_refs = (s0_ref, s1_ref)

    def attend(k_lo, k_len):
        def scores(h):
            sl = slice(h * DA_V_DIM, (h + 1) * DA_V_DIM)
            s = _nt_dot(k_ref[0, k_lo:k_lo + k_len, sl],
                        _stack_sub_heads(q_ref[0, :, sl], first_half))
            return _store_scores(s_refs[h % 2], 0, s)

        m_next = scores(0)
        for h in range(DA_HEADS):
            sl = slice(h * DA_V_DIM, (h + 1) * DA_V_DIM)
            m = m_next
            if h + 1 < DA_HEADS:
                m_next = scores(h + 1)
            o12 = _softmax_times_values(
                s_refs[h % 2], k_len, m, lambda r, n: vt_ref[0, sl, k_lo + r:k_lo + r + n]).T
            o = o12[0:TOK_TILE] - lam * o12[TOK_TILE:2 * TOK_TILE]
            ms = jnp.mean(o * o, axis=-1, keepdims=True)
            o = o * lax.rsqrt(ms + SUBLN_EPS) * g_ref[...] * (1.0 - lam_init)
            o_ref[0, :, sl] = o.astype(BF16)

    @pl.when(t < lat_tiles)
    def _():
        attend(0, n_lat + n_ctx)

    @pl.when(t >= lat_tiles)
    def _():
        attend(n_lat, n_ctx)


def _diff_attention(qb, kb, vbt, lam_vec, subln_g, lam_init, n_lat, n_ctx, with_ctx):
    bsz, n_tok, _ = qb.shape
    n_tiles = (n_tok if with_ctx else n_lat) // TOK_TILE
    full = pl.BlockSpec((1, n_tok, BRANCH_W), lambda b, t: (b, 0, 0))
    full_t = pl.BlockSpec((1, BRANCH_W, n_tok), lambda b, t: (b, 0, 0))
    tile = pl.BlockSpec((1, TOK_TILE, BRANCH_W), lambda b, t: (b, t, 0))
    return pl.pallas_call(
        functools.partial(_diffattn_kernel, n_lat=n_lat, n_ctx=n_ctx, lam_init=lam_init),
        grid=(bsz, n_tiles),
        in_specs=[tile, full, full_t, _resident((4, DA_HEAD_DIM)), _resident((1, DA_V_DIM))],
        out_specs=tile,
        out_shape=jax.ShapeDtypeStruct((bsz, n_tok, BRANCH_W), BF16),
        scratch_shapes=[pltpu.VMEM((n_tok, 2 * TOK_TILE), F32)] * 2,
        compiler_params=_params(2),
        name="diff_attention",
    )(qb, kb, vbt, lam_vec, subln_g.reshape(1, DA_V_DIM))


def _natten_kernel(q_ref, k_ref, vt_ref, *refs, n_lat, n_ctx):
    bias_refs, o_ref, s_refs = refs[:NA_GROUPS], refs[NA_GROUPS], refs[NA_GROUPS + 1:]
    s_idx = pl.program_id(1)
    rows = n_lat // GRID_W
    kh = min(NA_ROWS, rows)
    n_slab = NA_SLAB_ROWS * GRID_W
    n_q = NA_STEP_ROWS * GRID_W
    n_pairs = NA_HEADS // 2
    first_half = lax.broadcasted_iota(jnp.int32, (1, LANES), 1) < NA_HEAD_DIM
    pairs = [slice(p * LANES, (p + 1) * LANES) for p in range(n_pairs)]

    def run(window):
        n_keys = n_ctx + (n_slab if window else 0)
        k_los, maxes = [], []
        for g in range(NA_GROUPS):
            first_row = jnp.clip(NA_STEP_ROWS * (NA_GROUPS * s_idx + g) - kh // 2, 0, rows - kh)
            k_lo = pl.multiple_of(first_row * GRID_W, LANES)
            k_los.append(k_lo)
            for p, sl in enumerate(pairs):
                s_ref = s_refs[g * n_pairs + p]
                qs = _stack_sub_heads(q_ref[0, g * n_q:(g + 1) * n_q, sl], first_half)
                m = _store_scores(s_ref, 0, _nt_dot(k_ref[0, n_lat:n_lat + n_ctx, sl], qs))
                if window:
                    s_win = _nt_dot(k_ref[0, pl.ds(k_lo, n_slab), sl], qs) + bias_refs[g][0, 0, p]
                    m = jnp.maximum(m, _store_scores(s_ref, n_ctx, s_win))
                maxes.append(m)
        for g in range(NA_GROUPS):
            for p, sl in enumerate(pairs):
                def values_t(r, n):
                    if r < n_ctx:
                        return vt_ref[0, sl, n_lat + r:n_lat + r + n]
                    return vt_ref[0, sl, pl.ds(pl.multiple_of(k_los[g] + (r - n_ctx), LANES), n)]

                i = g * n_pairs + p
                o2 = _softmax_times_values(s_refs[i], n_keys, maxes[i], values_t).T
                o_ref[0, g * n_q:(g + 1) * n_q, sl] = jnp.where(
                    first_half, o2[0:n_q], o2[n_q:2 * n_q]).astype(BF16)

    window_steps = rows // (NA_STEP_ROWS * NA_GROUPS)

    @pl.when(s_idx < window_steps)
    def _():
        run(True)

    @pl.when(s_idx >= window_steps)
    def _():
        run(False)


def _natten_step_classes(rows):
    kh = min(NA_ROWS, rows)
    classes, first_steps = [], []
    for s in range(rows // NA_STEP_ROWS):
        slab = min(max(NA_STEP_ROWS * s - kh // 2, 0), rows - kh)
        geom = tuple((r - slab, min(max(r - kh // 2, 0), rows - kh) - slab)
                     for r in range(NA_STEP_ROWS * s, NA_STEP_ROWS * (s + 1)))
        geom = geom + (min(NA_SLAB_ROWS, rows - slab),)
        if not classes or classes[-1] != geom:
            assert geom not in classes
            classes.append(geom)
            first_steps.append(s)
    return classes, first_steps


def _natten_bias_kernel(r_ref, o_ref, *, classes, kh):
    cls = pl.program_id(1)
    kc = lax.broadcasted_iota(jnp.int32, (GRID_W, LANES), 0)
    lane = lax.broadcasted_iota(jnp.int32, (GRID_W, LANES), 1)
    second = lane >= GRID_W
    c = jnp.where(second, lane - GRID_W, lane)
    c0 = jnp.clip(c - NA_COLS // 2, 0, GRID_W - NA_COLS)
    col_ok = jnp.logical_and(kc >= c0, kc < c0 + NA_COLS)
    neg = jnp.full((GRID_W, LANES), NEG_INF, F32)

    def fill(geom):
        for h in range(NA_HEADS):
            for kr in range(NA_SLAB_ROWS):
                halves = []
                for j, (q_off, w_off) in enumerate(geom[:-1]):
                    if w_off <= kr < w_off + kh and kr < geom[-1]:
                        dr = kr - q_off + NA_ROWS - 1
                        row = jnp.broadcast_to(r_ref[0, h, dr:dr + 1, :], (GRID_W, LANES))
                        halves.append(pltpu.roll(row, (j * GRID_W - (NA_COLS - 1)) % LANES, axis=1,
                                                 stride=1, stride_axis=0))
                    else:
                        halves.append(neg)
                val = jnp.where(col_ok, jnp.where(second, halves[1], halves[0]), neg)
                o_ref[0, 0, h // 2, kr * GRID_W:(kr + 1) * GRID_W,
                      (h % 2) * LANES:(h % 2 + 1) * LANES] = val

    for ci, geom in enumerate(classes):
        pl.when(cls == ci)(functools.partial(fill, geom))


def _natten_bias(rpb, rows):
    kh = min(NA_ROWS, rows)
    classes, _ = _natten_step_classes(rows)
    n_dr, n_dc = 2 * NA_ROWS - 1, 2 * NA_COLS - 1
    assert NA_STEP_ROWS == 2 and n_dc <= GRID_W
    r = jnp.pad(rpb[..., ::-1].astype(F32) * LOG2_E,
                ((0, 0), (0, 0), (0, 2 * SUBLANES - n_dr), (0, LANES - n_dc)))
    return pl.pallas_call(
        functools.partial(_natten_bias_kernel, classes=classes, kh=kh),
        grid=(DEPTH, len(classes)),
        in_specs=[pl.BlockSpec((1, NA_HEADS, 2 * SUBLANES, LANES), lambda l, k: (l, 0, 0, 0))],
        out_specs=pl.BlockSpec((1, 1, NA_HEADS // 2, NA_SLAB_ROWS * GRID_W, 2 * LANES),
                               lambda l, k: (l, k, 0, 0, 0)),
        out_shape=jax.ShapeDtypeStruct(
            (DEPTH, len(classes), NA_HEADS // 2, NA_SLAB_ROWS * GRID_W, 2 * LANES), F32),
        compiler_params=_params(2),
        name="natten_bias_tables",
    )(r)


def _neighbourhood_attention(qc, kc, vct, bias, layer, n_lat, n_ctx, with_ctx):
    bsz, n_tok, _ = qc.shape
    rows = n_lat // GRID_W
    kh = min(NA_ROWS, rows)
    n_q = NA_STEP_ROWS * GRID_W
    assert (kh // 2) % NA_STEP_ROWS == 0 and (rows - kh) % NA_STEP_ROWS == 0 and n_q == LANES
    assert (rows - kh + NA_SLAB_ROWS) * GRID_W <= n_tok and kh + NA_STEP_ROWS <= NA_SLAB_ROWS
    assert n_ctx % PV_CHUNK == 0 and (NA_SLAB_ROWS * GRID_W) % LANES == 0
    _, first_steps = _natten_step_classes(rows)
    step_q = NA_GROUPS * n_q
    assert n_lat % step_q == 0 and n_ctx % step_q == 0
    n_steps = (n_tok if with_ctx else n_lat) // step_q

    def bias_spec(g):
        def index(b, s):
            group = NA_GROUPS * s + g
            cls = sum((group >= f).astype(jnp.int32) for f in first_steps[1:])
            return (layer, cls, 0, 0, 0)
        return pl.BlockSpec((1, 1) + bias.shape[2:], index)

    full = pl.BlockSpec((1, n_tok, BRANCH_W), lambda b, s: (b, 0, 0))
    full_t = pl.BlockSpec((1, BRANCH_W, n_tok), lambda b, s: (b, 0, 0))
    tile = pl.BlockSpec((1, step_q, BRANCH_W), lambda b, s: (b, s, 0))
    return pl.pallas_call(
        functools.partial(_natten_kernel, n_lat=n_lat, n_ctx=n_ctx),
        grid=(bsz, n_steps),
        in_specs=[tile, full, full_t] + [bias_spec(g) for g in range(NA_GROUPS)],
        out_specs=tile,
        out_shape=jax.ShapeDtypeStruct((bsz, n_tok, BRANCH_W), BF16),
        scratch_shapes=[pltpu.VMEM((n_ctx + NA_SLAB_ROWS * GRID_W, 2 * n_q), F32)]
                       * (NA_GROUPS * NA_HEADS // 2),
        compiler_params=_params(2),
        name="neighbourhood_attention",
    )(qc, kc, vct, *([bias] * NA_GROUPS))


def _merge_kernel(x_ref, mod_ref, fa_ref, db_ref, nc_ref, gate_ref, wa_ref, wb_ref, wc_ref,
                  wo_ref, o_ref):
    def gate(j):
        return gate_ref[0, :, j * D_MODEL:(j + 1) * D_MODEL].astype(F32)

    y = (gate(0) * _dot(fa_ref[0], wa_ref[...]) + gate(1) * _dot(db_ref[0], wb_ref[...])
         + gate(2) * _dot(nc_ref[0], wc_ref[...]))
    y = _dot(y.astype(BF16), wo_ref[...])
    o_ref[0] = x_ref[0] + mod_ref[2:3, :] * y


def _merge(h, mod, fa, db, nc, gates, w_a, w_b, w_c, w_out, layer, n_lat, with_ctx):
    bsz, n_tok, _ = h.shape
    lat_tiles = n_lat // TOK_TILE
    n_tiles = (n_tok if with_ctx else n_lat) // TOK_TILE
    tok = lambda w: pl.BlockSpec((1, TOK_TILE, w), lambda b, t: (b, t, 0))
    return pl.pallas_call(
        _merge_kernel,
        grid=(bsz, n_tiles),
        in_specs=[
            tok(D_MODEL),
            pl.BlockSpec((None, None, N_MOD, D_MODEL), lambda b, t: (b, t // lat_tiles, 0, 0)),
            tok(BRANCH_W), tok(BRANCH_W), tok(BRANCH_W), tok(N_BRANCH * D_MODEL),
            _resident_layer((BRANCH_W, D_MODEL), layer), _resident_layer((BRANCH_W, D_MODEL), layer),
            _resident_layer((BRANCH_W, D_MODEL), layer), _resident_layer((D_MODEL, D_MODEL), layer),
        ],
        out_specs=tok(D_MODEL),
        out_shape=jax.ShapeDtypeStruct((bsz, n_tiles * TOK_TILE, D_MODEL), F32),
        compiler_params=_params(2),
        name="merge_out_projection",
    )(h, mod, fa, db, nc, gates, w_a, w_b, w_c, w_out)


def _convffn_kernel(x_ref, prev_ref, next_ref, mod_ref, g_ref, wu_ref, cw_ref, cb_ref, wd_ref,
                    gf_ref, o_ref, lhs_ref, *, lat_tiles, n_tiles, final_norm):
    t = pl.program_id(1)
    g = g_ref[...]
    shift, scale = mod_ref[3:4, :], mod_ref[4:5, :]
    x = x_ref[0]
    has_prev = jnp.logical_and(t != 0, t != lat_tiles)
    has_next = jnp.logical_and(t != lat_tiles - 1, t != n_tiles - 1)
    prev = _norm_modulate(prev_ref[0], g, shift, scale) * has_prev.astype(F32)
    nxt = _norm_modulate(next_ref[0], g, shift, scale) * has_next.astype(F32)
    lhs_ref[0:SUBLANES, :] = prev
    lhs_ref[SUBLANES:SUBLANES + TOK_TILE, :] = _norm_modulate(x, g, shift, scale)
    lhs_ref[SUBLANES + TOK_TILE:, :] = nxt
    lhs = lhs_ref[...].astype(BF16)
    n_rows = TOK_TILE + 2 * SUBLANES

    def conv(u, col):
        w = cw_ref[:, col:col + FF_CHUNK]
        before = pltpu.roll(u, 1, axis=0)
        after = pltpu.roll(u, n_rows - 1, axis=0)
        v = before * w[0:1] + u * w[1:2] + after * w[2:3] + cb_ref[:, col:col + FF_CHUNK]
        return v[SUBLANES:SUBLANES + TOK_TILE]

    def up(j):
        col_a, col_b = j * FF_CHUNK, D_FF + j * FF_CHUNK
        return (_dot(lhs, wu_ref[:, col_a:col_a + FF_CHUNK]),
                _dot(lhs, wu_ref[:, col_b:col_b + FF_CHUNK]))

    n_chunks = D_FF // FF_CHUNK
    acc = jnp.zeros((TOK_TILE, D_MODEL), F32)
    ahead = [up(j) for j in range(FF_LOOKAHEAD)]
    for j in range(n_chunks):
        col_a, col_b = j * FF_CHUNK, D_FF + j * FF_CHUNK
        u_a, u_b = ahead.pop(0)
        if j + FF_LOOKAHEAD < n_chunks:
            ahead.append(up(j + FF_LOOKAHEAD))
        a = conv(u_a, col_a)
        b = conv(u_b, col_b)
        act = (a / (1.0 + jnp.exp(-a))) * b
        acc = acc + _dot(act.astype(BF16), wd_ref[col_a:col_a + FF_CHUNK, :])
    y = x + mod_ref[5:6, :] * acc
    if final_norm:
        ms = jnp.mean(y * y, axis=-1, keepdims=True)
        y = y * lax.rsqrt(ms + NORM_EPS) * gf_ref[...]
    o_ref[0] = y


def _conv_ffn(h, mod, g_ffn, w_up, conv_w, conv_b, w_down, layer, g_final, n_lat, with_ctx,
              final_norm):
    bsz, n_rows_in, _ = h.shape
    lat_tiles = n_lat // TOK_TILE
    n_tiles = n_rows_in // TOK_TILE if with_ctx else lat_tiles
    blocks_per_tile = TOK_TILE // SUBLANES
    last_block = n_rows_in // SUBLANES - 1
    return pl.pallas_call(
        functools.partial(_convffn_kernel, lat_tiles=lat_tiles, n_tiles=n_tiles,
                          final_norm=final_norm),
        grid=(bsz, n_tiles),
        in_specs=[
            pl.BlockSpec((1, TOK_TILE, D_MODEL), lambda b, t: (b, t, 0)),
            pl.BlockSpec((1, SUBLANES, D_MODEL),
                         lambda b, t: (b, jnp.maximum(t * blocks_per_tile - 1, 0), 0)),
            pl.BlockSpec((1, SUBLANES, D_MODEL),
                         lambda b, t: (b, jnp.minimum((t + 1) * blocks_per_tile, last_block), 0)),
            pl.BlockSpec((None, None, N_MOD, D_MODEL), lambda b, t: (b, t // lat_tiles, 0, 0)),
            _resident((1, D_MODEL)),
            _resident_layer((D_MODEL, 2 * D_FF), layer),
            _resident((3, 2 * D_FF)),
            _resident((1, 2 * D_FF)),
            _resident_layer((D_FF, D_MODEL), layer),
            _resident((1, D_MODEL)),
        ],
        out_specs=pl.BlockSpec((1, TOK_TILE, D_MODEL), lambda b, t: (b, t, 0)),
        out_shape=jax.ShapeDtypeStruct((bsz, n_tiles * TOK_TILE, D_MODEL), F32),
        scratch_shapes=[pltpu.VMEM((TOK_TILE + 2 * SUBLANES, D_MODEL), F32)],
        compiler_params=_params(2),
        name="conv_ffn",
    )(h, h, h, mod, g_ffn.reshape(1, D_MODEL), w_up, conv_w, conv_b.reshape(1, -1), w_down,
      g_final.reshape(1, D_MODEL))


def _rope_tables(n_lat, n_ctx):
    t = jnp.arange(n_lat, dtype=jnp.int32)
    pos = jnp.stack([(t // GRID_W).astype(F32), (t % GRID_W).astype(F32)], axis=1)
    n_freq = ROPE_AXIS_DIM // 2
    inv = ROPE_THETA ** (-jnp.arange(n_freq, dtype=F32) / n_freq)
    ang = pos[:, :, None] * inv
    lane = jnp.arange(LANES, dtype=jnp.int32)
    axis = (lane % DA_HEAD_DIM) // ROPE_AXIS_DIM
    freq = lane % n_freq
    second_half = (lane % ROPE_AXIS_DIM) >= n_freq
    ang_l = ang[:, axis, freq]
    cos, sin = jnp.cos(ang_l), jnp.sin(ang_l)
    sin_a = jnp.where(second_half, 0.0, -sin)
    sin_b = jnp.where(second_half, sin, 0.0)
    pad = lambda a, v: jnp.concatenate([a, jnp.full((n_ctx, LANES), v, F32)], axis=0)
    return pad(cos, 1.0), pad(sin_a, 0.0), pad(sin_b, 0.0)


def kernel(x, c, ctx, c_ctx, w_ada, b_ada, g_mix, g_ffn, w_in, b_gate, w_a, lam, subln_g, w_b,
           rpb, w_c, w_out, w_up, conv_w, conv_b, w_down, g_final):
    bsz, n_lat, _ = x.shape
    n_ctx = ctx.shape[1]
    assert n_lat % TOK_TILE == 0 and n_ctx == TOK_TILE and n_lat % GRID_W == 0

    rope_tabs = _rope_tables(n_lat, n_ctx)
    cl, sl = _dft_tables(n_lat, (n_lat * FN_GROUP_DIM) ** -0.5)
    cc, sc = _dft_tables(n_ctx, (n_ctx * FN_GROUP_DIM) ** -0.5)
    cg, sg = _dft_tables(FN_GROUP_DIM, 1.0)

    n_mod_rows = 2 * SUBLANES * (-(-(bsz + 1) // (2 * SUBLANES)))
    cvec = jnp.zeros((n_mod_rows, D_MODEL), F32).at[:bsz].set(c).at[bsz].set(c_ctx)
    mods = _modulation(cvec, w_ada, b_ada).reshape(DEPTH, n_mod_rows, N_MOD, D_MODEL)

    na_bias = _natten_bias(rpb, n_lat // GRID_W)

    w_in, w_a, w_b, w_c, w_out, w_up, w_down = (
        w.astype(BF16) for w in (w_in, w_a, w_b, w_c, w_out, w_up, w_down))

    h = jnp.concatenate([x, ctx], axis=1)
    for l in range(DEPTH):
        with_ctx = l != DEPTH - 1
        lam_init = 0.8 - 0.6 * math.exp(-0.3 * l)
        mod = jnp.stack([mods[l, :bsz],
                         jnp.broadcast_to(mods[l, bsz], (bsz, N_MOD, D_MODEL))], axis=1)
        fa_in, qb, kb, vbt, qc, kc, vct, gates = _in_projection(
            h, mod, g_mix[l], w_in, l, b_gate[l], rope_tabs, n_lat)
        fa = _fourier_mix(fa_in, (cl, sl, cc, sc, cg, sg), n_lat, n_ctx, with_ctx)
        db = _diff_attention(qb, kb, vbt, lam[l], subln_g[l], lam_init, n_lat, n_ctx, with_ctx)
        nc = _neighbourhood_attention(qc, kc, vct, na_bias, l, n_lat, n_ctx, with_ctx)
        h = _merge(h, mod, fa, db, nc, gates, w_a, w_b, w_c, w_out, l, n_lat, with_ctx)
        h = _conv_ffn(h, mod, g_ffn[l], w_up, conv_w[l], conv_b[l], w_down, l, g_final, n_lat,
                      with_ctx, final_norm=not with_ctx)
    return h
```

```python
import functools
import math

import jax
import jax.numpy as jnp
from jax import lax
from jax.experimental import pallas as pl
from jax.experimental.pallas import tpu as pltpu

D_MODEL = 1024
DEPTH = 4
GRID_W = 64
FN_GROUPS = 4
FN_GROUP_DIM = 128
FN_WIDTH = FN_GROUPS * FN_GROUP_DIM
DA_HEADS = 4
DA_HEAD_DIM = 64
DA_V_DIM = 2 * DA_HEAD_DIM
NA_HEADS = 8
NA_HEAD_DIM = 64
NA_ROWS = 8
NA_COLS = 16
BRANCH_W = 512
N_BRANCH = 3
ROPE_THETA = 10000.0
ROPE_AXIS_DIM = DA_HEAD_DIM // 2
D_FF = 2816
N_MOD = 6
NORM_EPS = 1e-6
SUBLN_EPS = 1e-5
NEG_INF = -1e30

LANES = 128
SUBLANES = 8
TOK_TILE = 256
FF_CHUNK = 256
FF_LOOKAHEAD = 2
NA_STEP_ROWS = 2
NA_GROUPS = 2
NA_SLAB_ROWS = 10
VMEM_LIMIT = 56 * 1024 * 1024
ONES_ROWS = 16
PV_CHUNK = 256
LOG2_E = math.log2(math.e)

BF16 = jnp.bfloat16
F32 = jnp.float32


def _params(n_grid_dims):
    return pltpu.CompilerParams(dimension_semantics=("arbitrary",) * n_grid_dims,
                                vmem_limit_bytes=VMEM_LIMIT)


def _resident(shape):
    return pl.BlockSpec(shape, lambda *_: (0,) * len(shape), pipeline_mode=pl.Buffered(1))


def _resident_layer(shape, layer):
    return pl.BlockSpec((None,) + tuple(shape), lambda *_: (layer,) + (0,) * len(shape),
                        pipeline_mode=pl.Buffered(1))


def _nt_dot(a, b):
    return lax.dot_general(a, b, (((1,), (1,)), ((), ())), preferred_element_type=F32)


def _dot(a, b):
    return jnp.dot(a, b, preferred_element_type=F32)


def _norm_modulate(x, g, shift, scale):
    ms = jnp.mean(x * x, axis=-1, keepdims=True)
    return (x * lax.rsqrt(ms + NORM_EPS) * g) * (1.0 + scale) + shift


def _split_bf16(v):
    hi = v.astype(BF16)
    lo = (v - hi.astype(F32)).astype(BF16)
    return hi, lo


def _mod_kernel(c_ref, w_ref, b_ref, o_ref):
    c = c_ref[...]
    s = c / (1.0 + jnp.exp(-c))
    s_hi, s_lo = _split_bf16(s)
    w_hi, w_lo = _split_bf16(w_ref[0])
    acc = _dot(s_hi, w_hi) + (_dot(s_hi, w_lo) + _dot(s_lo, w_hi))
    o_ref[0] = acc + b_ref[0]


def _modulation(cvec, w_ada, b_ada):
    n_rows = cvec.shape[0]
    n_out = N_MOD * D_MODEL
    tn = 1536
    return pl.pallas_call(
        _mod_kernel,
        grid=(DEPTH, n_out // tn),
        in_specs=[
            pl.BlockSpec((n_rows, D_MODEL), lambda l, j: (0, 0)),
            pl.BlockSpec((1, D_MODEL, tn), lambda l, j: (l, 0, j)),
            pl.BlockSpec((1, 1, tn), lambda l, j: (l, 0, j)),
        ],
        out_specs=pl.BlockSpec((1, n_rows, tn), lambda l, j: (l, 0, j)),
        out_shape=jax.ShapeDtypeStruct((DEPTH, n_rows, n_out), F32),
        compiler_params=_params(2),
        name="adaln_modulation",
    )(cvec, w_ada, b_ada.reshape(DEPTH, 1, n_out))


def _rope(p, cos, sin_a, sin_b):
    outs = []
    for k in range(p.shape[1] // LANES):
        xs = p[:, k * LANES:(k + 1) * LANES]
        from_hi = pltpu.roll(xs, LANES - ROPE_AXIS_DIM // 2, axis=1)
        from_lo = pltpu.roll(xs, ROPE_AXIS_DIM // 2, axis=1)
        outs.append(xs * cos + from_hi * sin_a + from_lo * sin_b)
    return jnp.concatenate(outs, axis=1)


def _inproj_kernel(x_ref, mod_ref, g_ref, w_ref, bg_ref, cos_ref, sa_ref, sb_ref,
                   fa_ref, qb_ref, kb_ref, vbt_ref, qc_ref, kc_ref, vct_ref, gate_ref):
    a = _norm_modulate(x_ref[0], g_ref[...], mod_ref[0:1, :], mod_ref[1:2, :]).astype(BF16)
    cos, sin_a, sin_b = cos_ref[...], sa_ref[...], sb_ref[...]

    def proj(seg):
        return _dot(a, w_ref[:, seg * BRANCH_W:(seg + 1) * BRANCH_W])

    fa_ref[0] = proj(0).astype(BF16)
    qb_ref[0] = (_rope(proj(1), cos, sin_a, sin_b) * (DA_HEAD_DIM ** -0.5 * LOG2_E)).astype(BF16)
    kb_ref[0] = _rope(proj(2), cos, sin_a, sin_b).astype(BF16)
    vbt_ref[0] = proj(3).T.astype(BF16)
    qc_ref[0] = (proj(4) * (NA_HEAD_DIM ** -0.5 * LOG2_E)).astype(BF16)
    kc_ref[0] = proj(5).astype(BF16)
    vct_ref[0] = proj(6).T.astype(BF16)
    for j in range(N_BRANCH * D_MODEL // BRANCH_W):
        z = proj(7 + j) + bg_ref[:, j * BRANCH_W:(j + 1) * BRANCH_W]
        gate_ref[0, :, j * BRANCH_W:(j + 1) * BRANCH_W] = (1.0 / (1.0 + jnp.exp(-z))).astype(BF16)


def _in_projection(h, mod, g_mix, w_in, layer, b_gate, rope_tabs, n_lat):
    bsz, n_tok, _ = h.shape
    n_tiles = n_tok // TOK_TILE
    lat_tiles = n_lat // TOK_TILE
    proj_w = w_in.shape[-1]
    tok = lambda w: pl.BlockSpec((1, TOK_TILE, w), lambda b, t: (b, t, 0))
    tab = pl.BlockSpec((TOK_TILE, LANES), lambda b, t: (t, 0))
    branch = jax.ShapeDtypeStruct((bsz, n_tok, BRANCH_W), BF16)
    tok_t = pl.BlockSpec((1, BRANCH_W, TOK_TILE), lambda b, t: (b, 0, t))
    branch_t = jax.ShapeDtypeStruct((bsz, BRANCH_W, n_tok), BF16)
    return pl.pallas_call(
        _inproj_kernel,
        grid=(bsz, n_tiles),
        in_specs=[
            tok(D_MODEL),
            pl.BlockSpec((None, None, N_MOD, D_MODEL), lambda b, t: (b, t // lat_tiles, 0, 0)),
            _resident((1, D_MODEL)),
            _resident_layer((D_MODEL, proj_w), layer),
            _resident((1, N_BRANCH * D_MODEL)),
            tab, tab, tab,
        ],
        out_specs=[tok(BRANCH_W)] * 3 + [tok_t] + [tok(BRANCH_W)] * 2 + [tok_t]
                  + [tok(N_BRANCH * D_MODEL)],
        out_shape=[branch] * 3 + [branch_t] + [branch] * 2 + [branch_t]
                  + [jax.ShapeDtypeStruct((bsz, n_tok, N_BRANCH * D_MODEL), BF16)],
        compiler_params=_params(2),
        name="in_projection",
    )(h, mod, g_mix.reshape(1, D_MODEL), w_in, b_gate.reshape(1, -1), *rope_tabs)


def _dft_kernel(u_ref, cl_ref, sl_ref, cc_ref, sc_ref, csg_ref, rev_ref, o_ref, ue_ref, uo_ref,
                *, n_lat, n_ctx, scale):
    t = pl.program_id(1)
    lat_tiles = n_lat // TOK_TILE
    half = n_lat // 2

    def finish(p, q):
        for g in range(FN_GROUPS):
            sl = slice(g * FN_GROUP_DIM, (g + 1) * FN_GROUP_DIM)
            pq = jnp.concatenate([p[:, sl], q[:, sl]], axis=1).astype(BF16)
            o_ref[0, :, sl] = _dot(pq, csg_ref[...]).astype(BF16)

    @pl.when(t == 0)
    def _():
        for b in range(half // TOK_TILE):
            hi = n_lat - (b + 1) * TOK_TILE
            first = u_ref[0, hi:hi + TOK_TILE, :]
            wrap = (u_ref[0, hi + TOK_TILE:hi + 2 * TOK_TILE, :] if b > 0
                    else jnp.zeros((TOK_TILE, FN_WIDTH), BF16))
            rev = _dot(rev_ref[...], jnp.concatenate([first, wrap], axis=0))
            lo = u_ref[0, b * TOK_TILE:(b + 1) * TOK_TILE, :].astype(F32)
            ue_ref[b * TOK_TILE:(b + 1) * TOK_TILE, :] = (lo + rev).astype(BF16)
            uo_ref[b * TOK_TILE:(b + 1) * TOK_TILE, :] = (lo - rev).astype(BF16)

    @pl.when(t < lat_tiles)
    def _():
        k = t * TOK_TILE + lax.broadcasted_iota(jnp.int32, (TOK_TILE, 1), 0)
        sign = (1 - 2 * (k & 1)).astype(F32) * scale
        p = _dot(cl_ref[...], ue_ref[...]) + sign * u_ref[0, half:half + 1, :].astype(F32)
        finish(p, _dot(sl_ref[...], uo_ref[...]))

    @pl.when(t >= lat_tiles)
    def _():
        u = u_ref[0, n_lat:n_lat + n_ctx, :]
        finish(_dot(cc_ref[...], u), _dot(sc_ref[...], u))


def _fourier_mix(fa_in, tabs, n_lat, n_ctx, with_ctx):
    bsz, n_tok, _ = fa_in.shape
    lat_tiles = n_lat // TOK_TILE
    n_tiles = n_tok // TOK_TILE if with_ctx else lat_tiles
    half = n_lat // 2
    assert half % TOK_TILE == 0
    cl, sl, cc, sc, csg, rev = tabs
    lat_tab = pl.BlockSpec((TOK_TILE, half), lambda b, t: (jnp.minimum(t, lat_tiles - 1), 0))
    return pl.pallas_call(
        functools.partial(_dft_kernel, n_lat=n_lat, n_ctx=n_ctx,
                          scale=(n_lat * FN_GROUP_DIM) ** -0.5),
        grid=(bsz, n_tiles),
        in_specs=[
            pl.BlockSpec((1, n_tok, FN_WIDTH), lambda b, t: (b, 0, 0)),
            lat_tab, lat_tab,
            _resident((n_ctx, n_ctx)), _resident((n_ctx, n_ctx)),
            _resident((2 * FN_GROUP_DIM, FN_GROUP_DIM)),
            _resident((TOK_TILE, 2 * TOK_TILE)),
        ],
        out_specs=pl.BlockSpec((1, TOK_TILE, FN_WIDTH), lambda b, t: (b, t, 0)),
        out_shape=jax.ShapeDtypeStruct((bsz, n_tok, FN_WIDTH), BF16),
        scratch_shapes=[pltpu.VMEM((half, FN_WIDTH), BF16)] * 2,
        compiler_params=_params(2),
        name="fourier_mix",
    )(fa_in, cl, sl, cc, sc, csg, rev)


def _dft_tables(n, n_cols, scale):
    k = jnp.arange(n, dtype=jnp.int32)
    ang = ((k[:, None] * k[None, :n_cols]) % n).astype(F32) * (2.0 * math.pi / n)
    return (jnp.cos(ang) * scale).astype(BF16), (jnp.sin(ang) * scale).astype(BF16)


def _reversal_matrix():
    i = jnp.arange(TOK_TILE, dtype=jnp.int32)[:, None]
    j = jnp.arange(2 * TOK_TILE, dtype=jnp.int32)[None, :]
    src = jnp.where(i == 0, TOK_TILE, TOK_TILE - i)
    return (j == src).astype(BF16)


def _stack_sub_heads(q, first_half):
    zero = jnp.zeros_like(q)
    return jnp.concatenate([jnp.where(first_half, q, zero), jnp.where(first_half, zero, q)], axis=0)


def _store_scores(s_ref, row, s):
    n = s.shape[0]
    s_ref[row:row + n, :] = s
    return jnp.max(s.reshape(n // SUBLANES, SUBLANES, -1).max(axis=0), axis=0, keepdims=True)


def _softmax_times_values(s_ref, n_rows, m, values_t):
    acc = None
    for r in range(0, n_rows, PV_CHUNK):
        n = min(PV_CHUNK, n_rows - r)
        e = jnp.exp2(s_ref[r:r + n, :] - m).astype(BF16)
        v = values_t(r, n)
        part = _dot(jnp.concatenate([v, jnp.ones((ONES_ROWS, n), BF16)], axis=0), e)
        acc = part if acc is None else acc + part
    n_ch = acc.shape[0] - ONES_ROWS
    return acc[0:n_ch] * (1.0 / acc[n_ch:n_ch + 1])


def _diffattn_kernel(q_ref, k_ref, vt_ref, lam_ref, g_ref, o_ref, s0_ref, s1_ref,
                     *, n_lat, n_ctx, lam_init):
    t = pl.program_id(1)
    lat_tiles = n_lat // TOK_TILE
    lv = lam_ref[...]
    lam = (jnp.exp(jnp.sum(lv[0:1] * lv[1:2], axis=-1, keepdims=True))
           - jnp.exp(jnp.sum(lv[2:3] * lv[3:4], axis=-1, keepdims=True)) + lam_init)
    first_half = lax.broadcasted_iota(jnp.int32, (1, LANES), 1) < DA_HEAD_DIM
    s_refs = (s0_ref, s1_ref)

    def attend(k_lo, k_len):
        def scores(h):
            sl = slice(h * DA_V_DIM, (h + 1) * DA_V_DIM)
            s = _nt_dot(k_ref[0, k_lo:k_lo + k_len, sl],
                        _stack_sub_heads(q_ref[0, :, sl], first_half))
            return _store_scores(s_refs[h % 2], 0, s)

        m_next = scores(0)
        for h in range(DA_HEADS):
            sl = slice(h * DA_V_DIM, (h + 1) * DA_V_DIM)
            m = m_next
            if h + 1 < DA_HEADS:
                m_next = scores(h + 1)
            o12 = _softmax_times_values(
                s_refs[h % 2], k_len, m, lambda r, n: vt_ref[0, sl, k_lo + r:k_lo + r + n]).T
            o = o12[0:TOK_TILE] - lam * o12[TOK_TILE:2 * TOK_TILE]
            ms = jnp.mean(o * o, axis=-1, keepdims=True)
            o = o * lax.rsqrt(ms + SUBLN_EPS) * g_ref[...] * (1.0 - lam_init)
            o_ref[0, :, sl] = o.astype(BF16)

    @pl.when(t < lat_tiles)
    def _():
        attend(0, n_lat + n_ctx)

    @pl.when(t >= lat_tiles)
    def _():
        attend(n_lat, n_ctx)


def _diff_attention(qb, kb, vbt, lam_vec, subln_g, lam_init, n_lat, n_ctx, with_ctx):
    bsz, n_tok, _ = qb.shape
    n_tiles = (n_tok if with_ctx else n_lat) // TOK_TILE
    full = pl.BlockSpec((1, n_tok, BRANCH_W), lambda b, t: (b, 0, 0))
    full_t = pl.BlockSpec((1, BRANCH_W, n_tok), lambda b, t: (b, 0, 0))
    tile = pl.BlockSpec((1, TOK_TILE, BRANCH_W), lambda b, t: (b, t, 0))
    return pl.pallas_call(
        functools.partial(_diffattn_kernel, n_lat=n_lat, n_ctx=n_ctx, lam_init=lam_init),
        grid=(bsz, n_tiles),
        in_specs=[tile, full, full_t, _resident((4, DA_HEAD_DIM)), _resident((1, DA_V_DIM))],
        out_specs=tile,
        out_shape=jax.ShapeDtypeStruct((bsz, n_tok, BRANCH_W), BF16),
        scratch_shapes=[pltpu.VMEM((n_tok, 2 * TOK_TILE), F32)] * 2,
        compiler_params=_params(2),
        name="diff_attention",
    )(qb, kb, vbt, lam_vec, subln_g.reshape(1, DA_V_DIM))


def _natten_kernel(q_ref, k_ref, vt_ref, *refs, n_lat, n_ctx):
    bias_refs, o_ref, s_refs = refs[:NA_GROUPS], refs[NA_GROUPS], refs[NA_GROUPS + 1:]
    s_idx = pl.program_id(1)
    rows = n_lat // GRID_W
    kh = min(NA_ROWS, rows)
    n_slab = NA_SLAB_ROWS * GRID_W
    n_q = NA_STEP_ROWS * GRID_W
    n_pairs = NA_HEADS // 2
    first_half = lax.broadcasted_iota(jnp.int32, (1, LANES), 1) < NA_HEAD_DIM
    pairs = [slice(p * LANES, (p + 1) * LANES) for p in range(n_pairs)]

    def run(window):
        n_keys = n_ctx + (n_slab if window else 0)
        k_los, maxes = [], []
        for g in range(NA_GROUPS):
            first_row = jnp.clip(NA_STEP_ROWS * (NA_GROUPS * s_idx + g) - kh // 2, 0, rows - kh)
            k_lo = pl.multiple_of(first_row * GRID_W, LANES)
            k_los.append(k_lo)
            for p, sl in enumerate(pairs):
                s_ref = s_refs[g * n_pairs + p]
                qs = _stack_sub_heads(q_ref[0, g * n_q:(g + 1) * n_q, sl], first_half)
                m = _store_scores(s_ref, 0, _nt_dot(k_ref[0, n_lat:n_lat + n_ctx, sl], qs))
                if window:
                    s_win = _nt_dot(k_ref[0, pl.ds(k_lo, n_slab), sl], qs) + bias_refs[g][0, 0, p]
                    m = jnp.maximum(m, _store_scores(s_ref, n_ctx, s_win))
                maxes.append(m)
        for g in range(NA_GROUPS):
            for p, sl in enumerate(pairs):
                def values_t(r, n):
                    if r < n_ctx:
                        return vt_ref[0, sl, n_lat + r:n_lat + r + n]
                    return vt_ref[0, sl, pl.ds(pl.multiple_of(k_los[g] + (r - n_ctx), LANES), n)]

                i = g * n_pairs + p
                o2 = _softmax_times_values(s_refs[i], n_keys, maxes[i], values_t).T
                o_ref[0, g * n_q:(g + 1) * n_q, sl] = jnp.where(
                    first_half, o2[0:n_q], o2[n_q:2 * n_q]).astype(BF16)

    window_steps = rows // (NA_STEP_ROWS * NA_GROUPS)

    @pl.when(s_idx < window_steps)
    def _():
        run(True)

    @pl.when(s_idx >= window_steps)
    def _():
        run(False)


def _natten_step_classes(rows):
    kh = min(NA_ROWS, rows)
    classes, first_steps = [], []
    for s in range(rows // NA_STEP_ROWS):
        slab = min(max(NA_STEP_ROWS * s - kh // 2, 0), rows - kh)
        geom = tuple((r - slab, min(max(r - kh // 2, 0), rows - kh) - slab)
                     for r in range(NA_STEP_ROWS * s, NA_STEP_ROWS * (s + 1)))
        geom = geom + (min(NA_SLAB_ROWS, rows - slab),)
        if not classes or classes[-1] != geom:
            assert geom not in classes
            classes.append(geom)
            first_steps.append(s)
    return classes, first_steps


def _natten_bias_kernel(r_ref, o_ref, *, classes, kh):
    cls = pl.program_id(1)
    kc = lax.broadcasted_iota(jnp.int32, (GRID_W, LANES), 0)
    lane = lax.broadcasted_iota(jnp.int32, (GRID_W, LANES), 1)
    second = lane >= GRID_W
    c = jnp.where(second, lane - GRID_W, lane)
    c0 = jnp.clip(c - NA_COLS // 2, 0, GRID_W - NA_COLS)
    col_ok = jnp.logical_and(kc >= c0, kc < c0 + NA_COLS)
    neg = jnp.full((GRID_W, LANES), NEG_INF, F32)

    def fill(geom):
        for h in range(NA_HEADS):
            for kr in range(NA_SLAB_ROWS):
                halves = []
                for j, (q_off, w_off) in enumerate(geom[:-1]):
                    if w_off <= kr < w_off + kh and kr < geom[-1]:
                        dr = kr - q_off + NA_ROWS - 1
                        row = jnp.broadcast_to(r_ref[0, h, dr:dr + 1, :], (GRID_W, LANES))
                        halves.append(pltpu.roll(row, (j * GRID_W - (NA_COLS - 1)) % LANES, axis=1,
                                                 stride=1, stride_axis=0))
                    else:
                        halves.append(neg)
                val = jnp.where(col_ok, jnp.where(second, halves[1], halves[0]), neg)
                o_ref[0, 0, h // 2, kr * GRID_W:(kr + 1) * GRID_W,
                      (h % 2) * LANES:(h % 2 + 1) * LANES] = val

    for ci, geom in enumerate(classes):
        pl.when(cls == ci)(functools.partial(fill, geom))


def _natten_bias(rpb, rows):
    kh = min(NA_ROWS, rows)
    classes, _ = _natten_step_classes(rows)
    n_dr, n_dc = 2 * NA_ROWS - 1, 2 * NA_COLS - 1
    assert NA_STEP_ROWS == 2 and n_dc <= GRID_W
    r = jnp.pad(rpb[..., ::-1].astype(F32) * LOG2_E,
                ((0, 0), (0, 0), (0, 2 * SUBLANES - n_dr), (0, LANES - n_dc)))
    return pl.pallas_call(
        functools.partial(_natten_bias_kernel, classes=classes, kh=kh),
        grid=(DEPTH, len(classes)),
        in_specs=[pl.BlockSpec((1, NA_HEADS, 2 * SUBLANES, LANES), lambda l, k: (l, 0, 0, 0))],
        out_specs=pl.BlockSpec((1, 1, NA_HEADS // 2, NA_SLAB_ROWS * GRID_W, 2 * LANES),
                               lambda l, k: (l, k, 0, 0, 0)),
        out_shape=jax.ShapeDtypeStruct(
            (DEPTH, len(classes), NA_HEADS // 2, NA_SLAB_ROWS * GRID_W, 2 * LANES), F32),
        compiler_params=_params(2),
        name="natten_bias_tables",
    )(r)


def _neighbourhood_attention(qc, kc, vct, bias, layer, n_lat, n_ctx, with_ctx):
    bsz, n_tok, _ = qc.shape
    rows = n_lat // GRID_W
    kh = min(NA_ROWS, rows)
    n_q = NA_STEP_ROWS * GRID_W
    assert (kh // 2) % NA_STEP_ROWS == 0 and (rows - kh) % NA_STEP_ROWS == 0 and n_q == LANES
    assert (rows - kh + NA_SLAB_ROWS) * GRID_W <= n_tok and kh + NA_STEP_ROWS <= NA_SLAB_ROWS
    assert n_ctx % PV_CHUNK == 0 and (NA_SLAB_ROWS * GRID_W) % LANES == 0
    _, first_steps = _natten_step_classes(rows)
    step_q = NA_GROUPS * n_q
    assert n_lat % step_q == 0 and n_ctx % step_q == 0
    n_steps = (n_tok if with_ctx else n_lat) // step_q

    def bias_spec(g):
        def index(b, s):
            group = NA_GROUPS * s + g
            cls = sum((group >= f).astype(jnp.int32) for f in first_steps[1:])
            return (layer, cls, 0, 0, 0)
        return pl.BlockSpec((1, 1) + bias.shape[2:], index)

    full = pl.BlockSpec((1, n_tok, BRANCH_W), lambda b, s: (b, 0, 0))
    full_t = pl.BlockSpec((1, BRANCH_W, n_tok), lambda b, s: (b, 0, 0))
    tile = pl.BlockSpec((1, step_q, BRANCH_W), lambda b, s: (b, s, 0))
    return pl.pallas_call(
        functools.partial(_natten_kernel, n_lat=n_lat, n_ctx=n_ctx),
        grid=(bsz, n_steps),
        in_specs=[tile, full, full_t] + [bias_spec(g) for g in range(NA_GROUPS)],
        out_specs=tile,
        out_shape=jax.ShapeDtypeStruct((bsz, n_tok, BRANCH_W), BF16),
        scratch_shapes=[pltpu.VMEM((n_ctx + NA_SLAB_ROWS * GRID_W, 2 * n_q), F32)]
                       * (NA_GROUPS * NA_HEADS // 2),
        compiler_params=_params(2),
        name="neighbourhood_attention",
    )(qc, kc, vct, *([bias] * NA_GROUPS))


def _merge_kernel(x_ref, mod_ref, fa_ref, db_ref, nc_ref, gate_ref, wa_ref, wb_ref, wc_ref,
                  wo_ref, o_ref):
    def gate(j):
        return gate_ref[0, :, j * D_MODEL:(j + 1) * D_MODEL].astype(F32)

    y = (gate(0) * _dot(fa_ref[0], wa_ref[...]) + gate(1) * _dot(db_ref[0], wb_ref[...])
         + gate(2) * _dot(nc_ref[0], wc_ref[...]))
    y = _dot(y.astype(BF16), wo_ref[...])
    o_ref[0] = x_ref[0] + mod_ref[2:3, :] * y


def _merge(h, mod, fa, db, nc, gates, w_a, w_b, w_c, w_out, layer, n_lat, with_ctx):
    bsz, n_tok, _ = h.shape
    lat_tiles = n_lat // TOK_TILE
    n_tiles = (n_tok if with_ctx else n_lat) // TOK_TILE
    tok = lambda w: pl.BlockSpec((1, TOK_TILE, w), lambda b, t: (b, t, 0))
    return pl.pallas_call(
        _merge_kernel,
        grid=(bsz, n_tiles),
        in_specs=[
            tok(D_MODEL),
            pl.BlockSpec((None, None, N_MOD, D_MODEL), lambda b, t: (b, t // lat_tiles, 0, 0)),
            tok(BRANCH_W), tok(BRANCH_W), tok(BRANCH_W), tok(N_BRANCH * D_MODEL),
            _resident_layer((BRANCH_W, D_MODEL), layer), _resident_layer((BRANCH_W, D_MODEL), layer),
            _resident_layer((BRANCH_W, D_MODEL), layer), _resident_layer((D_MODEL, D_MODEL), layer),
        ],
        out_specs=tok(D_MODEL),
        out_shape=jax.ShapeDtypeStruct((bsz, n_tiles * TOK_TILE, D_MODEL), F32),
        compiler_params=_params(2),
        name="merge_out_projection",
    )(h, mod, fa, db, nc, gates, w_a, w_b, w_c, w_out)


def _convffn_kernel(x_ref, prev_ref, next_ref, mod_ref, g_ref, wu_ref, cw_ref, cb_ref, wd_ref,
                    gf_ref, o_ref, lhs_ref, *, lat_tiles, n_tiles, final_norm):
    t = pl.program_id(1)
    g = g_ref[...]
    shift, scale = mod_ref[3:4, :], mod_ref[4:5, :]
    x = x_ref[0]
    has_prev = jnp.logical_and(t != 0, t != lat_tiles)
    has_next = jnp.logical_and(t != lat_tiles - 1, t != n_tiles - 1)
    prev = _norm_modulate(prev_ref[0], g, shift, scale) * has_prev.astype(F32)
    nxt = _norm_modulate(next_ref[0], g, shift, scale) * has_next.astype(F32)
    lhs_ref[0:SUBLANES, :] = prev
    lhs_ref[SUBLANES:SUBLANES + TOK_TILE, :] = _norm_modulate(x, g, shift, scale)
    lhs_ref[SUBLANES + TOK_TILE:, :] = nxt
    lhs = lhs_ref[...].astype(BF16)
    n_rows = TOK_TILE + 2 * SUBLANES

    def conv(u, col):
        w = cw_ref[:, col:col + FF_CHUNK]
        before = pltpu.roll(u, 1, axis=0)
        after = pltpu.roll(u, n_rows - 1, axis=0)
        v = before * w[0:1] + u * w[1:2] + after * w[2:3] + cb_ref[:, col:col + FF_CHUNK]
        return v[SUBLANES:SUBLANES + TOK_TILE]

    def up(j):
        col_a, col_b = j * FF_CHUNK, D_FF + j * FF_CHUNK
        return (_dot(lhs, wu_ref[:, col_a:col_a + FF_CHUNK]),
                _dot(lhs, wu_ref[:, col_b:col_b + FF_CHUNK]))

    n_chunks = D_FF // FF_CHUNK
    acc = jnp.zeros((TOK_TILE, D_MODEL), F32)
    ahead = [up(j) for j in range(FF_LOOKAHEAD)]
    for j in range(n_chunks):
        col_a, col_b = j * FF_CHUNK, D_FF + j * FF_CHUNK
        u_a, u_b = ahead.pop(0)
        if j + FF_LOOKAHEAD < n_chunks:
            ahead.append(up(j + FF_LOOKAHEAD))
        a = conv(u_a, col_a)
        b = conv(u_b, col_b)
        act = (a / (1.0 + jnp.exp(-a))) * b
        acc = acc + _dot(act.astype(BF16), wd_ref[col_a:col_a + FF_CHUNK, :])
    y = x + mod_ref[5:6, :] * acc
    if final_norm:
        ms = jnp.mean(y * y, axis=-1, keepdims=True)
        y = y * lax.rsqrt(ms + NORM_EPS) * gf_ref[...]
    o_ref[0] = y


def _conv_ffn(h, mod, g_ffn, w_up, conv_w, conv_b, w_down, layer, g_final, n_lat, with_ctx,
              final_norm):
    bsz, n_rows_in, _ = h.shape
    lat_tiles = n_lat // TOK_TILE
    n_tiles = n_rows_in // TOK_TILE if with_ctx else lat_tiles
    blocks_per_tile = TOK_TILE // SUBLANES
    last_block = n_rows_in // SUBLANES - 1
    return pl.pallas_call(
        functools.partial(_convffn_kernel, lat_tiles=lat_tiles, n_tiles=n_tiles,
                          final_norm=final_norm),
        grid=(bsz, n_tiles),
        in_specs=[
            pl.BlockSpec((1, TOK_TILE, D_MODEL), lambda b, t: (b, t, 0)),
            pl.BlockSpec((1, SUBLANES, D_MODEL),
                         lambda b, t: (b, jnp.maximum(t * blocks_per_tile - 1, 0), 0)),
            pl.BlockSpec((1, SUBLANES, D_MODEL),
                         lambda b, t: (b, jnp.minimum((t + 1) * blocks_per_tile, last_block), 0)),
            pl.BlockSpec((None, None, N_MOD, D_MODEL), lambda b, t: (b, t // lat_tiles, 0, 0)),
            _resident((1, D_MODEL)),
            _resident_layer((D_MODEL, 2 * D_FF), layer),
            _resident((3, 2 * D_FF)),
            _resident((1, 2 * D_FF)),
            _resident_layer((D_FF, D_MODEL), layer),
            _resident((1, D_MODEL)),
        ],
        out_specs=pl.BlockSpec((1, TOK_TILE, D_MODEL), lambda b, t: (b, t, 0)),
        out_shape=jax.ShapeDtypeStruct((bsz, n_tiles * TOK_TILE, D_MODEL), F32),
        scratch_shapes=[pltpu.VMEM((TOK_TILE + 2 * SUBLANES, D_MODEL), F32)],
        compiler_params=_params(2),
        name="conv_ffn",
    )(h, h, h, mod, g_ffn.reshape(1, D_MODEL), w_up, conv_w, conv_b.reshape(1, -1), w_down,
      g_final.reshape(1, D_MODEL))


def _rope_tables(n_lat, n_ctx):
    t = jnp.arange(n_lat, dtype=jnp.int32)
    pos = jnp.stack([(t // GRID_W).astype(F32), (t % GRID_W).astype(F32)], axis=1)
    n_freq = ROPE_AXIS_DIM // 2
    inv = ROPE_THETA ** (-jnp.arange(n_freq, dtype=F32) / n_freq)
    ang = pos[:, :, None] * inv
    lane = jnp.arange(LANES, dtype=jnp.int32)
    axis = (lane % DA_HEAD_DIM) // ROPE_AXIS_DIM
    freq = lane % n_freq
    second_half = (lane % ROPE_AXIS_DIM) >= n_freq
    ang_l = ang[:, axis, freq]
    cos, sin = jnp.cos(ang_l), jnp.sin(ang_l)
    sin_a = jnp.where(second_half, 0.0, -sin)
    sin_b = jnp.where(second_half, sin, 0.0)
    pad = lambda a, v: jnp.concatenate([a, jnp.full((n_ctx, LANES), v, F32)], axis=0)
    return pad(cos, 1.0), pad(sin_a, 0.0), pad(sin_b, 0.0)


def kernel(x, c, ctx, c_ctx, w_ada, b_ada, g_mix, g_ffn, w_in, b_gate, w_a, lam, subln_g, w_b,
           rpb, w_c, w_out, w_up, conv_w, conv_b, w_down, g_final):
    bsz, n_lat, _ = x.shape
    n_ctx = ctx.shape[1]
    assert n_lat % TOK_TILE == 0 and n_ctx == TOK_TILE and n_lat % GRID_W == 0

    rope_tabs = _rope_tables(n_lat, n_ctx)
    cl, sl = _dft_tables(n_lat, n_lat // 2, (n_lat * FN_GROUP_DIM) ** -0.5)
    cc, sc = _dft_tables(n_ctx, n_ctx, (n_ctx * FN_GROUP_DIM) ** -0.5)
    cg, sg = _dft_tables(FN_GROUP_DIM, FN_GROUP_DIM, 1.0)
    dft_tabs = (cl, sl, cc, sc, jnp.concatenate([cg, -sg], axis=0), _reversal_matrix())

    n_mod_rows = 2 * SUBLANES * (-(-(bsz + 1) // (2 * SUBLANES)))
    cvec = jnp.zeros((n_mod_rows, D_MODEL), F32).at[:bsz].set(c).at[bsz].set(c_ctx)
    mods = _modulation(cvec, w_ada, b_ada).reshape(DEPTH, n_mod_rows, N_MOD, D_MODEL)

    na_bias = _natten_bias(rpb, n_lat // GRID_W)

    w_in, w_a, w_b, w_c, w_out, w_up, w_down = (
        w.astype(BF16) for w in (w_in, w_a, w_b, w_c, w_out, w_up, w_down))

    h = jnp.concatenate([x, ctx], axis=1)
    for l in range(DEPTH):
        with_ctx = l != DEPTH - 1
        lam_init = 0.8 - 0.6 * math.exp(-0.3 * l)
        mod = jnp.stack([mods[l, :bsz],
                         jnp.broadcast_to(mods[l, bsz], (bsz, N_MOD, D_MODEL))], axis=1)
        fa_in, qb, kb, vbt, qc, kc, vct, gates = _in_projection(
            h, mod, g_mix[l], w_in, l, b_gate[l], rope_tabs, n_lat)
        fa = _fourier_mix(fa_in, dft_tabs, n_lat, n_ctx, with_ctx)
        db = _diff_attention(qb, kb, vbt, lam[l], subln_g[l], lam_init, n_lat, n_ctx, with_ctx)
        nc = _neighbourhood_attention(qc, kc, vct, na_bias, l, n_lat, n_ctx, with_ctx)
        h = _merge(h, mod, fa, db, nc, gates, w_a, w_b, w_c, w_out, l, n_lat, with_ctx)
        h = _conv_ffn(h, mod, g_ffn[l], w_up, conv_w[l], conv_b[l], w_down, l, g_final, n_lat,
                      with_ctx, final_norm=not with_ctx)
    return h
```

```python
import functools
import math

import jax
import jax.numpy as jnp
from jax import lax
from jax.experimental import pallas as pl
from jax.experimental.pallas import tpu as pltpu

D_MODEL = 1024
DEPTH = 4
GRID_W = 64
FN_GROUPS = 4
FN_GROUP_DIM = 128
FN_WIDTH = FN_GROUPS * FN_GROUP_DIM
DA_HEADS = 4
DA_HEAD_DIM = 64
DA_V_DIM = 2 * DA_HEAD_DIM
NA_HEADS = 8
NA_HEAD_DIM = 64
NA_ROWS = 8
NA_COLS = 16
BRANCH_W = 512
N_BRANCH = 3
ROPE_THETA = 10000.0
ROPE_AXIS_DIM = DA_HEAD_DIM // 2
D_FF = 2816
N_MOD = 6
NORM_EPS = 1e-6
SUBLN_EPS = 1e-5
NEG_INF = -1e30

LANES = 128
SUBLANES = 8
TOK_TILE = 256
FF_CHUNK = 256
FF_LOOKAHEAD = 2
NA_STEP_ROWS = 2
NA_GROUPS = 2
NA_SLAB_ROWS = 10
VMEM_LIMIT = 56 * 1024 * 1024
ONES_ROWS = 16
PV_CHUNK = 256
LOG2_E = math.log2(math.e)

BF16 = jnp.bfloat16
F32 = jnp.float32


def _params(n_grid_dims):
    return pltpu.CompilerParams(dimension_semantics=("arbitrary",) * n_grid_dims,
                                vmem_limit_bytes=VMEM_LIMIT)


def _resident(shape):
    return pl.BlockSpec(shape, lambda *_: (0,) * len(shape), pipeline_mode=pl.Buffered(1))


def _resident_layer(shape, layer):
    return pl.BlockSpec((None,) + tuple(shape), lambda *_: (layer,) + (0,) * len(shape),
                        pipeline_mode=pl.Buffered(1))


def _nt_dot(a, b):
    return lax.dot_general(a, b, (((1,), (1,)), ((), ())), preferred_element_type=F32)


def _dot(a, b):
    return jnp.dot(a, b, preferred_element_type=F32)


def _norm_modulate(x, g, shift, scale):
    ms = jnp.mean(x * x, axis=-1, keepdims=True)
    return (x * lax.rsqrt(ms + NORM_EPS) * g) * (1.0 + scale) + shift


def _split_bf16(v):
    hi = v.astype(BF16)
    lo = (v - hi.astype(F32)).astype(BF16)
    return hi, lo


def _mod_kernel(c_ref, w_ref, b_ref, o_ref):
    c = c_ref[...]
    s = c / (1.0 + jnp.exp(-c))
    s_hi, s_lo = _split_bf16(s)
    w_hi, w_lo = _split_bf16(w_ref[0])
    acc = _dot(s_hi, w_hi) + (_dot(s_hi, w_lo) + _dot(s_lo, w_hi))
    o_ref[0] = acc + b_ref[0]


def _modulation(cvec, w_ada, b_ada):
    n_rows = cvec.shape[0]
    n_out = N_MOD * D_MODEL
    tn = 1536
    return pl.pallas_call(
        _mod_kernel,
        grid=(DEPTH, n_out // tn),
        in_specs=[
            pl.BlockSpec((n_rows, D_MODEL), lambda l, j: (0, 0)),
            pl.BlockSpec((1, D_MODEL, tn), lambda l, j: (l, 0, j)),
            pl.BlockSpec((1, 1, tn), lambda l, j: (l, 0, j)),
        ],
        out_specs=pl.BlockSpec((1, n_rows, tn), lambda l, j: (l, 0, j)),
        out_shape=jax.ShapeDtypeStruct((DEPTH, n_rows, n_out), F32),
        compiler_params=_params(2),
        name="adaln_modulation",
    )(cvec, w_ada, b_ada.reshape(DEPTH, 1, n_out))


def _rope(p, cos, sin_a, sin_b):
    outs = []
    for k in range(p.shape[1] // LANES):
        xs = p[:, k * LANES:(k + 1) * LANES]
        from_hi = pltpu.roll(xs, LANES - ROPE_AXIS_DIM // 2, axis=1)
        from_lo = pltpu.roll(xs, ROPE_AXIS_DIM // 2, axis=1)
        outs.append(xs * cos + from_hi * sin_a + from_lo * sin_b)
    return jnp.concatenate(outs, axis=1)


def _inproj_kernel(x_ref, mod_ref, g_ref, w_ref, bg_ref, cos_ref, sa_ref, sb_ref,
                   fa_ref, qb_ref, kb_ref, vbt_ref, qc_ref, kc_ref, vct_ref, gate_ref):
    a = _norm_modulate(x_ref[0], g_ref[...], mod_ref[0:1, :], mod_ref[1:2, :]).astype(BF16)
    cos, sin_a, sin_b = cos_ref[...], sa_ref[...], sb_ref[...]

    def proj(seg):
        return _dot(a, w_ref[:, seg * BRANCH_W:(seg + 1) * BRANCH_W])

    fa_ref[0] = proj(0).astype(BF16)
    qb_ref[0] = (_rope(proj(1), cos, sin_a, sin_b) * (DA_HEAD_DIM ** -0.5 * LOG2_E)).astype(BF16)
    kb_ref[0] = _rope(proj(2), cos, sin_a, sin_b).astype(BF16)
    vbt_ref[0] = proj(3).T.astype(BF16)
    qc_ref[0] = (proj(4) * (NA_HEAD_DIM ** -0.5 * LOG2_E)).astype(BF16)
    kc_ref[0] = proj(5).astype(BF16)
    vct_ref[0] = proj(6).T.astype(BF16)
    for j in range(N_BRANCH * D_MODEL // BRANCH_W):
        z = proj(7 + j) + bg_ref[:, j * BRANCH_W:(j + 1) * BRANCH_W]
        gate_ref[0, :, j * BRANCH_W:(j + 1) * BRANCH_W] = (1.0 / (1.0 + jnp.exp(-z))).astype(BF16)


def _in_projection(h, mod, g_mix, w_in, layer, b_gate, rope_tabs, n_lat):
    bsz, n_tok, _ = h.shape
    n_tiles = n_tok // TOK_TILE
    lat_tiles = n_lat // TOK_TILE
    proj_w = w_in.shape[-1]
    tok = lambda w: pl.BlockSpec((1, TOK_TILE, w), lambda b, t: (b, t, 0))
    tab = pl.BlockSpec((TOK_TILE, LANES), lambda b, t: (t, 0))
    branch = jax.ShapeDtypeStruct((bsz, n_tok, BRANCH_W), BF16)
    tok_t = pl.BlockSpec((1, BRANCH_W, TOK_TILE), lambda b, t: (b, 0, t))
    branch_t = jax.ShapeDtypeStruct((bsz, BRANCH_W, n_tok), BF16)
    return pl.pallas_call(
        _inproj_kernel,
        grid=(bsz, n_tiles),
        in_specs=[
            tok(D_MODEL),
            pl.BlockSpec((None, None, N_MOD, D_MODEL), lambda b, t: (b, t // lat_tiles, 0, 0)),
            _resident((1, D_MODEL)),
            _resident_layer((D_MODEL, proj_w), layer),
            _resident((1, N_BRANCH * D_MODEL)),
            tab, tab, tab,
        ],
        out_specs=[tok(BRANCH_W)] * 3 + [tok_t] + [tok(BRANCH_W)] * 2 + [tok_t]
                  + [tok(N_BRANCH * D_MODEL)],
        out_shape=[branch] * 3 + [branch_t] + [branch] * 2 + [branch_t]
                  + [jax.ShapeDtypeStruct((bsz, n_tok, N_BRANCH * D_MODEL), BF16)],
        compiler_params=_params(2),
        name="in_projection",
    )(h, mod, g_mix.reshape(1, D_MODEL), w_in, b_gate.reshape(1, -1), *rope_tabs)


def _dft_kernel(u_ref, cl_ref, sl_ref, cc_ref, sc_ref, csg_ref, rev_ref, o_ref, ue_ref, uo_ref,
                *, n_lat, n_ctx, scale):
    t = pl.program_id(1)
    lat_tiles = n_lat // TOK_TILE
    half = n_lat // 2

    def finish(p, q):
        for g in range(FN_GROUPS):
            sl = slice(g * FN_GROUP_DIM, (g + 1) * FN_GROUP_DIM)
            pq = jnp.concatenate([p[:, sl], q[:, sl]], axis=1).astype(BF16)
            o_ref[0, :, sl] = _dot(pq, csg_ref[...]).astype(BF16)

    @pl.when(t == 0)
    def _():
        for b in range(half // TOK_TILE):
            hi = n_lat - (b + 1) * TOK_TILE
            first = u_ref[0, hi:hi + TOK_TILE, :]
            wrap = (u_ref[0, hi + TOK_TILE:hi + 2 * TOK_TILE, :] if b > 0
                    else jnp.zeros((TOK_TILE, FN_WIDTH), BF16))
            rev = _dot(rev_ref[...], jnp.concatenate([first, wrap], axis=0))
            lo = u_ref[0, b * TOK_TILE:(b + 1) * TOK_TILE, :].astype(F32)
            ue_ref[b * TOK_TILE:(b + 1) * TOK_TILE, :] = (lo + rev).astype(BF16)
            uo_ref[b * TOK_TILE:(b + 1) * TOK_TILE, :] = (lo - rev).astype(BF16)

    @pl.when(t < lat_tiles)
    def _():
        k = t * TOK_TILE + lax.broadcasted_iota(jnp.int32, (TOK_TILE, 1), 0)
        sign = (1 - 2 * (k & 1)).astype(F32) * scale
        rows = pl.ds(pl.multiple_of(t * TOK_TILE, TOK_TILE), TOK_TILE)
        p = _dot(cl_ref[rows, :], ue_ref[...]) + sign * u_ref[0, half:half + 1, :].astype(F32)
        finish(p, _dot(sl_ref[rows, :], uo_ref[...]))

    @pl.when(t >= lat_tiles)
    def _():
        u = u_ref[0, n_lat:n_lat + n_ctx, :]
        finish(_dot(cc_ref[...], u), _dot(sc_ref[...], u))


def _fourier_mix(fa_in, tabs, n_lat, n_ctx, with_ctx):
    bsz, n_tok, _ = fa_in.shape
    lat_tiles = n_lat // TOK_TILE
    n_tiles = n_tok // TOK_TILE if with_ctx else lat_tiles
    half = n_lat // 2
    assert half % TOK_TILE == 0
    cl, sl, cc, sc, csg, rev = tabs
    lat_tab = _resident((n_lat, half))
    return pl.pallas_call(
        functools.partial(_dft_kernel, n_lat=n_lat, n_ctx=n_ctx,
                          scale=(n_lat * FN_GROUP_DIM) ** -0.5),
        grid=(bsz, n_tiles),
        in_specs=[
            pl.BlockSpec((1, n_tok, FN_WIDTH), lambda b, t: (b, 0, 0)),
            lat_tab, lat_tab,
            _resident((n_ctx, n_ctx)), _resident((n_ctx, n_ctx)),
            _resident((2 * FN_GROUP_DIM, FN_GROUP_DIM)),
            _resident((TOK_TILE, 2 * TOK_TILE)),
        ],
        out_specs=pl.BlockSpec((1, TOK_TILE, FN_WIDTH), lambda b, t: (b, t, 0)),
        out_shape=jax.ShapeDtypeStruct((bsz, n_tok, FN_WIDTH), BF16),
        scratch_shapes=[pltpu.VMEM((half, FN_WIDTH), BF16)] * 2,
        compiler_params=_params(2),
        name="fourier_mix",
    )(fa_in, cl, sl, cc, sc, csg, rev)


def _dft_tables(n, n_cols, scale):
    k = jnp.arange(n, dtype=jnp.int32)
    ang = ((k[:, None] * k[None, :n_cols]) % n).astype(F32) * (2.0 * math.pi / n)
    return (jnp.cos(ang) * scale).astype(BF16), (jnp.sin(ang) * scale).astype(BF16)


def _reversal_matrix():
    i = jnp.arange(TOK_TILE, dtype=jnp.int32)[:, None]
    j = jnp.arange(2 * TOK_TILE, dtype=jnp.int32)[None, :]
    src = jnp.where(i == 0, TOK_TILE, TOK_TILE - i)
    return (j == src).astype(BF16)


def _stack_sub_heads(q, first_half):
    zero = jnp.zeros_like(q)
    return jnp.concatenate([jnp.where(first_half, q, zero), jnp.where(first_half, zero, q)], axis=0)


def _store_scores(s_ref, row, s):
    n = s.shape[0]
    s_ref[row:row + n, :] = s
    return jnp.max(s.reshape(n // SUBLANES, SUBLANES, -1).max(axis=0), axis=0, keepdims=True)


def _softmax_times_values(s_ref, n_rows, m, values_t):
    acc = None
    for r in range(0, n_rows, PV_CHUNK):
        n = min(PV_CHUNK, n_rows - r)
        e = jnp.exp2(s_ref[r:r + n, :] - m).astype(BF16)
        v = values_t(r, n)
        part = _dot(jnp.concatenate([v, jnp.ones((ONES_ROWS, n), BF16)], axis=0), e)
        acc = part if acc is None else acc + part
    n_ch = acc.shape[0] - ONES_ROWS
    return acc[0:n_ch] * (1.0 / acc[n_ch:n_ch + 1])


def _diffattn_kernel(q_ref, k_ref, vt_ref, lam_ref, g_ref, o_ref, s0_ref, s1_ref,
                     *, n_lat, n_ctx, lam_init):
    t = pl.program_id(1)
    lat_tiles = n_lat // TOK_TILE
    lv = lam_ref[...]
    lam = (jnp.exp(jnp.sum(lv[0:1] * lv[1:2], axis=-1, keepdims=True))
           - jnp.exp(jnp.sum(lv[2:3] * lv[3:4], axis=-1, keepdims=True)) + lam_init)
    first_half = lax.broadcasted_iota(jnp.int32, (1, LANES), 1) < DA_HEAD_DIM
    s_refs = (s0_ref, s1_ref)

    def attend(k_lo, k_len):
        def scores(h):
            sl = slice(h * DA_V_DIM, (h + 1) * DA_V_DIM)
            s = _nt_dot(k_ref[0, k_lo:k_lo + k_len, sl],
                        _stack_sub_heads(q_ref[0, :, sl], first_half))
            return _store_scores(s_refs[h % 2], 0, s)

        m_next = scores(0)
        for h in range(DA_HEADS):
            sl = slice(h * DA_V_DIM, (h + 1) * DA_V_DIM)
            m = m_next
            if h + 1 < DA_HEADS:
                m_next = scores(h + 1)
            o12 = _softmax_times_values(
                s_refs[h % 2], k_len, m, lambda r, n: vt_ref[0, sl, k_lo + r:k_lo + r + n]).T
            o = o12[0:TOK_TILE] - lam * o12[TOK_TILE:2 * TOK_TILE]
            ms = jnp.mean(o * o, axis=-1, keepdims=True)
            o = o * lax.rsqrt(ms + SUBLN_EPS) * g_ref[...] * (1.0 - lam_init)
            o_ref[0, :, sl] = o.astype(BF16)

    @pl.when(t < lat_tiles)
    def _():
        attend(0, n_lat + n_ctx)

    @pl.when(t >= lat_tiles)
    def _():
        attend(n_lat, n_ctx)


def _diff_attention(qb, kb, vbt, lam_vec, subln_g, lam_init, n_lat, n_ctx, with_ctx):
    bsz, n_tok, _ = qb.shape
    n_tiles = (n_tok if with_ctx else n_lat) // TOK_TILE
    full = pl.BlockSpec((1, n_tok, BRANCH_W), lambda b, t: (b, 0, 0))
    full_t = pl.BlockSpec((1, BRANCH_W, n_tok), lambda b, t: (b, 0, 0))
    tile = pl.BlockSpec((1, TOK_TILE, BRANCH_W), lambda b, t: (b, t, 0))
    return pl.pallas_call(
        functools.partial(_diffattn_kernel, n_lat=n_lat, n_ctx=n_ctx, lam_init=lam_init),
        grid=(bsz, n_tiles),
        in_specs=[tile, full, full_t, _resident((4, DA_HEAD_DIM)), _resident((1, DA_V_DIM))],
        out_specs=tile,
        out_shape=jax.ShapeDtypeStruct((bsz, n_tok, BRANCH_W), BF16),
        scratch_shapes=[pltpu.VMEM((n_tok, 2 * TOK_TILE), F32)] * 2,
        compiler_params=_params(2),
        name="diff_attention",
    )(qb, kb, vbt, lam_vec, subln_g.reshape(1, DA_V_DIM))


def _natten_kernel(q_ref, k_ref, vt_ref, *refs, n_lat, n_ctx):
    bias_refs, o_ref, s_refs = refs[:NA_GROUPS], refs[NA_GROUPS], refs[NA_GROUPS + 1:]
    s_idx = pl.program_id(1)
    rows = n_lat // GRID_W
    kh = min(NA_ROWS, rows)
    n_slab = NA_SLAB_ROWS * GRID_W
    n_q = NA_STEP_ROWS * GRID_W
    n_pairs = NA_HEADS // 2
    first_half = lax.broadcasted_iota(jnp.int32, (1, LANES), 1) < NA_HEAD_DIM
    pairs = [slice(p * LANES, (p + 1) * LANES) for p in range(n_pairs)]

    def run(window):
        n_keys = n_ctx + (n_slab if window else 0)
        k_los, maxes = [], []
        for g in range(NA_GROUPS):
            first_row = jnp.clip(NA_STEP_ROWS * (NA_GROUPS * s_idx + g) - kh // 2, 0, rows - kh)
            k_lo = pl.multiple_of(first_row * GRID_W, LANES)
            k_los.append(k_lo)
            for p, sl in enumerate(pairs):
                s_ref = s_refs[g * n_pairs + p]
                qs = _stack_sub_heads(q_ref[0, g * n_q:(g + 1) * n_q, sl], first_half)
                m = _store_scores(s_ref, 0, _nt_dot(k_ref[0, n_lat:n_lat + n_ctx, sl], qs))
                if window:
                    s_win = _nt_dot(k_ref[0, pl.ds(k_lo, n_slab), sl], qs) + bias_refs[g][0, 0, p]
                    m = jnp.maximum(m, _store_scores(s_ref, n_ctx, s_win))
                maxes.append(m)
        for g in range(NA_GROUPS):
            for p, sl in enumerate(pairs):
                def values_t(r, n):
                    if r < n_ctx:
                        return vt_ref[0, sl, n_lat + r:n_lat + r + n]
                    return vt_ref[0, sl, pl.ds(pl.multiple_of(k_los[g] + (r - n_ctx), LANES), n)]

                i = g * n_pairs + p
                o2 = _softmax_times_values(s_refs[i], n_keys, maxes[i], values_t).T
                o_ref[0, g * n_q:(g + 1) * n_q, sl] = jnp.where(
                    first_half, o2[0:n_q], o2[n_q:2 * n_q]).astype(BF16)

    window_steps = rows // (NA_STEP_ROWS * NA_GROUPS)

    @pl.when(s_idx < window_steps)
    def _():
        run(True)

    @pl.when(s_idx >= window_steps)
    def _():
        run(False)


def _natten_step_classes(rows):
    kh = min(NA_ROWS, rows)
    classes, first_steps = [], []
    for s in range(rows // NA_STEP_ROWS):
        slab = min(max(NA_STEP_ROWS * s - kh // 2, 0), rows - kh)
        geom = tuple((r - slab, min(max(r - kh // 2, 0), rows - kh) - slab)
                     for r in range(NA_STEP_ROWS * s, NA_STEP_ROWS * (s + 1)))
        geom = geom + (min(NA_SLAB_ROWS, rows - slab),)
        if not classes or classes[-1] != geom:
            assert geom not in classes
            classes.append(geom)
            first_steps.append(s)
    return classes, first_steps


def _natten_bias_kernel(r_ref, o_ref, *, classes, kh):
    cls = pl.program_id(1)
    kc = lax.broadcasted_iota(jnp.int32, (GRID_W, LANES), 0)
    lane = lax.broadcasted_iota(jnp.int32, (GRID_W, LANES), 1)
    second = lane >= GRID_W
    c = jnp.where(second, lane - GRID_W, lane)
    c0 = jnp.clip(c - NA_COLS // 2, 0, GRID_W - NA_COLS)
    col_ok = jnp.logical_and(kc >= c0, kc < c0 + NA_COLS)
    neg = jnp.full((GRID_W, LANES), NEG_INF, F32)

    def fill(geom):
        for h in range(NA_HEADS):
            for kr in range(NA_SLAB_ROWS):
                halves = []
                for j, (q_off, w_off) in enumerate(geom[:-1]):
                    if w_off <= kr < w_off + kh and kr < geom[-1]:
                        dr = kr - q_off + NA_ROWS - 1
                        row = jnp.broadcast_to(r_ref[0, h, dr:dr + 1, :], (GRID_W, LANES))
                        halves.append(pltpu.roll(row, (j * GRID_W - (NA_COLS - 1)) % LANES, axis=1,
                                                 stride=1, stride_axis=0))
                    else:
                        halves.append(neg)
                val = jnp.where(col_ok, jnp.where(second, halves[1], halves[0]), neg)
                o_ref[0, 0, h // 2, kr * GRID_W:(kr + 1) * GRID_W,
                      (h % 2) * LANES:(h % 2 + 1) * LANES] = val

    for ci, geom in enumerate(classes):
        pl.when(cls == ci)(functools.partial(fill, geom))


def _natten_bias(rpb, rows):
    kh = min(NA_ROWS, rows)
    classes, _ = _natten_step_classes(rows)
    n_dr, n_dc = 2 * NA_ROWS - 1, 2 * NA_COLS - 1
    assert NA_STEP_ROWS == 2 and n_dc <= GRID_W
    r = jnp.pad(rpb[..., ::-1].astype(F32) * LOG2_E,
                ((0, 0), (0, 0), (0, 2 * SUBLANES - n_dr), (0, LANES - n_dc)))
    return pl.pallas_call(
        functools.partial(_natten_bias_kernel, classes=classes, kh=kh),
        grid=(DEPTH, len(classes)),
        in_specs=[pl.BlockSpec((1, NA_HEADS, 2 * SUBLANES, LANES), lambda l, k: (l, 0, 0, 0))],
        out_specs=pl.BlockSpec((1, 1, NA_HEADS // 2, NA_SLAB_ROWS * GRID_W, 2 * LANES),
                               lambda l, k: (l, k, 0, 0, 0)),
        out_shape=jax.ShapeDtypeStruct(
            (DEPTH, len(classes), NA_HEADS // 2, NA_SLAB_ROWS * GRID_W, 2 * LANES), F32),
        compiler_params=_params(2),
        name="natten_bias_tables",
    )(r)


def _neighbourhood_attention(qc, kc, vct, bias, layer, n_lat, n_ctx, with_ctx):
    bsz, n_tok, _ = qc.shape
    rows = n_lat // GRID_W
    kh = min(NA_ROWS, rows)
    n_q = NA_STEP_ROWS * GRID_W
    assert (kh // 2) % NA_STEP_ROWS == 0 and (rows - kh) % NA_STEP_ROWS == 0 and n_q == LANES
    assert (rows - kh + NA_SLAB_ROWS) * GRID_W <= n_tok and kh + NA_STEP_ROWS <= NA_SLAB_ROWS
    assert n_ctx % PV_CHUNK == 0 and (NA_SLAB_ROWS * GRID_W) % LANES == 0
    _, first_steps = _natten_step_classes(rows)
    step_q = NA_GROUPS * n_q
    assert n_lat % step_q == 0 and n_ctx % step_q == 0
    n_steps = (n_tok if with_ctx else n_lat) // step_q

    def bias_spec(g):
        def index(b, s):
            group = NA_GROUPS * s + g
            cls = sum((group >= f).astype(jnp.int32) for f in first_steps[1:])
            return (layer, cls, 0, 0, 0)
        return pl.BlockSpec((1, 1) + bias.shape[2:], index)

    full = pl.BlockSpec((1, n_tok, BRANCH_W), lambda b, s: (b, 0, 0))
    full_t = pl.BlockSpec((1, BRANCH_W, n_tok), lambda b, s: (b, 0, 0))
    tile = pl.BlockSpec((1, step_q, BRANCH_W), lambda b, s: (b, s, 0))
    return pl.pallas_call(
        functools.partial(_natten_kernel, n_lat=n_lat, n_ctx=n_ctx),
        grid=(bsz, n_steps),
        in_specs=[tile, full, full_t] + [bias_spec(g) for g in range(NA_GROUPS)],
        out_specs=tile,
        out_shape=jax.ShapeDtypeStruct((bsz, n_tok, BRANCH_W), BF16),
        scratch_shapes=[pltpu.VMEM((n_ctx + NA_SLAB_ROWS * GRID_W, 2 * n_q), F32)]
                       * (NA_GROUPS * NA_HEADS // 2),
        compiler_params=_params(2),
        name="neighbourhood_attention",
    )(qc, kc, vct, *([bias] * NA_GROUPS))


def _merge_kernel(x_ref, mod_ref, fa_ref, db_ref, nc_ref, gate_ref, wa_ref, wb_ref, wc_ref,
                  wo_ref, o_ref):
    def gate(j):
        return gate_ref[0, :, j * D_MODEL:(j + 1) * D_MODEL].astype(F32)

    y = (gate(0) * _dot(fa_ref[0], wa_ref[...]) + gate(1) * _dot(db_ref[0], wb_ref[...])
         + gate(2) * _dot(nc_ref[0], wc_ref[...]))
    y = _dot(y.astype(BF16), wo_ref[...])
    o_ref[0] = x_ref[0] + mod_ref[2:3, :] * y


def _merge(h, mod, fa, db, nc, gates, w_a, w_b, w_c, w_out, layer, n_lat, with_ctx):
    bsz, n_tok, _ = h.shape
    lat_tiles = n_lat // TOK_TILE
    n_tiles = (n_tok if with_ctx else n_lat) // TOK_TILE
    tok = lambda w: pl.BlockSpec((1, TOK_TILE, w), lambda b, t: (b, t, 0))
    return pl.pallas_call(
        _merge_kernel,
        grid=(bsz, n_tiles),
        in_specs=[
            tok(D_MODEL),
            pl.BlockSpec((None, None, N_MOD, D_MODEL), lambda b, t: (b, t // lat_tiles, 0, 0)),
            tok(BRANCH_W), tok(BRANCH_W), tok(BRANCH_W), tok(N_BRANCH * D_MODEL),
            _resident_layer((BRANCH_W, D_MODEL), layer), _resident_layer((BRANCH_W, D_MODEL), layer),
            _resident_layer((BRANCH_W, D_MODEL), layer), _resident_layer((D_MODEL, D_MODEL), layer),
        ],
        out_specs=tok(D_MODEL),
        out_shape=jax.ShapeDtypeStruct((bsz, n_tiles * TOK_TILE, D_MODEL), F32),
        compiler_params=_params(2),
        name="merge_out_projection",
    )(h, mod, fa, db, nc, gates, w_a, w_b, w_c, w_out)


def _convffn_kernel(x_ref, prev_ref, next_ref, mod_ref, g_ref, wu_ref, cw_ref, cb_ref, wd_ref,
                    gf_ref, o_ref, lhs_ref, *, lat_tiles, n_tiles, final_norm):
    t = pl.program_id(1)
    g = g_ref[...]
    shift, scale = mod_ref[3:4, :], mod_ref[4:5, :]
    x = x_ref[0]
    has_prev = jnp.logical_and(t != 0, t != lat_tiles)
    has_next = jnp.logical_and(t != lat_tiles - 1, t != n_tiles - 1)
    prev = _norm_modulate(prev_ref[0], g, shift, scale) * has_prev.astype(F32)
    nxt = _norm_modulate(next_ref[0], g, shift, scale) * has_next.astype(F32)
    lhs_ref[0:SUBLANES, :] = prev
    lhs_ref[SUBLANES:SUBLANES + TOK_TILE, :] = _norm_modulate(x, g, shift, scale)
    lhs_ref[SUBLANES + TOK_TILE:, :] = nxt
    lhs = lhs_ref[...].astype(BF16)
    n_rows = TOK_TILE + 2 * SUBLANES

    def conv(u, col):
        w = cw_ref[:, col:col + FF_CHUNK]
        before = pltpu.roll(u, 1, axis=0)
        after = pltpu.roll(u, n_rows - 1, axis=0)
        v = before * w[0:1] + u * w[1:2] + after * w[2:3] + cb_ref[:, col:col + FF_CHUNK]
        return v[SUBLANES:SUBLANES + TOK_TILE]

    def up(j):
        col_a, col_b = j * FF_CHUNK, D_FF + j * FF_CHUNK
        return (_dot(lhs, wu_ref[:, col_a:col_a + FF_CHUNK]),
                _dot(lhs, wu_ref[:, col_b:col_b + FF_CHUNK]))

    n_chunks = D_FF // FF_CHUNK
    acc = jnp.zeros((TOK_TILE, D_MODEL), F32)
    ahead = [up(j) for j in range(FF_LOOKAHEAD)]
    for j in range(n_chunks):
        col_a, col_b = j * FF_CHUNK, D_FF + j * FF_CHUNK
        u_a, u_b = ahead.pop(0)
        if j + FF_LOOKAHEAD < n_chunks:
            ahead.append(up(j + FF_LOOKAHEAD))
        a = conv(u_a, col_a)
        b = conv(u_b, col_b)
        act = (a / (1.0 + jnp.exp(-a))) * b
        acc = acc + _dot(act.astype(BF16), wd_ref[col_a:col_a + FF_CHUNK, :])
    y = x + mod_ref[5:6, :] * acc
    if final_norm:
        ms = jnp.mean(y * y, axis=-1, keepdims=True)
        y = y * lax.rsqrt(ms + NORM_EPS) * gf_ref[...]
    o_ref[0] = y


def _conv_ffn(h, mod, g_ffn, w_up, conv_w, conv_b, w_down, layer, g_final, n_lat, with_ctx,
              final_norm):
    bsz, n_rows_in, _ = h.shape
    lat_tiles = n_lat // TOK_TILE
    n_tiles = n_rows_in // TOK_TILE if with_ctx else lat_tiles
    blocks_per_tile = TOK_TILE // SUBLANES
    last_block = n_rows_in // SUBLANES - 1
    return pl.pallas_call(
        functools.partial(_convffn_kernel, lat_tiles=lat_tiles, n_tiles=n_tiles,
                          final_norm=final_norm),
        grid=(bsz, n_tiles),
        in_specs=[
            pl.BlockSpec((1, TOK_TILE, D_MODEL), lambda b, t: (b, t, 0)),
            pl.BlockSpec((1, SUBLANES, D_MODEL),
                         lambda b, t: (b, jnp.maximum(t * blocks_per_tile - 1, 0), 0)),
            pl.BlockSpec((1, SUBLANES, D_MODEL),
                         lambda b, t: (b, jnp.minimum((t + 1) * blocks_per_tile, last_block), 0)),
            pl.BlockSpec((None, None, N_MOD, D_MODEL), lambda b, t: (b, t // lat_tiles, 0, 0)),
            _resident((1, D_MODEL)),
            _resident_layer((D_MODEL, 2 * D_FF), layer),
            _resident((3, 2 * D_FF)),
            _resident((1, 2 * D_FF)),
            _resident_layer((D_FF, D_MODEL), layer),
            _resident((1, D_MODEL)),
        ],
        out_specs=pl.BlockSpec((1, TOK_TILE, D_MODEL), lambda b, t: (b, t, 0)),
        out_shape=jax.ShapeDtypeStruct((bsz, n_tiles * TOK_TILE, D_MODEL), F32),
        scratch_shapes=[pltpu.VMEM((TOK_TILE + 2 * SUBLANES, D_MODEL), F32)],
        compiler_params=_params(2),
        name="conv_ffn",
    )(h, h, h, mod, g_ffn.reshape(1, D_MODEL), w_up, conv_w, conv_b.reshape(1, -1), w_down,
      g_final.reshape(1, D_MODEL))


def _rope_tables(n_lat, n_ctx):
    t = jnp.arange(n_lat, dtype=jnp.int32)
    pos = jnp.stack([(t // GRID_W).astype(F32), (t % GRID_W).astype(F32)], axis=1)
    n_freq = ROPE_AXIS_DIM // 2
    inv = ROPE_THETA ** (-jnp.arange(n_freq, dtype=F32) / n_freq)
    ang = pos[:, :, None] * inv
    lane = jnp.arange(LANES, dtype=jnp.int32)
    axis = (lane % DA_HEAD_DIM) // ROPE_AXIS_DIM
    freq = lane % n_freq
    second_half = (lane % ROPE_AXIS_DIM) >= n_freq
    ang_l = ang[:, axis, freq]
    cos, sin = jnp.cos(ang_l), jnp.sin(ang_l)
    sin_a = jnp.where(second_half, 0.0, -sin)
    sin_b = jnp.where(second_half, sin, 0.0)
    pad = lambda a, v: jnp.concatenate([a, jnp.full((n_ctx, LANES), v, F32)], axis=0)
    return pad(cos, 1.0), pad(sin_a, 0.0), pad(sin_b, 0.0)


def kernel(x, c, ctx, c_ctx, w_ada, b_ada, g_mix, g_ffn, w_in, b_gate, w_a, lam, subln_g, w_b,
           rpb, w_c, w_out, w_up, conv_w, conv_b, w_down, g_final):
    bsz, n_lat, _ = x.shape
    n_ctx = ctx.shape[1]
    assert n_lat % TOK_TILE == 0 and n_ctx == TOK_TILE and n_lat % GRID_W == 0

    rope_tabs = _rope_tables(n_lat, n_ctx)
    cl, sl = _dft_tables(n_lat, n_lat // 2, (n_lat * FN_GROUP_DIM) ** -0.5)
    cc, sc = _dft_tables(n_ctx, n_ctx, (n_ctx * FN_GROUP_DIM) ** -0.5)
    cg, sg = _dft_tables(FN_GROUP_DIM, FN_GROUP_DIM, 1.0)
    dft_tabs = (cl, sl, cc, sc, jnp.concatenate([cg, -sg], axis=0), _reversal_matrix())

    n_mod_rows = 2 * SUBLANES * (-(-(bsz + 1) // (2 * SUBLANES)))
    cvec = jnp.zeros((n_mod_rows, D_MODEL), F32).at[:bsz].set(c).at[bsz].set(c_ctx)
    mods = _modulation(cvec, w_ada, b_ada).reshape(DEPTH, n_mod_rows, N_MOD, D_MODEL)

    na_bias = _natten_bias(rpb, n_lat // GRID_W)

    w_in, w_a, w_b, w_c, w_out, w_up, w_down = (
        w.astype(BF16) for w in (w_in, w_a, w_b, w_c, w_out, w_up, w_down))

    h = jnp.concatenate([x, ctx], axis=1)
    for l in range(DEPTH):
        with_ctx = l != DEPTH - 1
        lam_init = 0.8 - 0.6 * math.exp(-0.3 * l)
        mod = jnp.stack([mods[l, :bsz],
                         jnp.broadcast_to(mods[l, bsz], (bsz, N_MOD, D_MODEL))], axis=1)
        fa_in, qb, kb, vbt, qc, kc, vct, gates = _in_projection(
            h, mod, g_mix[l], w_in, l, b_gate[l], rope_tabs, n_lat)
        fa = _fourier_mix(fa_in, dft_tabs, n_lat, n_ctx, with_ctx)
        db = _diff_attention(qb, kb, vbt, lam[l], subln_g[l], lam_init, n_lat, n_ctx, with_ctx)
        nc = _neighbourhood_attention(qc, kc, vct, na_bias, l, n_lat, n_ctx, with_ctx)
        h = _merge(h, mod, fa, db, nc, gates, w_a, w_b, w_c, w_out, l, n_lat, with_ctx)
        h = _conv_ffn(h, mod, g_ffn[l], w_up, conv_w[l], conv_b[l], w_down, l, g_final, n_lat,
                      with_ctx, final_norm=not with_ctx)
    return h
```

```python
import functools
import math

import jax
import jax.numpy as jnp
from jax import lax
from jax.experimental import pallas as pl
from jax.experimental.pallas import tpu as pltpu

D_MODEL = 1024
DEPTH = 4
GRID_W = 64
FN_GROUPS = 4
FN_GROUP_DIM = 128
FN_WIDTH = FN_GROUPS * FN_GROUP_DIM
DA_HEADS = 4
DA_HEAD_DIM = 64
DA_V_DIM = 2 * DA_HEAD_DIM
NA_HEADS = 8
NA_HEAD_DIM = 64
NA_ROWS = 8
NA_COLS = 16
BRANCH_W = 512
N_BRANCH = 3
ROPE_THETA = 10000.0
ROPE_AXIS_DIM = DA_HEAD_DIM // 2
D_FF = 2816
N_MOD = 6
NORM_EPS = 1e-6
SUBLN_EPS = 1e-5
NEG_INF = -1e30

LANES = 128
SUBLANES = 8
TOK_TILE = 256
SUB_TILES = 2
STEP_ROWS = SUB_TILES * TOK_TILE
FF_CHUNK = 256
FF_LOOKAHEAD = 2
NA_STEP_ROWS = 2
NA_GROUPS = 2
NA_SLAB_ROWS = 10
VMEM_LIMIT = 56 * 1024 * 1024
ONES_ROWS = 16
PV_CHUNK = 256
LOG2_E = math.log2(math.e)

BF16 = jnp.bfloat16
F32 = jnp.float32


def _params(n_grid_dims):
    return pltpu.CompilerParams(dimension_semantics=("arbitrary",) * n_grid_dims,
                                vmem_limit_bytes=VMEM_LIMIT)


def _resident(shape):
    return pl.BlockSpec(shape, lambda *_: (0,) * len(shape), pipeline_mode=pl.Buffered(1))


def _resident_layer(shape, layer):
    return pl.BlockSpec((None,) + tuple(shape), lambda *_: (layer,) + (0,) * len(shape),
                        pipeline_mode=pl.Buffered(1))


def _for_sub_tiles(t, lat_subs, n_subs, body, ctx_body=None):
    full_steps = lat_subs // SUB_TILES
    if n_subs == lat_subs:
        for s in range(SUB_TILES):
            body(s)
        return

    @pl.when(t < full_steps)
    def _():
        for s in range(SUB_TILES):
            body(s)

    @pl.when(t >= full_steps)
    def _():
        for s in range(n_subs - lat_subs):
            (ctx_body or body)(s)


def _nt_dot(a, b):
    return lax.dot_general(a, b, (((1,), (1,)), ((), ())), preferred_element_type=F32)


def _dot(a, b):
    return jnp.dot(a, b, preferred_element_type=F32)


def _norm_modulate(x, g, shift, scale):
    ms = jnp.mean(x * x, axis=-1, keepdims=True)
    return (x * lax.rsqrt(ms + NORM_EPS) * g) * (1.0 + scale) + shift


def _split_bf16(v):
    hi = v.astype(BF16)
    lo = (v - hi.astype(F32)).astype(BF16)
    return hi, lo


def _mod_kernel(c_ref, w_ref, b_ref, o_ref):
    c = c_ref[...]
    s = c / (1.0 + jnp.exp(-c))
    s_hi, s_lo = _split_bf16(s)
    w_hi, w_lo = _split_bf16(w_ref[0])
    acc = _dot(s_hi, w_hi) + (_dot(s_hi, w_lo) + _dot(s_lo, w_hi))
    o_ref[0] = acc + b_ref[0]


def _modulation(cvec, w_ada, b_ada):
    n_rows = cvec.shape[0]
    n_out = N_MOD * D_MODEL
    tn = 1536
    return pl.pallas_call(
        _mod_kernel,
        grid=(DEPTH, n_out // tn),
        in_specs=[
            pl.BlockSpec((n_rows, D_MODEL), lambda l, j: (0, 0)),
            pl.BlockSpec((1, D_MODEL, tn), lambda l, j: (l, 0, j)),
            pl.BlockSpec((1, 1, tn), lambda l, j: (l, 0, j)),
        ],
        out_specs=pl.BlockSpec((1, n_rows, tn), lambda l, j: (l, 0, j)),
        out_shape=jax.ShapeDtypeStruct((DEPTH, n_rows, n_out), F32),
        compiler_params=_params(2),
        name="adaln_modulation",
    )(cvec, w_ada, b_ada.reshape(DEPTH, 1, n_out))


def _rope(p, cos, sin_a, sin_b):
    outs = []
    for k in range(p.shape[1] // LANES):
        xs = p[:, k * LANES:(k + 1) * LANES]
        from_hi = pltpu.roll(xs, LANES - ROPE_AXIS_DIM // 2, axis=1)
        from_lo = pltpu.roll(xs, ROPE_AXIS_DIM // 2, axis=1)
        outs.append(xs * cos + from_hi * sin_a + from_lo * sin_b)
    return jnp.concatenate(outs, axis=1)


def _inproj_kernel(x_ref, mod_ref, g_ref, w_ref, bg_ref, cos_ref, sa_ref, sb_ref,
                   fa_ref, qb_ref, kb_ref, vbt_ref, qc_ref, kc_ref, vct_ref, gate_ref,
                   *, lat_subs, n_subs):
    def sub_tile(s):
        rows = slice(s * TOK_TILE, (s + 1) * TOK_TILE)
        a = _norm_modulate(x_ref[0, rows, :], g_ref[...], mod_ref[0:1, :],
                           mod_ref[1:2, :]).astype(BF16)
        cos, sin_a, sin_b = cos_ref[rows, :], sa_ref[rows, :], sb_ref[rows, :]

        def proj(seg):
            return _dot(a, w_ref[:, seg * BRANCH_W:(seg + 1) * BRANCH_W])

        fa_ref[0, rows, :] = proj(0).astype(BF16)
        qb_ref[0, rows, :] = (_rope(proj(1), cos, sin_a, sin_b)
                              * (DA_HEAD_DIM ** -0.5 * LOG2_E)).astype(BF16)
        kb_ref[0, rows, :] = _rope(proj(2), cos, sin_a, sin_b).astype(BF16)
        vbt_ref[0, :, rows] = proj(3).T.astype(BF16)
        qc_ref[0, rows, :] = (proj(4) * (NA_HEAD_DIM ** -0.5 * LOG2_E)).astype(BF16)
        kc_ref[0, rows, :] = proj(5).astype(BF16)
        vct_ref[0, :, rows] = proj(6).T.astype(BF16)
        for j in range(N_BRANCH * D_MODEL // BRANCH_W):
            cols = slice(j * BRANCH_W, (j + 1) * BRANCH_W)
            z = proj(7 + j) + bg_ref[:, cols]
            gate_ref[0, rows, cols] = (1.0 / (1.0 + jnp.exp(-z))).astype(BF16)

    _for_sub_tiles(pl.program_id(1), lat_subs, n_subs, sub_tile)


def _in_projection(h, mod, g_mix, w_in, layer, b_gate, rope_tabs, n_lat):
    bsz, n_tok, _ = h.shape
    full_steps = n_lat // STEP_ROWS
    proj_w = w_in.shape[-1]
    tok = lambda w: pl.BlockSpec((1, STEP_ROWS, w), lambda b, t: (b, t, 0))
    tab = pl.BlockSpec((STEP_ROWS, LANES), lambda b, t: (t, 0))
    branch = jax.ShapeDtypeStruct((bsz, n_tok, BRANCH_W), BF16)
    tok_t = pl.BlockSpec((1, BRANCH_W, STEP_ROWS), lambda b, t: (b, 0, t))
    branch_t = jax.ShapeDtypeStruct((bsz, BRANCH_W, n_tok), BF16)
    return pl.pallas_call(
        functools.partial(_inproj_kernel, lat_subs=n_lat // TOK_TILE, n_subs=n_tok // TOK_TILE),
        grid=(bsz, pl.cdiv(n_tok, STEP_ROWS)),
        in_specs=[
            tok(D_MODEL),
            pl.BlockSpec((None, None, N_MOD, D_MODEL), lambda b, t: (b, t // full_steps, 0, 0)),
            _resident((1, D_MODEL)),
            _resident_layer((D_MODEL, proj_w), layer),
            _resident((1, N_BRANCH * D_MODEL)),
            tab, tab, tab,
        ],
        out_specs=[tok(BRANCH_W)] * 3 + [tok_t] + [tok(BRANCH_W)] * 2 + [tok_t]
                  + [tok(N_BRANCH * D_MODEL)],
        out_shape=[branch] * 3 + [branch_t] + [branch] * 2 + [branch_t]
                  + [jax.ShapeDtypeStruct((bsz, n_tok, N_BRANCH * D_MODEL), BF16)],
        compiler_params=_params(2),
        name="in_projection",
    )(h, mod, g_mix.reshape(1, D_MODEL), w_in, b_gate.reshape(1, -1), *rope_tabs)


def _dft_kernel(u_ref, cl_ref, sl_ref, cc_ref, sc_ref, csg_ref, rev_ref, o_ref, ue_ref, uo_ref,
                *, n_lat, n_ctx, n_subs, scale):
    t = pl.program_id(1)
    half = n_lat // 2

    def finish(s, p, q):
        rows = slice(s * TOK_TILE, (s + 1) * TOK_TILE)
        for g in range(FN_GROUPS):
            sl = slice(g * FN_GROUP_DIM, (g + 1) * FN_GROUP_DIM)
            pq = jnp.concatenate([p[:, sl], q[:, sl]], axis=1).astype(BF16)
            o_ref[0, rows, sl] = _dot(pq, csg_ref[...]).astype(BF16)

    @pl.when(t == 0)
    def _():
        for b in range(half // TOK_TILE):
            hi = n_lat - (b + 1) * TOK_TILE
            first = u_ref[0, hi:hi + TOK_TILE, :]
            wrap = (u_ref[0, hi + TOK_TILE:hi + 2 * TOK_TILE, :] if b > 0
                    else jnp.zeros((TOK_TILE, FN_WIDTH), BF16))
            rev = _dot(rev_ref[...], jnp.concatenate([first, wrap], axis=0))
            lo = u_ref[0, b * TOK_TILE:(b + 1) * TOK_TILE, :].astype(F32)
            ue_ref[b * TOK_TILE:(b + 1) * TOK_TILE, :] = (lo + rev).astype(BF16)
            uo_ref[b * TOK_TILE:(b + 1) * TOK_TILE, :] = (lo - rev).astype(BF16)

    def latent(s):
        k0 = pl.multiple_of((t * SUB_TILES + s) * TOK_TILE, TOK_TILE)
        k = k0 + lax.broadcasted_iota(jnp.int32, (TOK_TILE, 1), 0)
        sign = (1 - 2 * (k & 1)).astype(F32) * scale
        rows = pl.ds(k0, TOK_TILE)
        p = _dot(cl_ref[rows, :], ue_ref[...]) + sign * u_ref[0, half:half + 1, :].astype(F32)
        finish(s, p, _dot(sl_ref[rows, :], uo_ref[...]))

    def context(s):
        u = u_ref[0, n_lat:n_lat + n_ctx, :]
        finish(s, _dot(cc_ref[...], u), _dot(sc_ref[...], u))

    _for_sub_tiles(t, n_lat // TOK_TILE, n_subs, latent, context)


def _fourier_mix(fa_in, tabs, n_lat, n_ctx, with_ctx):
    bsz, n_tok, _ = fa_in.shape
    n_rows = n_tok if with_ctx else n_lat
    half = n_lat // 2
    assert half % TOK_TILE == 0
    cl, sl, cc, sc, csg, rev = tabs
    lat_tab = _resident((n_lat, half))
    return pl.pallas_call(
        functools.partial(_dft_kernel, n_lat=n_lat, n_ctx=n_ctx, n_subs=n_rows // TOK_TILE,
                          scale=(n_lat * FN_GROUP_DIM) ** -0.5),
        grid=(bsz, pl.cdiv(n_rows, STEP_ROWS)),
        in_specs=[
            pl.BlockSpec((1, n_tok, FN_WIDTH), lambda b, t: (b, 0, 0)),
            lat_tab, lat_tab,
            _resident((n_ctx, n_ctx)), _resident((n_ctx, n_ctx)),
            _resident((2 * FN_GROUP_DIM, FN_GROUP_DIM)),
            _resident((TOK_TILE, 2 * TOK_TILE)),
        ],
        out_specs=pl.BlockSpec((1, STEP_ROWS, FN_WIDTH), lambda b, t: (b, t, 0)),
        out_shape=jax.ShapeDtypeStruct((bsz, n_tok, FN_WIDTH), BF16),
        scratch_shapes=[pltpu.VMEM((half, FN_WIDTH), BF16)] * 2,
        compiler_params=_params(2),
        name="fourier_mix",
    )(fa_in, cl, sl, cc, sc, csg, rev)


def _dft_tables(n, n_cols, scale):
    k = jnp.arange(n, dtype=jnp.int32)
    ang = ((k[:, None] * k[None, :n_cols]) % n).astype(F32) * (2.0 * math.pi / n)
    return (jnp.cos(ang) * scale).astype(BF16), (jnp.sin(ang) * scale).astype(BF16)


def _reversal_matrix():
    i = jnp.arange(TOK_TILE, dtype=jnp.int32)[:, None]
    j = jnp.arange(2 * TOK_TILE, dtype=jnp.int32)[None, :]
    src = jnp.where(i == 0, TOK_TILE, TOK_TILE - i)
    return (j == src).astype(BF16)


def _stack_sub_heads(q, first_half):
    zero = jnp.zeros_like(q)
    return jnp.concatenate([jnp.where(first_half, q, zero), jnp.where(first_half, zero, q)], axis=0)


def _store_scores(s_ref, row, s):
    n = s.shape[0]
    s_ref[row:row + n, :] = s
    return jnp.max(s.reshape(n // SUBLANES, SUBLANES, -1).max(axis=0), axis=0, keepdims=True)


def _softmax_times_values(s_ref, n_rows, m, values_t):
    acc = None
    for r in range(0, n_rows, PV_CHUNK):
        n = min(PV_CHUNK, n_rows - r)
        e = jnp.exp2(s_ref[r:r + n, :] - m).astype(BF16)
        v = values_t(r, n)
        part = _dot(jnp.concatenate([v, jnp.ones((ONES_ROWS, n), BF16)], axis=0), e)
        acc = part if acc is None else acc + part
    n_ch = acc.shape[0] - ONES_ROWS
    return acc[0:n_ch] * (1.0 / acc[n_ch:n_ch + 1])


def _diffattn_kernel(q_ref, k_ref, vt_ref, lam_ref, g_ref, o_ref, s0_ref, s1_ref,
                     *, n_lat, n_ctx, with_ctx, lam_init):
    t = pl.program_id(1)
    lv = lam_ref[...]
    lam = (jnp.exp(jnp.sum(lv[0:1] * lv[1:2], axis=-1, keepdims=True))
           - jnp.exp(jnp.sum(lv[2:3] * lv[3:4], axis=-1, keepdims=True)) + lam_init)
    first_half = lax.broadcasted_iota(jnp.int32, (1, LANES), 1) < DA_HEAD_DIM
    s_refs = (s0_ref, s1_ref)

    def attend(n_sub_tiles, k_lo, k_len):
        items = [(s, h) for s in range(n_sub_tiles) for h in range(DA_HEADS)]

        def scores(i):
            s, h = items[i]
            sl = slice(h * DA_V_DIM, (h + 1) * DA_V_DIM)
            q = q_ref[0, s * TOK_TILE:(s + 1) * TOK_TILE, sl]
            sc = _nt_dot(k_ref[0, k_lo:k_lo + k_len, sl], _stack_sub_heads(q, first_half))
            return _store_scores(s_refs[i % 2], 0, sc)

        m_next = scores(0)
        for i, (s, h) in enumerate(items):
            sl = slice(h * DA_V_DIM, (h + 1) * DA_V_DIM)
            m = m_next
            if i + 1 < len(items):
                m_next = scores(i + 1)
            o12 = _softmax_times_values(
                s_refs[i % 2], k_len, m, lambda r, n: vt_ref[0, sl, k_lo + r:k_lo + r + n]).T
            o = o12[0:TOK_TILE] - lam * o12[TOK_TILE:2 * TOK_TILE]
            ms = jnp.mean(o * o, axis=-1, keepdims=True)
            o = o * lax.rsqrt(ms + SUBLN_EPS) * g_ref[...] * (1.0 - lam_init)
            o_ref[0, s * TOK_TILE:(s + 1) * TOK_TILE, sl] = o.astype(BF16)

    if not with_ctx:
        attend(SUB_TILES, 0, n_lat + n_ctx)
        return

    @pl.when(t < n_lat // STEP_ROWS)
    def _():
        attend(SUB_TILES, 0, n_lat + n_ctx)

    @pl.when(t >= n_lat // STEP_ROWS)
    def _():
        attend(n_ctx // TOK_TILE, n_lat, n_ctx)


def _diff_attention(qb, kb, vbt, lam_vec, subln_g, lam_init, n_lat, n_ctx, with_ctx):
    bsz, n_tok, _ = qb.shape
    n_rows = n_tok if with_ctx else n_lat
    full = pl.BlockSpec((1, n_tok, BRANCH_W), lambda b, t: (b, 0, 0))
    full_t = pl.BlockSpec((1, BRANCH_W, n_tok), lambda b, t: (b, 0, 0))
    tile = pl.BlockSpec((1, STEP_ROWS, BRANCH_W), lambda b, t: (b, t, 0))
    return pl.pallas_call(
        functools.partial(_diffattn_kernel, n_lat=n_lat, n_ctx=n_ctx, with_ctx=with_ctx,
                          lam_init=lam_init),
        grid=(bsz, pl.cdiv(n_rows, STEP_ROWS)),
        in_specs=[tile, full, full_t, _resident((4, DA_HEAD_DIM)), _resident((1, DA_V_DIM))],
        out_specs=tile,
        out_shape=jax.ShapeDtypeStruct((bsz, n_tok, BRANCH_W), BF16),
        scratch_shapes=[pltpu.VMEM((n_tok, 2 * TOK_TILE), F32)] * 2,
        compiler_params=_params(2),
        name="diff_attention",
    )(qb, kb, vbt, lam_vec, subln_g.reshape(1, DA_V_DIM))


def _natten_kernel(q_ref, k_ref, vt_ref, *refs, n_lat, n_ctx):
    bias_refs, o_ref, s_refs = refs[:NA_GROUPS], refs[NA_GROUPS], refs[NA_GROUPS + 1:]
    s_idx = pl.program_id(1)
    rows = n_lat // GRID_W
    kh = min(NA_ROWS, rows)
    n_slab = NA_SLAB_ROWS * GRID_W
    n_q = NA_STEP_ROWS * GRID_W
    n_pairs = NA_HEADS // 2
    first_half = lax.broadcasted_iota(jnp.int32, (1, LANES), 1) < NA_HEAD_DIM
    pairs = [slice(p * LANES, (p + 1) * LANES) for p in range(n_pairs)]

    def run(window):
        n_keys = n_ctx + (n_slab if window else 0)
        k_los, maxes = [], []
        for g in range(NA_GROUPS):
            first_row = jnp.clip(NA_STEP_ROWS * (NA_GROUPS * s_idx + g) - kh // 2, 0, rows - kh)
            k_lo = pl.multiple_of(first_row * GRID_W, LANES)
            k_los.append(k_lo)
            for p, sl in enumerate(pairs):
                s_ref = s_refs[g * n_pairs + p]
                qs = _stack_sub_heads(q_ref[0, g * n_q:(g + 1) * n_q, sl], first_half)
                m = _store_scores(s_ref, 0, _nt_dot(k_ref[0, n_lat:n_lat + n_ctx, sl], qs))
                if window:
                    s_win = _nt_dot(k_ref[0, pl.ds(k_lo, n_slab), sl], qs) + bias_refs[g][0, 0, p]
                    m = jnp.maximum(m, _store_scores(s_ref, n_ctx, s_win))
                maxes.append(m)
        for g in range(NA_GROUPS):
            for p, sl in enumerate(pairs):
                def values_t(r, n):
                    if r < n_ctx:
                        return vt_ref[0, sl, n_lat + r:n_lat + r + n]
                    return vt_ref[0, sl, pl.ds(pl.multiple_of(k_los[g] + (r - n_ctx), LANES), n)]

                i = g * n_pairs + p
                o2 = _softmax_times_values(s_refs[i], n_keys, maxes[i], values_t).T
                o_ref[0, g * n_q:(g + 1) * n_q, sl] = jnp.where(
                    first_half, o2[0:n_q], o2[n_q:2 * n_q]).astype(BF16)

    window_steps = rows // (NA_STEP_ROWS * NA_GROUPS)

    @pl.when(s_idx < window_steps)
    def _():
        run(True)

    @pl.when(s_idx >= window_steps)
    def _():
        run(False)


def _natten_step_classes(rows):
    kh = min(NA_ROWS, rows)
    classes, first_steps = [], []
    for s in range(rows // NA_STEP_ROWS):
        slab = min(max(NA_STEP_ROWS * s - kh // 2, 0), rows - kh)
        geom = tuple((r - slab, min(max(r - kh // 2, 0), rows - kh) - slab)
                     for r in range(NA_STEP_ROWS * s, NA_STEP_ROWS * (s + 1)))
        geom = geom + (min(NA_SLAB_ROWS, rows - slab),)
        if not classes or classes[-1] != geom:
            assert geom not in classes
            classes.append(geom)
            first_steps.append(s)
    return classes, first_steps


def _natten_bias_kernel(r_ref, o_ref, *, classes, kh):
    cls = pl.program_id(1)
    kc = lax.broadcasted_iota(jnp.int32, (GRID_W, LANES), 0)
    lane = lax.broadcasted_iota(jnp.int32, (GRID_W, LANES), 1)
    second = lane >= GRID_W
    c = jnp.where(second, lane - GRID_W, lane)
    c0 = jnp.clip(c - NA_COLS // 2, 0, GRID_W - NA_COLS)
    col_ok = jnp.logical_and(kc >= c0, kc < c0 + NA_COLS)
    neg = jnp.full((GRID_W, LANES), NEG_INF, F32)

    def fill(geom):
        for h in range(NA_HEADS):
            for kr in range(NA_SLAB_ROWS):
                halves = []
                for j, (q_off, w_off) in enumerate(geom[:-1]):
                    if w_off <= kr < w_off + kh and kr < geom[-1]:
                        dr = kr - q_off + NA_ROWS - 1
                        row = jnp.broadcast_to(r_ref[0, h, dr:dr + 1, :], (GRID_W, LANES))
                        halves.append(pltpu.roll(row, (j * GRID_W - (NA_COLS - 1)) % LANES, axis=1,
                                                 stride=1, stride_axis=0))
                    else:
                        halves.append(neg)
                val = jnp.where(col_ok, jnp.where(second, halves[1], halves[0]), neg)
                o_ref[0, 0, h // 2, kr * GRID_W:(kr + 1) * GRID_W,
                      (h % 2) * LANES:(h % 2 + 1) * LANES] = val

    for ci, geom in enumerate(classes):
        pl.when(cls == ci)(functools.partial(fill, geom))


def _natten_bias(rpb, rows):
    kh = min(NA_ROWS, rows)
    classes, _ = _natten_step_classes(rows)
    n_dr, n_dc = 2 * NA_ROWS - 1, 2 * NA_COLS - 1
    assert NA_STEP_ROWS == 2 and n_dc <= GRID_W
    r = jnp.pad(rpb[..., ::-1].astype(F32) * LOG2_E,
                ((0, 0), (0, 0), (0, 2 * SUBLANES - n_dr), (0, LANES - n_dc)))
    return pl.pallas_call(
        functools.partial(_natten_bias_kernel, classes=classes, kh=kh),
        grid=(DEPTH, len(classes)),
        in_specs=[pl.BlockSpec((1, NA_HEADS, 2 * SUBLANES, LANES), lambda l, k: (l, 0, 0, 0))],
        out_specs=pl.BlockSpec((1, 1, NA_HEADS // 2, NA_SLAB_ROWS * GRID_W, 2 * LANES),
                               lambda l, k: (l, k, 0, 0, 0)),
        out_shape=jax.ShapeDtypeStruct(
            (DEPTH, len(classes), NA_HEADS // 2, NA_SLAB_ROWS * GRID_W, 2 * LANES), F32),
        compiler_params=_params(2),
        name="natten_bias_tables",
    )(r)


def _neighbourhood_attention(qc, kc, vct, bias, layer, n_lat, n_ctx, with_ctx):
    bsz, n_tok, _ = qc.shape
    rows = n_lat // GRID_W
    kh = min(NA_ROWS, rows)
    n_q = NA_STEP_ROWS * GRID_W
    assert (kh // 2) % NA_STEP_ROWS == 0 and (rows - kh) % NA_STEP_ROWS == 0 and n_q == LANES
    assert (rows - kh + NA_SLAB_ROWS) * GRID_W <= n_tok and kh + NA_STEP_ROWS <= NA_SLAB_ROWS
    assert n_ctx % PV_CHUNK == 0 and (NA_SLAB_ROWS * GRID_W) % LANES == 0
    _, first_steps = _natten_step_classes(rows)
    step_q = NA_GROUPS * n_q
    assert n_lat % step_q == 0 and n_ctx % step_q == 0
    n_steps = (n_tok if with_ctx else n_lat) // step_q

    def bias_spec(g):
        def index(b, s):
            group = NA_GROUPS * s + g
            cls = sum((group >= f).astype(jnp.int32) for f in first_steps[1:])
            return (layer, cls, 0, 0, 0)
        return pl.BlockSpec((1, 1) + bias.shape[2:], index)

    full = pl.BlockSpec((1, n_tok, BRANCH_W), lambda b, s: (b, 0, 0))
    full_t = pl.BlockSpec((1, BRANCH_W, n_tok), lambda b, s: (b, 0, 0))
    tile = pl.BlockSpec((1, step_q, BRANCH_W), lambda b, s: (b, s, 0))
    return pl.pallas_call(
        functools.partial(_natten_kernel, n_lat=n_lat, n_ctx=n_ctx),
        grid=(bsz, n_steps),
        in_specs=[tile, full, full_t] + [bias_spec(g) for g in range(NA_GROUPS)],
        out_specs=tile,
        out_shape=jax.ShapeDtypeStruct((bsz, n_tok, BRANCH_W), BF16),
        scratch_shapes=[pltpu.VMEM((n_ctx + NA_SLAB_ROWS * GRID_W, 2 * n_q), F32)]
                       * (NA_GROUPS * NA_HEADS // 2),
        compiler_params=_params(2),
        name="neighbourhood_attention",
    )(qc, kc, vct, *([bias] * NA_GROUPS))


def _merge_kernel(x_ref, mod_ref, fa_ref, db_ref, nc_ref, gate_ref, wa_ref, wb_ref, wc_ref,
                  wo_ref, o_ref, *, lat_subs, n_subs):
    def sub_tile(s):
        rows = slice(s * TOK_TILE, (s + 1) * TOK_TILE)

        def gate(j):
            return gate_ref[0, rows, j * D_MODEL:(j + 1) * D_MODEL].astype(F32)

        y = (gate(0) * _dot(fa_ref[0, rows, :], wa_ref[...])
             + gate(1) * _dot(db_ref[0, rows, :], wb_ref[...])
             + gate(2) * _dot(nc_ref[0, rows, :], wc_ref[...]))
        y = _dot(y.astype(BF16), wo_ref[...])
        o_ref[0, rows, :] = x_ref[0, rows, :] + mod_ref[2:3, :] * y

    _for_sub_tiles(pl.program_id(1), lat_subs, n_subs, sub_tile)


def _merge(h, mod, fa, db, nc, gates, w_a, w_b, w_c, w_out, layer, n_lat, with_ctx):
    bsz, n_tok, _ = h.shape
    n_rows = n_tok if with_ctx else n_lat
    full_steps = n_lat // STEP_ROWS
    tok = lambda w: pl.BlockSpec((1, STEP_ROWS, w), lambda b, t: (b, t, 0))
    return pl.pallas_call(
        functools.partial(_merge_kernel, lat_subs=n_lat // TOK_TILE, n_subs=n_rows // TOK_TILE),
        grid=(bsz, pl.cdiv(n_rows, STEP_ROWS)),
        in_specs=[
            tok(D_MODEL),
            pl.BlockSpec((None, None, N_MOD, D_MODEL), lambda b, t: (b, t // full_steps, 0, 0)),
            tok(BRANCH_W), tok(BRANCH_W), tok(BRANCH_W), tok(N_BRANCH * D_MODEL),
            _resident_layer((BRANCH_W, D_MODEL), layer), _resident_layer((BRANCH_W, D_MODEL), layer),
            _resident_layer((BRANCH_W, D_MODEL), layer), _resident_layer((D_MODEL, D_MODEL), layer),
        ],
        out_specs=tok(D_MODEL),
        out_shape=jax.ShapeDtypeStruct((bsz, n_rows, D_MODEL), F32),
        compiler_params=_params(2),
        name="merge_out_projection",
    )(h, mod, fa, db, nc, gates, w_a, w_b, w_c, w_out)


def _convffn_kernel(x_ref, prev_ref, next_ref, mod_ref, g_ref, wu_ref, cw_ref, cb_ref, wd_ref,
                    gf_ref, o_ref, *lhs_refs, lat_subs, n_subs, final_norm):
    t = pl.program_id(1)
    g = g_ref[...]
    shift, scale = mod_ref[3:4, :], mod_ref[4:5, :]
    n_rows = TOK_TILE + 2 * SUBLANES

    def conv(u, col):
        w = cw_ref[:, col:col + FF_CHUNK]
        before = pltpu.roll(u, 1, axis=0)
        after = pltpu.roll(u, n_rows - 1, axis=0)
        v = before * w[0:1] + u * w[1:2] + after * w[2:3] + cb_ref[:, col:col + FF_CHUNK]
        return v[SUBLANES:SUBLANES + TOK_TILE]

    def sub_tile(s):
        idx = t * SUB_TILES + s
        lo = s * TOK_TILE
        x = x_ref[0, lo:lo + TOK_TILE, :]
        has_prev = jnp.logical_and(idx != 0, idx != lat_subs)
        has_next = jnp.logical_and(idx != lat_subs - 1, idx != n_subs - 1)
        prev = prev_ref[0] if s == 0 else x_ref[0, lo - SUBLANES:lo, :]
        nxt = (next_ref[0] if s == SUB_TILES - 1
               else x_ref[0, lo + TOK_TILE:lo + TOK_TILE + SUBLANES, :])
        lhs_ref = lhs_refs[s]
        lhs_ref[0:SUBLANES, :] = jnp.where(has_prev, _norm_modulate(prev, g, shift, scale), 0.0)
        lhs_ref[SUBLANES:SUBLANES + TOK_TILE, :] = _norm_modulate(x, g, shift, scale)
        lhs_ref[SUBLANES + TOK_TILE:, :] = jnp.where(
            has_next, _norm_modulate(nxt, g, shift, scale), 0.0)
        lhs = lhs_ref[...].astype(BF16)

        def up(j):
            col_a, col_b = j * FF_CHUNK, D_FF + j * FF_CHUNK
            return (_dot(lhs, wu_ref[:, col_a:col_a + FF_CHUNK]),
                    _dot(lhs, wu_ref[:, col_b:col_b + FF_CHUNK]))

        n_chunks = D_FF // FF_CHUNK
        acc = jnp.zeros((TOK_TILE, D_MODEL), F32)
        ahead = [up(j) for j in range(FF_LOOKAHEAD)]
        for j in range(n_chunks):
            col_a, col_b = j * FF_CHUNK, D_FF + j * FF_CHUNK
            u_a, u_b = ahead.pop(0)
            if j + FF_LOOKAHEAD < n_chunks:
                ahead.append(up(j + FF_LOOKAHEAD))
            a = conv(u_a, col_a)
            b = conv(u_b, col_b)
            act = (a / (1.0 + jnp.exp(-a))) * b
            acc = acc + _dot(act.astype(BF16), wd_ref[col_a:col_a + FF_CHUNK, :])
        y = x + mod_ref[5:6, :] * acc
        if final_norm:
            ms = jnp.mean(y * y, axis=-1, keepdims=True)
            y = y * lax.rsqrt(ms + NORM_EPS) * gf_ref[...]
        o_ref[0, lo:lo + TOK_TILE, :] = y

    _for_sub_tiles(t, lat_subs, n_subs, sub_tile)


def _conv_ffn(h, mod, g_ffn, w_up, conv_w, conv_b, w_down, layer, g_final, n_lat, with_ctx,
              final_norm):
    bsz, n_rows_in, _ = h.shape
    n_rows = n_rows_in if with_ctx else n_lat
    lat_subs, n_subs = n_lat // TOK_TILE, n_rows // TOK_TILE
    full_steps = n_lat // STEP_ROWS
    blocks_per_step = STEP_ROWS // SUBLANES
    last_block = n_rows_in // SUBLANES - 1
    return pl.pallas_call(
        functools.partial(_convffn_kernel, lat_subs=lat_subs, n_subs=n_subs,
                          final_norm=final_norm),
        grid=(bsz, pl.cdiv(n_rows, STEP_ROWS)),
        in_specs=[
            pl.BlockSpec((1, STEP_ROWS, D_MODEL), lambda b, t: (b, t, 0)),
            pl.BlockSpec((1, SUBLANES, D_MODEL),
                         lambda b, t: (b, jnp.maximum(t * blocks_per_step - 1, 0), 0)),
            pl.BlockSpec((1, SUBLANES, D_MODEL),
                         lambda b, t: (b, jnp.minimum((t + 1) * blocks_per_step, last_block), 0)),
            pl.BlockSpec((None, None, N_MOD, D_MODEL), lambda b, t: (b, t // full_steps, 0, 0)),
            _resident((1, D_MODEL)),
            _resident_layer((D_MODEL, 2 * D_FF), layer),
            _resident((3, 2 * D_FF)),
            _resident((1, 2 * D_FF)),
            _resident_layer((D_FF, D_MODEL), layer),
            _resident((1, D_MODEL)),
        ],
        out_specs=pl.BlockSpec((1, STEP_ROWS, D_MODEL), lambda b, t: (b, t, 0)),
        out_shape=jax.ShapeDtypeStruct((bsz, n_rows, D_MODEL), F32),
        scratch_shapes=[pltpu.VMEM((TOK_TILE + 2 * SUBLANES, D_MODEL), F32)] * SUB_TILES,
        compiler_params=_params(2),
        name="conv_ffn",
    )(h, h, h, mod, g_ffn.reshape(1, D_MODEL), w_up, conv_w, conv_b.reshape(1, -1), w_down,
      g_final.reshape(1, D_MODEL))


def _rope_tables(n_lat, n_ctx):
    t = jnp.arange(n_lat, dtype=jnp.int32)
    pos = jnp.stack([(t // GRID_W).astype(F32), (t % GRID_W).astype(F32)], axis=1)
    n_freq = ROPE_AXIS_DIM // 2
    inv = ROPE_THETA ** (-jnp.arange(n_freq, dtype=F32) / n_freq)
    ang = pos[:, :, None] * inv
    lane = jnp.arange(LANES, dtype=jnp.int32)
    axis = (lane % DA_HEAD_DIM) // ROPE_AXIS_DIM
    freq = lane % n_freq
    second_half = (lane % ROPE_AXIS_DIM) >= n_freq
    ang_l = ang[:, axis, freq]
    cos, sin = jnp.cos(ang_l), jnp.sin(ang_l)
    sin_a = jnp.where(second_half, 0.0, -sin)
    sin_b = jnp.where(second_half, sin, 0.0)
    pad = lambda a, v: jnp.concatenate([a, jnp.full((n_ctx, LANES), v, F32)], axis=0)
    return pad(cos, 1.0), pad(sin_a, 0.0), pad(sin_b, 0.0)


def kernel(x, c, ctx, c_ctx, w_ada, b_ada, g_mix, g_ffn, w_in, b_gate, w_a, lam, subln_g, w_b,
           rpb, w_c, w_out, w_up, conv_w, conv_b, w_down, g_final):
    bsz, n_lat, _ = x.shape
    n_ctx = ctx.shape[1]
    assert n_lat % STEP_ROWS == 0 and n_ctx == TOK_TILE and n_lat % GRID_W == 0

    rope_tabs = _rope_tables(n_lat, n_ctx)
    cl, sl = _dft_tables(n_lat, n_lat // 2, (n_lat * FN_GROUP_DIM) ** -0.5)
    cc, sc = _dft_tables(n_ctx, n_ctx, (n_ctx * FN_GROUP_DIM) ** -0.5)
    cg, sg = _dft_tables(FN_GROUP_DIM, FN_GROUP_DIM, 1.0)
    dft_tabs = (cl, sl, cc, sc, jnp.concatenate([cg, -sg], axis=0), _reversal_matrix())

    n_mod_rows = 2 * SUBLANES * (-(-(bsz + 1) // (2 * SUBLANES)))
    cvec = jnp.zeros((n_mod_rows, D_MODEL), F32).at[:bsz].set(c).at[bsz].set(c_ctx)
    mods = _modulation(cvec, w_ada, b_ada).reshape(DEPTH, n_mod_rows, N_MOD, D_MODEL)

    na_bias = _natten_bias(rpb, n_lat // GRID_W)

    w_in, w_a, w_b, w_c, w_out, w_up, w_down = (
        w.astype(BF16) for w in (w_in, w_a, w_b, w_c, w_out, w_up, w_down))

    h = jnp.concatenate([x, ctx], axis=1)
    for l in range(DEPTH):
        with_ctx = l != DEPTH - 1
        lam_init = 0.8 - 0.6 * math.exp(-0.3 * l)
        mod = jnp.stack([mods[l, :bsz],
                         jnp.broadcast_to(mods[l, bsz], (bsz, N_MOD, D_MODEL))], axis=1)
        fa_in, qb, kb, vbt, qc, kc, vct, gates = _in_projection(
            h, mod, g_mix[l], w_in, l, b_gate[l], rope_tabs, n_lat)
        fa = _fourier_mix(fa_in, dft_tabs, n_lat, n_ctx, with_ctx)
        db = _diff_attention(qb, kb, vbt, lam[l], subln_g[l], lam_init, n_lat, n_ctx, with_ctx)
        nc = _neighbourhood_attention(qc, kc, vct, na_bias, l, n_lat, n_ctx, with_ctx)
        h = _merge(h, mod, fa, db, nc, gates, w_a, w_b, w_c, w_out, l, n_lat, with_ctx)
        h = _conv_ffn(h, mod, g_ffn[l], w_up, conv_w[l], conv_b[l], w_down, l, g_final, n_lat,
                      with_ctx, final_norm=not with_ctx)
    return h
```

```python
import functools
import math

import jax
import jax.numpy as jnp
from jax import lax
from jax.experimental import pallas as pl
from jax.experimental.pallas import tpu as pltpu

D_MODEL = 1024
DEPTH = 4
GRID_W = 64
FN_GROUPS = 4
FN_GROUP_DIM = 128
FN_WIDTH = FN_GROUPS * FN_GROUP_DIM
DA_HEADS = 4
DA_HEAD_DIM = 64
DA_V_DIM = 2 * DA_HEAD_DIM
NA_HEADS = 8
NA_HEAD_DIM = 64
NA_ROWS = 8
NA_COLS = 16
BRANCH_W = 512
N_BRANCH = 3
ROPE_THETA = 10000.0
ROPE_AXIS_DIM = DA_HEAD_DIM // 2
D_FF = 2816
N_MOD = 6
NORM_EPS = 1e-6
SUBLN_EPS = 1e-5
NEG_INF = -1e30

LANES = 128
SUBLANES = 8
TOK_TILE = 256
SUB_TILES = 2
STEP_ROWS = SUB_TILES * TOK_TILE
FF_CHUNK = 256
FF_LOOKAHEAD = 2
NA_STEP_ROWS = 2
NA_GROUPS = 2
NA_SLAB_ROWS = 10
VMEM_LIMIT = 56 * 1024 * 1024
ONES_ROWS = 16
PV_CHUNK = 256
LOG2_E = math.log2(math.e)

BF16 = jnp.bfloat16
F32 = jnp.float32


def _params(n_grid_dims):
    return pltpu.CompilerParams(dimension_semantics=("arbitrary",) * n_grid_dims,
                                vmem_limit_bytes=VMEM_LIMIT)


def _resident(shape):
    return pl.BlockSpec(shape, lambda *_: (0,) * len(shape), pipeline_mode=pl.Buffered(1))


def _resident_layer(shape, layer):
    return pl.BlockSpec((None,) + tuple(shape), lambda *_: (layer,) + (0,) * len(shape),
                        pipeline_mode=pl.Buffered(1))


def _for_sub_tiles(t, lat_subs, n_subs, body, ctx_body=None):
    full_steps = lat_subs // SUB_TILES
    if n_subs == lat_subs:
        for s in range(SUB_TILES):
            body(s)
        return

    @pl.when(t < full_steps)
    def _():
        for s in range(SUB_TILES):
            body(s)

    @pl.when(t >= full_steps)
    def _():
        for s in range(n_subs - lat_subs):
            (ctx_body or body)(s)


def _nt_dot(a, b):
    return lax.dot_general(a, b, (((1,), (1,)), ((), ())), preferred_element_type=F32)


def _dot(a, b):
    return jnp.dot(a, b, preferred_element_type=F32)


def _norm_modulate(x, g, shift, scale):
    ms = jnp.mean(x * x, axis=-1, keepdims=True)
    return (x * lax.rsqrt(ms + NORM_EPS) * g) * (1.0 + scale) + shift


def _split_bf16(v):
    hi = v.astype(BF16)
    lo = (v - hi.astype(F32)).astype(BF16)
    return hi, lo


def _mod_kernel(c_ref, w_ref, b_ref, o_ref):
    c = c_ref[...]
    s = c / (1.0 + jnp.exp(-c))
    s_hi, s_lo = _split_bf16(s)
    w_hi, w_lo = _split_bf16(w_ref[0])
    acc = _dot(s_hi, w_hi) + (_dot(s_hi, w_lo) + _dot(s_lo, w_hi))
    o_ref[0] = acc + b_ref[0]


def _modulation(cvec, w_ada, b_ada):
    n_rows = cvec.shape[0]
    n_out = N_MOD * D_MODEL
    tn = 1536
    return pl.pallas_call(
        _mod_kernel,
        grid=(DEPTH, n_out // tn),
        in_specs=[
            pl.BlockSpec((n_rows, D_MODEL), lambda l, j: (0, 0)),
            pl.BlockSpec((1, D_MODEL, tn), lambda l, j: (l, 0, j)),
            pl.BlockSpec((1, 1, tn), lambda l, j: (l, 0, j)),
        ],
        out_specs=pl.BlockSpec((1, n_rows, tn), lambda l, j: (l, 0, j)),
        out_shape=jax.ShapeDtypeStruct((DEPTH, n_rows, n_out), F32),
        compiler_params=_params(2),
        name="adaln_modulation",
    )(cvec, w_ada, b_ada.reshape(DEPTH, 1, n_out))


def _rope(p, cos, sin_a, sin_b):
    outs = []
    for k in range(p.shape[1] // LANES):
        xs = p[:, k * LANES:(k + 1) * LANES]
        from_hi = pltpu.roll(xs, LANES - ROPE_AXIS_DIM // 2, axis=1)
        from_lo = pltpu.roll(xs, ROPE_AXIS_DIM // 2, axis=1)
        outs.append(xs * cos + from_hi * sin_a + from_lo * sin_b)
    return jnp.concatenate(outs, axis=1)


def _inproj_kernel(x_ref, xc_ref, mod_ref, g_ref, w_ref, bg_ref, cos_ref, sa_ref, sb_ref,
                   fa_ref, qb_ref, kb_ref, vbt_ref, qc_ref, kc_ref, vct_ref, gate_ref,
                   *, lat_subs, n_subs):
    def sub_tile(s, src_ref=x_ref):
        rows = slice(s * TOK_TILE, (s + 1) * TOK_TILE)
        a = _norm_modulate(src_ref[0, rows, :], g_ref[...], mod_ref[0:1, :],
                           mod_ref[1:2, :]).astype(BF16)
        cos, sin_a, sin_b = cos_ref[rows, :], sa_ref[rows, :], sb_ref[rows, :]

        def proj(seg):
            return _dot(a, w_ref[:, seg * BRANCH_W:(seg + 1) * BRANCH_W])

        fa_ref[0, rows, :] = proj(0).astype(BF16)
        qb_ref[0, rows, :] = (_rope(proj(1), cos, sin_a, sin_b)
                              * (DA_HEAD_DIM ** -0.5 * LOG2_E)).astype(BF16)
        kb_ref[0, rows, :] = _rope(proj(2), cos, sin_a, sin_b).astype(BF16)
        vbt_ref[0, :, rows] = proj(3).T.astype(BF16)
        qc_ref[0, rows, :] = (proj(4) * (NA_HEAD_DIM ** -0.5 * LOG2_E)).astype(BF16)
        kc_ref[0, rows, :] = proj(5).astype(BF16)
        vct_ref[0, :, rows] = proj(6).T.astype(BF16)
        for j in range(N_BRANCH * D_MODEL // BRANCH_W):
            cols = slice(j * BRANCH_W, (j + 1) * BRANCH_W)
            z = proj(7 + j) + bg_ref[:, cols]
            gate_ref[0, rows, cols] = (1.0 / (1.0 + jnp.exp(-z))).astype(BF16)

    _for_sub_tiles(pl.program_id(1), lat_subs, n_subs, sub_tile,
                   functools.partial(sub_tile, src_ref=xc_ref))


def _stream_specs(stream, n_lat):
    _, _, ctx_block = stream
    last = n_lat // STEP_ROWS - 1
    return [pl.BlockSpec((1, STEP_ROWS, D_MODEL), lambda b, t: (b, jnp.minimum(t, last), 0)),
            pl.BlockSpec((1, TOK_TILE, D_MODEL), lambda b, t: (b, ctx_block, 0))]


def _in_projection(stream, mod, g_mix, w_in, layer, b_gate, rope_tabs, n_lat, n_ctx):
    bsz, n_tok = stream[0].shape[0], n_lat + n_ctx
    full_steps = n_lat // STEP_ROWS
    proj_w = w_in.shape[-1]
    tok = lambda w: pl.BlockSpec((1, STEP_ROWS, w), lambda b, t: (b, t, 0))
    tab = pl.BlockSpec((STEP_ROWS, LANES), lambda b, t: (t, 0))
    branch = jax.ShapeDtypeStruct((bsz, n_tok, BRANCH_W), BF16)
    tok_t = pl.BlockSpec((1, BRANCH_W, STEP_ROWS), lambda b, t: (b, 0, t))
    branch_t = jax.ShapeDtypeStruct((bsz, BRANCH_W, n_tok), BF16)
    return pl.pallas_call(
        functools.partial(_inproj_kernel, lat_subs=n_lat // TOK_TILE, n_subs=n_tok // TOK_TILE),
        grid=(bsz, pl.cdiv(n_tok, STEP_ROWS)),
        in_specs=_stream_specs(stream, n_lat) + [
            pl.BlockSpec((None, None, N_MOD, D_MODEL), lambda b, t: (b, t // full_steps, 0, 0)),
            _resident((1, D_MODEL)),
            _resident_layer((D_MODEL, proj_w), layer),
            _resident((1, N_BRANCH * D_MODEL)),
            tab, tab, tab,
        ],
        out_specs=[tok(BRANCH_W)] * 3 + [tok_t] + [tok(BRANCH_W)] * 2 + [tok_t]
                  + [tok(N_BRANCH * D_MODEL)],
        out_shape=[branch] * 3 + [branch_t] + [branch] * 2 + [branch_t]
                  + [jax.ShapeDtypeStruct((bsz, n_tok, N_BRANCH * D_MODEL), BF16)],
        compiler_params=_params(2),
        name="in_projection",
    )(stream[0], stream[1], mod, g_mix.reshape(1, D_MODEL), w_in, b_gate.reshape(1, -1),
      *rope_tabs)


def _dft_kernel(u_ref, cl_ref, sl_ref, cc_ref, sc_ref, csg_ref, rev_ref, o_ref, ue_ref, uo_ref,
                *, n_lat, n_ctx, n_subs, scale):
    t = pl.program_id(1)
    half = n_lat // 2

    def finish(s, p, q):
        rows = slice(s * TOK_TILE, (s + 1) * TOK_TILE)
        for g in range(FN_GROUPS):
            sl = slice(g * FN_GROUP_DIM, (g + 1) * FN_GROUP_DIM)
            pq = jnp.concatenate([p[:, sl], q[:, sl]], axis=1).astype(BF16)
            o_ref[0, rows, sl] = _dot(pq, csg_ref[...]).astype(BF16)

    @pl.when(t == 0)
    def _():
        for b in range(half // TOK_TILE):
            hi = n_lat - (b + 1) * TOK_TILE
            first = u_ref[0, hi:hi + TOK_TILE, :]
            wrap = (u_ref[0, hi + TOK_TILE:hi + 2 * TOK_TILE, :] if b > 0
                    else jnp.zeros((TOK_TILE, FN_WIDTH), BF16))
            rev = _dot(rev_ref[...], jnp.concatenate([first, wrap], axis=0))
            lo = u_ref[0, b * TOK_TILE:(b + 1) * TOK_TILE, :].astype(F32)
            ue_ref[b * TOK_TILE:(b + 1) * TOK_TILE, :] = (lo + rev).astype(BF16)
            uo_ref[b * TOK_TILE:(b + 1) * TOK_TILE, :] = (lo - rev).astype(BF16)

    def latent(s):
        k0 = pl.multiple_of((t * SUB_TILES + s) * TOK_TILE, TOK_TILE)
        k = k0 + lax.broadcasted_iota(jnp.int32, (TOK_TILE, 1), 0)
        sign = (1 - 2 * (k & 1)).astype(F32) * scale
        rows = pl.ds(k0, TOK_TILE)
        p = _dot(cl_ref[rows, :], ue_ref[...]) + sign * u_ref[0, half:half + 1, :].astype(F32)
        finish(s, p, _dot(sl_ref[rows, :], uo_ref[...]))

    def context(s):
        u = u_ref[0, n_lat:n_lat + n_ctx, :]
        finish(s, _dot(cc_ref[...], u), _dot(sc_ref[...], u))

    _for_sub_tiles(t, n_lat // TOK_TILE, n_subs, latent, context)


def _fourier_mix(fa_in, tabs, n_lat, n_ctx, with_ctx):
    bsz, n_tok, _ = fa_in.shape
    n_rows = n_tok if with_ctx else n_lat
    half = n_lat // 2
    assert half % TOK_TILE == 0
    cl, sl, cc, sc, csg, rev = tabs
    lat_tab = _resident((n_lat, half))
    return pl.pallas_call(
        functools.partial(_dft_kernel, n_lat=n_lat, n_ctx=n_ctx, n_subs=n_rows // TOK_TILE,
                          scale=(n_lat * FN_GROUP_DIM) ** -0.5),
        grid=(bsz, pl.cdiv(n_rows, STEP_ROWS)),
        in_specs=[
            pl.BlockSpec((1, n_tok, FN_WIDTH), lambda b, t: (b, 0, 0)),
            lat_tab, lat_tab,
            _resident((n_ctx, n_ctx)), _resident((n_ctx, n_ctx)),
            _resident((2 * FN_GROUP_DIM, FN_GROUP_DIM)),
            _resident((TOK_TILE, 2 * TOK_TILE)),
        ],
        out_specs=pl.BlockSpec((1, STEP_ROWS, FN_WIDTH), lambda b, t: (b, t, 0)),
        out_shape=jax.ShapeDtypeStruct((bsz, n_tok, FN_WIDTH), BF16),
        scratch_shapes=[pltpu.VMEM((half, FN_WIDTH), BF16)] * 2,
        compiler_params=_params(2),
        name="fourier_mix",
    )(fa_in, cl, sl, cc, sc, csg, rev)


def _dft_tables(n, n_cols, scale):
    k = jnp.arange(n, dtype=jnp.int32)
    ang = ((k[:, None] * k[None, :n_cols]) % n).astype(F32) * (2.0 * math.pi / n)
    return (jnp.cos(ang) * scale).astype(BF16), (jnp.sin(ang) * scale).astype(BF16)


def _reversal_matrix():
    i = jnp.arange(TOK_TILE, dtype=jnp.int32)[:, None]
    j = jnp.arange(2 * TOK_TILE, dtype=jnp.int32)[None, :]
    src = jnp.where(i == 0, TOK_TILE, TOK_TILE - i)
    return (j == src).astype(BF16)


def _stack_sub_heads(q, first_half):
    zero = jnp.zeros_like(q)
    return jnp.concatenate([jnp.where(first_half, q, zero), jnp.where(first_half, zero, q)], axis=0)


def _store_scores(s_ref, row, s):
    n = s.shape[0]
    s_ref[row:row + n, :] = s
    return jnp.max(s.reshape(n // SUBLANES, SUBLANES, -1).max(axis=0), axis=0, keepdims=True)


def _softmax_times_values(s_ref, n_rows, m, values_t):
    acc = None
    for r in range(0, n_rows, PV_CHUNK):
        n = min(PV_CHUNK, n_rows - r)
        e = jnp.exp2(s_ref[r:r + n, :] - m).astype(BF16)
        v = values_t(r, n)
        part = _dot(jnp.concatenate([v, jnp.ones((ONES_ROWS, n), BF16)], axis=0), e)
        acc = part if acc is None else acc + part
    n_ch = acc.shape[0] - ONES_ROWS
    return acc[0:n_ch] * (1.0 / acc[n_ch:n_ch + 1])


def _diffattn_kernel(q_ref, k_ref, vt_ref, lam_ref, g_ref, o_ref, s0_ref, s1_ref,
                     *, n_lat, n_ctx, lam_init):
    t = pl.program_id(1)
    lv = lam_ref[...]
    lam = (jnp.exp(jnp.sum(lv[0:1] * lv[1:2], axis=-1, keepdims=True))
           - jnp.exp(jnp.sum(lv[2:3] * lv[3:4], axis=-1, keepdims=True)) + lam_init)
    first_half = lax.broadcasted_iota(jnp.int32, (1, LANES), 1) < DA_HEAD_DIM
    s_refs = (s0_ref, s1_ref)

    def attend(n_sub_tiles, k_lo, k_len):
        items = [(s, h) for s in range(n_sub_tiles) for h in range(DA_HEADS)]

        def scores(i):
            s, h = items[i]
            sl = slice(h * DA_V_DIM, (h + 1) * DA_V_DIM)
            q = q_ref[0, s * TOK_TILE:(s + 1) * TOK_TILE, sl]
            sc = _nt_dot(k_ref[0, k_lo:k_lo + k_len, sl], _stack_sub_heads(q, first_half))
            return _store_scores(s_refs[i % 2], 0, sc)

        m_next = scores(0)
        for i, (s, h) in enumerate(items):
            sl = slice(h * DA_V_DIM, (h + 1) * DA_V_DIM)
            m = m_next
            if i + 1 < len(items):
                m_next = scores(i + 1)
            o12 = _softmax_times_values(
                s_refs[i % 2], k_len, m, lambda r, n: vt_ref[0, sl, k_lo + r:k_lo + r + n]).T
            o = o12[0:TOK_TILE] - lam * o12[TOK_TILE:2 * TOK_TILE]
            ms = jnp.mean(o * o, axis=-1, keepdims=True)
            o = o * lax.rsqrt(ms + SUBLN_EPS) * g_ref[...] * (1.0 - lam_init)
            o_ref[0, s * TOK_TILE:(s + 1) * TOK_TILE, sl] = o.astype(BF16)

    @pl.when(t < n_lat // STEP_ROWS)
    def _():
        attend(SUB_TILES, 0, n_lat + n_ctx)

    @pl.when(t >= n_lat // STEP_ROWS)
    def _():
        attend(n_ctx // TOK_TILE, n_lat, n_ctx)


def _diff_attention(qb, kb, vbt, lam_vec, subln_g, lam_init, n_lat, n_ctx, with_ctx):
    bsz, n_tok, _ = qb.shape
    n_rows = n_tok if with_ctx else n_lat
    full = pl.BlockSpec((1, n_tok, BRANCH_W), lambda b, t: (b, 0, 0))
    full_t = pl.BlockSpec((1, BRANCH_W, n_tok), lambda b, t: (b, 0, 0))
    tile = pl.BlockSpec((1, STEP_ROWS, BRANCH_W), lambda b, t: (b, t, 0))
    return pl.pallas_call(
        functools.partial(_diffattn_kernel, n_lat=n_lat, n_ctx=n_ctx, lam_init=lam_init),
        grid=(bsz, pl.cdiv(n_rows, STEP_ROWS)),
        in_specs=[tile, full, full_t, _resident((4, DA_HEAD_DIM)), _resident((1, DA_V_DIM))],
        out_specs=tile,
        out_shape=jax.ShapeDtypeStruct((bsz, n_tok, BRANCH_W), BF16),
        scratch_shapes=[pltpu.VMEM((n_tok, 2 * TOK_TILE), F32)] * 2,
        compiler_params=_params(2),
        name="diff_attention",
    )(qb, kb, vbt, lam_vec, subln_g.reshape(1, DA_V_DIM))


def _natten_kernel(q_ref, k_ref, vt_ref, *refs, n_lat, n_ctx):
    bias_refs, o_ref, s_refs = refs[:NA_GROUPS], refs[NA_GROUPS], refs[NA_GROUPS + 1:]
    s_idx = pl.program_id(1)
    rows = n_lat // GRID_W
    kh = min(NA_ROWS, rows)
    n_slab = NA_SLAB_ROWS * GRID_W
    n_q = NA_STEP_ROWS * GRID_W
    n_pairs = NA_HEADS // 2
    first_half = lax.broadcasted_iota(jnp.int32, (1, LANES), 1) < NA_HEAD_DIM
    pairs = [slice(p * LANES, (p + 1) * LANES) for p in range(n_pairs)]

    def run(window):
        n_keys = n_ctx + (n_slab if window else 0)
        k_los, maxes = [], []
        for g in range(NA_GROUPS):
            first_row = jnp.clip(NA_STEP_ROWS * (NA_GROUPS * s_idx + g) - kh // 2, 0, rows - kh)
            k_lo = pl.multiple_of(first_row * GRID_W, LANES)
            k_los.append(k_lo)
            for p, sl in enumerate(pairs):
                s_ref = s_refs[g * n_pairs + p]
                qs = _stack_sub_heads(q_ref[0, g * n_q:(g + 1) * n_q, sl], first_half)
                m = _store_scores(s_ref, 0, _nt_dot(k_ref[0, n_lat:n_lat + n_ctx, sl], qs))
                if window:
                    s_win = _nt_dot(k_ref[0, pl.ds(k_lo, n_slab), sl], qs) + bias_refs[g][0, 0, p]
                    m = jnp.maximum(m, _store_scores(s_ref, n_ctx, s_win))
                maxes.append(m)
        for g in range(NA_GROUPS):
            for p, sl in enumerate(pairs):
                def values_t(r, n):
                    if r < n_ctx:
                        return vt_ref[0, sl, n_lat + r:n_lat + r + n]
                    return vt_ref[0, sl, pl.ds(pl.multiple_of(k_los[g] + (r - n_ctx), LANES), n)]

                i = g * n_pairs + p
                o2 = _softmax_times_values(s_refs[i], n_keys, maxes[i], values_t).T
                o_ref[0, g * n_q:(g + 1) * n_q, sl] = jnp.where(
                    first_half, o2[0:n_q], o2[n_q:2 * n_q]).astype(BF16)

    window_steps = rows // (NA_STEP_ROWS * NA_GROUPS)

    @pl.when(s_idx < window_steps)
    def _():
        run(True)

    @pl.when(s_idx >= window_steps)
    def _():
        run(False)


def _natten_step_classes(rows):
    kh = min(NA_ROWS, rows)
    classes, first_steps = [], []
    for s in range(rows // NA_STEP_ROWS):
        slab = min(max(NA_STEP_ROWS * s - kh // 2, 0), rows - kh)
        geom = tuple((r - slab, min(max(r - kh // 2, 0), rows - kh) - slab)
                     for r in range(NA_STEP_ROWS * s, NA_STEP_ROWS * (s + 1)))
        geom = geom + (min(NA_SLAB_ROWS, rows - slab),)
        if not classes or classes[-1] != geom:
            assert geom not in classes
            classes.append(geom)
            first_steps.append(s)
    return classes, first_steps


def _natten_bias_kernel(r_ref, o_ref, *, classes, kh):
    cls = pl.program_id(1)
    kc = lax.broadcasted_iota(jnp.int32, (GRID_W, LANES), 0)
    lane = lax.broadcasted_iota(jnp.int32, (GRID_W, LANES), 1)
    second = lane >= GRID_W
    c = jnp.where(second, lane - GRID_W, lane)
    c0 = jnp.clip(c - NA_COLS // 2, 0, GRID_W - NA_COLS)
    col_ok = jnp.logical_and(kc >= c0, kc < c0 + NA_COLS)
    neg = jnp.full((GRID_W, LANES), NEG_INF, F32)

    def fill(geom):
        for h in range(NA_HEADS):
            for kr in range(NA_SLAB_ROWS):
                halves = []
                for j, (q_off, w_off) in enumerate(geom[:-1]):
                    if w_off <= kr < w_off + kh and kr < geom[-1]:
                        dr = kr - q_off + NA_ROWS - 1
                        row = jnp.broadcast_to(r_ref[0, h, dr:dr + 1, :], (GRID_W, LANES))
                        halves.append(pltpu.roll(row, (j * GRID_W - (NA_COLS - 1)) % LANES, axis=1,
                                                 stride=1, stride_axis=0))
                    else:
                        halves.append(neg)
                val = jnp.where(col_ok, jnp.where(second, halves[1], halves[0]), neg)
                o_ref[0, 0, h // 2, kr * GRID_W:(kr + 1) * GRID_W,
                      (h % 2) * LANES:(h % 2 + 1) * LANES] = val

    for ci, geom in enumerate(classes):
        pl.when(cls == ci)(functools.partial(fill, geom))


def _natten_bias(rpb, rows):
    kh = min(NA_ROWS, rows)
    classes, _ = _natten_step_classes(rows)
    n_dr, n_dc = 2 * NA_ROWS - 1, 2 * NA_COLS - 1
    assert NA_STEP_ROWS == 2 and n_dc <= GRID_W
    r = jnp.pad(rpb[..., ::-1].astype(F32) * LOG2_E,
                ((0, 0), (0, 0), (0, 2 * SUBLANES - n_dr), (0, LANES - n_dc)))
    return pl.pallas_call(
        functools.partial(_natten_bias_kernel, classes=classes, kh=kh),
        grid=(DEPTH, len(classes)),
        in_specs=[pl.BlockSpec((1, NA_HEADS, 2 * SUBLANES, LANES), lambda l, k: (l, 0, 0, 0))],
        out_specs=pl.BlockSpec((1, 1, NA_HEADS // 2, NA_SLAB_ROWS * GRID_W, 2 * LANES),
                               lambda l, k: (l, k, 0, 0, 0)),
        out_shape=jax.ShapeDtypeStruct(
            (DEPTH, len(classes), NA_HEADS // 2, NA_SLAB_ROWS * GRID_W, 2 * LANES), F32),
        compiler_params=_params(2),
        name="natten_bias_tables",
    )(r)


def _neighbourhood_attention(qc, kc, vct, bias, layer, n_lat, n_ctx, with_ctx):
    bsz, n_tok, _ = qc.shape
    rows = n_lat // GRID_W
    kh = min(NA_ROWS, rows)
    n_q = NA_STEP_ROWS * GRID_W
    assert (kh // 2) % NA_STEP_ROWS == 0 and (rows - kh) % NA_STEP_ROWS == 0 and n_q == LANES
    assert (rows - kh + NA_SLAB_ROWS) * GRID_W <= n_tok and kh + NA_STEP_ROWS <= NA_SLAB_ROWS
    assert n_ctx % PV_CHUNK == 0 and (NA_SLAB_ROWS * GRID_W) % LANES == 0
    _, first_steps = _natten_step_classes(rows)
    step_q = NA_GROUPS * n_q
    assert n_lat % step_q == 0 and n_ctx % step_q == 0
    n_steps = (n_tok if with_ctx else n_lat) // step_q

    def bias_spec(g):
        def index(b, s):
            group = NA_GROUPS * s + g
            cls = sum((group >= f).astype(jnp.int32) for f in first_steps[1:])
            return (layer, cls, 0, 0, 0)
        return pl.BlockSpec((1, 1) + bias.shape[2:], index)

    full = pl.BlockSpec((1, n_tok, BRANCH_W), lambda b, s: (b, 0, 0))
    full_t = pl.BlockSpec((1, BRANCH_W, n_tok), lambda b, s: (b, 0, 0))
    tile = pl.BlockSpec((1, step_q, BRANCH_W), lambda b, s: (b, s, 0))
    return pl.pallas_call(
        functools.partial(_natten_kernel, n_lat=n_lat, n_ctx=n_ctx),
        grid=(bsz, n_steps),
        in_specs=[tile, full, full_t] + [bias_spec(g) for g in range(NA_GROUPS)],
        out_specs=tile,
        out_shape=jax.ShapeDtypeStruct((bsz, n_tok, BRANCH_W), BF16),
        scratch_shapes=[pltpu.VMEM((n_ctx + NA_SLAB_ROWS * GRID_W, 2 * n_q), F32)]
                       * (NA_GROUPS * NA_HEADS // 2),
        compiler_params=_params(2),
        name="neighbourhood_attention",
    )(qc, kc, vct, *([bias] * NA_GROUPS))


def _merge_kernel(x_ref, xc_ref, mod_ref, fa_ref, db_ref, nc_ref, gate_ref, wa_ref, wb_ref, wc_ref,
                  wo_ref, o_ref, *, lat_subs, n_subs):
    def sub_tile(s, src_ref=x_ref):
        rows = slice(s * TOK_TILE, (s + 1) * TOK_TILE)

        def gate(j):
            return gate_ref[0, rows, j * D_MODEL:(j + 1) * D_MODEL].astype(F32)

        y = (gate(0) * _dot(fa_ref[0, rows, :], wa_ref[...])
             + gate(1) * _dot(db_ref[0, rows, :], wb_ref[...])
             + gate(2) * _dot(nc_ref[0, rows, :], wc_ref[...]))
        y = _dot(y.astype(BF16), wo_ref[...])
        o_ref[0, rows, :] = src_ref[0, rows, :] + mod_ref[2:3, :] * y

    _for_sub_tiles(pl.program_id(1), lat_subs, n_subs, sub_tile,
                   functools.partial(sub_tile, src_ref=xc_ref))


def _merge(stream, mod, fa, db, nc, gates, w_a, w_b, w_c, w_out, layer, n_lat, with_ctx):
    bsz, n_tok = fa.shape[0], fa.shape[1]
    n_rows = n_tok if with_ctx else n_lat
    full_steps = n_lat // STEP_ROWS
    tok = lambda w: pl.BlockSpec((1, STEP_ROWS, w), lambda b, t: (b, t, 0))
    return pl.pallas_call(
        functools.partial(_merge_kernel, lat_subs=n_lat // TOK_TILE, n_subs=n_rows // TOK_TILE),
        grid=(bsz, pl.cdiv(n_rows, STEP_ROWS)),
        in_specs=_stream_specs(stream, n_lat) + [
            pl.BlockSpec((None, None, N_MOD, D_MODEL), lambda b, t: (b, t // full_steps, 0, 0)),
            tok(BRANCH_W), tok(BRANCH_W), tok(BRANCH_W), tok(N_BRANCH * D_MODEL),
            _resident_layer((BRANCH_W, D_MODEL), layer), _resident_layer((BRANCH_W, D_MODEL), layer),
            _resident_layer((BRANCH_W, D_MODEL), layer), _resident_layer((D_MODEL, D_MODEL), layer),
        ],
        out_specs=tok(D_MODEL),
        out_shape=jax.ShapeDtypeStruct((bsz, n_rows, D_MODEL), F32),
        compiler_params=_params(2),
        name="merge_out_projection",
    )(stream[0], stream[1], mod, fa, db, nc, gates, w_a, w_b, w_c, w_out)


def _convffn_kernel(x_ref, prev_ref, next_ref, mod_ref, g_ref, wu_ref, cw_ref, cb_ref, wd_ref,
                    gf_ref, o_ref, *lhs_refs, lat_subs, n_subs, final_norm):
    t = pl.program_id(1)
    g = g_ref[...]
    shift, scale = mod_ref[3:4, :], mod_ref[4:5, :]
    n_rows = TOK_TILE + 2 * SUBLANES

    def conv(u, col):
        w = cw_ref[:, col:col + FF_CHUNK]
        before = pltpu.roll(u, 1, axis=0)
        after = pltpu.roll(u, n_rows - 1, axis=0)
        v = before * w[0:1] + u * w[1:2] + after * w[2:3] + cb_ref[:, col:col + FF_CHUNK]
        return v[SUBLANES:SUBLANES + TOK_TILE]

    def sub_tile(s):
        idx = t * SUB_TILES + s
        lo = s * TOK_TILE
        x = x_ref[0, lo:lo + TOK_TILE, :]
        has_prev = jnp.logical_and(idx != 0, idx != lat_subs)
        has_next = jnp.logical_and(idx != lat_subs - 1, idx != n_subs - 1)
        prev = prev_ref[0] if s == 0 else x_ref[0, lo - SUBLANES:lo, :]
        nxt = (next_ref[0] if s == SUB_TILES - 1
               else x_ref[0, lo + TOK_TILE:lo + TOK_TILE + SUBLANES, :])
        lhs_ref = lhs_refs[s]
        lhs_ref[0:SUBLANES, :] = jnp.where(has_prev, _norm_modulate(prev, g, shift, scale), 0.0)
        lhs_ref[SUBLANES:SUBLANES + TOK_TILE, :] = _norm_modulate(x, g, shift, scale)
        lhs_ref[SUBLANES + TOK_TILE:, :] = jnp.where(
            has_next, _norm_modulate(nxt, g, shift, scale), 0.0)
        lhs = lhs_ref[...].astype(BF16)

        def up(j):
            col_a, col_b = j * FF_CHUNK, D_FF + j * FF_CHUNK
            return (_dot(lhs, wu_ref[:, col_a:col_a + FF_CHUNK]),
                    _dot(lhs, wu_ref[:, col_b:col_b + FF_CHUNK]))

        n_chunks = D_FF // FF_CHUNK
        acc = jnp.zeros((TOK_TILE, D_MODEL), F32)
        ahead = [up(j) for j in range(FF_LOOKAHEAD)]
        for j in range(n_chunks):
            col_a, col_b = j * FF_CHUNK, D_FF + j * FF_CHUNK
            u_a, u_b = ahead.pop(0)
            if j + FF_LOOKAHEAD < n_chunks:
                ahead.append(up(j + FF_LOOKAHEAD))
            a = conv(u_a, col_a)
            b = conv(u_b, col_b)
            act = (a / (1.0 + jnp.exp(-a))) * b
            acc = acc + _dot(act.astype(BF16), wd_ref[col_a:col_a + FF_CHUNK, :])
        y = x + mod_ref[5:6, :] * acc
        if final_norm:
            ms = jnp.mean(y * y, axis=-1, keepdims=True)
            y = y * lax.rsqrt(ms + NORM_EPS) * gf_ref[...]
        o_ref[0, lo:lo + TOK_TILE, :] = y

    _for_sub_tiles(t, lat_subs, n_subs, sub_tile)


def _conv_ffn(h, mod, g_ffn, w_up, conv_w, conv_b, w_down, layer, g_final, n_lat, with_ctx,
              final_norm):
    bsz, n_rows_in, _ = h.shape
    n_rows = n_rows_in if with_ctx else n_lat
    lat_subs, n_subs = n_lat // TOK_TILE, n_rows // TOK_TILE
    full_steps = n_lat // STEP_ROWS
    blocks_per_step = STEP_ROWS // SUBLANES
    last_block = n_rows_in // SUBLANES - 1
    return pl.pallas_call(
        functools.partial(_convffn_kernel, lat_subs=lat_subs, n_subs=n_subs,
                          final_norm=final_norm),
        grid=(bsz, pl.cdiv(n_rows, STEP_ROWS)),
        in_specs=[
            pl.BlockSpec((1, STEP_ROWS, D_MODEL), lambda b, t: (b, t, 0)),
            pl.BlockSpec((1, SUBLANES, D_MODEL),
                         lambda b, t: (b, jnp.maximum(t * blocks_per_step - 1, 0), 0)),
            pl.BlockSpec((1, SUBLANES, D_MODEL),
                         lambda b, t: (b, jnp.minimum((t + 1) * blocks_per_step, last_block), 0)),
            pl.BlockSpec((None, None, N_MOD, D_MODEL), lambda b, t: (b, t // full_steps, 0, 0)),
            _resident((1, D_MODEL)),
            _resident_layer((D_MODEL, 2 * D_FF), layer),
            _resident((3, 2 * D_FF)),
            _resident((1, 2 * D_FF)),
            _resident_layer((D_FF, D_MODEL), layer),
            _resident((1, D_MODEL)),
        ],
        out_specs=pl.BlockSpec((1, STEP_ROWS, D_MODEL), lambda b, t: (b, t, 0)),
        out_shape=jax.ShapeDtypeStruct((bsz, n_rows, D_MODEL), F32),
        scratch_shapes=[pltpu.VMEM((TOK_TILE + 2 * SUBLANES, D_MODEL), F32)] * SUB_TILES,
        compiler_params=_params(2),
        name="conv_ffn",
    )(h, h, h, mod, g_ffn.reshape(1, D_MODEL), w_up, conv_w, conv_b.reshape(1, -1), w_down,
      g_final.reshape(1, D_MODEL))


def _rope_tables(n_lat, n_ctx):
    t = jnp.arange(n_lat, dtype=jnp.int32)
    pos = jnp.stack([(t // GRID_W).astype(F32), (t % GRID_W).astype(F32)], axis=1)
    n_freq = ROPE_AXIS_DIM // 2
    inv = ROPE_THETA ** (-jnp.arange(n_freq, dtype=F32) / n_freq)
    ang = pos[:, :, None] * inv
    lane = jnp.arange(LANES, dtype=jnp.int32)
    axis = (lane % DA_HEAD_DIM) // ROPE_AXIS_DIM
    freq = lane % n_freq
    second_half = (lane % ROPE_AXIS_DIM) >= n_freq
    ang_l = ang[:, axis, freq]
    cos, sin = jnp.cos(ang_l), jnp.sin(ang_l)
    sin_a = jnp.where(second_half, 0.0, -sin)
    sin_b = jnp.where(second_half, sin, 0.0)
    pad = lambda a, v: jnp.concatenate([a, jnp.full((n_ctx, LANES), v, F32)], axis=0)
    return pad(cos, 1.0), pad(sin_a, 0.0), pad(sin_b, 0.0)


def kernel(x, c, ctx, c_ctx, w_ada, b_ada, g_mix, g_ffn, w_in, b_gate, w_a, lam, subln_g, w_b,
           rpb, w_c, w_out, w_up, conv_w, conv_b, w_down, g_final):
    bsz, n_lat, _ = x.shape
    n_ctx = ctx.shape[1]
    assert n_lat % STEP_ROWS == 0 and n_ctx == TOK_TILE and n_lat % GRID_W == 0

    rope_tabs = _rope_tables(n_lat, n_ctx)
    cl, sl = _dft_tables(n_lat, n_lat // 2, (n_lat * FN_GROUP_DIM) ** -0.5)
    cc, sc = _dft_tables(n_ctx, n_ctx, (n_ctx * FN_GROUP_DIM) ** -0.5)
    cg, sg = _dft_tables(FN_GROUP_DIM, FN_GROUP_DIM, 1.0)
    dft_tabs = (cl, sl, cc, sc, jnp.concatenate([cg, -sg], axis=0), _reversal_matrix())

    n_mod_rows = 2 * SUBLANES * (-(-(bsz + 1) // (2 * SUBLANES)))
    cvec = jnp.zeros((n_mod_rows, D_MODEL), F32).at[:bsz].set(c).at[bsz].set(c_ctx)
    mods = _modulation(cvec, w_ada, b_ada).reshape(DEPTH, n_mod_rows, N_MOD, D_MODEL)

    na_bias = _natten_bias(rpb, n_lat // GRID_W)

    w_in, w_a, w_b, w_c, w_out, w_up, w_down = (
        w.astype(BF16) for w in (w_in, w_a, w_b, w_c, w_out, w_up, w_down))

    stream = (x, ctx, 0)
    for l in range(DEPTH):
        with_ctx = l != DEPTH - 1
        lam_init = 0.8 - 0.6 * math.exp(-0.3 * l)
        mod = jnp.stack([mods[l, :bsz],
                         jnp.broadcast_to(mods[l, bsz], (bsz, N_MOD, D_MODEL))], axis=1)
        fa_in, qb, kb, vbt, qc, kc, vct, gates = _in_projection(
            stream, mod, g_mix[l], w_in, l, b_gate[l], rope_tabs, n_lat, n_ctx)
        fa = _fourier_mix(fa_in, dft_tabs, n_lat, n_ctx, with_ctx)
        db = _diff_attention(qb, kb, vbt, lam[l], subln_g[l], lam_init, n_lat, n_ctx, with_ctx)
        nc = _neighbourhood_attention(qc, kc, vct, na_bias, l, n_lat, n_ctx, with_ctx)
        h = _merge(stream, mod, fa, db, nc, gates, w_a, w_b, w_c, w_out, l, n_lat, with_ctx)
        h = _conv_ffn(h, mod, g_ffn[l], w_up, conv_w[l], conv_b[l], w_down, l, g_final, n_lat,
                      with_ctx, final_norm=not with_ctx)
        stream = (h, h, n_lat // n_ctx)
    return h
```

```python
import functools
import math

import jax
import jax.numpy as jnp
from jax import lax
from jax.experimental import pallas as pl
from jax.experimental.pallas import tpu as pltpu

D_MODEL = 1024
DEPTH = 4
GRID_W = 64
FN_GROUPS = 4
FN_GROUP_DIM = 128
FN_WIDTH = FN_GROUPS * FN_GROUP_DIM
DA_HEADS = 4
DA_HEAD_DIM = 64
DA_V_DIM = 2 * DA_HEAD_DIM
NA_HEADS = 8
NA_HEAD_DIM = 64
NA_ROWS = 8
NA_COLS = 16
BRANCH_W = 512
N_BRANCH = 3
ROPE_THETA = 10000.0
ROPE_AXIS_DIM = DA_HEAD_DIM // 2
D_FF = 2816
N_MOD = 6
NORM_EPS = 1e-6
SUBLN_EPS = 1e-5
NEG_INF = -1e30

LANES = 128
SUBLANES = 8
TOK_TILE = 256
SUB_TILES = 2
STEP_ROWS = SUB_TILES * TOK_TILE
FF_CHUNK = 256
FF_LOOKAHEAD = 2
NA_STEP_ROWS = 2
NA_GROUPS = 2
NA_SLAB_ROWS = 10
VMEM_LIMIT = 56 * 1024 * 1024
ONES_ROWS = 16
PV_CHUNK = 256
LOG2_E = math.log2(math.e)

BF16 = jnp.bfloat16
F32 = jnp.float32


def _params(n_grid_dims):
    return pltpu.CompilerParams(dimension_semantics=("arbitrary",) * n_grid_dims,
                                vmem_limit_bytes=VMEM_LIMIT)


def _resident(shape):
    return pl.BlockSpec(shape, lambda *_: (0,) * len(shape), pipeline_mode=pl.Buffered(1))


def _resident_layer(shape, layer):
    return pl.BlockSpec((None,) + tuple(shape), lambda *_: (layer,) + (0,) * len(shape),
                        pipeline_mode=pl.Buffered(1))


def _for_sub_tiles(t, lat_subs, n_subs, body, ctx_body=None):
    full_steps = lat_subs // SUB_TILES
    if n_subs == lat_subs:
        for s in range(SUB_TILES):
            body(s)
        return

    @pl.when(t < full_steps)
    def _():
        for s in range(SUB_TILES):
            body(s)

    @pl.when(t >= full_steps)
    def _():
        for s in range(n_subs - lat_subs):
            (ctx_body or body)(s)


def _nt_dot(a, b):
    return lax.dot_general(a, b, (((1,), (1,)), ((), ())), preferred_element_type=F32)


def _dot(a, b):
    return jnp.dot(a, b, preferred_element_type=F32)


def _norm_modulate(x, g, shift, scale):
    ms = jnp.mean(x * x, axis=-1, keepdims=True)
    return (x * lax.rsqrt(ms + NORM_EPS) * g) * (1.0 + scale) + shift


def _split_bf16(v):
    hi = v.astype(BF16)
    lo = (v - hi.astype(F32)).astype(BF16)
    return hi, lo


def _mod_kernel(c_ref, w_ref, b_ref, o_ref):
    c = c_ref[...]
    s = c / (1.0 + jnp.exp(-c))
    s_hi, s_lo = _split_bf16(s)
    w_hi, w_lo = _split_bf16(w_ref[0])
    acc = _dot(s_hi, w_hi) + (_dot(s_hi, w_lo) + _dot(s_lo, w_hi))
    o_ref[0] = acc + b_ref[0]


def _modulation(cvec, w_ada, b_ada):
    n_rows = cvec.shape[0]
    n_out = N_MOD * D_MODEL
    tn = 1536
    return pl.pallas_call(
        _mod_kernel,
        grid=(DEPTH, n_out // tn),
        in_specs=[
            pl.BlockSpec((n_rows, D_MODEL), lambda l, j: (0, 0)),
            pl.BlockSpec((1, D_MODEL, tn), lambda l, j: (l, 0, j)),
            pl.BlockSpec((1, 1, tn), lambda l, j: (l, 0, j)),
        ],
        out_specs=pl.BlockSpec((1, n_rows, tn), lambda l, j: (l, 0, j)),
        out_shape=jax.ShapeDtypeStruct((DEPTH, n_rows, n_out), F32),
        compiler_params=_params(2),
        name="adaln_modulation",
    )(cvec, w_ada, b_ada.reshape(DEPTH, 1, n_out))


def _rope(p, cos, sin_a, sin_b):
    outs = []
    for k in range(p.shape[1] // LANES):
        xs = p[:, k * LANES:(k + 1) * LANES]
        from_hi = pltpu.roll(xs, LANES - ROPE_AXIS_DIM // 2, axis=1)
        from_lo = pltpu.roll(xs, ROPE_AXIS_DIM // 2, axis=1)
        outs.append(xs * cos + from_hi * sin_a + from_lo * sin_b)
    return jnp.concatenate(outs, axis=1)


def _inproj_kernel(x_ref, xc_ref, mod_ref, g_ref, w_ref, bg_ref, cos_ref, sa_ref, sb_ref,
                   fa_ref, qb_ref, kb_ref, vbt_ref, qc_ref, kc_ref, vct_ref, gate_ref,
                   *, lat_subs, n_subs):
    def sub_tile(s, src_ref=x_ref):
        rows = slice(s * TOK_TILE, (s + 1) * TOK_TILE)
        a = _norm_modulate(src_ref[0, rows, :], g_ref[...], mod_ref[0:1, :],
                           mod_ref[1:2, :]).astype(BF16)
        cos, sin_a, sin_b = cos_ref[rows, :], sa_ref[rows, :], sb_ref[rows, :]

        def proj(seg):
            return _dot(a, w_ref[:, seg * BRANCH_W:(seg + 1) * BRANCH_W])

        fa_ref[0, rows, :] = proj(0).astype(BF16)
        qb_ref[0, rows, :] = (_rope(proj(1), cos, sin_a, sin_b)
                              * (DA_HEAD_DIM ** -0.5 * LOG2_E)).astype(BF16)
        kb_ref[0, rows, :] = _rope(proj(2), cos, sin_a, sin_b).astype(BF16)
        vbt_ref[0, :, rows] = proj(3).T.astype(BF16)
        qc_ref[0, rows, :] = (proj(4) * (NA_HEAD_DIM ** -0.5 * LOG2_E)).astype(BF16)
        kc_ref[0, rows, :] = proj(5).astype(BF16)
        vct_ref[0, :, rows] = proj(6).T.astype(BF16)
        for j in range(N_BRANCH * D_MODEL // BRANCH_W):
            cols = slice(j * BRANCH_W, (j + 1) * BRANCH_W)
            z = proj(7 + j) + bg_ref[:, cols]
            gate_ref[0, rows, cols] = (1.0 / (1.0 + jnp.exp(-z))).astype(BF16)

    _for_sub_tiles(pl.program_id(1), lat_subs, n_subs, sub_tile,
                   functools.partial(sub_tile, src_ref=xc_ref))


def _stream_specs(stream, n_lat):
    _, _, ctx_block = stream
    last = n_lat // STEP_ROWS - 1
    return [pl.BlockSpec((1, STEP_ROWS, D_MODEL), lambda b, t: (b, jnp.minimum(t, last), 0)),
            pl.BlockSpec((1, TOK_TILE, D_MODEL), lambda b, t: (b, ctx_block, 0))]


def _in_projection(stream, mod, g_mix, w_in, layer, b_gate, rope_tabs, n_lat, n_ctx):
    bsz, n_tok = stream[0].shape[0], n_lat + n_ctx
    full_steps = n_lat // STEP_ROWS
    proj_w = w_in.shape[-1]
    tok = lambda w: pl.BlockSpec((1, STEP_ROWS, w), lambda b, t: (b, t, 0))
    tab = pl.BlockSpec((STEP_ROWS, LANES), lambda b, t: (t, 0))
    branch = jax.ShapeDtypeStruct((bsz, n_tok, BRANCH_W), BF16)
    tok_t = pl.BlockSpec((1, BRANCH_W, STEP_ROWS), lambda b, t: (b, 0, t))
    branch_t = jax.ShapeDtypeStruct((bsz, BRANCH_W, n_tok), BF16)
    return pl.pallas_call(
        functools.partial(_inproj_kernel, lat_subs=n_lat // TOK_TILE, n_subs=n_tok // TOK_TILE),
        grid=(bsz, pl.cdiv(n_tok, STEP_ROWS)),
        in_specs=_stream_specs(stream, n_lat) + [
            pl.BlockSpec((None, None, N_MOD, D_MODEL), lambda b, t: (b, t // full_steps, 0, 0)),
            _resident((1, D_MODEL)),
            _resident_layer((D_MODEL, proj_w), layer),
            _resident((1, N_BRANCH * D_MODEL)),
            tab, tab, tab,
        ],
        out_specs=[tok(BRANCH_W)] * 3 + [tok_t] + [tok(BRANCH_W)] * 2 + [tok_t]
                  + [tok(N_BRANCH * D_MODEL)],
        out_shape=[branch] * 3 + [branch_t] + [branch] * 2 + [branch_t]
                  + [jax.ShapeDtypeStruct((bsz, n_tok, N_BRANCH * D_MODEL), BF16)],
        compiler_params=_params(2),
        name="in_projection",
    )(stream[0], stream[1], mod, g_mix.reshape(1, D_MODEL), w_in, b_gate.reshape(1, -1),
      *rope_tabs)


def _dft_kernel(u_ref, cl_ref, sl_ref, cc_ref, sc_ref, csg_ref, rev_ref, o_ref, ue_ref, uo_ref,
                *, n_lat, n_ctx, n_subs, scale):
    t = pl.program_id(1)
    half = n_lat // 2

    def finish(s, p, q):
        rows = slice(s * TOK_TILE, (s + 1) * TOK_TILE)
        for g in range(FN_GROUPS):
            sl = slice(g * FN_GROUP_DIM, (g + 1) * FN_GROUP_DIM)
            pq = jnp.concatenate([p[:, sl], q[:, sl]], axis=1).astype(BF16)
            o_ref[0, rows, sl] = _dot(pq, csg_ref[...]).astype(BF16)

    @pl.when(t == 0)
    def _():
        for b in range(half // TOK_TILE):
            hi = n_lat - (b + 1) * TOK_TILE
            first = u_ref[0, hi:hi + TOK_TILE, :]
            wrap = (u_ref[0, hi + TOK_TILE:hi + 2 * TOK_TILE, :] if b > 0
                    else jnp.zeros((TOK_TILE, FN_WIDTH), BF16))
            rev = _dot(rev_ref[...], jnp.concatenate([first, wrap], axis=0))
            lo = u_ref[0, b * TOK_TILE:(b + 1) * TOK_TILE, :].astype(F32)
            ue_ref[b * TOK_TILE:(b + 1) * TOK_TILE, :] = (lo + rev).astype(BF16)
            uo_ref[b * TOK_TILE:(b + 1) * TOK_TILE, :] = (lo - rev).astype(BF16)

    def latent(s):
        k0 = pl.multiple_of((t * SUB_TILES + s) * TOK_TILE, TOK_TILE)
        k = k0 + lax.broadcasted_iota(jnp.int32, (TOK_TILE, 1), 0)
        sign = (1 - 2 * (k & 1)).astype(F32) * scale
        rows = pl.ds(k0, TOK_TILE)
        p = _dot(cl_ref[rows, :], ue_ref[...]) + sign * u_ref[0, half:half + 1, :].astype(F32)
        finish(s, p, _dot(sl_ref[rows, :], uo_ref[...]))

    def context(s):
        u = u_ref[0, n_lat:n_lat + n_ctx, :]
        finish(s, _dot(cc_ref[...], u), _dot(sc_ref[...], u))

    _for_sub_tiles(t, n_lat // TOK_TILE, n_subs, latent, context)


def _fourier_mix(fa_in, tabs, n_lat, n_ctx, with_ctx):
    bsz, n_tok, _ = fa_in.shape
    n_rows = n_tok if with_ctx else n_lat
    half = n_lat // 2
    assert half % TOK_TILE == 0
    cl, sl, cc, sc, csg, rev = tabs
    lat_tab = _resident((n_lat, half))
    return pl.pallas_call(
        functools.partial(_dft_kernel, n_lat=n_lat, n_ctx=n_ctx, n_subs=n_rows // TOK_TILE,
                          scale=(n_lat * FN_GROUP_DIM) ** -0.5),
        grid=(bsz, pl.cdiv(n_rows, STEP_ROWS)),
        in_specs=[
            pl.BlockSpec((1, n_tok, FN_WIDTH), lambda b, t: (b, 0, 0)),
            lat_tab, lat_tab,
            _resident((n_ctx, n_ctx)), _resident((n_ctx, n_ctx)),
            _resident((2 * FN_GROUP_DIM, FN_GROUP_DIM)),
            _resident((TOK_TILE, 2 * TOK_TILE)),
        ],
        out_specs=pl.BlockSpec((1, STEP_ROWS, FN_WIDTH), lambda b, t: (b, t, 0)),
        out_shape=jax.ShapeDtypeStruct((bsz, n_tok, FN_WIDTH), BF16),
        scratch_shapes=[pltpu.VMEM((half, FN_WIDTH), BF16)] * 2,
        compiler_params=_params(2),
        name="fourier_mix",
    )(fa_in, cl, sl, cc, sc, csg, rev)


def _dft_tables(n, n_cols, scale):
    k = jnp.arange(n, dtype=jnp.int32)
    ang = ((k[:, None] * k[None, :n_cols]) % n).astype(F32) * (2.0 * math.pi / n)
    return (jnp.cos(ang) * scale).astype(BF16), (jnp.sin(ang) * scale).astype(BF16)


def _reversal_matrix():
    i = jnp.arange(TOK_TILE, dtype=jnp.int32)[:, None]
    j = jnp.arange(2 * TOK_TILE, dtype=jnp.int32)[None, :]
    src = jnp.where(i == 0, TOK_TILE, TOK_TILE - i)
    return (j == src).astype(BF16)


def _stack_sub_heads(q, first_half):
    zero = jnp.zeros_like(q)
    return jnp.concatenate([jnp.where(first_half, q, zero), jnp.where(first_half, zero, q)], axis=0)


def _store_scores(s_ref, row, s):
    n = s.shape[0]
    s_ref[row:row + n, :] = s
    return jnp.max(s.reshape(n // SUBLANES, SUBLANES, -1).max(axis=0), axis=0, keepdims=True)


def _softmax_times_values(s_ref, n_rows, m, values_t):
    acc = None
    for r in range(0, n_rows, PV_CHUNK):
        n = min(PV_CHUNK, n_rows - r)
        e = jnp.exp2(s_ref[r:r + n, :] - m).astype(BF16)
        v = values_t(r, n)
        part = _dot(jnp.concatenate([v, jnp.ones((ONES_ROWS, n), BF16)], axis=0), e)
        acc = part if acc is None else acc + part
    n_ch = acc.shape[0] - ONES_ROWS
    return acc[0:n_ch] * (1.0 / acc[n_ch:n_ch + 1])


def _diffattn_kernel(q_ref, k_ref, vt_ref, lam_ref, g_ref, o_ref, s0_ref, s1_ref,
                     *, n_lat, n_ctx, lam_init):
    t = pl.program_id(1)
    lv = lam_ref[...]
    lam = (jnp.exp(jnp.sum(lv[0:1] * lv[1:2], axis=-1, keepdims=True))
           - jnp.exp(jnp.sum(lv[2:3] * lv[3:4], axis=-1, keepdims=True)) + lam_init)
    first_half = lax.broadcasted_iota(jnp.int32, (1, LANES), 1) < DA_HEAD_DIM
    s_refs = (s0_ref, s1_ref)

    def attend(n_sub_tiles, k_lo, k_len):
        items = [(s, h) for s in range(n_sub_tiles) for h in range(DA_HEADS)]

        def scores(i):
            s, h = items[i]
            sl = slice(h * DA_V_DIM, (h + 1) * DA_V_DIM)
            q = q_ref[0, s * TOK_TILE:(s + 1) * TOK_TILE, sl]
            sc = _nt_dot(k_ref[0, k_lo:k_lo + k_len, sl], _stack_sub_heads(q, first_half))
            return _store_scores(s_refs[i % 2], 0, sc)

        m_next = scores(0)
        for i, (s, h) in enumerate(items):
            sl = slice(h * DA_V_DIM, (h + 1) * DA_V_DIM)
            m = m_next
            if i + 1 < len(items):
                m_next = scores(i + 1)
            o12 = _softmax_times_values(
                s_refs[i % 2], k_len, m, lambda r, n: vt_ref[0, sl, k_lo + r:k_lo + r + n]).T
            o = o12[0:TOK_TILE] - lam * o12[TOK_TILE:2 * TOK_TILE]
            ms = jnp.mean(o * o, axis=-1, keepdims=True)
            o = o * lax.rsqrt(ms + SUBLN_EPS) * g_ref[...] * (1.0 - lam_init)
            o_ref[0, s * TOK_TILE:(s + 1) * TOK_TILE, sl] = o.astype(BF16)

    @pl.when(t < n_lat // STEP_ROWS)
    def _():
        attend(SUB_TILES, 0, n_lat + n_ctx)

    @pl.when(t >= n_lat // STEP_ROWS)
    def _():
        attend(n_ctx // TOK_TILE, n_lat, n_ctx)


def _diff_attention(qb, kb, vbt, lam_vec, subln_g, lam_init, n_lat, n_ctx, with_ctx):
    bsz, n_tok, _ = qb.shape
    n_rows = n_tok if with_ctx else n_lat
    full = pl.BlockSpec((1, n_tok, BRANCH_W), lambda b, t: (b, 0, 0))
    full_t = pl.BlockSpec((1, BRANCH_W, n_tok), lambda b, t: (b, 0, 0))
    tile = pl.BlockSpec((1, STEP_ROWS, BRANCH_W), lambda b, t: (b, t, 0))
    return pl.pallas_call(
        functools.partial(_diffattn_kernel, n_lat=n_lat, n_ctx=n_ctx, lam_init=lam_init),
        grid=(bsz, pl.cdiv(n_rows, STEP_ROWS)),
        in_specs=[tile, full, full_t, _resident((4, DA_HEAD_DIM)), _resident((1, DA_V_DIM))],
        out_specs=tile,
        out_shape=jax.ShapeDtypeStruct((bsz, n_tok, BRANCH_W), BF16),
        scratch_shapes=[pltpu.VMEM((n_tok, 2 * TOK_TILE), F32)] * 2,
        compiler_params=_params(2),
        name="diff_attention",
    )(qb, kb, vbt, lam_vec, subln_g.reshape(1, DA_V_DIM))


def _natten_kernel(q_ref, k_ref, vt_ref, *refs, n_lat, n_ctx):
    bias_refs, o_ref, s_refs = refs[:NA_GROUPS], refs[NA_GROUPS], refs[NA_GROUPS + 1:]
    s_idx = pl.program_id(1)
    rows = n_lat // GRID_W
    kh = min(NA_ROWS, rows)
    n_slab = NA_SLAB_ROWS * GRID_W
    n_q = NA_STEP_ROWS * GRID_W
    n_pairs = NA_HEADS // 2
    first_half = lax.broadcasted_iota(jnp.int32, (1, LANES), 1) < NA_HEAD_DIM
    pairs = [slice(p * LANES, (p + 1) * LANES) for p in range(n_pairs)]

    def run(window):
        n_keys = n_ctx + (n_slab if window else 0)
        k_los, maxes = [], []
        for g in range(NA_GROUPS):
            first_row = jnp.clip(NA_STEP_ROWS * (NA_GROUPS * s_idx + g) - kh // 2, 0, rows - kh)
            k_lo = pl.multiple_of(first_row * GRID_W, LANES)
            k_los.append(k_lo)
            for p, sl in enumerate(pairs):
                s_ref = s_refs[g * n_pairs + p]
                qs = _stack_sub_heads(q_ref[0, g * n_q:(g + 1) * n_q, sl], first_half)
                m = _store_scores(s_ref, 0, _nt_dot(k_ref[0, n_lat:n_lat + n_ctx, sl], qs))
                if window:
                    s_win = _nt_dot(k_ref[0, pl.ds(k_lo, n_slab), sl], qs) + bias_refs[g][0, 0, p]
                    m = jnp.maximum(m, _store_scores(s_ref, n_ctx, s_win))
                maxes.append(m)
        for g in range(NA_GROUPS):
            for p, sl in enumerate(pairs):
                def values_t(r, n):
                    if r < n_ctx:
                        return vt_ref[0, sl, n_lat + r:n_lat + r + n]
                    return vt_ref[0, sl, pl.ds(pl.multiple_of(k_los[g] + (r - n_ctx), LANES), n)]

                i = g * n_pairs + p
                o2 = _softmax_times_values(s_refs[i], n_keys, maxes[i], values_t).T
                o_ref[0, g * n_q:(g + 1) * n_q, sl] = jnp.where(
                    first_half, o2[0:n_q], o2[n_q:2 * n_q]).astype(BF16)

    window_steps = rows // (NA_STEP_ROWS * NA_GROUPS)

    @pl.when(s_idx < window_steps)
    def _():
        run(True)

    @pl.when(s_idx >= window_steps)
    def _():
        run(False)


def _natten_step_classes(rows):
    kh = min(NA_ROWS, rows)
    classes, first_steps = [], []
    for s in range(rows // NA_STEP_ROWS):
        slab = min(max(NA_STEP_ROWS * s - kh // 2, 0), rows - kh)
        geom = tuple((r - slab, min(max(r - kh // 2, 0), rows - kh) - slab)
                     for r in range(NA_STEP_ROWS * s, NA_STEP_ROWS * (s + 1)))
        geom = geom + (min(NA_SLAB_ROWS, rows - slab),)
        if not classes or classes[-1] != geom:
            assert geom not in classes
            classes.append(geom)
            first_steps.append(s)
    return classes, first_steps


def _natten_bias_kernel(r_ref, o_ref, *, classes, kh):
    cls = pl.program_id(1)
    kc = lax.broadcasted_iota(jnp.int32, (GRID_W, LANES), 0)
    lane = lax.broadcasted_iota(jnp.int32, (GRID_W, LANES), 1)
    second = lane >= GRID_W
    c = jnp.where(second, lane - GRID_W, lane)
    c0 = jnp.clip(c - NA_COLS // 2, 0, GRID_W - NA_COLS)
    col_ok = jnp.logical_and(kc >= c0, kc < c0 + NA_COLS)
    neg = jnp.full((GRID_W, LANES), NEG_INF, F32)

    def fill(geom):
        for h in range(NA_HEADS):
            for kr in range(NA_SLAB_ROWS):
                halves = []
                for j, (q_off, w_off) in enumerate(geom[:-1]):
                    if w_off <= kr < w_off + kh and kr < geom[-1]:
                        dr = kr - q_off + NA_ROWS - 1
                        row = jnp.broadcast_to(r_ref[0, h, dr:dr + 1, :], (GRID_W, LANES))
                        halves.append(pltpu.roll(row, (j * GRID_W - (NA_COLS - 1)) % LANES, axis=1,
                                                 stride=1, stride_axis=0))
                    else:
                        halves.append(neg)
                val = jnp.where(col_ok, jnp.where(second, halves[1], halves[0]), neg)
                o_ref[0, 0, h // 2, kr * GRID_W:(kr + 1) * GRID_W,
                      (h % 2) * LANES:(h % 2 + 1) * LANES] = val

    for ci, geom in enumerate(classes):
        pl.when(cls == ci)(functools.partial(fill, geom))


def _natten_bias(rpb, rows):
    kh = min(NA_ROWS, rows)
    classes, _ = _natten_step_classes(rows)
    n_dr, n_dc = 2 * NA_ROWS - 1, 2 * NA_COLS - 1
    assert NA_STEP_ROWS == 2 and n_dc <= GRID_W
    r = jnp.pad(rpb[..., ::-1].astype(F32) * LOG2_E,
                ((0, 0), (0, 0), (0, 2 * SUBLANES - n_dr), (0, LANES - n_dc)))
    return pl.pallas_call(
        functools.partial(_natten_bias_kernel, classes=classes, kh=kh),
        grid=(DEPTH, len(classes)),
        in_specs=[pl.BlockSpec((1, NA_HEADS, 2 * SUBLANES, LANES), lambda l, k: (l, 0, 0, 0))],
        out_specs=pl.BlockSpec((1, 1, NA_HEADS // 2, NA_SLAB_ROWS * GRID_W, 2 * LANES),
                               lambda l, k: (l, k, 0, 0, 0)),
        out_shape=jax.ShapeDtypeStruct(
            (DEPTH, len(classes), NA_HEADS // 2, NA_SLAB_ROWS * GRID_W, 2 * LANES), F32),
        compiler_params=_params(2),
        name="natten_bias_tables",
    )(r)


def _neighbourhood_attention(qc, kc, vct, bias, layer, n_lat, n_ctx, with_ctx):
    bsz, n_tok, _ = qc.shape
    rows = n_lat // GRID_W
    kh = min(NA_ROWS, rows)
    n_q = NA_STEP_ROWS * GRID_W
    assert (kh // 2) % NA_STEP_ROWS == 0 and (rows - kh) % NA_STEP_ROWS == 0 and n_q == LANES
    assert (rows - kh + NA_SLAB_ROWS) * GRID_W <= n_tok and kh + NA_STEP_ROWS <= NA_SLAB_ROWS
    assert n_ctx % PV_CHUNK == 0 and (NA_SLAB_ROWS * GRID_W) % LANES == 0
    _, first_steps = _natten_step_classes(rows)
    step_q = NA_GROUPS * n_q
    assert n_lat % step_q == 0 and n_ctx % step_q == 0
    n_steps = (n_tok if with_ctx else n_lat) // step_q

    def bias_spec(g):
        def index(b, s):
            group = NA_GROUPS * s + g
            cls = sum((group >= f).astype(jnp.int32) for f in first_steps[1:])
            return (layer, cls, 0, 0, 0)
        return pl.BlockSpec((1, 1) + bias.shape[2:], index)

    full = pl.BlockSpec((1, n_tok, BRANCH_W), lambda b, s: (b, 0, 0))
    full_t = pl.BlockSpec((1, BRANCH_W, n_tok), lambda b, s: (b, 0, 0))
    tile = pl.BlockSpec((1, step_q, BRANCH_W), lambda b, s: (b, s, 0))
    return pl.pallas_call(
        functools.partial(_natten_kernel, n_lat=n_lat, n_ctx=n_ctx),
        grid=(bsz, n_steps),
        in_specs=[tile, full, full_t] + [bias_spec(g) for g in range(NA_GROUPS)],
        out_specs=tile,
        out_shape=jax.ShapeDtypeStruct((bsz, n_tok, BRANCH_W), BF16),
        scratch_shapes=[pltpu.VMEM((n_ctx + NA_SLAB_ROWS * GRID_W, 2 * n_q), F32)]
                       * (NA_GROUPS * NA_HEADS // 2),
        compiler_params=_params(2),
        name="neighbourhood_attention",
    )(qc, kc, vct, *([bias] * NA_GROUPS))


def _merge_kernel(x_ref, xc_ref, mod_ref, fa_ref, db_ref, nc_ref, gate_ref, wa_ref, wb_ref, wc_ref,
                  wo_ref, o_ref, *, lat_subs, n_subs):
    def sub_tile(s, src_ref=x_ref):
        rows = slice(s * TOK_TILE, (s + 1) * TOK_TILE)

        def gate(j):
            return gate_ref[0, rows, j * D_MODEL:(j + 1) * D_MODEL].astype(F32)

        y = (gate(0) * _dot(fa_ref[0, rows, :], wa_ref[...])
             + gate(1) * _dot(db_ref[0, rows, :], wb_ref[...])
             + gate(2) * _dot(nc_ref[0, rows, :], wc_ref[...]))
        y = _dot(y.astype(BF16), wo_ref[...])
        o_ref[0, rows, :] = src_ref[0, rows, :] + mod_ref[2:3, :] * y

    _for_sub_tiles(pl.program_id(1), lat_subs, n_subs, sub_tile,
                   functools.partial(sub_tile, src_ref=xc_ref))


def _merge(stream, mod, fa, db, nc, gates, w_a, w_b, w_c, w_out, layer, n_lat, with_ctx):
    bsz, n_tok = fa.shape[0], fa.shape[1]
    n_rows = n_tok if with_ctx else n_lat
    full_steps = n_lat // STEP_ROWS
    tok = lambda w: pl.BlockSpec((1, STEP_ROWS, w), lambda b, t: (b, t, 0))
    return pl.pallas_call(
        functools.partial(_merge_kernel, lat_subs=n_lat // TOK_TILE, n_subs=n_rows // TOK_TILE),
        grid=(bsz, pl.cdiv(n_rows, STEP_ROWS)),
        in_specs=_stream_specs(stream, n_lat) + [
            pl.BlockSpec((None, None, N_MOD, D_MODEL), lambda b, t: (b, t // full_steps, 0, 0)),
            tok(BRANCH_W), tok(BRANCH_W), tok(BRANCH_W), tok(N_BRANCH * D_MODEL),
            _resident_layer((BRANCH_W, D_MODEL), layer), _resident_layer((BRANCH_W, D_MODEL), layer),
            _resident_layer((BRANCH_W, D_MODEL), layer), _resident_layer((D_MODEL, D_MODEL), layer),
        ],
        out_specs=tok(D_MODEL),
        out_shape=jax.ShapeDtypeStruct((bsz, n_rows, D_MODEL), F32),
        compiler_params=_params(2),
        name="merge_out_projection",
    )(stream[0], stream[1], mod, fa, db, nc, gates, w_a, w_b, w_c, w_out)


def _convffn_kernel(x_ref, prev_ref, next_ref, mod_ref, g_ref, wu_ref, cw_ref, cb_ref, wd_ref,
                    gf_ref, o_ref, *lhs_refs, lat_subs, n_subs, final_norm):
    t = pl.program_id(1)
    g = g_ref[...]
    shift, scale = mod_ref[3:4, :], mod_ref[4:5, :]
    n_rows = TOK_TILE + 2 * SUBLANES

    def conv(u, col):
        w = cw_ref[:, col:col + FF_CHUNK]
        before = pltpu.roll(u, 1, axis=0)
        after = pltpu.roll(u, n_rows - 1, axis=0)
        v = before * w[0:1] + u * w[1:2] + after * w[2:3] + cb_ref[:, col:col + FF_CHUNK]
        return v[SUBLANES:SUBLANES + TOK_TILE]

    def prepare(s):
        idx = t * SUB_TILES + s
        lo = s * TOK_TILE
        x = x_ref[0, lo:lo + TOK_TILE, :]
        has_prev = jnp.logical_and(idx != 0, idx != lat_subs)
        has_next = jnp.logical_and(idx != lat_subs - 1, idx != n_subs - 1)
        prev = prev_ref[0] if s == 0 else x_ref[0, lo - SUBLANES:lo, :]
        nxt = (next_ref[0] if s == SUB_TILES - 1
               else x_ref[0, lo + TOK_TILE:lo + TOK_TILE + SUBLANES, :])
        lhs_ref = lhs_refs[s]
        lhs_ref[0:SUBLANES, :] = jnp.where(has_prev, _norm_modulate(prev, g, shift, scale), 0.0)
        lhs_ref[SUBLANES:SUBLANES + TOK_TILE, :] = _norm_modulate(x, g, shift, scale)
        lhs_ref[SUBLANES + TOK_TILE:, :] = jnp.where(
            has_next, _norm_modulate(nxt, g, shift, scale), 0.0)
        return x, lhs_ref[...].astype(BF16)

    def run(n_sub_tiles):
        tiles = [prepare(s) for s in range(n_sub_tiles)]
        items = [(j, s) for j in range(D_FF // FF_CHUNK) for s in range(n_sub_tiles)]

        def up(item):
            j, s = item
            lhs = tiles[s][1]
            col_a, col_b = j * FF_CHUNK, D_FF + j * FF_CHUNK
            return (_dot(lhs, wu_ref[:, col_a:col_a + FF_CHUNK]),
                    _dot(lhs, wu_ref[:, col_b:col_b + FF_CHUNK]))

        look = FF_LOOKAHEAD * n_sub_tiles
        accs = [jnp.zeros((TOK_TILE, D_MODEL), F32) for _ in range(n_sub_tiles)]
        ahead = [up(item) for item in items[:look]]
        for i, (j, s) in enumerate(items):
            col_a, col_b = j * FF_CHUNK, D_FF + j * FF_CHUNK
            u_a, u_b = ahead.pop(0)
            if i + look < len(items):
                ahead.append(up(items[i + look]))
            a = conv(u_a, col_a)
            b = conv(u_b, col_b)
            act = (a / (1.0 + jnp.exp(-a))) * b
            accs[s] = accs[s] + _dot(act.astype(BF16), wd_ref[col_a:col_a + FF_CHUNK, :])
        for s in range(n_sub_tiles):
            y = tiles[s][0] + mod_ref[5:6, :] * accs[s]
            if final_norm:
                ms = jnp.mean(y * y, axis=-1, keepdims=True)
                y = y * lax.rsqrt(ms + NORM_EPS) * gf_ref[...]
            o_ref[0, s * TOK_TILE:(s + 1) * TOK_TILE, :] = y

    full_steps = lat_subs // SUB_TILES
    pl.when(t < full_steps)(functools.partial(run, SUB_TILES))
    if n_subs > lat_subs:
        pl.when(t >= full_steps)(functools.partial(run, n_subs - lat_subs))


def _conv_ffn(h, mod, g_ffn, w_up, conv_w, conv_b, w_down, layer, g_final, n_lat, with_ctx,
              final_norm):
    bsz, n_rows_in, _ = h.shape
    n_rows = n_rows_in if with_ctx else n_lat
    lat_subs, n_subs = n_lat // TOK_TILE, n_rows // TOK_TILE
    full_steps = n_lat // STEP_ROWS
    blocks_per_step = STEP_ROWS // SUBLANES
    last_block = n_rows_in // SUBLANES - 1
    return pl.pallas_call(
        functools.partial(_convffn_kernel, lat_subs=lat_subs, n_subs=n_subs,
                          final_norm=final_norm),
        grid=(bsz, pl.cdiv(n_rows, STEP_ROWS)),
        in_specs=[
            pl.BlockSpec((1, STEP_ROWS, D_MODEL), lambda b, t: (b, t, 0)),
            pl.BlockSpec((1, SUBLANES, D_MODEL),
                         lambda b, t: (b, jnp.maximum(t * blocks_per_step - 1, 0), 0)),
            pl.BlockSpec((1, SUBLANES, D_MODEL),
                         lambda b, t: (b, jnp.minimum((t + 1) * blocks_per_step, last_block), 0)),
            pl.BlockSpec((None, None, N_MOD, D_MODEL), lambda b, t: (b, t // full_steps, 0, 0)),
            _resident((1, D_MODEL)),
            _resident_layer((D_MODEL, 2 * D_FF), layer),
            _resident((3, 2 * D_FF)),
            _resident((1, 2 * D_FF)),
            _resident_layer((D_FF, D_MODEL), layer),
            _resident((1, D_MODEL)),
        ],
        out_specs=pl.BlockSpec((1, STEP_ROWS, D_MODEL), lambda b, t: (b, t, 0)),
        out_shape=jax.ShapeDtypeStruct((bsz, n_rows, D_MODEL), F32),
        scratch_shapes=[pltpu.VMEM((TOK_TILE + 2 * SUBLANES, D_MODEL), F32)] * SUB_TILES,
        compiler_params=_params(2),
        name="conv_ffn",
    )(h, h, h, mod, g_ffn.reshape(1, D_MODEL), w_up, conv_w, conv_b.reshape(1, -1), w_down,
      g_final.reshape(1, D_MODEL))


def _rope_tables(n_lat, n_ctx):
    t = jnp.arange(n_lat, dtype=jnp.int32)
    pos = jnp.stack([(t // GRID_W).astype(F32), (t % GRID_W).astype(F32)], axis=1)
    n_freq = ROPE_AXIS_DIM // 2
    inv = ROPE_THETA ** (-jnp.arange(n_freq, dtype=F32) / n_freq)
    ang = pos[:, :, None] * inv
    lane = jnp.arange(LANES, dtype=jnp.int32)
    axis = (lane % DA_HEAD_DIM) // ROPE_AXIS_DIM
    freq = lane % n_freq
    second_half = (lane % ROPE_AXIS_DIM) >= n_freq
    ang_l = ang[:, axis, freq]
    cos, sin = jnp.cos(ang_l), jnp.sin(ang_l)
    sin_a = jnp.where(second_half, 0.0, -sin)
    sin_b = jnp.where(second_half, sin, 0.0)
    pad = lambda a, v: jnp.concatenate([a, jnp.full((n_ctx, LANES), v, F32)], axis=0)
    return pad(cos, 1.0), pad(sin_a, 0.0), pad(sin_b, 0.0)


def kernel(x, c, ctx, c_ctx, w_ada, b_ada, g_mix, g_ffn, w_in, b_gate, w_a, lam, subln_g, w_b,
           rpb, w_c, w_out, w_up, conv_w, conv_b, w_down, g_final):
    bsz, n_lat, _ = x.shape
    n_ctx = ctx.shape[1]
    assert n_lat % STEP_ROWS == 0 and n_ctx == TOK_TILE and n_lat % GRID_W == 0

    rope_tabs = _rope_tables(n_lat, n_ctx)
    cl, sl = _dft_tables(n_lat, n_lat // 2, (n_lat * FN_GROUP_DIM) ** -0.5)
    cc, sc = _dft_tables(n_ctx, n_ctx, (n_ctx * FN_GROUP_DIM) ** -0.5)
    cg, sg = _dft_tables(FN_GROUP_DIM, FN_GROUP_DIM, 1.0)
    dft_tabs = (cl, sl, cc, sc, jnp.concatenate([cg, -sg], axis=0), _reversal_matrix())

    n_mod_rows = 2 * SUBLANES * (-(-(bsz + 1) // (2 * SUBLANES)))
    cvec = jnp.zeros((n_mod_rows, D_MODEL), F32).at[:bsz].set(c).at[bsz].set(c_ctx)
    mods = _modulation(cvec, w_ada, b_ada).reshape(DEPTH, n_mod_rows, N_MOD, D_MODEL)

    na_bias = _natten_bias(rpb, n_lat // GRID_W)

    w_in, w_a, w_b, w_c, w_out, w_up, w_down = (
        w.astype(BF16) for w in (w_in, w_a, w_b, w_c, w_out, w_up, w_down))

    stream = (x, ctx, 0)
    for l in range(DEPTH):
        with_ctx = l != DEPTH - 1
        lam_init = 0.8 - 0.6 * math.exp(-0.3 * l)
        mod = jnp.stack([mods[l, :bsz],
                         jnp.broadcast_to(mods[l, bsz], (bsz, N_MOD, D_MODEL))], axis=1)
        fa_in, qb, kb, vbt, qc, kc, vct, gates = _in_projection(
            stream, mod, g_mix[l], w_in, l, b_gate[l], rope_tabs, n_lat, n_ctx)
        fa = _fourier_mix(fa_in, dft_tabs, n_lat, n_ctx, with_ctx)
        db = _diff_attention(qb, kb, vbt, lam[l], subln_g[l], lam_init, n_lat, n_ctx, with_ctx)
        nc = _neighbourhood_attention(qc, kc, vct, na_bias, l, n_lat, n_ctx, with_ctx)
        h = _merge(stream, mod, fa, db, nc, gates, w_a, w_b, w_c, w_out, l, n_lat, with_ctx)
        h = _conv_ffn(h, mod, g_ffn[l], w_up, conv_w[l], conv_b[l], w_down, l, g_final, n_lat,
                      with_ctx, final_norm=not with_ctx)
        stream = (h, h, n_lat // n_ctx)
    return h
```

```python
import functools
import math

import jax
import jax.numpy as jnp
from jax import lax
from jax.experimental import pallas as pl
from jax.experimental.pallas import tpu as pltpu

D_MODEL = 1024
DEPTH = 4
GRID_W = 64
FN_GROUPS = 4
FN_GROUP_DIM = 128
FN_WIDTH = FN_GROUPS * FN_GROUP_DIM
DA_HEADS = 4
DA_HEAD_DIM = 64
DA_V_DIM = 2 * DA_HEAD_DIM
NA_HEADS = 8
NA_HEAD_DIM = 64
NA_ROWS = 8
NA_COLS = 16
BRANCH_W = 512
N_BRANCH = 3
ROPE_THETA = 10000.0
ROPE_AXIS_DIM = DA_HEAD_DIM // 2
D_FF = 2816
N_MOD = 6
NORM_EPS = 1e-6
SUBLN_EPS = 1e-5
NEG_INF = -1e30

LANES = 128
SUBLANES = 8
TOK_TILE = 256
SUB_TILES = 2
STEP_ROWS = SUB_TILES * TOK_TILE
FF_CHUNK = 256
DA_LOOKAHEAD = 1
FF_LOOKAHEAD = 2
NA_STEP_ROWS = 2
NA_GROUPS = 2
NA_SLAB_ROWS = 10
VMEM_LIMIT = 56 * 1024 * 1024
ONES_ROWS = 16
PV_CHUNK = 256
LOG2_E = math.log2(math.e)

BF16 = jnp.bfloat16
F32 = jnp.float32


def _params(n_grid_dims):
    return pltpu.CompilerParams(dimension_semantics=("arbitrary",) * n_grid_dims,
                                vmem_limit_bytes=VMEM_LIMIT)


def _resident(shape):
    return pl.BlockSpec(shape, lambda *_: (0,) * len(shape), pipeline_mode=pl.Buffered(1))


def _resident_layer(shape, layer):
    return pl.BlockSpec((None,) + tuple(shape), lambda *_: (layer,) + (0,) * len(shape),
                        pipeline_mode=pl.Buffered(1))


def _for_sub_tiles(t, lat_subs, n_subs, body, ctx_body=None):
    full_steps = lat_subs // SUB_TILES
    if n_subs == lat_subs:
        for s in range(SUB_TILES):
            body(s)
        return

    @pl.when(t < full_steps)
    def _():
        for s in range(SUB_TILES):
            body(s)

    @pl.when(t >= full_steps)
    def _():
        for s in range(n_subs - lat_subs):
            (ctx_body or body)(s)


def _nt_dot(a, b):
    return lax.dot_general(a, b, (((1,), (1,)), ((), ())), preferred_element_type=F32)


def _dot(a, b):
    return jnp.dot(a, b, preferred_element_type=F32)


def _norm_modulate(x, g, shift, scale):
    ms = jnp.mean(x * x, axis=-1, keepdims=True)
    return (x * lax.rsqrt(ms + NORM_EPS) * g) * (1.0 + scale) + shift


def _split_bf16(v):
    hi = v.astype(BF16)
    lo = (v - hi.astype(F32)).astype(BF16)
    return hi, lo


def _mod_kernel(c_ref, w_ref, b_ref, o_ref):
    c = c_ref[...]
    s = c / (1.0 + jnp.exp(-c))
    s_hi, s_lo = _split_bf16(s)
    w_hi, w_lo = _split_bf16(w_ref[0])
    acc = _dot(s_hi, w_hi) + (_dot(s_hi, w_lo) + _dot(s_lo, w_hi))
    o_ref[0] = acc + b_ref[0]


def _modulation(cvec, w_ada, b_ada):
    n_rows = cvec.shape[0]
    n_out = N_MOD * D_MODEL
    tn = 1536
    return pl.pallas_call(
        _mod_kernel,
        grid=(DEPTH, n_out // tn),
        in_specs=[
            pl.BlockSpec((n_rows, D_MODEL), lambda l, j: (0, 0)),
            pl.BlockSpec((1, D_MODEL, tn), lambda l, j: (l, 0, j)),
            pl.BlockSpec((1, 1, tn), lambda l, j: (l, 0, j)),
        ],
        out_specs=pl.BlockSpec((1, n_rows, tn), lambda l, j: (l, 0, j)),
        out_shape=jax.ShapeDtypeStruct((DEPTH, n_rows, n_out), F32),
        compiler_params=_params(2),
        name="adaln_modulation",
    )(cvec, w_ada, b_ada.reshape(DEPTH, 1, n_out))


def _rope(p, cos, sin_a, sin_b):
    outs = []
    for k in range(p.shape[1] // LANES):
        xs = p[:, k * LANES:(k + 1) * LANES]
        from_hi = pltpu.roll(xs, LANES - ROPE_AXIS_DIM // 2, axis=1)
        from_lo = pltpu.roll(xs, ROPE_AXIS_DIM // 2, axis=1)
        outs.append(xs * cos + from_hi * sin_a + from_lo * sin_b)
    return jnp.concatenate(outs, axis=1)


def _inproj_kernel(x_ref, xc_ref, mod_ref, g_ref, w_ref, cos_ref, sa_ref, sb_ref,
                   fa_ref, qb_ref, kb_ref, vbt_ref, qc_ref, kc_ref, vct_ref,
                   *, lat_subs, n_subs):
    def sub_tile(s, src_ref=x_ref):
        rows = slice(s * TOK_TILE, (s + 1) * TOK_TILE)
        a = _norm_modulate(src_ref[0, rows, :], g_ref[...], mod_ref[0:1, :],
                           mod_ref[1:2, :]).astype(BF16)
        cos, sin_a, sin_b = cos_ref[rows, :], sa_ref[rows, :], sb_ref[rows, :]

        def proj(seg):
            return _dot(a, w_ref[:, seg * BRANCH_W:(seg + 1) * BRANCH_W])

        fa_ref[0, rows, :] = proj(0).astype(BF16)
        qb_ref[0, rows, :] = (_rope(proj(1), cos, sin_a, sin_b)
                              * (DA_HEAD_DIM ** -0.5 * LOG2_E)).astype(BF16)
        kb_ref[0, rows, :] = _rope(proj(2), cos, sin_a, sin_b).astype(BF16)
        vbt_ref[0, :, rows] = proj(3).T.astype(BF16)
        qc_ref[0, rows, :] = (proj(4) * (NA_HEAD_DIM ** -0.5 * LOG2_E)).astype(BF16)
        kc_ref[0, rows, :] = proj(5).astype(BF16)
        vct_ref[0, :, rows] = proj(6).T.astype(BF16)

    _for_sub_tiles(pl.program_id(1), lat_subs, n_subs, sub_tile,
                   functools.partial(sub_tile, src_ref=xc_ref))


def _stream_specs(stream, n_lat):
    _, _, ctx_block = stream
    last = n_lat // STEP_ROWS - 1
    return [pl.BlockSpec((1, STEP_ROWS, D_MODEL), lambda b, t: (b, jnp.minimum(t, last), 0)),
            pl.BlockSpec((1, TOK_TILE, D_MODEL), lambda b, t: (b, ctx_block, 0))]


def _in_projection(stream, mod, g_mix, w_branch, layer, rope_tabs, n_lat, n_ctx):
    bsz, n_tok = stream[0].shape[0], n_lat + n_ctx
    full_steps = n_lat // STEP_ROWS
    proj_w = w_branch.shape[-1]
    tok = lambda w: pl.BlockSpec((1, STEP_ROWS, w), lambda b, t: (b, t, 0))
    tab = pl.BlockSpec((STEP_ROWS, LANES), lambda b, t: (t, 0))
    branch = jax.ShapeDtypeStruct((bsz, n_tok, BRANCH_W), BF16)
    tok_t = pl.BlockSpec((1, BRANCH_W, STEP_ROWS), lambda b, t: (b, 0, t))
    branch_t = jax.ShapeDtypeStruct((bsz, BRANCH_W, n_tok), BF16)
    return pl.pallas_call(
        functools.partial(_inproj_kernel, lat_subs=n_lat // TOK_TILE, n_subs=n_tok // TOK_TILE),
        grid=(bsz, pl.cdiv(n_tok, STEP_ROWS)),
        in_specs=_stream_specs(stream, n_lat) + [
            pl.BlockSpec((None, None, N_MOD, D_MODEL), lambda b, t: (b, t // full_steps, 0, 0)),
            _resident((1, D_MODEL)),
            _resident_layer((D_MODEL, proj_w), layer),
            tab, tab, tab,
        ],
        out_specs=[tok(BRANCH_W)] * 3 + [tok_t] + [tok(BRANCH_W)] * 2 + [tok_t],
        out_shape=[branch] * 3 + [branch_t] + [branch] * 2 + [branch_t],
        compiler_params=_params(2),
        name="in_projection",
    )(stream[0], stream[1], mod, g_mix.reshape(1, D_MODEL), w_branch, *rope_tabs)


def _dft_kernel(u_ref, cl_ref, sl_ref, cc_ref, sc_ref, csg_ref, rev_ref, o_ref, ue_ref, uo_ref,
                *, n_lat, n_ctx, n_subs, scale):
    t = pl.program_id(1)
    half = n_lat // 2

    def finish(s, p, q):
        rows = slice(s * TOK_TILE, (s + 1) * TOK_TILE)
        for g in range(FN_GROUPS):
            sl = slice(g * FN_GROUP_DIM, (g + 1) * FN_GROUP_DIM)
            pq = jnp.concatenate([p[:, sl], q[:, sl]], axis=1).astype(BF16)
            o_ref[0, rows, sl] = _dot(pq, csg_ref[...]).astype(BF16)

    @pl.when(t == 0)
    def _():
        for b in range(half // TOK_TILE):
            hi = n_lat - (b + 1) * TOK_TILE
            first = u_ref[0, hi:hi + TOK_TILE, :]
            wrap = (u_ref[0, hi + TOK_TILE:hi + 2 * TOK_TILE, :] if b > 0
                    else jnp.zeros((TOK_TILE, FN_WIDTH), BF16))
            rev = _dot(rev_ref[...], jnp.concatenate([first, wrap], axis=0))
            lo = u_ref[0, b * TOK_TILE:(b + 1) * TOK_TILE, :].astype(F32)
            ue_ref[b * TOK_TILE:(b + 1) * TOK_TILE, :] = (lo + rev).astype(BF16)
            uo_ref[b * TOK_TILE:(b + 1) * TOK_TILE, :] = (lo - rev).astype(BF16)

    def latent(s):
        k0 = pl.multiple_of((t * SUB_TILES + s) * TOK_TILE, TOK_TILE)
        k = k0 + lax.broadcasted_iota(jnp.int32, (TOK_TILE, 1), 0)
        sign = (1 - 2 * (k & 1)).astype(F32) * scale
        rows = pl.ds(k0, TOK_TILE)
        p = _dot(cl_ref[rows, :], ue_ref[...]) + sign * u_ref[0, half:half + 1, :].astype(F32)
        finish(s, p, _dot(sl_ref[rows, :], uo_ref[...]))

    def context(s):
        u = u_ref[0, n_lat:n_lat + n_ctx, :]
        finish(s, _dot(cc_ref[...], u), _dot(sc_ref[...], u))

    _for_sub_tiles(t, n_lat // TOK_TILE, n_subs, latent, context)


def _fourier_mix(fa_in, tabs, n_lat, n_ctx, with_ctx):
    bsz, n_tok, _ = fa_in.shape
    n_rows = n_tok if with_ctx else n_lat
    half = n_lat // 2
    assert half % TOK_TILE == 0
    cl, sl, cc, sc, csg, rev = tabs
    lat_tab = _resident((n_lat, half))
    return pl.pallas_call(
        functools.partial(_dft_kernel, n_lat=n_lat, n_ctx=n_ctx, n_subs=n_rows // TOK_TILE,
                          scale=(n_lat * FN_GROUP_DIM) ** -0.5),
        grid=(bsz, pl.cdiv(n_rows, STEP_ROWS)),
        in_specs=[
            pl.BlockSpec((1, n_tok, FN_WIDTH), lambda b, t: (b, 0, 0)),
            lat_tab, lat_tab,
            _resident((n_ctx, n_ctx)), _resident((n_ctx, n_ctx)),
            _resident((2 * FN_GROUP_DIM, FN_GROUP_DIM)),
            _resident((TOK_TILE, 2 * TOK_TILE)),
        ],
        out_specs=pl.BlockSpec((1, STEP_ROWS, FN_WIDTH), lambda b, t: (b, t, 0)),
        out_shape=jax.ShapeDtypeStruct((bsz, n_tok, FN_WIDTH), BF16),
        scratch_shapes=[pltpu.VMEM((half, FN_WIDTH), BF16)] * 2,
        compiler_params=_params(2),
        name="fourier_mix",
    )(fa_in, cl, sl, cc, sc, csg, rev)


def _dft_tables(n, n_cols, scale):
    k = jnp.arange(n, dtype=jnp.int32)
    ang = ((k[:, None] * k[None, :n_cols]) % n).astype(F32) * (2.0 * math.pi / n)
    return (jnp.cos(ang) * scale).astype(BF16), (jnp.sin(ang) * scale).astype(BF16)


def _reversal_matrix():
    i = jnp.arange(TOK_TILE, dtype=jnp.int32)[:, None]
    j = jnp.arange(2 * TOK_TILE, dtype=jnp.int32)[None, :]
    src = jnp.where(i == 0, TOK_TILE, TOK_TILE - i)
    return (j == src).astype(BF16)


def _stack_sub_heads(q, first_half):
    zero = jnp.zeros_like(q)
    return jnp.concatenate([jnp.where(first_half, q, zero), jnp.where(first_half, zero, q)], axis=0)


def _store_scores(s_ref, row, s):
    n = s.shape[0]
    s_ref[row:row + n, :] = s
    return jnp.max(s.reshape(n // SUBLANES, SUBLANES, -1).max(axis=0), axis=0, keepdims=True)


def _softmax_times_values(s_ref, n_rows, m, values_t):
    acc = None
    for r in range(0, n_rows, PV_CHUNK):
        n = min(PV_CHUNK, n_rows - r)
        e = jnp.exp2(s_ref[r:r + n, :] - m).astype(BF16)
        v = values_t(r, n)
        part = _dot(jnp.concatenate([v, jnp.ones((ONES_ROWS, n), BF16)], axis=0), e)
        acc = part if acc is None else acc + part
    n_ch = acc.shape[0] - ONES_ROWS
    return acc[0:n_ch] * (1.0 / acc[n_ch:n_ch + 1])


def _diffattn_kernel(q_ref, k_ref, vt_ref, lam_ref, g_ref, o_ref, *s_refs,
                     n_lat, n_ctx, lam_init):
    t = pl.program_id(1)
    lv = lam_ref[...]
    lam = (jnp.exp(jnp.sum(lv[0:1] * lv[1:2], axis=-1, keepdims=True))
           - jnp.exp(jnp.sum(lv[2:3] * lv[3:4], axis=-1, keepdims=True)) + lam_init)
    first_half = lax.broadcasted_iota(jnp.int32, (1, LANES), 1) < DA_HEAD_DIM

    def attend(n_sub_tiles, k_lo, k_len):
        items = [(s, h) for s in range(n_sub_tiles) for h in range(DA_HEADS)]

        def scores(i):
            s, h = items[i]
            sl = slice(h * DA_V_DIM, (h + 1) * DA_V_DIM)
            q = q_ref[0, s * TOK_TILE:(s + 1) * TOK_TILE, sl]
            sc = _nt_dot(k_ref[0, k_lo:k_lo + k_len, sl], _stack_sub_heads(q, first_half))
            return _store_scores(s_refs[i % len(s_refs)], 0, sc)

        ahead = [scores(i) for i in range(min(DA_LOOKAHEAD, len(items)))]
        for i, (s, h) in enumerate(items):
            sl = slice(h * DA_V_DIM, (h + 1) * DA_V_DIM)
            m = ahead.pop(0)
            if i + DA_LOOKAHEAD < len(items):
                ahead.append(scores(i + DA_LOOKAHEAD))
            o12 = _softmax_times_values(
                s_refs[i % len(s_refs)], k_len, m,
                lambda r, n: vt_ref[0, sl, k_lo + r:k_lo + r + n]).T
            o = o12[0:TOK_TILE] - lam * o12[TOK_TILE:2 * TOK_TILE]
            ms = jnp.mean(o * o, axis=-1, keepdims=True)
            o = o * lax.rsqrt(ms + SUBLN_EPS) * g_ref[...] * (1.0 - lam_init)
            o_ref[0, s * TOK_TILE:(s + 1) * TOK_TILE, sl] = o.astype(BF16)

    @pl.when(t < n_lat // STEP_ROWS)
    def _():
        attend(SUB_TILES, 0, n_lat + n_ctx)

    @pl.when(t >= n_lat // STEP_ROWS)
    def _():
        attend(n_ctx // TOK_TILE, n_lat, n_ctx)


def _diff_attention(qb, kb, vbt, lam_vec, subln_g, lam_init, n_lat, n_ctx, with_ctx):
    bsz, n_tok, _ = qb.shape
    n_rows = n_tok if with_ctx else n_lat
    full = pl.BlockSpec((1, n_tok, BRANCH_W), lambda b, t: (b, 0, 0))
    full_t = pl.BlockSpec((1, BRANCH_W, n_tok), lambda b, t: (b, 0, 0))
    tile = pl.BlockSpec((1, STEP_ROWS, BRANCH_W), lambda b, t: (b, t, 0))
    return pl.pallas_call(
        functools.partial(_diffattn_kernel, n_lat=n_lat, n_ctx=n_ctx, lam_init=lam_init),
        grid=(bsz, pl.cdiv(n_rows, STEP_ROWS)),
        in_specs=[tile, full, full_t, _resident((4, DA_HEAD_DIM)), _resident((1, DA_V_DIM))],
        out_specs=tile,
        out_shape=jax.ShapeDtypeStruct((bsz, n_tok, BRANCH_W), BF16),
        scratch_shapes=[pltpu.VMEM((n_tok, 2 * TOK_TILE), F32)] * (DA_LOOKAHEAD + 1),
        compiler_params=_params(2),
        name="diff_attention",
    )(qb, kb, vbt, lam_vec, subln_g.reshape(1, DA_V_DIM))


def _natten_kernel(q_ref, k_ref, vt_ref, *refs, n_lat, n_ctx):
    bias_refs, o_ref, s_refs = refs[:NA_GROUPS], refs[NA_GROUPS], refs[NA_GROUPS + 1:]
    s_idx = pl.program_id(1)
    rows = n_lat // GRID_W
    kh = min(NA_ROWS, rows)
    n_slab = NA_SLAB_ROWS * GRID_W
    n_q = NA_STEP_ROWS * GRID_W
    n_pairs = NA_HEADS // 2
    first_half = lax.broadcasted_iota(jnp.int32, (1, LANES), 1) < NA_HEAD_DIM
    pairs = [slice(p * LANES, (p + 1) * LANES) for p in range(n_pairs)]

    def run(window, n_groups):
        n_keys = n_ctx + (n_slab if window else 0)
        k_los, maxes = [], []
        for g in range(n_groups):
            first_row = jnp.clip(NA_STEP_ROWS * (NA_GROUPS * s_idx + g) - kh // 2, 0, rows - kh)
            k_lo = pl.multiple_of(first_row * GRID_W, LANES)
            k_los.append(k_lo)
            for p, sl in enumerate(pairs):
                s_ref = s_refs[g * n_pairs + p]
                qs = _stack_sub_heads(q_ref[0, g * n_q:(g + 1) * n_q, sl], first_half)
                m = _store_scores(s_ref, 0, _nt_dot(k_ref[0, n_lat:n_lat + n_ctx, sl], qs))
                if window:
                    s_win = _nt_dot(k_ref[0, pl.ds(k_lo, n_slab), sl], qs) + bias_refs[g][0, 0, p]
                    m = jnp.maximum(m, _store_scores(s_ref, n_ctx, s_win))
                maxes.append(m)
        for g in range(n_groups):
            for p, sl in enumerate(pairs):
                def values_t(r, n):
                    if r < n_ctx:
                        return vt_ref[0, sl, n_lat + r:n_lat + r + n]
                    return vt_ref[0, sl, pl.ds(pl.multiple_of(k_los[g] + (r - n_ctx), LANES), n)]

                i = g * n_pairs + p
                o2 = _softmax_times_values(s_refs[i], n_keys, maxes[i], values_t).T
                o_ref[0, g * n_q:(g + 1) * n_q, sl] = jnp.where(
                    first_half, o2[0:n_q], o2[n_q:2 * n_q]).astype(BF16)

    window_steps = rows // (NA_STEP_ROWS * NA_GROUPS)

    @pl.when(s_idx < window_steps)
    def _():
        run(True, NA_GROUPS)

    @pl.when(s_idx >= window_steps)
    def _():
        run(False, min(NA_GROUPS, n_ctx // n_q))


def _natten_step_classes(rows):
    kh = min(NA_ROWS, rows)
    classes, first_steps = [], []
    for s in range(rows // NA_STEP_ROWS):
        slab = min(max(NA_STEP_ROWS * s - kh // 2, 0), rows - kh)
        geom = tuple((r - slab, min(max(r - kh // 2, 0), rows - kh) - slab)
                     for r in range(NA_STEP_ROWS * s, NA_STEP_ROWS * (s + 1)))
        geom = geom + (min(NA_SLAB_ROWS, rows - slab),)
        if not classes or classes[-1] != geom:
            assert geom not in classes
            classes.append(geom)
            first_steps.append(s)
    return classes, first_steps


def _natten_bias_kernel(r_ref, o_ref, *, classes, kh):
    cls = pl.program_id(1)
    kc = lax.broadcasted_iota(jnp.int32, (GRID_W, LANES), 0)
    lane = lax.broadcasted_iota(jnp.int32, (GRID_W, LANES), 1)
    second = lane >= GRID_W
    c = jnp.where(second, lane - GRID_W, lane)
    c0 = jnp.clip(c - NA_COLS // 2, 0, GRID_W - NA_COLS)
    col_ok = jnp.logical_and(kc >= c0, kc < c0 + NA_COLS)
    neg = jnp.full((GRID_W, LANES), NEG_INF, F32)

    def fill(geom):
        for h in range(NA_HEADS):
            for kr in range(NA_SLAB_ROWS):
                halves = []
                for j, (q_off, w_off) in enumerate(geom[:-1]):
                    if w_off <= kr < w_off + kh and kr < geom[-1]:
                        dr = kr - q_off + NA_ROWS - 1
                        row = jnp.broadcast_to(r_ref[0, h, dr:dr + 1, :], (GRID_W, LANES))
                        halves.append(pltpu.roll(row, (j * GRID_W - (NA_COLS - 1)) % LANES, axis=1,
                                                 stride=1, stride_axis=0))
                    else:
                        halves.append(neg)
                val = jnp.where(col_ok, jnp.where(second, halves[1], halves[0]), neg)
                o_ref[0, 0, h // 2, kr * GRID_W:(kr + 1) * GRID_W,
                      (h % 2) * LANES:(h % 2 + 1) * LANES] = val

    for ci, geom in enumerate(classes):
        pl.when(cls == ci)(functools.partial(fill, geom))


def _natten_bias(rpb, rows):
    kh = min(NA_ROWS, rows)
    classes, _ = _natten_step_classes(rows)
    n_dr, n_dc = 2 * NA_ROWS - 1, 2 * NA_COLS - 1
    assert NA_STEP_ROWS == 2 and n_dc <= GRID_W
    r = jnp.pad(rpb[..., ::-1].astype(F32) * LOG2_E,
                ((0, 0), (0, 0), (0, 2 * SUBLANES - n_dr), (0, LANES - n_dc)))
    return pl.pallas_call(
        functools.partial(_natten_bias_kernel, classes=classes, kh=kh),
        grid=(DEPTH, len(classes)),
        in_specs=[pl.BlockSpec((1, NA_HEADS, 2 * SUBLANES, LANES), lambda l, k: (l, 0, 0, 0))],
        out_specs=pl.BlockSpec((1, 1, NA_HEADS // 2, NA_SLAB_ROWS * GRID_W, 2 * LANES),
                               lambda l, k: (l, k, 0, 0, 0)),
        out_shape=jax.ShapeDtypeStruct(
            (DEPTH, len(classes), NA_HEADS // 2, NA_SLAB_ROWS * GRID_W, 2 * LANES), F32),
        compiler_params=_params(2),
        name="natten_bias_tables",
    )(r)


def _neighbourhood_attention(qc, kc, vct, bias, layer, n_lat, n_ctx, with_ctx):
    bsz, n_tok, _ = qc.shape
    rows = n_lat // GRID_W
    kh = min(NA_ROWS, rows)
    n_q = NA_STEP_ROWS * GRID_W
    assert (kh // 2) % NA_STEP_ROWS == 0 and (rows - kh) % NA_STEP_ROWS == 0 and n_q == LANES
    assert (rows - kh + NA_SLAB_ROWS) * GRID_W <= n_tok and kh + NA_STEP_ROWS <= NA_SLAB_ROWS
    assert n_ctx % PV_CHUNK == 0 and (NA_SLAB_ROWS * GRID_W) % LANES == 0
    _, first_steps = _natten_step_classes(rows)
    step_q = NA_GROUPS * n_q
    assert n_lat % step_q == 0 and n_ctx % n_q == 0 and n_ctx <= step_q
    n_steps = pl.cdiv(n_tok if with_ctx else n_lat, step_q)

    def bias_spec(g):
        def index(b, s):
            group = NA_GROUPS * s + g
            cls = sum((group >= f).astype(jnp.int32) for f in first_steps[1:])
            return (layer, cls, 0, 0, 0)
        return pl.BlockSpec((1, 1) + bias.shape[2:], index)

    full = pl.BlockSpec((1, n_tok, BRANCH_W), lambda b, s: (b, 0, 0))
    full_t = pl.BlockSpec((1, BRANCH_W, n_tok), lambda b, s: (b, 0, 0))
    tile = pl.BlockSpec((1, step_q, BRANCH_W), lambda b, s: (b, s, 0))
    return pl.pallas_call(
        functools.partial(_natten_kernel, n_lat=n_lat, n_ctx=n_ctx),
        grid=(bsz, n_steps),
        in_specs=[tile, full, full_t] + [bias_spec(g) for g in range(NA_GROUPS)],
        out_specs=tile,
        out_shape=jax.ShapeDtypeStruct((bsz, n_tok, BRANCH_W), BF16),
        scratch_shapes=[pltpu.VMEM((n_ctx + NA_SLAB_ROWS * GRID_W, 2 * n_q), F32)]
                       * (NA_GROUPS * NA_HEADS // 2),
        compiler_params=_params(2),
        name="neighbourhood_attention",
    )(qc, kc, vct, *([bias] * NA_GROUPS))


def _merge_kernel(x_ref, xc_ref, mod_ref, g_ref, fa_ref, db_ref, nc_ref, wg_ref, bg_ref, wa_ref,
                  wb_ref, wc_ref, wo_ref, o_ref, *, lat_subs, n_subs):
    def sub_tile(s, src_ref=x_ref):
        rows = slice(s * TOK_TILE, (s + 1) * TOK_TILE)
        x = src_ref[0, rows, :]
        a = _norm_modulate(x, g_ref[...], mod_ref[0:1, :], mod_ref[1:2, :]).astype(BF16)

        def gate(j):
            cols = slice(j * D_MODEL, (j + 1) * D_MODEL)
            z = _dot(a, wg_ref[:, cols]) + bg_ref[:, cols]
            return 1.0 / (1.0 + jnp.exp(-z))

        y = (gate(0) * _dot(fa_ref[0, rows, :], wa_ref[...])
             + gate(1) * _dot(db_ref[0, rows, :], wb_ref[...])
             + gate(2) * _dot(nc_ref[0, rows, :], wc_ref[...]))
        y = _dot(y.astype(BF16), wo_ref[...])
        o_ref[0, rows, :] = x + mod_ref[2:3, :] * y

    _for_sub_tiles(pl.program_id(1), lat_subs, n_subs, sub_tile,
                   functools.partial(sub_tile, src_ref=xc_ref))


def _merge(stream, mod, g_mix, fa, db, nc, w_g, b_gate, w_a, w_b, w_c, w_out, layer, n_lat,
           with_ctx):
    bsz, n_tok = fa.shape[0], fa.shape[1]
    n_rows = n_tok if with_ctx else n_lat
    full_steps = n_lat // STEP_ROWS
    tok = lambda w: pl.BlockSpec((1, STEP_ROWS, w), lambda b, t: (b, t, 0))
    return pl.pallas_call(
        functools.partial(_merge_kernel, lat_subs=n_lat // TOK_TILE, n_subs=n_rows // TOK_TILE),
        grid=(bsz, pl.cdiv(n_rows, STEP_ROWS)),
        in_specs=_stream_specs(stream, n_lat) + [
            pl.BlockSpec((None, None, N_MOD, D_MODEL), lambda b, t: (b, t // full_steps, 0, 0)),
            _resident((1, D_MODEL)),
            tok(BRANCH_W), tok(BRANCH_W), tok(BRANCH_W),
            _resident_layer((D_MODEL, N_BRANCH * D_MODEL), layer),
            _resident((1, N_BRANCH * D_MODEL)),
            _resident_layer((BRANCH_W, D_MODEL), layer), _resident_layer((BRANCH_W, D_MODEL), layer),
            _resident_layer((BRANCH_W, D_MODEL), layer), _resident_layer((D_MODEL, D_MODEL), layer),
        ],
        out_specs=tok(D_MODEL),
        out_shape=jax.ShapeDtypeStruct((bsz, n_rows, D_MODEL), F32),
        compiler_params=_params(2),
        name="merge_out_projection",
    )(stream[0], stream[1], mod, g_mix.reshape(1, D_MODEL), fa, db, nc, w_g,
      b_gate.reshape(1, -1), w_a, w_b, w_c, w_out)


def _convffn_kernel(x_ref, prev_ref, next_ref, mod_ref, g_ref, wu_ref, cw_ref, cb_ref, wd_ref,
                    gf_ref, o_ref, *lhs_refs, lat_subs, n_subs, final_norm):
    t = pl.program_id(1)
    g = g_ref[...]
    shift, scale = mod_ref[3:4, :], mod_ref[4:5, :]
    n_rows = TOK_TILE + 2 * SUBLANES

    def conv(u, col):
        w = cw_ref[:, col:col + FF_CHUNK]
        before = pltpu.roll(u, 1, axis=0)
        after = pltpu.roll(u, n_rows - 1, axis=0)
        v = before * w[0:1] + u * w[1:2] + after * w[2:3] + cb_ref[:, col:col + FF_CHUNK]
        return v[SUBLANES:SUBLANES + TOK_TILE]

    def prepare(s):
        idx = t * SUB_TILES + s
        lo = s * TOK_TILE
        x = x_ref[0, lo:lo + TOK_TILE, :]
        has_prev = jnp.logical_and(idx != 0, idx != lat_subs)
        has_next = jnp.logical_and(idx != lat_subs - 1, idx != n_subs - 1)
        prev = prev_ref[0] if s == 0 else x_ref[0, lo - SUBLANES:lo, :]
        nxt = (next_ref[0] if s == SUB_TILES - 1
               else x_ref[0, lo + TOK_TILE:lo + TOK_TILE + SUBLANES, :])
        lhs_ref = lhs_refs[s]
        lhs_ref[0:SUBLANES, :] = jnp.where(has_prev, _norm_modulate(prev, g, shift, scale), 0.0)
        lhs_ref[SUBLANES:SUBLANES + TOK_TILE, :] = _norm_modulate(x, g, shift, scale)
        lhs_ref[SUBLANES + TOK_TILE:, :] = jnp.where(
            has_next, _norm_modulate(nxt, g, shift, scale), 0.0)
        return x, lhs_ref[...].astype(BF16)

    def run(n_sub_tiles):
        tiles = [prepare(s) for s in range(n_sub_tiles)]
        items = [(j, s) for j in range(D_FF // FF_CHUNK) for s in range(n_sub_tiles)]

        def up(item):
            j, s = item
            lhs = tiles[s][1]
            col_a, col_b = j * FF_CHUNK, D_FF + j * FF_CHUNK
            return (_dot(lhs, wu_ref[:, col_a:col_a + FF_CHUNK]),
                    _dot(lhs, wu_ref[:, col_b:col_b + FF_CHUNK]))

        look = FF_LOOKAHEAD * n_sub_tiles
        accs = [jnp.zeros((TOK_TILE, D_MODEL), F32) for _ in range(n_sub_tiles)]
        ahead = [up(item) for item in items[:look]]
        for i, (j, s) in enumerate(items):
            col_a, col_b = j * FF_CHUNK, D_FF + j * FF_CHUNK
            u_a, u_b = ahead.pop(0)
            if i + look < len(items):
                ahead.append(up(items[i + look]))
            a = conv(u_a, col_a)
            b = conv(u_b, col_b)
            act = (a / (1.0 + jnp.exp(-a))) * b
            accs[s] = accs[s] + _dot(act.astype(BF16), wd_ref[col_a:col_a + FF_CHUNK, :])
        for s in range(n_sub_tiles):
            y = tiles[s][0] + mod_ref[5:6, :] * accs[s]
            if final_norm:
                ms = jnp.mean(y * y, axis=-1, keepdims=True)
                y = y * lax.rsqrt(ms + NORM_EPS) * gf_ref[...]
            o_ref[0, s * TOK_TILE:(s + 1) * TOK_TILE, :] = y

    full_steps = lat_subs // SUB_TILES
    pl.when(t < full_steps)(functools.partial(run, SUB_TILES))
    if n_subs > lat_subs:
        pl.when(t >= full_steps)(functools.partial(run, n_subs - lat_subs))


def _conv_ffn(h, mod, g_ffn, w_up, conv_w, conv_b, w_down, layer, g_final, n_lat, with_ctx,
              final_norm):
    bsz, n_rows_in, _ = h.shape
    n_rows = n_rows_in if with_ctx else n_lat
    lat_subs, n_subs = n_lat // TOK_TILE, n_rows // TOK_TILE
    full_steps = n_lat // STEP_ROWS
    blocks_per_step = STEP_ROWS // SUBLANES
    last_block = n_rows_in // SUBLANES - 1
    return pl.pallas_call(
        functools.partial(_convffn_kernel, lat_subs=lat_subs, n_subs=n_subs,
                          final_norm=final_norm),
        grid=(bsz, pl.cdiv(n_rows, STEP_ROWS)),
        in_specs=[
            pl.BlockSpec((1, STEP_ROWS, D_MODEL), lambda b, t: (b, t, 0)),
            pl.BlockSpec((1, SUBLANES, D_MODEL),
                         lambda b, t: (b, jnp.maximum(t * blocks_per_step - 1, 0), 0)),
            pl.BlockSpec((1, SUBLANES, D_MODEL),
                         lambda b, t: (b, jnp.minimum((t + 1) * blocks_per_step, last_block), 0)),
            pl.BlockSpec((None, None, N_MOD, D_MODEL), lambda b, t: (b, t // full_steps, 0, 0)),
            _resident((1, D_MODEL)),
            _resident_layer((D_MODEL, 2 * D_FF), layer),
            _resident((3, 2 * D_FF)),
            _resident((1, 2 * D_FF)),
            _resident_layer((D_FF, D_MODEL), layer),
            _resident((1, D_MODEL)),
        ],
        out_specs=pl.BlockSpec((1, STEP_ROWS, D_MODEL), lambda b, t: (b, t, 0)),
        out_shape=jax.ShapeDtypeStruct((bsz, n_rows, D_MODEL), F32),
        scratch_shapes=[pltpu.VMEM((TOK_TILE + 2 * SUBLANES, D_MODEL), F32)] * SUB_TILES,
        compiler_params=_params(2),
        name="conv_ffn",
    )(h, h, h, mod, g_ffn.reshape(1, D_MODEL), w_up, conv_w, conv_b.reshape(1, -1), w_down,
      g_final.reshape(1, D_MODEL))


def _rope_tables(n_lat, n_ctx):
    t = jnp.arange(n_lat, dtype=jnp.int32)
    pos = jnp.stack([(t // GRID_W).astype(F32), (t % GRID_W).astype(F32)], axis=1)
    n_freq = ROPE_AXIS_DIM // 2
    inv = ROPE_THETA ** (-jnp.arange(n_freq, dtype=F32) / n_freq)
    ang = pos[:, :, None] * inv
    lane = jnp.arange(LANES, dtype=jnp.int32)
    axis = (lane % DA_HEAD_DIM) // ROPE_AXIS_DIM
    freq = lane % n_freq
    second_half = (lane % ROPE_AXIS_DIM) >= n_freq
    ang_l = ang[:, axis, freq]
    cos, sin = jnp.cos(ang_l), jnp.sin(ang_l)
    sin_a = jnp.where(second_half, 0.0, -sin)
    sin_b = jnp.where(second_half, sin, 0.0)
    pad = lambda a, v: jnp.concatenate([a, jnp.full((n_ctx, LANES), v, F32)], axis=0)
    return pad(cos, 1.0), pad(sin_a, 0.0), pad(sin_b, 0.0)


def kernel(x, c, ctx, c_ctx, w_ada, b_ada, g_mix, g_ffn, w_in, b_gate, w_a, lam, subln_g, w_b,
           rpb, w_c, w_out, w_up, conv_w, conv_b, w_down, g_final):
    bsz, n_lat, _ = x.shape
    n_ctx = ctx.shape[1]
    assert n_lat % STEP_ROWS == 0 and n_ctx == TOK_TILE and n_lat % GRID_W == 0

    rope_tabs = _rope_tables(n_lat, n_ctx)
    cl, sl = _dft_tables(n_lat, n_lat // 2, (n_lat * FN_GROUP_DIM) ** -0.5)
    cc, sc = _dft_tables(n_ctx, n_ctx, (n_ctx * FN_GROUP_DIM) ** -0.5)
    cg, sg = _dft_tables(FN_GROUP_DIM, FN_GROUP_DIM, 1.0)
    dft_tabs = (cl, sl, cc, sc, jnp.concatenate([cg, -sg], axis=0), _reversal_matrix())

    n_mod_rows = 2 * SUBLANES * (-(-(bsz + 1) // (2 * SUBLANES)))
    cvec = jnp.zeros((n_mod_rows, D_MODEL), F32).at[:bsz].set(c).at[bsz].set(c_ctx)
    mods = _modulation(cvec, w_ada, b_ada).reshape(DEPTH, n_mod_rows, N_MOD, D_MODEL)

    na_bias = _natten_bias(rpb, n_lat // GRID_W)

    n_branch_cols = w_in.shape[-1] - N_BRANCH * D_MODEL
    w_branch, w_gate, w_a, w_b, w_c, w_out, w_up, w_down = (
        w.astype(BF16) for w in (w_in[..., :n_branch_cols], w_in[..., n_branch_cols:],
                                 w_a, w_b, w_c, w_out, w_up, w_down))

    stream = (x, ctx, 0)
    for l in range(DEPTH):
        with_ctx = l != DEPTH - 1
        lam_init = 0.8 - 0.6 * math.exp(-0.3 * l)
        mod = jnp.stack([mods[l, :bsz],
                         jnp.broadcast_to(mods[l, bsz], (bsz, N_MOD, D_MODEL))], axis=1)
        fa_in, qb, kb, vbt, qc, kc, vct = _in_projection(
            stream, mod, g_mix[l], w_branch, l, rope_tabs, n_lat, n_ctx)
        fa = _fourier_mix(fa_in, dft_tabs, n_lat, n_ctx, with_ctx)
        db = _diff_attention(qb, kb, vbt, lam[l], subln_g[l], lam_init, n_lat, n_ctx, with_ctx)
        nc = _neighbourhood_attention(qc, kc, vct, na_bias, l, n_lat, n_ctx, with_ctx)
        h = _merge(stream, mod, g_mix[l], fa, db, nc, w_gate, b_gate[l], w_a, w_b, w_c, w_out, l,
                   n_lat, with_ctx)
        h = _conv_ffn(h, mod, g_ffn[l], w_up, conv_w[l], conv_b[l], w_down, l, g_final, n_lat,
                      with_ctx, final_norm=not with_ctx)
        stream = (h, h, n_lat // n_ctx)
    return h
```

```python
import functools
import math

import jax
import jax.numpy as jnp
from jax import lax
from jax.experimental import pallas as pl
from jax.experimental.pallas import tpu as pltpu

D_MODEL = 1024
DEPTH = 4
GRID_W = 64
FN_GROUPS = 4
FN_GROUP_DIM = 128
FN_WIDTH = FN_GROUPS * FN_GROUP_DIM
DA_HEADS = 4
DA_HEAD_DIM = 64
DA_V_DIM = 2 * DA_HEAD_DIM
NA_HEADS = 8
NA_HEAD_DIM = 64
NA_ROWS = 8
NA_COLS = 16
BRANCH_W = 512
N_BRANCH = 3
ROPE_THETA = 10000.0
ROPE_AXIS_DIM = DA_HEAD_DIM // 2
D_FF = 2816
N_MOD = 6
NORM_EPS = 1e-6
SUBLN_EPS = 1e-5
NEG_INF = -1e30

LANES = 128
SUBLANES = 8
TOK_TILE = 256
SUB_TILES = 2
STEP_ROWS = SUB_TILES * TOK_TILE
FF_CHUNK = 256
DA_LOOKAHEAD = 1
FF_LOOKAHEAD = 2
NA_STEP_ROWS = 2
NA_GROUPS = 2
NA_SLAB_ROWS = 10
VMEM_LIMIT = 56 * 1024 * 1024
ONES_ROWS = 16
PV_CHUNK = 256
LOG2_E = math.log2(math.e)

BF16 = jnp.bfloat16
F32 = jnp.float32


def _params(n_grid_dims):
    return pltpu.CompilerParams(dimension_semantics=("arbitrary",) * n_grid_dims,
                                vmem_limit_bytes=VMEM_LIMIT)


def _resident(shape):
    return pl.BlockSpec(shape, lambda *_: (0,) * len(shape), pipeline_mode=pl.Buffered(1))


def _resident_layer(shape, layer):
    return pl.BlockSpec((None,) + tuple(shape), lambda *_: (layer,) + (0,) * len(shape),
                        pipeline_mode=pl.Buffered(1))


def _for_sub_tiles(t, lat_subs, n_subs, body, ctx_body=None):
    full_steps = lat_subs // SUB_TILES
    if n_subs == lat_subs:
        for s in range(SUB_TILES):
            body(s)
        return

    @pl.when(t < full_steps)
    def _():
        for s in range(SUB_TILES):
            body(s)

    @pl.when(t >= full_steps)
    def _():
        for s in range(n_subs - lat_subs):
            (ctx_body or body)(s)


def _nt_dot(a, b):
    return lax.dot_general(a, b, (((1,), (1,)), ((), ())), preferred_element_type=F32)


def _dot(a, b):
    return jnp.dot(a, b, preferred_element_type=F32)


def _norm_modulate(x, g, shift, scale):
    ms = jnp.mean(x * x, axis=-1, keepdims=True)
    return (x * lax.rsqrt(ms + NORM_EPS) * g) * (1.0 + scale) + shift


def _split_bf16(v):
    hi = v.astype(BF16)
    lo = (v - hi.astype(F32)).astype(BF16)
    return hi, lo


def _mod_kernel(c_ref, w_ref, b_ref, o_ref):
    c = c_ref[...]
    s = c / (1.0 + jnp.exp(-c))
    s_hi, s_lo = _split_bf16(s)
    w_hi, w_lo = _split_bf16(w_ref[0])
    acc = _dot(s_hi, w_hi) + (_dot(s_hi, w_lo) + _dot(s_lo, w_hi))
    o_ref[0] = acc + b_ref[0]


def _modulation(cvec, w_ada, b_ada):
    n_rows = cvec.shape[0]
    n_out = N_MOD * D_MODEL
    tn = 1536
    return pl.pallas_call(
        _mod_kernel,
        grid=(DEPTH, n_out // tn),
        in_specs=[
            pl.BlockSpec((n_rows, D_MODEL), lambda l, j: (0, 0)),
            pl.BlockSpec((1, D_MODEL, tn), lambda l, j: (l, 0, j)),
            pl.BlockSpec((1, 1, tn), lambda l, j: (l, 0, j)),
        ],
        out_specs=pl.BlockSpec((1, n_rows, tn), lambda l, j: (l, 0, j)),
        out_shape=jax.ShapeDtypeStruct((DEPTH, n_rows, n_out), F32),
        compiler_params=_params(2),
        name="adaln_modulation",
    )(cvec, w_ada, b_ada.reshape(DEPTH, 1, n_out))


def _rope(p, cos, sin_a, sin_b):
    outs = []
    for k in range(p.shape[1] // LANES):
        xs = p[:, k * LANES:(k + 1) * LANES]
        from_hi = pltpu.roll(xs, LANES - ROPE_AXIS_DIM // 2, axis=1)
        from_lo = pltpu.roll(xs, ROPE_AXIS_DIM // 2, axis=1)
        outs.append(xs * cos + from_hi * sin_a + from_lo * sin_b)
    return jnp.concatenate(outs, axis=1)


def _inproj_kernel(x_ref, xc_ref, mod_ref, g_ref, w_ref, cos_ref, sa_ref, sb_ref,
                   fa_ref, qb_ref, kb_ref, vbt_ref, qc_ref, kc_ref, vct_ref,
                   *, lat_subs, n_subs):
    def sub_tile(s, src_ref=x_ref):
        rows = slice(s * TOK_TILE, (s + 1) * TOK_TILE)
        a = _norm_modulate(src_ref[0, rows, :], g_ref[...], mod_ref[0:1, :],
                           mod_ref[1:2, :]).astype(BF16)
        cos, sin_a, sin_b = cos_ref[rows, :], sa_ref[rows, :], sb_ref[rows, :]

        def proj(seg):
            return _dot(a, w_ref[:, seg * BRANCH_W:(seg + 1) * BRANCH_W])

        fa_ref[0, rows, :] = proj(0).astype(BF16)
        qb_ref[0, rows, :] = (_rope(proj(1), cos, sin_a, sin_b)
                              * (DA_HEAD_DIM ** -0.5 * LOG2_E)).astype(BF16)
        kb_ref[0, rows, :] = _rope(proj(2), cos, sin_a, sin_b).astype(BF16)
        vbt_ref[0, :, rows] = proj(3).T.astype(BF16)
        qc_ref[0, rows, :] = (proj(4) * (NA_HEAD_DIM ** -0.5 * LOG2_E)).astype(BF16)
        kc_ref[0, rows, :] = proj(5).astype(BF16)
        vct_ref[0, :, rows] = proj(6).T.astype(BF16)

    _for_sub_tiles(pl.program_id(1), lat_subs, n_subs, sub_tile,
                   functools.partial(sub_tile, src_ref=xc_ref))


def _stream_specs(stream, n_lat):
    _, _, ctx_block = stream
    last = n_lat // STEP_ROWS - 1
    return [pl.BlockSpec((1, STEP_ROWS, D_MODEL), lambda b, t: (b, jnp.minimum(t, last), 0)),
            pl.BlockSpec((1, TOK_TILE, D_MODEL), lambda b, t: (b, ctx_block, 0))]


def _in_projection(stream, mod, g_mix, w_branch, layer, rope_tabs, n_lat, n_ctx):
    bsz, n_tok = stream[0].shape[0], n_lat + n_ctx
    full_steps = n_lat // STEP_ROWS
    proj_w = w_branch.shape[-1]
    tok = lambda w: pl.BlockSpec((1, STEP_ROWS, w), lambda b, t: (b, t, 0))
    tab = pl.BlockSpec((STEP_ROWS, LANES), lambda b, t: (t, 0))
    branch = jax.ShapeDtypeStruct((bsz, n_tok, BRANCH_W), BF16)
    tok_t = pl.BlockSpec((1, BRANCH_W, STEP_ROWS), lambda b, t: (b, 0, t))
    branch_t = jax.ShapeDtypeStruct((bsz, BRANCH_W, n_tok), BF16)
    return pl.pallas_call(
        functools.partial(_inproj_kernel, lat_subs=n_lat // TOK_TILE, n_subs=n_tok // TOK_TILE),
        grid=(bsz, pl.cdiv(n_tok, STEP_ROWS)),
        in_specs=_stream_specs(stream, n_lat) + [
            pl.BlockSpec((None, None, N_MOD, D_MODEL), lambda b, t: (b, t // full_steps, 0, 0)),
            _resident((1, D_MODEL)),
            _resident_layer((D_MODEL, proj_w), layer),
            tab, tab, tab,
        ],
        out_specs=[tok(BRANCH_W)] * 3 + [tok_t] + [tok(BRANCH_W)] * 2 + [tok_t],
        out_shape=[branch] * 3 + [branch_t] + [branch] * 2 + [branch_t],
        compiler_params=_params(2),
        name="in_projection",
    )(stream[0], stream[1], mod, g_mix.reshape(1, D_MODEL), w_branch, *rope_tabs)


def _dft_kernel(u_ref, cl_ref, sl_ref, cc_ref, sc_ref, csg_ref, rev_ref, o_ref, ue_ref, uo_ref,
                *, n_lat, n_ctx, n_subs, scale):
    t = pl.program_id(1)
    half = n_lat // 2

    def finish(s, p, q):
        rows = slice(s * TOK_TILE, (s + 1) * TOK_TILE)
        for g in range(FN_GROUPS):
            sl = slice(g * FN_GROUP_DIM, (g + 1) * FN_GROUP_DIM)
            pq = jnp.concatenate([p[:, sl], q[:, sl]], axis=1).astype(BF16)
            o_ref[0, rows, sl] = _dot(pq, csg_ref[...]).astype(BF16)

    @pl.when(t == 0)
    def _():
        for b in range(half // TOK_TILE):
            hi = n_lat - (b + 1) * TOK_TILE
            first = u_ref[0, hi:hi + TOK_TILE, :]
            wrap = (u_ref[0, hi + TOK_TILE:hi + 2 * TOK_TILE, :] if b > 0
                    else jnp.zeros((TOK_TILE, FN_WIDTH), BF16))
            rev = _dot(rev_ref[...], jnp.concatenate([first, wrap], axis=0))
            lo = u_ref[0, b * TOK_TILE:(b + 1) * TOK_TILE, :].astype(F32)
            ue_ref[b * TOK_TILE:(b + 1) * TOK_TILE, :] = (lo + rev).astype(BF16)
            uo_ref[b * TOK_TILE:(b + 1) * TOK_TILE, :] = (lo - rev).astype(BF16)

    def latent(s):
        k0 = pl.multiple_of((t * SUB_TILES + s) * TOK_TILE, TOK_TILE)
        k = k0 + lax.broadcasted_iota(jnp.int32, (TOK_TILE, 1), 0)
        sign = (1 - 2 * (k & 1)).astype(F32) * scale
        rows = pl.ds(k0, TOK_TILE)
        p = _dot(cl_ref[rows, :], ue_ref[...]) + sign * u_ref[0, half:half + 1, :].astype(F32)
        finish(s, p, _dot(sl_ref[rows, :], uo_ref[...]))

    def context(s):
        u = u_ref[0, n_lat:n_lat + n_ctx, :]
        finish(s, _dot(cc_ref[...], u), _dot(sc_ref[...], u))

    _for_sub_tiles(t, n_lat // TOK_TILE, n_subs, latent, context)


def _fourier_mix(fa_in, tabs, n_lat, n_ctx, with_ctx):
    bsz, n_tok, _ = fa_in.shape
    n_rows = n_tok if with_ctx else n_lat
    half = n_lat // 2
    assert half % TOK_TILE == 0
    cl, sl, cc, sc, csg, rev = tabs
    lat_tab = _resident((n_lat, half))
    return pl.pallas_call(
        functools.partial(_dft_kernel, n_lat=n_lat, n_ctx=n_ctx, n_subs=n_rows // TOK_TILE,
                          scale=(n_lat * FN_GROUP_DIM) ** -0.5),
        grid=(bsz, pl.cdiv(n_rows, STEP_ROWS)),
        in_specs=[
            pl.BlockSpec((1, n_tok, FN_WIDTH), lambda b, t: (b, 0, 0)),
            lat_tab, lat_tab,
            _resident((n_ctx, n_ctx)), _resident((n_ctx, n_ctx)),
            _resident((2 * FN_GROUP_DIM, FN_GROUP_DIM)),
            _resident((TOK_TILE, 2 * TOK_TILE)),
        ],
        out_specs=pl.BlockSpec((1, STEP_ROWS, FN_WIDTH), lambda b, t: (b, t, 0)),
        out_shape=jax.ShapeDtypeStruct((bsz, n_tok, FN_WIDTH), BF16),
        scratch_shapes=[pltpu.VMEM((half, FN_WIDTH), BF16)] * 2,
        compiler_params=_params(2),
        name="fourier_mix",
    )(fa_in, cl, sl, cc, sc, csg, rev)


def _dft_tables(n, n_cols, scale):
    k = jnp.arange(n, dtype=jnp.int32)
    ang = ((k[:, None] * k[None, :n_cols]) % n).astype(F32) * (2.0 * math.pi / n)
    return (jnp.cos(ang) * scale).astype(BF16), (jnp.sin(ang) * scale).astype(BF16)


def _reversal_matrix():
    i = jnp.arange(TOK_TILE, dtype=jnp.int32)[:, None]
    j = jnp.arange(2 * TOK_TILE, dtype=jnp.int32)[None, :]
    src = jnp.where(i == 0, TOK_TILE, TOK_TILE - i)
    return (j == src).astype(BF16)


def _stack_sub_heads(q, first_half):
    zero = jnp.zeros_like(q)
    return jnp.concatenate([jnp.where(first_half, q, zero), jnp.where(first_half, zero, q)], axis=0)


def _store_scores(s_ref, row, s):
    n = s.shape[0]
    s_ref[row:row + n, :] = s
    return jnp.max(s.reshape(n // SUBLANES, SUBLANES, -1).max(axis=0), axis=0, keepdims=True)


def _softmax_times_values(s_ref, n_rows, m, values_t):
    acc = None
    for r in range(0, n_rows, PV_CHUNK):
        n = min(PV_CHUNK, n_rows - r)
        e = jnp.exp2(s_ref[r:r + n, :] - m).astype(BF16)
        v = values_t(r, n)
        part = _dot(jnp.concatenate([v, jnp.ones((ONES_ROWS, n), BF16)], axis=0), e)
        acc = part if acc is None else acc + part
    n_ch = acc.shape[0] - ONES_ROWS
    return acc[0:n_ch] * (1.0 / acc[n_ch:n_ch + 1])


def _diffattn_kernel(q_ref, k_ref, vt_ref, lam_ref, g_ref, o_ref, *s_refs,
                     n_lat, n_ctx, lam_init):
    t = pl.program_id(1)
    lv = lam_ref[...]
    lam = (jnp.exp(jnp.sum(lv[0:1] * lv[1:2], axis=-1, keepdims=True))
           - jnp.exp(jnp.sum(lv[2:3] * lv[3:4], axis=-1, keepdims=True)) + lam_init)
    first_half = lax.broadcasted_iota(jnp.int32, (1, LANES), 1) < DA_HEAD_DIM

    def attend(n_sub_tiles, k_lo, k_len):
        items = [(s, h) for s in range(n_sub_tiles) for h in range(DA_HEADS)]

        def scores(i):
            s, h = items[i]
            sl = slice(h * DA_V_DIM, (h + 1) * DA_V_DIM)
            q = q_ref[0, s * TOK_TILE:(s + 1) * TOK_TILE, sl]
            sc = _nt_dot(k_ref[0, k_lo:k_lo + k_len, sl], _stack_sub_heads(q, first_half))
            return _store_scores(s_refs[i % len(s_refs)], 0, sc)

        ahead = [scores(i) for i in range(min(DA_LOOKAHEAD, len(items)))]
        for i, (s, h) in enumerate(items):
            sl = slice(h * DA_V_DIM, (h + 1) * DA_V_DIM)
            m = ahead.pop(0)
            if i + DA_LOOKAHEAD < len(items):
                ahead.append(scores(i + DA_LOOKAHEAD))
            o12 = _softmax_times_values(
                s_refs[i % len(s_refs)], k_len, m,
                lambda r, n: vt_ref[0, sl, k_lo + r:k_lo + r + n]).T
            o = o12[0:TOK_TILE] - lam * o12[TOK_TILE:2 * TOK_TILE]
            ms = jnp.mean(o * o, axis=-1, keepdims=True)
            o = o * lax.rsqrt(ms + SUBLN_EPS) * g_ref[...] * (1.0 - lam_init)
            o_ref[0, s * TOK_TILE:(s + 1) * TOK_TILE, sl] = o.astype(BF16)

    @pl.when(t < n_lat // STEP_ROWS)
    def _():
        attend(SUB_TILES, 0, n_lat + n_ctx)

    @pl.when(t >= n_lat // STEP_ROWS)
    def _():
        attend(n_ctx // TOK_TILE, n_lat, n_ctx)


def _diff_attention(qb, kb, vbt, lam_vec, subln_g, lam_init, n_lat, n_ctx, with_ctx):
    bsz, n_tok, _ = qb.shape
    n_rows = n_tok if with_ctx else n_lat
    full = pl.BlockSpec((1, n_tok, BRANCH_W), lambda b, t: (b, 0, 0))
    full_t = pl.BlockSpec((1, BRANCH_W, n_tok), lambda b, t: (b, 0, 0))
    tile = pl.BlockSpec((1, STEP_ROWS, BRANCH_W), lambda b, t: (b, t, 0))
    return pl.pallas_call(
        functools.partial(_diffattn_kernel, n_lat=n_lat, n_ctx=n_ctx, lam_init=lam_init),
        grid=(bsz, pl.cdiv(n_rows, STEP_ROWS)),
        in_specs=[tile, full, full_t, _resident((4, DA_HEAD_DIM)), _resident((1, DA_V_DIM))],
        out_specs=tile,
        out_shape=jax.ShapeDtypeStruct((bsz, n_tok, BRANCH_W), BF16),
        scratch_shapes=[pltpu.VMEM((n_tok, 2 * TOK_TILE), F32)] * (DA_LOOKAHEAD + 1),
        compiler_params=_params(2),
        name="diff_attention",
    )(qb, kb, vbt, lam_vec, subln_g.reshape(1, DA_V_DIM))


def _natten_kernel(q_ref, k_ref, vt_ref, bias_ref, o_ref, *s_refs, n_lat, n_ctx, class_starts):
    s_idx = pl.program_id(1)
    rows = n_lat // GRID_W
    kh = min(NA_ROWS, rows)
    n_slab = NA_SLAB_ROWS * GRID_W
    n_q = NA_STEP_ROWS * GRID_W
    n_pairs = NA_HEADS // 2
    first_half = lax.broadcasted_iota(jnp.int32, (1, LANES), 1) < NA_HEAD_DIM
    pairs = [slice(p * LANES, (p + 1) * LANES) for p in range(n_pairs)]

    def run(window, n_groups):
        n_keys = n_ctx + (n_slab if window else 0)
        k_los, maxes = [], []
        for g in range(n_groups):
            group = NA_GROUPS * s_idx + g
            first_row = jnp.clip(NA_STEP_ROWS * group - kh // 2, 0, rows - kh)
            k_lo = pl.multiple_of(first_row * GRID_W, LANES)
            k_los.append(k_lo)
            cls = sum((group >= f).astype(jnp.int32) for f in class_starts)
            for p, sl in enumerate(pairs):
                s_ref = s_refs[g * n_pairs + p]
                qs = _stack_sub_heads(q_ref[0, g * n_q:(g + 1) * n_q, sl], first_half)
                m = _store_scores(s_ref, 0, _nt_dot(k_ref[0, n_lat:n_lat + n_ctx, sl], qs))
                if window:
                    s_win = _nt_dot(k_ref[0, pl.ds(k_lo, n_slab), sl], qs) + bias_ref[cls, p]
                    m = jnp.maximum(m, _store_scores(s_ref, n_ctx, s_win))
                maxes.append(m)
        for g in range(n_groups):
            for p, sl in enumerate(pairs):
                def values_t(r, n):
                    if r < n_ctx:
                        return vt_ref[0, sl, n_lat + r:n_lat + r + n]
                    return vt_ref[0, sl, pl.ds(pl.multiple_of(k_los[g] + (r - n_ctx), LANES), n)]

                i = g * n_pairs + p
                o2 = _softmax_times_values(s_refs[i], n_keys, maxes[i], values_t).T
                o_ref[0, g * n_q:(g + 1) * n_q, sl] = jnp.where(
                    first_half, o2[0:n_q], o2[n_q:2 * n_q]).astype(BF16)

    window_steps = rows // (NA_STEP_ROWS * NA_GROUPS)

    @pl.when(s_idx < window_steps)
    def _():
        run(True, NA_GROUPS)

    @pl.when(s_idx >= window_steps)
    def _():
        run(False, min(NA_GROUPS, n_ctx // n_q))


def _natten_step_classes(rows):
    kh = min(NA_ROWS, rows)
    classes, first_steps = [], []
    for s in range(rows // NA_STEP_ROWS):
        slab = min(max(NA_STEP_ROWS * s - kh // 2, 0), rows - kh)
        geom = tuple((r - slab, min(max(r - kh // 2, 0), rows - kh) - slab)
                     for r in range(NA_STEP_ROWS * s, NA_STEP_ROWS * (s + 1)))
        geom = geom + (min(NA_SLAB_ROWS, rows - slab),)
        if not classes or classes[-1] != geom:
            assert geom not in classes
            classes.append(geom)
            first_steps.append(s)
    return classes, first_steps


def _natten_bias_kernel(r_ref, o_ref, *, classes, kh):
    cls = pl.program_id(1)
    kc = lax.broadcasted_iota(jnp.int32, (GRID_W, LANES), 0)
    lane = lax.broadcasted_iota(jnp.int32, (GRID_W, LANES), 1)
    second = lane >= GRID_W
    c = jnp.where(second, lane - GRID_W, lane)
    c0 = jnp.clip(c - NA_COLS // 2, 0, GRID_W - NA_COLS)
    col_ok = jnp.logical_and(kc >= c0, kc < c0 + NA_COLS)
    neg = jnp.full((GRID_W, LANES), NEG_INF, F32)

    def fill(geom):
        for h in range(NA_HEADS):
            for kr in range(NA_SLAB_ROWS):
                halves = []
                for j, (q_off, w_off) in enumerate(geom[:-1]):
                    if w_off <= kr < w_off + kh and kr < geom[-1]:
                        dr = kr - q_off + NA_ROWS - 1
                        row = jnp.broadcast_to(r_ref[0, h, dr:dr + 1, :], (GRID_W, LANES))
                        halves.append(pltpu.roll(row, (j * GRID_W - (NA_COLS - 1)) % LANES, axis=1,
                                                 stride=1, stride_axis=0))
                    else:
                        halves.append(neg)
                val = jnp.where(col_ok, jnp.where(second, halves[1], halves[0]), neg)
                o_ref[0, 0, h // 2, kr * GRID_W:(kr + 1) * GRID_W,
                      (h % 2) * LANES:(h % 2 + 1) * LANES] = val

    for ci, geom in enumerate(classes):
        pl.when(cls == ci)(functools.partial(fill, geom))


def _natten_bias(rpb, rows):
    kh = min(NA_ROWS, rows)
    classes, _ = _natten_step_classes(rows)
    n_dr, n_dc = 2 * NA_ROWS - 1, 2 * NA_COLS - 1
    assert NA_STEP_ROWS == 2 and n_dc <= GRID_W
    r = jnp.pad(rpb[..., ::-1].astype(F32) * LOG2_E,
                ((0, 0), (0, 0), (0, 2 * SUBLANES - n_dr), (0, LANES - n_dc)))
    return pl.pallas_call(
        functools.partial(_natten_bias_kernel, classes=classes, kh=kh),
        grid=(DEPTH, len(classes)),
        in_specs=[pl.BlockSpec((1, NA_HEADS, 2 * SUBLANES, LANES), lambda l, k: (l, 0, 0, 0))],
        out_specs=pl.BlockSpec((1, 1, NA_HEADS // 2, NA_SLAB_ROWS * GRID_W, 2 * LANES),
                               lambda l, k: (l, k, 0, 0, 0)),
        out_shape=jax.ShapeDtypeStruct(
            (DEPTH, len(classes), NA_HEADS // 2, NA_SLAB_ROWS * GRID_W, 2 * LANES), F32),
        compiler_params=_params(2),
        name="natten_bias_tables",
    )(r)


def _neighbourhood_attention(qc, kc, vct, bias, layer, n_lat, n_ctx, with_ctx):
    bsz, n_tok, _ = qc.shape
    rows = n_lat // GRID_W
    kh = min(NA_ROWS, rows)
    n_q = NA_STEP_ROWS * GRID_W
    assert (kh // 2) % NA_STEP_ROWS == 0 and (rows - kh) % NA_STEP_ROWS == 0 and n_q == LANES
    assert (rows - kh + NA_SLAB_ROWS) * GRID_W <= n_tok and kh + NA_STEP_ROWS <= NA_SLAB_ROWS
    assert n_ctx % PV_CHUNK == 0 and (NA_SLAB_ROWS * GRID_W) % LANES == 0
    _, first_steps = _natten_step_classes(rows)
    step_q = NA_GROUPS * n_q
    assert n_lat % step_q == 0 and n_ctx % n_q == 0 and n_ctx <= step_q
    n_steps = pl.cdiv(n_tok if with_ctx else n_lat, step_q)

    full = pl.BlockSpec((1, n_tok, BRANCH_W), lambda b, s: (b, 0, 0))
    full_t = pl.BlockSpec((1, BRANCH_W, n_tok), lambda b, s: (b, 0, 0))
    tile = pl.BlockSpec((1, step_q, BRANCH_W), lambda b, s: (b, s, 0))
    return pl.pallas_call(
        functools.partial(_natten_kernel, n_lat=n_lat, n_ctx=n_ctx,
                          class_starts=tuple(first_steps[1:])),
        grid=(bsz, n_steps),
        in_specs=[tile, full, full_t, _resident_layer(bias.shape[1:], layer)],
        out_specs=tile,
        out_shape=jax.ShapeDtypeStruct((bsz, n_tok, BRANCH_W), BF16),
        scratch_shapes=[pltpu.VMEM((n_ctx + NA_SLAB_ROWS * GRID_W, 2 * n_q), F32)]
                       * (NA_GROUPS * NA_HEADS // 2),
        compiler_params=_params(2),
        name="neighbourhood_attention",
    )(qc, kc, vct, bias)


def _merge_kernel(x_ref, xc_ref, mod_ref, g_ref, fa_ref, db_ref, nc_ref, wg_ref, bg_ref, wa_ref,
                  wb_ref, wc_ref, wo_ref, o_ref, *, lat_subs, n_subs):
    def sub_tile(s, src_ref=x_ref):
        rows = slice(s * TOK_TILE, (s + 1) * TOK_TILE)
        x = src_ref[0, rows, :]
        a = _norm_modulate(x, g_ref[...], mod_ref[0:1, :], mod_ref[1:2, :]).astype(BF16)

        def gate(j):
            cols = slice(j * D_MODEL, (j + 1) * D_MODEL)
            z = _dot(a, wg_ref[:, cols]) + bg_ref[:, cols]
            return 1.0 / (1.0 + jnp.exp(-z))

        y = (gate(0) * _dot(fa_ref[0, rows, :], wa_ref[...])
             + gate(1) * _dot(db_ref[0, rows, :], wb_ref[...])
             + gate(2) * _dot(nc_ref[0, rows, :], wc_ref[...]))
        y = _dot(y.astype(BF16), wo_ref[...])
        o_ref[0, rows, :] = x + mod_ref[2:3, :] * y

    _for_sub_tiles(pl.program_id(1), lat_subs, n_subs, sub_tile,
                   functools.partial(sub_tile, src_ref=xc_ref))


def _merge(stream, mod, g_mix, fa, db, nc, w_g, b_gate, w_a, w_b, w_c, w_out, layer, n_lat,
           with_ctx):
    bsz, n_tok = fa.shape[0], fa.shape[1]
    n_rows = n_tok if with_ctx else n_lat
    full_steps = n_lat // STEP_ROWS
    tok = lambda w: pl.BlockSpec((1, STEP_ROWS, w), lambda b, t: (b, t, 0))
    return pl.pallas_call(
        functools.partial(_merge_kernel, lat_subs=n_lat // TOK_TILE, n_subs=n_rows // TOK_TILE),
        grid=(bsz, pl.cdiv(n_rows, STEP_ROWS)),
        in_specs=_stream_specs(stream, n_lat) + [
            pl.BlockSpec((None, None, N_MOD, D_MODEL), lambda b, t: (b, t // full_steps, 0, 0)),
            _resident((1, D_MODEL)),
            tok(BRANCH_W), tok(BRANCH_W), tok(BRANCH_W),
            _resident_layer((D_MODEL, N_BRANCH * D_MODEL), layer),
            _resident((1, N_BRANCH * D_MODEL)),
            _resident_layer((BRANCH_W, D_MODEL), layer), _resident_layer((BRANCH_W, D_MODEL), layer),
            _resident_layer((BRANCH_W, D_MODEL), layer), _resident_layer((D_MODEL, D_MODEL), layer),
        ],
        out_specs=tok(D_MODEL),
        out_shape=jax.ShapeDtypeStruct((bsz, n_rows, D_MODEL), F32),
        compiler_params=_params(2),
        name="merge_out_projection",
    )(stream[0], stream[1], mod, g_mix.reshape(1, D_MODEL), fa, db, nc, w_g,
      b_gate.reshape(1, -1), w_a, w_b, w_c, w_out)


def _convffn_kernel(x_ref, prev_ref, next_ref, mod_ref, g_ref, wu_ref, cw_ref, cb_ref, wd_ref,
                    gf_ref, o_ref, *lhs_refs, lat_subs, n_subs, final_norm):
    t = pl.program_id(1)
    g = g_ref[...]
    shift, scale = mod_ref[3:4, :], mod_ref[4:5, :]
    n_rows = TOK_TILE + 2 * SUBLANES

    def conv(u, col):
        w = cw_ref[:, col:col + FF_CHUNK]
        before = pltpu.roll(u, 1, axis=0)
        after = pltpu.roll(u, n_rows - 1, axis=0)
        v = before * w[0:1] + u * w[1:2] + after * w[2:3] + cb_ref[:, col:col + FF_CHUNK]
        return v[SUBLANES:SUBLANES + TOK_TILE]

    def prepare(s):
        idx = t * SUB_TILES + s
        lo = s * TOK_TILE
        x = x_ref[0, lo:lo + TOK_TILE, :]
        has_prev = jnp.logical_and(idx != 0, idx != lat_subs)
        has_next = jnp.logical_and(idx != lat_subs - 1, idx != n_subs - 1)
        prev = prev_ref[0] if s == 0 else x_ref[0, lo - SUBLANES:lo, :]
        nxt = (next_ref[0] if s == SUB_TILES - 1
               else x_ref[0, lo + TOK_TILE:lo + TOK_TILE + SUBLANES, :])
        lhs_ref = lhs_refs[s]
        lhs_ref[0:SUBLANES, :] = jnp.where(has_prev, _norm_modulate(prev, g, shift, scale), 0.0)
        lhs_ref[SUBLANES:SUBLANES + TOK_TILE, :] = _norm_modulate(x, g, shift, scale)
        lhs_ref[SUBLANES + TOK_TILE:, :] = jnp.where(
            has_next, _norm_modulate(nxt, g, shift, scale), 0.0)
        return x, lhs_ref[...].astype(BF16)

    def run(n_sub_tiles):
        tiles = [prepare(s) for s in range(n_sub_tiles)]
        items = [(j, s) for j in range(D_FF // FF_CHUNK) for s in range(n_sub_tiles)]

        def up(item):
            j, s = item
            lhs = tiles[s][1]
            col_a, col_b = j * FF_CHUNK, D_FF + j * FF_CHUNK
            return (_dot(lhs, wu_ref[:, col_a:col_a + FF_CHUNK]),
                    _dot(lhs, wu_ref[:, col_b:col_b + FF_CHUNK]))

        look = FF_LOOKAHEAD * n_sub_tiles
        accs = [jnp.zeros((TOK_TILE, D_MODEL), F32) for _ in range(n_sub_tiles)]
        ahead = [up(item) for item in items[:look]]
        for i, (j, s) in enumerate(items):
            col_a, col_b = j * FF_CHUNK, D_FF + j * FF_CHUNK
            u_a, u_b = ahead.pop(0)
            if i + look < len(items):
                ahead.append(up(items[i + look]))
            a = conv(u_a, col_a)
            b = conv(u_b, col_b)
            act = (a / (1.0 + jnp.exp(-a))) * b
            accs[s] = accs[s] + _dot(act.astype(BF16), wd_ref[col_a:col_a + FF_CHUNK, :])
        for s in range(n_sub_tiles):
            y = tiles[s][0] + mod_ref[5:6, :] * accs[s]
            if final_norm:
                ms = jnp.mean(y * y, axis=-1, keepdims=True)
                y = y * lax.rsqrt(ms + NORM_EPS) * gf_ref[...]
            o_ref[0, s * TOK_TILE:(s + 1) * TOK_TILE, :] = y

    full_steps = lat_subs // SUB_TILES
    pl.when(t < full_steps)(functools.partial(run, SUB_TILES))
    if n_subs > lat_subs:
        pl.when(t >= full_steps)(functools.partial(run, n_subs - lat_subs))


def _conv_ffn(h, mod, g_ffn, w_up, conv_w, conv_b, w_down, layer, g_final, n_lat, with_ctx,
              final_norm):
    bsz, n_rows_in, _ = h.shape
    n_rows = n_rows_in if with_ctx else n_lat
    lat_subs, n_subs = n_lat // TOK_TILE, n_rows // TOK_TILE
    full_steps = n_lat // STEP_ROWS
    blocks_per_step = STEP_ROWS // SUBLANES
    last_block = n_rows_in // SUBLANES - 1
    return pl.pallas_call(
        functools.partial(_convffn_kernel, lat_subs=lat_subs, n_subs=n_subs,
                          final_norm=final_norm),
        grid=(bsz, pl.cdiv(n_rows, STEP_ROWS)),
        in_specs=[
            pl.BlockSpec((1, STEP_ROWS, D_MODEL), lambda b, t: (b, t, 0)),
            pl.BlockSpec((1, SUBLANES, D_MODEL),
                         lambda b, t: (b, jnp.maximum(t * blocks_per_step - 1, 0), 0)),
            pl.BlockSpec((1, SUBLANES, D_MODEL),
                         lambda b, t: (b, jnp.minimum((t + 1) * blocks_per_step, last_block), 0)),
            pl.BlockSpec((None, None, N_MOD, D_MODEL), lambda b, t: (b, t // full_steps, 0, 0)),
            _resident((1, D_MODEL)),
            _resident_layer((D_MODEL, 2 * D_FF), layer),
            _resident((3, 2 * D_FF)),
            _resident((1, 2 * D_FF)),
            _resident_layer((D_FF, D_MODEL), layer),
            _resident((1, D_MODEL)),
        ],
        out_specs=pl.BlockSpec((1, STEP_ROWS, D_MODEL), lambda b, t: (b, t, 0)),
        out_shape=jax.ShapeDtypeStruct((bsz, n_rows, D_MODEL), F32),
        scratch_shapes=[pltpu.VMEM((TOK_TILE + 2 * SUBLANES, D_MODEL), F32)] * SUB_TILES,
        compiler_params=_params(2),
        name="conv_ffn",
    )(h, h, h, mod, g_ffn.reshape(1, D_MODEL), w_up, conv_w, conv_b.reshape(1, -1), w_down,
      g_final.reshape(1, D_MODEL))


def _rope_tables(n_lat, n_ctx):
    t = jnp.arange(n_lat, dtype=jnp.int32)
    pos = jnp.stack([(t // GRID_W).astype(F32), (t % GRID_W).astype(F32)], axis=1)
    n_freq = ROPE_AXIS_DIM // 2
    inv = ROPE_THETA ** (-jnp.arange(n_freq, dtype=F32) / n_freq)
    ang = pos[:, :, None] * inv
    lane = jnp.arange(LANES, dtype=jnp.int32)
    axis = (lane % DA_HEAD_DIM) // ROPE_AXIS_DIM
    freq = lane % n_freq
    second_half = (lane % ROPE_AXIS_DIM) >= n_freq
    ang_l = ang[:, axis, freq]
    cos, sin = jnp.cos(ang_l), jnp.sin(ang_l)
    sin_a = jnp.where(second_half, 0.0, -sin)
    sin_b = jnp.where(second_half, sin, 0.0)
    pad = lambda a, v: jnp.concatenate([a, jnp.full((n_ctx, LANES), v, F32)], axis=0)
    return pad(cos, 1.0), pad(sin_a, 0.0), pad(sin_b, 0.0)


def kernel(x, c, ctx, c_ctx, w_ada, b_ada, g_mix, g_ffn, w_in, b_gate, w_a, lam, subln_g, w_b,
           rpb, w_c, w_out, w_up, conv_w, conv_b, w_down, g_final):
    bsz, n_lat, _ = x.shape
    n_ctx = ctx.shape[1]
    assert n_lat % STEP_ROWS == 0 and n_ctx == TOK_TILE and n_lat % GRID_W == 0

    rope_tabs = _rope_tables(n_lat, n_ctx)
    cl, sl = _dft_tables(n_lat, n_lat // 2, (n_lat * FN_GROUP_DIM) ** -0.5)
    cc, sc = _dft_tables(n_ctx, n_ctx, (n_ctx * FN_GROUP_DIM) ** -0.5)
    cg, sg = _dft_tables(FN_GROUP_DIM, FN_GROUP_DIM, 1.0)
    dft_tabs = (cl, sl, cc, sc, jnp.concatenate([cg, -sg], axis=0), _reversal_matrix())

    n_mod_rows = 2 * SUBLANES * (-(-(bsz + 1) // (2 * SUBLANES)))
    cvec = jnp.zeros((n_mod_rows, D_MODEL), F32).at[:bsz].set(c).at[bsz].set(c_ctx)
    mods = _modulation(cvec, w_ada, b_ada).reshape(DEPTH, n_mod_rows, N_MOD, D_MODEL)

    na_bias = _natten_bias(rpb, n_lat // GRID_W)

    n_branch_cols = w_in.shape[-1] - N_BRANCH * D_MODEL
    w_branch, w_gate, w_a, w_b, w_c, w_out, w_up, w_down = (
        w.astype(BF16) for w in (w_in[..., :n_branch_cols], w_in[..., n_branch_cols:],
                                 w_a, w_b, w_c, w_out, w_up, w_down))

    stream = (x, ctx, 0)
    for l in range(DEPTH):
        with_ctx = l != DEPTH - 1
        lam_init = 0.8 - 0.6 * math.exp(-0.3 * l)
        mod = jnp.stack([mods[l, :bsz],
                         jnp.broadcast_to(mods[l, bsz], (bsz, N_MOD, D_MODEL))], axis=1)
        fa_in, qb, kb, vbt, qc, kc, vct = _in_projection(
            stream, mod, g_mix[l], w_branch, l, rope_tabs, n_lat, n_ctx)
        fa = _fourier_mix(fa_in, dft_tabs, n_lat, n_ctx, with_ctx)
        db = _diff_attention(qb, kb, vbt, lam[l], subln_g[l], lam_init, n_lat, n_ctx, with_ctx)
        nc = _neighbourhood_attention(qc, kc, vct, na_bias, l, n_lat, n_ctx, with_ctx)
        h = _merge(stream, mod, g_mix[l], fa, db, nc, w_gate, b_gate[l], w_a, w_b, w_c, w_out, l,
                   n_lat, with_ctx)
        h = _conv_ffn(h, mod, g_ffn[l], w_up, conv_w[l], conv_b[l], w_down, l, g_final, n_lat,
                      with_ctx, final_norm=not with_ctx)
        stream = (h, h, n_lat // n_ctx)
    return h
```

```python
import functools
import math

import jax
import jax.numpy as jnp
from jax import lax
from jax.experimental import pallas as pl
from jax.experimental.pallas import tpu as pltpu

D_MODEL = 1024
DEPTH = 4
GRID_W = 64
FN_GROUPS = 4
FN_GROUP_DIM = 128
FN_WIDTH = FN_GROUPS * FN_GROUP_DIM
DA_HEADS = 4
DA_HEAD_DIM = 64
DA_V_DIM = 2 * DA_HEAD_DIM
NA_HEADS = 8
NA_HEAD_DIM = 64
NA_ROWS = 8
NA_COLS = 16
BRANCH_W = 512
N_BRANCH = 3
ROPE_THETA = 10000.0
ROPE_AXIS_DIM = DA_HEAD_DIM // 2
D_FF = 2816
N_MOD = 6
NORM_EPS = 1e-6
SUBLN_EPS = 1e-5
NEG_INF = -1e30

LANES = 128
SUBLANES = 8
TOK_TILE = 256
SUB_TILES = 2
STEP_ROWS = SUB_TILES * TOK_TILE
FF_CHUNK = 256
DA_LOOKAHEAD = 1
FF_LOOKAHEAD = 2
NA_STEP_ROWS = 2
NA_GROUPS = 2
NA_SLAB_ROWS = 10
VMEM_LIMIT = 56 * 1024 * 1024
ONES_ROWS = 16
PV_CHUNK = 256
LOG2_E = math.log2(math.e)

BF16 = jnp.bfloat16
F32 = jnp.float32


def _params(n_grid_dims):
    return pltpu.CompilerParams(dimension_semantics=("arbitrary",) * n_grid_dims,
                                vmem_limit_bytes=VMEM_LIMIT)


def _resident(shape):
    return pl.BlockSpec(shape, lambda *_: (0,) * len(shape), pipeline_mode=pl.Buffered(1))


def _resident_layer(shape, layer):
    return pl.BlockSpec((None,) + tuple(shape), lambda *_: (layer,) + (0,) * len(shape),
                        pipeline_mode=pl.Buffered(1))


def _for_sub_tiles(t, lat_subs, n_subs, body, ctx_body=None):
    full_steps = lat_subs // SUB_TILES
    if n_subs == lat_subs:
        for s in range(SUB_TILES):
            body(s)
        return

    @pl.when(t < full_steps)
    def _():
        for s in range(SUB_TILES):
            body(s)

    @pl.when(t >= full_steps)
    def _():
        for s in range(n_subs - lat_subs):
            (ctx_body or body)(s)


def _nt_dot(a, b):
    return lax.dot_general(a, b, (((1,), (1,)), ((), ())), preferred_element_type=F32)


def _dot(a, b):
    return jnp.dot(a, b, preferred_element_type=F32)


def _norm_modulate(x, g, shift, scale):
    ms = jnp.mean(x * x, axis=-1, keepdims=True)
    return (x * lax.rsqrt(ms + NORM_EPS) * g) * (1.0 + scale) + shift


def _split_bf16(v):
    hi = v.astype(BF16)
    lo = (v - hi.astype(F32)).astype(BF16)
    return hi, lo


def _mod_kernel(c_ref, w_ref, b_ref, o_ref):
    c = c_ref[...]
    s = c / (1.0 + jnp.exp(-c))
    s_hi, s_lo = _split_bf16(s)
    w_hi, w_lo = _split_bf16(w_ref[0])
    acc = _dot(s_hi, w_hi) + (_dot(s_hi, w_lo) + _dot(s_lo, w_hi))
    o_ref[0] = acc + b_ref[0]


def _modulation(cvec, w_ada, b_ada):
    n_rows = cvec.shape[0]
    n_out = N_MOD * D_MODEL
    tn = 1536
    return pl.pallas_call(
        _mod_kernel,
        grid=(DEPTH, n_out // tn),
        in_specs=[
            pl.BlockSpec((n_rows, D_MODEL), lambda l, j: (0, 0)),
            pl.BlockSpec((1, D_MODEL, tn), lambda l, j: (l, 0, j)),
            pl.BlockSpec((1, 1, tn), lambda l, j: (l, 0, j)),
        ],
        out_specs=pl.BlockSpec((1, n_rows, tn), lambda l, j: (l, 0, j)),
        out_shape=jax.ShapeDtypeStruct((DEPTH, n_rows, n_out), F32),
        compiler_params=_params(2),
        name="adaln_modulation",
    )(cvec, w_ada, b_ada.reshape(DEPTH, 1, n_out))


def _rope(p, cos, sin_a, sin_b):
    outs = []
    for k in range(p.shape[1] // LANES):
        xs = p[:, k * LANES:(k + 1) * LANES]
        from_hi = pltpu.roll(xs, LANES - ROPE_AXIS_DIM // 2, axis=1)
        from_lo = pltpu.roll(xs, ROPE_AXIS_DIM // 2, axis=1)
        outs.append(xs * cos + from_hi * sin_a + from_lo * sin_b)
    return jnp.concatenate(outs, axis=1)


def _inproj_kernel(x_ref, xc_ref, mod_ref, g_ref, w_ref, rope_ref,
                   fa_ref, qb_ref, kb_ref, vbt_ref, qc_ref, kc_ref, vct_ref,
                   *, lat_subs, n_subs):
    def sub_tile(s, src_ref=x_ref):
        rows = slice(s * TOK_TILE, (s + 1) * TOK_TILE)
        a = _norm_modulate(src_ref[0, rows, :], g_ref[...], mod_ref[0:1, :],
                           mod_ref[1:2, :]).astype(BF16)
        tok0 = pl.multiple_of(pl.program_id(1) * STEP_ROWS + s * TOK_TILE, TOK_TILE)
        cos, sin_a, sin_b = (rope_ref[pl.ds(tok0, TOK_TILE), k * LANES:(k + 1) * LANES]
                             for k in range(3))

        def proj(seg):
            return _dot(a, w_ref[:, seg * BRANCH_W:(seg + 1) * BRANCH_W])

        fa_ref[0, rows, :] = proj(0).astype(BF16)
        qb_ref[0, rows, :] = (_rope(proj(1), cos, sin_a, sin_b)
                              * (DA_HEAD_DIM ** -0.5 * LOG2_E)).astype(BF16)
        kb_ref[0, rows, :] = _rope(proj(2), cos, sin_a, sin_b).astype(BF16)
        vbt_ref[0, :, rows] = proj(3).T.astype(BF16)
        qc_ref[0, rows, :] = (proj(4) * (NA_HEAD_DIM ** -0.5 * LOG2_E)).astype(BF16)
        kc_ref[0, rows, :] = proj(5).astype(BF16)
        vct_ref[0, :, rows] = proj(6).T.astype(BF16)

    _for_sub_tiles(pl.program_id(1), lat_subs, n_subs, sub_tile,
                   functools.partial(sub_tile, src_ref=xc_ref))


def _stream_specs(stream, n_lat):
    _, _, ctx_block = stream
    last = n_lat // STEP_ROWS - 1
    return [pl.BlockSpec((1, STEP_ROWS, D_MODEL), lambda b, t: (b, jnp.minimum(t, last), 0)),
            pl.BlockSpec((1, TOK_TILE, D_MODEL), lambda b, t: (b, ctx_block, 0))]


def _in_projection(stream, mod, g_mix, w_branch, layer, rope_tabs, n_lat, n_ctx):
    bsz, n_tok = stream[0].shape[0], n_lat + n_ctx
    full_steps = n_lat // STEP_ROWS
    proj_w = w_branch.shape[-1]
    tok = lambda w: pl.BlockSpec((1, STEP_ROWS, w), lambda b, t: (b, t, 0))
    branch = jax.ShapeDtypeStruct((bsz, n_tok, BRANCH_W), BF16)
    tok_t = pl.BlockSpec((1, BRANCH_W, STEP_ROWS), lambda b, t: (b, 0, t))
    branch_t = jax.ShapeDtypeStruct((bsz, BRANCH_W, n_tok), BF16)
    return pl.pallas_call(
        functools.partial(_inproj_kernel, lat_subs=n_lat // TOK_TILE, n_subs=n_tok // TOK_TILE),
        grid=(bsz, pl.cdiv(n_tok, STEP_ROWS)),
        in_specs=_stream_specs(stream, n_lat) + [
            pl.BlockSpec((None, None, N_MOD, D_MODEL), lambda b, t: (b, t // full_steps, 0, 0)),
            _resident((1, D_MODEL)),
            _resident_layer((D_MODEL, proj_w), layer),
            _resident((n_tok, 3 * LANES)),
        ],
        out_specs=[tok(BRANCH_W)] * 3 + [tok_t] + [tok(BRANCH_W)] * 2 + [tok_t],
        out_shape=[branch] * 3 + [branch_t] + [branch] * 2 + [branch_t],
        compiler_params=_params(2),
        name="in_projection",
    )(stream[0], stream[1], mod, g_mix.reshape(1, D_MODEL), w_branch, rope_tabs)


def _dft_kernel(u_ref, cl_ref, sl_ref, cc_ref, sc_ref, csg_ref, rev_ref, o_ref, ue_ref, uo_ref,
                *, n_lat, n_ctx, n_subs, scale):
    t = pl.program_id(1)
    half = n_lat // 2

    def finish(s, p, q):
        rows = slice(s * TOK_TILE, (s + 1) * TOK_TILE)
        for g in range(FN_GROUPS):
            sl = slice(g * FN_GROUP_DIM, (g + 1) * FN_GROUP_DIM)
            pq = jnp.concatenate([p[:, sl], q[:, sl]], axis=1).astype(BF16)
            o_ref[0, rows, sl] = _dot(pq, csg_ref[...]).astype(BF16)

    @pl.when(t == 0)
    def _():
        for b in range(half // TOK_TILE):
            hi = n_lat - (b + 1) * TOK_TILE
            first = u_ref[0, hi:hi + TOK_TILE, :]
            wrap = (u_ref[0, hi + TOK_TILE:hi + 2 * TOK_TILE, :] if b > 0
                    else jnp.zeros((TOK_TILE, FN_WIDTH), BF16))
            rev = _dot(rev_ref[...], jnp.concatenate([first, wrap], axis=0))
            lo = u_ref[0, b * TOK_TILE:(b + 1) * TOK_TILE, :].astype(F32)
            ue_ref[b * TOK_TILE:(b + 1) * TOK_TILE, :] = (lo + rev).astype(BF16)
            uo_ref[b * TOK_TILE:(b + 1) * TOK_TILE, :] = (lo - rev).astype(BF16)

    def latent(s):
        k0 = pl.multiple_of((t * SUB_TILES + s) * TOK_TILE, TOK_TILE)
        k = k0 + lax.broadcasted_iota(jnp.int32, (TOK_TILE, 1), 0)
        sign = (1 - 2 * (k & 1)).astype(F32) * scale
        rows = pl.ds(k0, TOK_TILE)
        p = _dot(cl_ref[rows, :], ue_ref[...]) + sign * u_ref[0, half:half + 1, :].astype(F32)
        finish(s, p, _dot(sl_ref[rows, :], uo_ref[...]))

    def context(s):
        u = u_ref[0, n_lat:n_lat + n_ctx, :]
        finish(s, _dot(cc_ref[...], u), _dot(sc_ref[...], u))

    _for_sub_tiles(t, n_lat // TOK_TILE, n_subs, latent, context)


def _fourier_mix(fa_in, tabs, n_lat, n_ctx, with_ctx):
    bsz, n_tok, _ = fa_in.shape
    n_rows = n_tok if with_ctx else n_lat
    half = n_lat // 2
    assert half % TOK_TILE == 0
    cl, sl, cc, sc, csg, rev = tabs
    lat_tab = _resident((n_lat, half))
    return pl.pallas_call(
        functools.partial(_dft_kernel, n_lat=n_lat, n_ctx=n_ctx, n_subs=n_rows // TOK_TILE,
                          scale=(n_lat * FN_GROUP_DIM) ** -0.5),
        grid=(bsz, pl.cdiv(n_rows, STEP_ROWS)),
        in_specs=[
            pl.BlockSpec((1, n_tok, FN_WIDTH), lambda b, t: (b, 0, 0)),
            lat_tab, lat_tab,
            _resident((n_ctx, n_ctx)), _resident((n_ctx, n_ctx)),
            _resident((2 * FN_GROUP_DIM, FN_GROUP_DIM)),
            _resident((TOK_TILE, 2 * TOK_TILE)),
        ],
        out_specs=pl.BlockSpec((1, STEP_ROWS, FN_WIDTH), lambda b, t: (b, t, 0)),
        out_shape=jax.ShapeDtypeStruct((bsz, n_rows, FN_WIDTH), BF16),
        scratch_shapes=[pltpu.VMEM((half, FN_WIDTH), BF16)] * 2,
        compiler_params=_params(2),
        name="fourier_mix",
    )(fa_in, cl, sl, cc, sc, csg, rev)


def _dft_tables(n, n_cols, scale):
    k = jnp.arange(n, dtype=jnp.int32)
    ang = ((k[:, None] * k[None, :n_cols]) % n).astype(F32) * (2.0 * math.pi / n)
    return (jnp.cos(ang) * scale).astype(BF16), (jnp.sin(ang) * scale).astype(BF16)


def _reversal_matrix():
    i = jnp.arange(TOK_TILE, dtype=jnp.int32)[:, None]
    j = jnp.arange(2 * TOK_TILE, dtype=jnp.int32)[None, :]
    src = jnp.where(i == 0, TOK_TILE, TOK_TILE - i)
    return (j == src).astype(BF16)


def _stack_sub_heads(q, first_half):
    zero = jnp.zeros_like(q)
    return jnp.concatenate([jnp.where(first_half, q, zero), jnp.where(first_half, zero, q)], axis=0)


def _store_scores(s_ref, row, s):
    n = s.shape[0]
    s_ref[row:row + n, :] = s
    return jnp.max(s.reshape(n // SUBLANES, SUBLANES, -1).max(axis=0), axis=0, keepdims=True)


def _softmax_times_values(s_ref, n_rows, m, values_t):
    acc = None
    for r in range(0, n_rows, PV_CHUNK):
        n = min(PV_CHUNK, n_rows - r)
        e = jnp.exp2(s_ref[r:r + n, :] - m).astype(BF16)
        v = values_t(r, n)
        part = _dot(jnp.concatenate([v, jnp.ones((ONES_ROWS, n), BF16)], axis=0), e)
        acc = part if acc is None else acc + part
    n_ch = acc.shape[0] - ONES_ROWS
    return acc[0:n_ch] * (1.0 / acc[n_ch:n_ch + 1])


def _diffattn_kernel(q_ref, k_ref, vt_ref, lam_ref, g_ref, o_ref, *s_refs,
                     n_lat, n_ctx, lam_init):
    t = pl.program_id(1)
    lv = lam_ref[...]
    lam = (jnp.exp(jnp.sum(lv[0:1] * lv[1:2], axis=-1, keepdims=True))
           - jnp.exp(jnp.sum(lv[2:3] * lv[3:4], axis=-1, keepdims=True)) + lam_init)
    first_half = lax.broadcasted_iota(jnp.int32, (1, LANES), 1) < DA_HEAD_DIM

    def attend(n_sub_tiles, k_lo, k_len):
        items = [(s, h) for s in range(n_sub_tiles) for h in range(DA_HEADS)]

        def scores(i):
            s, h = items[i]
            sl = slice(h * DA_V_DIM, (h + 1) * DA_V_DIM)
            q = q_ref[0, s * TOK_TILE:(s + 1) * TOK_TILE, sl]
            sc = _nt_dot(k_ref[0, k_lo:k_lo + k_len, sl], _stack_sub_heads(q, first_half))
            return _store_scores(s_refs[i % len(s_refs)], 0, sc)

        ahead = [scores(i) for i in range(min(DA_LOOKAHEAD, len(items)))]
        for i, (s, h) in enumerate(items):
            sl = slice(h * DA_V_DIM, (h + 1) * DA_V_DIM)
            m = ahead.pop(0)
            if i + DA_LOOKAHEAD < len(items):
                ahead.append(scores(i + DA_LOOKAHEAD))
            o12 = _softmax_times_values(
                s_refs[i % len(s_refs)], k_len, m,
                lambda r, n: vt_ref[0, sl, k_lo + r:k_lo + r + n]).T
            o = o12[0:TOK_TILE] - lam * o12[TOK_TILE:2 * TOK_TILE]
            ms = jnp.mean(o * o, axis=-1, keepdims=True)
            o = o * lax.rsqrt(ms + SUBLN_EPS) * g_ref[...] * (1.0 - lam_init)
            o_ref[0, s * TOK_TILE:(s + 1) * TOK_TILE, sl] = o.astype(BF16)

    @pl.when(t < n_lat // STEP_ROWS)
    def _():
        attend(SUB_TILES, 0, n_lat + n_ctx)

    @pl.when(t >= n_lat // STEP_ROWS)
    def _():
        attend(n_ctx // TOK_TILE, n_lat, n_ctx)


def _diff_attention(qb, kb, vbt, lam_vec, subln_g, lam_init, n_lat, n_ctx, with_ctx):
    bsz, n_tok, _ = qb.shape
    n_rows = n_tok if with_ctx else n_lat
    full = pl.BlockSpec((1, n_tok, BRANCH_W), lambda b, t: (b, 0, 0))
    full_t = pl.BlockSpec((1, BRANCH_W, n_tok), lambda b, t: (b, 0, 0))
    tile = pl.BlockSpec((1, STEP_ROWS, BRANCH_W), lambda b, t: (b, t, 0))
    return pl.pallas_call(
        functools.partial(_diffattn_kernel, n_lat=n_lat, n_ctx=n_ctx, lam_init=lam_init),
        grid=(bsz, pl.cdiv(n_rows, STEP_ROWS)),
        in_specs=[tile, full, full_t, _resident((4, DA_HEAD_DIM)), _resident((1, DA_V_DIM))],
        out_specs=tile,
        out_shape=jax.ShapeDtypeStruct((bsz, n_rows, BRANCH_W), BF16),
        scratch_shapes=[pltpu.VMEM((n_tok, 2 * TOK_TILE), F32)] * (DA_LOOKAHEAD + 1),
        compiler_params=_params(2),
        name="diff_attention",
    )(qb, kb, vbt, lam_vec, subln_g.reshape(1, DA_V_DIM))


def _natten_kernel(q_ref, k_ref, vt_ref, bias_ref, o_ref, *s_refs, n_lat, n_ctx, class_starts):
    s_idx = pl.program_id(1)
    rows = n_lat // GRID_W
    kh = min(NA_ROWS, rows)
    n_slab = NA_SLAB_ROWS * GRID_W
    n_q = NA_STEP_ROWS * GRID_W
    n_pairs = NA_HEADS // 2
    first_half = lax.broadcasted_iota(jnp.int32, (1, LANES), 1) < NA_HEAD_DIM
    pairs = [slice(p * LANES, (p + 1) * LANES) for p in range(n_pairs)]

    def run(window, n_groups):
        n_keys = n_ctx + (n_slab if window else 0)
        k_los, maxes = [], []
        for g in range(n_groups):
            group = NA_GROUPS * s_idx + g
            first_row = jnp.clip(NA_STEP_ROWS * group - kh // 2, 0, rows - kh)
            k_lo = pl.multiple_of(first_row * GRID_W, LANES)
            k_los.append(k_lo)
            cls = sum((group >= f).astype(jnp.int32) for f in class_starts)
            for p, sl in enumerate(pairs):
                s_ref = s_refs[g * n_pairs + p]
                qs = _stack_sub_heads(q_ref[0, g * n_q:(g + 1) * n_q, sl], first_half)
                m = _store_scores(s_ref, 0, _nt_dot(k_ref[0, n_lat:n_lat + n_ctx, sl], qs))
                if window:
                    s_win = _nt_dot(k_ref[0, pl.ds(k_lo, n_slab), sl], qs) + bias_ref[cls, p]
                    m = jnp.maximum(m, _store_scores(s_ref, n_ctx, s_win))
                maxes.append(m)
        for g in range(n_groups):
            for p, sl in enumerate(pairs):
                def values_t(r, n):
                    if r < n_ctx:
                        return vt_ref[0, sl, n_lat + r:n_lat + r + n]
                    return vt_ref[0, sl, pl.ds(pl.multiple_of(k_los[g] + (r - n_ctx), LANES), n)]

                i = g * n_pairs + p
                o2 = _softmax_times_values(s_refs[i], n_keys, maxes[i], values_t).T
                o_ref[0, g * n_q:(g + 1) * n_q, sl] = jnp.where(
                    first_half, o2[0:n_q], o2[n_q:2 * n_q]).astype(BF16)

    window_steps = rows // (NA_STEP_ROWS * NA_GROUPS)

    @pl.when(s_idx < window_steps)
    def _():
        run(True, NA_GROUPS)

    @pl.when(s_idx >= window_steps)
    def _():
        run(False, min(NA_GROUPS, n_ctx // n_q))


def _natten_step_classes(rows):
    kh = min(NA_ROWS, rows)
    classes, first_steps = [], []
    for s in range(rows // NA_STEP_ROWS):
        slab = min(max(NA_STEP_ROWS * s - kh // 2, 0), rows - kh)
        geom = tuple((r - slab, min(max(r - kh // 2, 0), rows - kh) - slab)
                     for r in range(NA_STEP_ROWS * s, NA_STEP_ROWS * (s + 1)))
        geom = geom + (min(NA_SLAB_ROWS, rows - slab),)
        if not classes or classes[-1] != geom:
            assert geom not in classes
            classes.append(geom)
            first_steps.append(s)
    return classes, first_steps


def _natten_bias_kernel(r_ref, o_ref, *, classes, kh):
    cls = pl.program_id(1)
    kc = lax.broadcasted_iota(jnp.int32, (GRID_W, LANES), 0)
    lane = lax.broadcasted_iota(jnp.int32, (GRID_W, LANES), 1)
    second = lane >= GRID_W
    c = jnp.where(second, lane - GRID_W, lane)
    c0 = jnp.clip(c - NA_COLS // 2, 0, GRID_W - NA_COLS)
    col_ok = jnp.logical_and(kc >= c0, kc < c0 + NA_COLS)
    neg = jnp.full((GRID_W, LANES), NEG_INF, F32)

    def fill(geom):
        for h in range(NA_HEADS):
            for kr in range(NA_SLAB_ROWS):
                halves = []
                for j, (q_off, w_off) in enumerate(geom[:-1]):
                    if w_off <= kr < w_off + kh and kr < geom[-1]:
                        dr = kr - q_off + NA_ROWS - 1
                        row = jnp.broadcast_to(r_ref[0, h, dr:dr + 1, :], (GRID_W, LANES))
                        halves.append(pltpu.roll(row, (j * GRID_W - (NA_COLS - 1)) % LANES, axis=1,
                                                 stride=1, stride_axis=0))
                    else:
                        halves.append(neg)
                val = jnp.where(col_ok, jnp.where(second, halves[1], halves[0]), neg)
                o_ref[0, 0, h // 2, kr * GRID_W:(kr + 1) * GRID_W,
                      (h % 2) * LANES:(h % 2 + 1) * LANES] = val

    for ci, geom in enumerate(classes):
        pl.when(cls == ci)(functools.partial(fill, geom))


def _natten_bias(rpb, rows):
    kh = min(NA_ROWS, rows)
    classes, _ = _natten_step_classes(rows)
    n_dr, n_dc = 2 * NA_ROWS - 1, 2 * NA_COLS - 1
    assert NA_STEP_ROWS == 2 and n_dc <= GRID_W
    r = jnp.pad(rpb[..., ::-1].astype(F32) * LOG2_E,
                ((0, 0), (0, 0), (0, 2 * SUBLANES - n_dr), (0, LANES - n_dc)))
    return pl.pallas_call(
        functools.partial(_natten_bias_kernel, classes=classes, kh=kh),
        grid=(DEPTH, len(classes)),
        in_specs=[pl.BlockSpec((1, NA_HEADS, 2 * SUBLANES, LANES), lambda l, k: (l, 0, 0, 0))],
        out_specs=pl.BlockSpec((1, 1, NA_HEADS // 2, NA_SLAB_ROWS * GRID_W, 2 * LANES),
                               lambda l, k: (l, k, 0, 0, 0)),
        out_shape=jax.ShapeDtypeStruct(
            (DEPTH, len(classes), NA_HEADS // 2, NA_SLAB_ROWS * GRID_W, 2 * LANES), F32),
        compiler_params=_params(2),
        name="natten_bias_tables",
    )(r)


def _neighbourhood_attention(qc, kc, vct, bias, layer, n_lat, n_ctx, with_ctx):
    bsz, n_tok, _ = qc.shape
    rows = n_lat // GRID_W
    kh = min(NA_ROWS, rows)
    n_q = NA_STEP_ROWS * GRID_W
    assert (kh // 2) % NA_STEP_ROWS == 0 and (rows - kh) % NA_STEP_ROWS == 0 and n_q == LANES
    assert (rows - kh + NA_SLAB_ROWS) * GRID_W <= n_tok and kh + NA_STEP_ROWS <= NA_SLAB_ROWS
    assert n_ctx % PV_CHUNK == 0 and (NA_SLAB_ROWS * GRID_W) % LANES == 0
    _, first_steps = _natten_step_classes(rows)
    step_q = NA_GROUPS * n_q
    assert n_lat % step_q == 0 and n_ctx % n_q == 0 and n_ctx <= step_q
    n_steps = pl.cdiv(n_tok if with_ctx else n_lat, step_q)

    full = pl.BlockSpec((1, n_tok, BRANCH_W), lambda b, s: (b, 0, 0))
    full_t = pl.BlockSpec((1, BRANCH_W, n_tok), lambda b, s: (b, 0, 0))
    tile = pl.BlockSpec((1, step_q, BRANCH_W), lambda b, s: (b, s, 0))
    return pl.pallas_call(
        functools.partial(_natten_kernel, n_lat=n_lat, n_ctx=n_ctx,
                          class_starts=tuple(first_steps[1:])),
        grid=(bsz, n_steps),
        in_specs=[tile, full, full_t, _resident_layer(bias.shape[1:], layer)],
        out_specs=tile,
        out_shape=jax.ShapeDtypeStruct((bsz, n_tok if with_ctx else n_lat, BRANCH_W), BF16),
        scratch_shapes=[pltpu.VMEM((n_ctx + NA_SLAB_ROWS * GRID_W, 2 * n_q), F32)]
                       * (NA_GROUPS * NA_HEADS // 2),
        compiler_params=_params(2),
        name="neighbourhood_attention",
    )(qc, kc, vct, bias)


def _merge_kernel(x_ref, xc_ref, mod_ref, g_ref, fa_ref, db_ref, nc_ref, wg_ref, bg_ref, wa_ref,
                  wb_ref, wc_ref, wo_ref, o_ref, *, lat_subs, n_subs):
    def sub_tile(s, src_ref=x_ref):
        rows = slice(s * TOK_TILE, (s + 1) * TOK_TILE)
        x = src_ref[0, rows, :]
        a = _norm_modulate(x, g_ref[...], mod_ref[0:1, :], mod_ref[1:2, :]).astype(BF16)

        def gate(j):
            cols = slice(j * D_MODEL, (j + 1) * D_MODEL)
            z = _dot(a, wg_ref[:, cols]) + bg_ref[:, cols]
            return 1.0 / (1.0 + jnp.exp(-z))

        y = (gate(0) * _dot(fa_ref[0, rows, :], wa_ref[...])
             + gate(1) * _dot(db_ref[0, rows, :], wb_ref[...])
             + gate(2) * _dot(nc_ref[0, rows, :], wc_ref[...]))
        y = _dot(y.astype(BF16), wo_ref[...])
        o_ref[0, rows, :] = x + mod_ref[2:3, :] * y

    _for_sub_tiles(pl.program_id(1), lat_subs, n_subs, sub_tile,
                   functools.partial(sub_tile, src_ref=xc_ref))


def _merge(stream, mod, g_mix, fa, db, nc, w_g, b_gate, w_a, w_b, w_c, w_out, layer, n_lat,
           with_ctx):
    bsz, n_tok = fa.shape[0], fa.shape[1]
    n_rows = n_tok if with_ctx else n_lat
    full_steps = n_lat // STEP_ROWS
    tok = lambda w: pl.BlockSpec((1, STEP_ROWS, w), lambda b, t: (b, t, 0))
    return pl.pallas_call(
        functools.partial(_merge_kernel, lat_subs=n_lat // TOK_TILE, n_subs=n_rows // TOK_TILE),
        grid=(bsz, pl.cdiv(n_rows, STEP_ROWS)),
        in_specs=_stream_specs(stream, n_lat) + [
            pl.BlockSpec((None, None, N_MOD, D_MODEL), lambda b, t: (b, t // full_steps, 0, 0)),
            _resident((1, D_MODEL)),
            tok(BRANCH_W), tok(BRANCH_W), tok(BRANCH_W),
            _resident_layer((D_MODEL, N_BRANCH * D_MODEL), layer),
            _resident((1, N_BRANCH * D_MODEL)),
            _resident_layer((BRANCH_W, D_MODEL), layer), _resident_layer((BRANCH_W, D_MODEL), layer),
            _resident_layer((BRANCH_W, D_MODEL), layer), _resident_layer((D_MODEL, D_MODEL), layer),
        ],
        out_specs=tok(D_MODEL),
        out_shape=jax.ShapeDtypeStruct((bsz, n_rows, D_MODEL), F32),
        compiler_params=_params(2),
        name="merge_out_projection",
    )(stream[0], stream[1], mod, g_mix.reshape(1, D_MODEL), fa, db, nc, w_g,
      b_gate.reshape(1, -1), w_a, w_b, w_c, w_out)


def _convffn_kernel(x_ref, prev_ref, next_ref, mod_ref, g_ref, wu_ref, cw_ref, cb_ref, wd_ref,
                    gf_ref, o_ref, *lhs_refs, lat_subs, n_subs, final_norm):
    t = pl.program_id(1)
    g = g_ref[...]
    shift, scale = mod_ref[3:4, :], mod_ref[4:5, :]
    n_rows = TOK_TILE + 2 * SUBLANES

    def conv(u, col):
        w = cw_ref[:, col:col + FF_CHUNK]
        before = pltpu.roll(u, 1, axis=0)
        after = pltpu.roll(u, n_rows - 1, axis=0)
        v = before * w[0:1] + u * w[1:2] + after * w[2:3] + cb_ref[:, col:col + FF_CHUNK]
        return v[SUBLANES:SUBLANES + TOK_TILE]

    def prepare(s):
        idx = t * SUB_TILES + s
        lo = s * TOK_TILE
        x = x_ref[0, lo:lo + TOK_TILE, :]
        has_prev = jnp.logical_and(idx != 0, idx != lat_subs)
        has_next = jnp.logical_and(idx != lat_subs - 1, idx != n_subs - 1)
        prev = prev_ref[0] if s == 0 else x_ref[0, lo - SUBLANES:lo, :]
        nxt = (next_ref[0] if s == SUB_TILES - 1
               else x_ref[0, lo + TOK_TILE:lo + TOK_TILE + SUBLANES, :])
        lhs_ref = lhs_refs[s]
        lhs_ref[0:SUBLANES, :] = jnp.where(has_prev, _norm_modulate(prev, g, shift, scale), 0.0)
        lhs_ref[SUBLANES:SUBLANES + TOK_TILE, :] = _norm_modulate(x, g, shift, scale)
        lhs_ref[SUBLANES + TOK_TILE:, :] = jnp.where(
            has_next, _norm_modulate(nxt, g, shift, scale), 0.0)
        return x, lhs_ref[...].astype(BF16)

    def run(n_sub_tiles):
        tiles = [prepare(s) for s in range(n_sub_tiles)]
        items = [(j, s) for j in range(D_FF // FF_CHUNK) for s in range(n_sub_tiles)]

        def up(item):
            j, s = item
            lhs = tiles[s][1]
            col_a, col_b = j * FF_CHUNK, D_FF + j * FF_CHUNK
            return (_dot(lhs, wu_ref[:, col_a:col_a + FF_CHUNK]),
                    _dot(lhs, wu_ref[:, col_b:col_b + FF_CHUNK]))

        look = FF_LOOKAHEAD * n_sub_tiles
        accs = [jnp.zeros((TOK_TILE, D_MODEL), F32) for _ in range(n_sub_tiles)]
        ahead = [up(item) for item in items[:look]]
        for i, (j, s) in enumerate(items):
            col_a, col_b = j * FF_CHUNK, D_FF + j * FF_CHUNK
            u_a, u_b = ahead.pop(0)
            if i + look < len(items):
                ahead.append(up(items[i + look]))
            a = conv(u_a, col_a)
            b = conv(u_b, col_b)
            act = (a / (1.0 + jnp.exp(-a))) * b
            accs[s] = accs[s] + _dot(act.astype(BF16), wd_ref[col_a:col_a + FF_CHUNK, :])
        for s in range(n_sub_tiles):
            y = tiles[s][0] + mod_ref[5:6, :] * accs[s]
            if final_norm:
                ms = jnp.mean(y * y, axis=-1, keepdims=True)
                y = y * lax.rsqrt(ms + NORM_EPS) * gf_ref[...]
            o_ref[0, s * TOK_TILE:(s + 1) * TOK_TILE, :] = y

    full_steps = lat_subs // SUB_TILES
    pl.when(t < full_steps)(functools.partial(run, SUB_TILES))
    if n_subs > lat_subs:
        pl.when(t >= full_steps)(functools.partial(run, n_subs - lat_subs))


def _conv_ffn(h, mod, g_ffn, w_up, conv_w, conv_b, w_down, layer, g_final, n_lat, with_ctx,
              final_norm):
    bsz, n_rows_in, _ = h.shape
    n_rows = n_rows_in if with_ctx else n_lat
    lat_subs, n_subs = n_lat // TOK_TILE, n_rows // TOK_TILE
    full_steps = n_lat // STEP_ROWS
    blocks_per_step = STEP_ROWS // SUBLANES
    last_block = n_rows_in // SUBLANES - 1
    return pl.pallas_call(
        functools.partial(_convffn_kernel, lat_subs=lat_subs, n_subs=n_subs,
                          final_norm=final_norm),
        grid=(bsz, pl.cdiv(n_rows, STEP_ROWS)),
        in_specs=[
            pl.BlockSpec((1, STEP_ROWS, D_MODEL), lambda b, t: (b, t, 0)),
            pl.BlockSpec((1, SUBLANES, D_MODEL),
                         lambda b, t: (b, jnp.maximum(t * blocks_per_step - 1, 0), 0)),
            pl.BlockSpec((1, SUBLANES, D_MODEL),
                         lambda b, t: (b, jnp.minimum((t + 1) * blocks_per_step, last_block), 0)),
            pl.BlockSpec((None, None, N_MOD, D_MODEL), lambda b, t: (b, t // full_steps, 0, 0)),
            _resident((1, D_MODEL)),
            _resident_layer((D_MODEL, 2 * D_FF), layer),
            _resident((3, 2 * D_FF)),
            _resident((1, 2 * D_FF)),
            _resident_layer((D_FF, D_MODEL), layer),
            _resident((1, D_MODEL)),
        ],
        out_specs=pl.BlockSpec((1, STEP_ROWS, D_MODEL), lambda b, t: (b, t, 0)),
        out_shape=jax.ShapeDtypeStruct((bsz, n_rows, D_MODEL), F32),
        scratch_shapes=[pltpu.VMEM((TOK_TILE + 2 * SUBLANES, D_MODEL), F32)] * SUB_TILES,
        compiler_params=_params(2),
        name="conv_ffn",
    )(h, h, h, mod, g_ffn.reshape(1, D_MODEL), w_up, conv_w, conv_b.reshape(1, -1), w_down,
      g_final.reshape(1, D_MODEL))


def _rope_tables(n_lat, n_ctx):
    t = jnp.arange(n_lat, dtype=jnp.int32)
    pos = jnp.stack([(t // GRID_W).astype(F32), (t % GRID_W).astype(F32)], axis=1)
    n_freq = ROPE_AXIS_DIM // 2
    inv = ROPE_THETA ** (-jnp.arange(n_freq, dtype=F32) / n_freq)
    ang = pos[:, :, None] * inv
    lane = jnp.arange(LANES, dtype=jnp.int32)
    axis = (lane % DA_HEAD_DIM) // ROPE_AXIS_DIM
    freq = lane % n_freq
    second_half = (lane % ROPE_AXIS_DIM) >= n_freq
    ang_l = ang[:, axis, freq]
    cos, sin = jnp.cos(ang_l), jnp.sin(ang_l)
    sin_a = jnp.where(second_half, 0.0, -sin)
    sin_b = jnp.where(second_half, sin, 0.0)
    pad = lambda a, v: jnp.concatenate([a, jnp.full((n_ctx, LANES), v, F32)], axis=0)
    return jnp.concatenate([pad(cos, 1.0), pad(sin_a, 0.0), pad(sin_b, 0.0)], axis=1)


def kernel(x, c, ctx, c_ctx, w_ada, b_ada, g_mix, g_ffn, w_in, b_gate, w_a, lam, subln_g, w_b,
           rpb, w_c, w_out, w_up, conv_w, conv_b, w_down, g_final):
    bsz, n_lat, _ = x.shape
    n_ctx = ctx.shape[1]
    assert n_lat % STEP_ROWS == 0 and n_ctx == TOK_TILE and n_lat % GRID_W == 0

    rope_tabs = _rope_tables(n_lat, n_ctx)
    cl, sl = _dft_tables(n_lat, n_lat // 2, (n_lat * FN_GROUP_DIM) ** -0.5)
    cc, sc = _dft_tables(n_ctx, n_ctx, (n_ctx * FN_GROUP_DIM) ** -0.5)
    cg, sg = _dft_tables(FN_GROUP_DIM, FN_GROUP_DIM, 1.0)
    dft_tabs = (cl, sl, cc, sc, jnp.concatenate([cg, -sg], axis=0), _reversal_matrix())

    n_mod_rows = 2 * SUBLANES * (-(-(bsz + 1) // (2 * SUBLANES)))
    cvec = jnp.zeros((n_mod_rows, D_MODEL), F32).at[:bsz].set(c).at[bsz].set(c_ctx)
    mods = _modulation(cvec, w_ada, b_ada).reshape(DEPTH, n_mod_rows, N_MOD, D_MODEL)

    na_bias = _natten_bias(rpb, n_lat // GRID_W)

    n_branch_cols = w_in.shape[-1] - N_BRANCH * D_MODEL
    w_branch, w_gate, w_a, w_b, w_c, w_out, w_up, w_down = (
        w.astype(BF16) for w in (w_in[..., :n_branch_cols], w_in[..., n_branch_cols:],
                                 w_a, w_b, w_c, w_out, w_up, w_down))

    stream = (x, ctx, 0)
    for l in range(DEPTH):
        with_ctx = l != DEPTH - 1
        lam_init = 0.8 - 0.6 * math.exp(-0.3 * l)
        mod = jnp.stack([mods[l, :bsz],
                         jnp.broadcast_to(mods[l, bsz], (bsz, N_MOD, D_MODEL))], axis=1)
        fa_in, qb, kb, vbt, qc, kc, vct = _in_projection(
            stream, mod, g_mix[l], w_branch, l, rope_tabs, n_lat, n_ctx)
        fa = _fourier_mix(fa_in, dft_tabs, n_lat, n_ctx, with_ctx)
        db = _diff_attention(qb, kb, vbt, lam[l], subln_g[l], lam_init, n_lat, n_ctx, with_ctx)
        nc = _neighbourhood_attention(qc, kc, vct, na_bias, l, n_lat, n_ctx, with_ctx)
        h = _merge(stream, mod, g_mix[l], fa, db, nc, w_gate, b_gate[l], w_a, w_b, w_c, w_out, l,
                   n_lat, with_ctx)
        h = _conv_ffn(h, mod, g_ffn[l], w_up, conv_w[l], conv_b[l], w_down, l, g_final, n_lat,
                      with_ctx, final_norm=not with_ctx)
        stream = (h, h, n_lat // n_ctx)
    return h
```

```python
import functools
import math

import jax
import jax.numpy as jnp
from jax import lax
from jax.experimental import pallas as pl
from jax.experimental.pallas import tpu as pltpu

D_MODEL = 1024
DEPTH = 4
GRID_W = 64
FN_GROUPS = 4
FN_GROUP_DIM = 128
FN_WIDTH = FN_GROUPS * FN_GROUP_DIM
DA_HEADS = 4
DA_HEAD_DIM = 64
DA_V_DIM = 2 * DA_HEAD_DIM
NA_HEADS = 8
NA_HEAD_DIM = 64
NA_ROWS = 8
NA_COLS = 16
BRANCH_W = 512
N_BRANCH = 3
ROPE_THETA = 10000.0
ROPE_AXIS_DIM = DA_HEAD_DIM // 2
D_FF = 2816
N_MOD = 6
NORM_EPS = 1e-6
SUBLN_EPS = 1e-5
NEG_INF = -1e30

LANES = 128
SUBLANES = 8
TOK_TILE = 256
SUB_TILES = 2
STEP_ROWS = SUB_TILES * TOK_TILE
FF_CHUNK = 256
DA_LOOKAHEAD = 1
FF_LOOKAHEAD = 2
NA_STEP_ROWS = 2
NA_GROUPS = 2
NA_SLAB_ROWS = 10
VMEM_LIMIT = 56 * 1024 * 1024
ONES_ROWS = 16
PV_CHUNK = 256
LOG2_E = math.log2(math.e)

BF16 = jnp.bfloat16
F32 = jnp.float32


def _params(n_grid_dims):
    return pltpu.CompilerParams(dimension_semantics=("arbitrary",) * n_grid_dims,
                                vmem_limit_bytes=VMEM_LIMIT)


def _resident(shape):
    return pl.BlockSpec(shape, lambda *_: (0,) * len(shape), pipeline_mode=pl.Buffered(1))


def _resident_layer(shape, layer):
    return pl.BlockSpec((None,) + tuple(shape), lambda *_: (layer,) + (0,) * len(shape),
                        pipeline_mode=pl.Buffered(1))


def _for_sub_tiles(t, lat_subs, n_subs, body, ctx_body=None):
    full_steps = lat_subs // SUB_TILES
    if n_subs == lat_subs:
        for s in range(SUB_TILES):
            body(s)
        return

    @pl.when(t < full_steps)
    def _():
        for s in range(SUB_TILES):
            body(s)

    @pl.when(t >= full_steps)
    def _():
        for s in range(n_subs - lat_subs):
            (ctx_body or body)(s)


def _tile_of_step(step, n_steps, ctx_first):
    return (step + n_steps - 1) % n_steps if ctx_first else step


def _nt_dot(a, b):
    return lax.dot_general(a, b, (((1,), (1,)), ((), ())), preferred_element_type=F32)


def _dot(a, b):
    return jnp.dot(a, b, preferred_element_type=F32)


def _norm_modulate(x, g, shift, scale):
    ms = jnp.mean(x * x, axis=-1, keepdims=True)
    return (x * lax.rsqrt(ms + NORM_EPS) * g) * (1.0 + scale) + shift


def _split_bf16(v):
    hi = v.astype(BF16)
    lo = (v - hi.astype(F32)).astype(BF16)
    return hi, lo


def _mod_kernel(c_ref, w_ref, b_ref, o_ref):
    c = c_ref[...]
    s = c / (1.0 + jnp.exp(-c))
    s_hi, s_lo = _split_bf16(s)
    w_hi, w_lo = _split_bf16(w_ref[0])
    acc = _dot(s_hi, w_hi) + (_dot(s_hi, w_lo) + _dot(s_lo, w_hi))
    o_ref[0] = acc + b_ref[0]


def _modulation(cvec, w_ada, b_ada):
    n_rows = cvec.shape[0]
    n_out = N_MOD * D_MODEL
    tn = 1536
    return pl.pallas_call(
        _mod_kernel,
        grid=(DEPTH, n_out // tn),
        in_specs=[
            pl.BlockSpec((n_rows, D_MODEL), lambda l, j: (0, 0)),
            pl.BlockSpec((1, D_MODEL, tn), lambda l, j: (l, 0, j)),
            pl.BlockSpec((1, 1, tn), lambda l, j: (l, 0, j)),
        ],
        out_specs=pl.BlockSpec((1, n_rows, tn), lambda l, j: (l, 0, j)),
        out_shape=jax.ShapeDtypeStruct((DEPTH, n_rows, n_out), F32),
        compiler_params=_params(2),
        name="adaln_modulation",
    )(cvec, w_ada, b_ada.reshape(DEPTH, 1, n_out))


def _rope(p, cos, sin_a, sin_b):
    outs = []
    for k in range(p.shape[1] // LANES):
        xs = p[:, k * LANES:(k + 1) * LANES]
        from_hi = pltpu.roll(xs, LANES - ROPE_AXIS_DIM // 2, axis=1)
        from_lo = pltpu.roll(xs, ROPE_AXIS_DIM // 2, axis=1)
        outs.append(xs * cos + from_hi * sin_a + from_lo * sin_b)
    return jnp.concatenate(outs, axis=1)


def _inproj_kernel(x_ref, xc_ref, mod_ref, g_ref, w_ref, rope_ref,
                   fa_ref, qb_ref, kb_ref, vbt_ref, qc_ref, kc_ref, vct_ref,
                   *, lat_subs, n_subs):
    def sub_tile(s, src_ref=x_ref):
        rows = slice(s * TOK_TILE, (s + 1) * TOK_TILE)
        a = _norm_modulate(src_ref[0, rows, :], g_ref[...], mod_ref[0:1, :],
                           mod_ref[1:2, :]).astype(BF16)
        tok0 = pl.multiple_of(pl.program_id(1) * STEP_ROWS + s * TOK_TILE, TOK_TILE)
        cos, sin_a, sin_b = (rope_ref[pl.ds(tok0, TOK_TILE), k * LANES:(k + 1) * LANES]
                             for k in range(3))

        def proj(seg):
            return _dot(a, w_ref[:, seg * BRANCH_W:(seg + 1) * BRANCH_W])

        fa_ref[0, rows, :] = proj(0).astype(BF16)
        qb_ref[0, rows, :] = (_rope(proj(1), cos, sin_a, sin_b)
                              * (DA_HEAD_DIM ** -0.5 * LOG2_E)).astype(BF16)
        kb_ref[0, rows, :] = _rope(proj(2), cos, sin_a, sin_b).astype(BF16)
        vbt_ref[0, :, rows] = proj(3).T.astype(BF16)
        qc_ref[0, rows, :] = (proj(4) * (NA_HEAD_DIM ** -0.5 * LOG2_E)).astype(BF16)
        kc_ref[0, rows, :] = proj(5).astype(BF16)
        vct_ref[0, :, rows] = proj(6).T.astype(BF16)

    _for_sub_tiles(pl.program_id(1), lat_subs, n_subs, sub_tile,
                   functools.partial(sub_tile, src_ref=xc_ref))


def _stream_specs(stream, n_lat):
    _, _, ctx_block = stream
    last = n_lat // STEP_ROWS - 1
    return [pl.BlockSpec((1, STEP_ROWS, D_MODEL), lambda b, t: (b, jnp.minimum(t, last), 0)),
            pl.BlockSpec((1, TOK_TILE, D_MODEL), lambda b, t: (b, ctx_block, 0))]


def _in_projection(stream, mod, g_mix, w_branch, layer, rope_tabs, n_lat, n_ctx):
    bsz, n_tok = stream[0].shape[0], n_lat + n_ctx
    full_steps = n_lat // STEP_ROWS
    proj_w = w_branch.shape[-1]
    tok = lambda w: pl.BlockSpec((1, STEP_ROWS, w), lambda b, t: (b, t, 0))
    branch = jax.ShapeDtypeStruct((bsz, n_tok, BRANCH_W), BF16)
    tok_t = pl.BlockSpec((1, BRANCH_W, STEP_ROWS), lambda b, t: (b, 0, t))
    branch_t = jax.ShapeDtypeStruct((bsz, BRANCH_W, n_tok), BF16)
    return pl.pallas_call(
        functools.partial(_inproj_kernel, lat_subs=n_lat // TOK_TILE, n_subs=n_tok // TOK_TILE),
        grid=(bsz, pl.cdiv(n_tok, STEP_ROWS)),
        in_specs=_stream_specs(stream, n_lat) + [
            pl.BlockSpec((None, None, N_MOD, D_MODEL), lambda b, t: (b, t // full_steps, 0, 0)),
            _resident((1, D_MODEL)),
            _resident_layer((D_MODEL, proj_w), layer),
            _resident((n_tok, 3 * LANES)),
        ],
        out_specs=[tok(BRANCH_W)] * 3 + [tok_t] + [tok(BRANCH_W)] * 2 + [tok_t],
        out_shape=[branch] * 3 + [branch_t] + [branch] * 2 + [branch_t],
        compiler_params=_params(2),
        name="in_projection",
    )(stream[0], stream[1], mod, g_mix.reshape(1, D_MODEL), w_branch, rope_tabs)


def _dft_kernel(u_ref, cl_ref, sl_ref, cc_ref, sc_ref, csg_ref, rev_ref, o_ref, ue_ref, uo_ref,
                *, n_lat, n_ctx, n_subs, scale, n_steps, ctx_first):
    t = _tile_of_step(pl.program_id(1), n_steps, ctx_first)
    half = n_lat // 2

    def finish(s, p, q):
        rows = slice(s * TOK_TILE, (s + 1) * TOK_TILE)
        for g in range(FN_GROUPS):
            sl = slice(g * FN_GROUP_DIM, (g + 1) * FN_GROUP_DIM)
            pq = jnp.concatenate([p[:, sl], q[:, sl]], axis=1).astype(BF16)
            o_ref[0, rows, sl] = _dot(pq, csg_ref[...]).astype(BF16)

    @pl.when(pl.program_id(1) == 0)
    def _():
        for b in range(half // TOK_TILE):
            hi = n_lat - (b + 1) * TOK_TILE
            first = u_ref[0, hi:hi + TOK_TILE, :]
            wrap = (u_ref[0, hi + TOK_TILE:hi + 2 * TOK_TILE, :] if b > 0
                    else jnp.zeros((TOK_TILE, FN_WIDTH), BF16))
            rev = _dot(rev_ref[...], jnp.concatenate([first, wrap], axis=0))
            lo = u_ref[0, b * TOK_TILE:(b + 1) * TOK_TILE, :].astype(F32)
            ue_ref[b * TOK_TILE:(b + 1) * TOK_TILE, :] = (lo + rev).astype(BF16)
            uo_ref[b * TOK_TILE:(b + 1) * TOK_TILE, :] = (lo - rev).astype(BF16)

    def latent(s):
        k0 = pl.multiple_of((t * SUB_TILES + s) * TOK_TILE, TOK_TILE)
        k = k0 + lax.broadcasted_iota(jnp.int32, (TOK_TILE, 1), 0)
        sign = (1 - 2 * (k & 1)).astype(F32) * scale
        rows = pl.ds(k0, TOK_TILE)
        p = _dot(cl_ref[rows, :], ue_ref[...]) + sign * u_ref[0, half:half + 1, :].astype(F32)
        finish(s, p, _dot(sl_ref[rows, :], uo_ref[...]))

    def context(s):
        u = u_ref[0, n_lat:n_lat + n_ctx, :]
        finish(s, _dot(cc_ref[...], u), _dot(sc_ref[...], u))

    _for_sub_tiles(t, n_lat // TOK_TILE, n_subs, latent, context)


def _fourier_mix(fa_in, tabs, n_lat, n_ctx, with_ctx):
    bsz, n_tok, _ = fa_in.shape
    n_rows = n_tok if with_ctx else n_lat
    half = n_lat // 2
    assert half % TOK_TILE == 0
    cl, sl, cc, sc, csg, rev = tabs
    lat_tab = _resident((n_lat, half))
    n_steps = pl.cdiv(n_rows, STEP_ROWS)
    return pl.pallas_call(
        functools.partial(_dft_kernel, n_lat=n_lat, n_ctx=n_ctx, n_subs=n_rows // TOK_TILE,
                          scale=(n_lat * FN_GROUP_DIM) ** -0.5, n_steps=n_steps,
                          ctx_first=with_ctx),
        grid=(bsz, n_steps),
        in_specs=[
            pl.BlockSpec((1, n_tok, FN_WIDTH), lambda b, t: (b, 0, 0)),
            lat_tab, lat_tab,
            _resident((n_ctx, n_ctx)), _resident((n_ctx, n_ctx)),
            _resident((2 * FN_GROUP_DIM, FN_GROUP_DIM)),
            _resident((TOK_TILE, 2 * TOK_TILE)),
        ],
        out_specs=pl.BlockSpec((1, STEP_ROWS, FN_WIDTH),
                               lambda b, t: (b, _tile_of_step(t, n_steps, with_ctx), 0)),
        out_shape=jax.ShapeDtypeStruct((bsz, n_rows, FN_WIDTH), BF16),
        scratch_shapes=[pltpu.VMEM((half, FN_WIDTH), BF16)] * 2,
        compiler_params=_params(2),
        name="fourier_mix",
    )(fa_in, cl, sl, cc, sc, csg, rev)


def _dft_tables(n, n_cols, scale):
    k = jnp.arange(n, dtype=jnp.int32)
    ang = ((k[:, None] * k[None, :n_cols]) % n).astype(F32) * (2.0 * math.pi / n)
    return (jnp.cos(ang) * scale).astype(BF16), (jnp.sin(ang) * scale).astype(BF16)


def _reversal_matrix():
    i = jnp.arange(TOK_TILE, dtype=jnp.int32)[:, None]
    j = jnp.arange(2 * TOK_TILE, dtype=jnp.int32)[None, :]
    src = jnp.where(i == 0, TOK_TILE, TOK_TILE - i)
    return (j == src).astype(BF16)


def _stack_sub_heads(q, first_half):
    zero = jnp.zeros_like(q)
    return jnp.concatenate([jnp.where(first_half, q, zero), jnp.where(first_half, zero, q)], axis=0)


def _store_scores(s_ref, row, s):
    n = s.shape[0]
    s_ref[row:row + n, :] = s
    return jnp.max(s.reshape(n // SUBLANES, SUBLANES, -1).max(axis=0), axis=0, keepdims=True)


def _softmax_times_values(s_ref, n_rows, m, values_t):
    acc = None
    for r in range(0, n_rows, PV_CHUNK):
        n = min(PV_CHUNK, n_rows - r)
        e = jnp.exp2(s_ref[r:r + n, :] - m).astype(BF16)
        v = values_t(r, n)
        part = _dot(jnp.concatenate([v, jnp.ones((ONES_ROWS, n), BF16)], axis=0), e)
        acc = part if acc is None else acc + part
    n_ch = acc.shape[0] - ONES_ROWS
    return acc[0:n_ch] * (1.0 / acc[n_ch:n_ch + 1])


def _diffattn_kernel(q_ref, k_ref, vt_ref, lam_ref, g_ref, o_ref, *s_refs,
                     n_lat, n_ctx, lam_init, n_steps, ctx_first):
    t = _tile_of_step(pl.program_id(1), n_steps, ctx_first)
    lv = lam_ref[...]
    lam = (jnp.exp(jnp.sum(lv[0:1] * lv[1:2], axis=-1, keepdims=True))
           - jnp.exp(jnp.sum(lv[2:3] * lv[3:4], axis=-1, keepdims=True)) + lam_init)
    first_half = lax.broadcasted_iota(jnp.int32, (1, LANES), 1) < DA_HEAD_DIM

    def attend(n_sub_tiles, k_lo, k_len):
        items = [(s, h) for s in range(n_sub_tiles) for h in range(DA_HEADS)]

        def scores(i):
            s, h = items[i]
            sl = slice(h * DA_V_DIM, (h + 1) * DA_V_DIM)
            q = q_ref[0, s * TOK_TILE:(s + 1) * TOK_TILE, sl]
            sc = _nt_dot(k_ref[0, k_lo:k_lo + k_len, sl], _stack_sub_heads(q, first_half))
            return _store_scores(s_refs[i % len(s_refs)], 0, sc)

        ahead = [scores(i) for i in range(min(DA_LOOKAHEAD, len(items)))]
        for i, (s, h) in enumerate(items):
            sl = slice(h * DA_V_DIM, (h + 1) * DA_V_DIM)
            m = ahead.pop(0)
            if i + DA_LOOKAHEAD < len(items):
                ahead.append(scores(i + DA_LOOKAHEAD))
            o12 = _softmax_times_values(
                s_refs[i % len(s_refs)], k_len, m,
                lambda r, n: vt_ref[0, sl, k_lo + r:k_lo + r + n]).T
            o = o12[0:TOK_TILE] - lam * o12[TOK_TILE:2 * TOK_TILE]
            ms = jnp.mean(o * o, axis=-1, keepdims=True)
            o = o * lax.rsqrt(ms + SUBLN_EPS) * g_ref[...] * (1.0 - lam_init)
            o_ref[0, s * TOK_TILE:(s + 1) * TOK_TILE, sl] = o.astype(BF16)

    @pl.when(t < n_lat // STEP_ROWS)
    def _():
        attend(SUB_TILES, 0, n_lat + n_ctx)

    @pl.when(t >= n_lat // STEP_ROWS)
    def _():
        attend(n_ctx // TOK_TILE, n_lat, n_ctx)


def _diff_attention(qb, kb, vbt, lam_vec, subln_g, lam_init, n_lat, n_ctx, with_ctx):
    bsz, n_tok, _ = qb.shape
    n_rows = n_tok if with_ctx else n_lat
    full = pl.BlockSpec((1, n_tok, BRANCH_W), lambda b, t: (b, 0, 0))
    full_t = pl.BlockSpec((1, BRANCH_W, n_tok), lambda b, t: (b, 0, 0))
    n_steps = pl.cdiv(n_rows, STEP_ROWS)
    tile = pl.BlockSpec((1, STEP_ROWS, BRANCH_W),
                        lambda b, t: (b, _tile_of_step(t, n_steps, with_ctx), 0))
    return pl.pallas_call(
        functools.partial(_diffattn_kernel, n_lat=n_lat, n_ctx=n_ctx, lam_init=lam_init,
                          n_steps=n_steps, ctx_first=with_ctx),
        grid=(bsz, n_steps),
        in_specs=[tile, full, full_t, _resident((4, DA_HEAD_DIM)), _resident((1, DA_V_DIM))],
        out_specs=tile,
        out_shape=jax.ShapeDtypeStruct((bsz, n_rows, BRANCH_W), BF16),
        scratch_shapes=[pltpu.VMEM((n_tok, 2 * TOK_TILE), F32)] * (DA_LOOKAHEAD + 1),
        compiler_params=_params(2),
        name="diff_attention",
    )(qb, kb, vbt, lam_vec, subln_g.reshape(1, DA_V_DIM))


def _natten_kernel(q_ref, k_ref, vt_ref, bias_ref, o_ref, *s_refs, n_lat, n_ctx, class_starts,
                   n_steps, ctx_first):
    s_idx = _tile_of_step(pl.program_id(1), n_steps, ctx_first)
    rows = n_lat // GRID_W
    kh = min(NA_ROWS, rows)
    n_slab = NA_SLAB_ROWS * GRID_W
    n_q = NA_STEP_ROWS * GRID_W
    n_pairs = NA_HEADS // 2
    first_half = lax.broadcasted_iota(jnp.int32, (1, LANES), 1) < NA_HEAD_DIM
    pairs = [slice(p * LANES, (p + 1) * LANES) for p in range(n_pairs)]

    def run(window, n_groups):
        n_keys = n_ctx + (n_slab if window else 0)
        k_los, maxes = [], []
        for g in range(n_groups):
            group = NA_GROUPS * s_idx + g
            first_row = jnp.clip(NA_STEP_ROWS * group - kh // 2, 0, rows - kh)
            k_lo = pl.multiple_of(first_row * GRID_W, LANES)
            k_los.append(k_lo)
            cls = sum((group >= f).astype(jnp.int32) for f in class_starts)
            for p, sl in enumerate(pairs):
                s_ref = s_refs[g * n_pairs + p]
                qs = _stack_sub_heads(q_ref[0, g * n_q:(g + 1) * n_q, sl], first_half)
                m = _store_scores(s_ref, 0, _nt_dot(k_ref[0, n_lat:n_lat + n_ctx, sl], qs))
                if window:
                    s_win = _nt_dot(k_ref[0, pl.ds(k_lo, n_slab), sl], qs) + bias_ref[cls, p]
                    m = jnp.maximum(m, _store_scores(s_ref, n_ctx, s_win))
                maxes.append(m)
        for g in range(n_groups):
            for p, sl in enumerate(pairs):
                def values_t(r, n):
                    if r < n_ctx:
                        return vt_ref[0, sl, n_lat + r:n_lat + r + n]
                    return vt_ref[0, sl, pl.ds(pl.multiple_of(k_los[g] + (r - n_ctx), LANES), n)]

                i = g * n_pairs + p
                o2 = _softmax_times_values(s_refs[i], n_keys, maxes[i], values_t).T
                o_ref[0, g * n_q:(g + 1) * n_q, sl] = jnp.where(
                    first_half, o2[0:n_q], o2[n_q:2 * n_q]).astype(BF16)

    window_steps = rows // (NA_STEP_ROWS * NA_GROUPS)

    @pl.when(s_idx < window_steps)
    def _():
        run(True, NA_GROUPS)

    @pl.when(s_idx >= window_steps)
    def _():
        run(False, min(NA_GROUPS, n_ctx // n_q))


def _natten_step_classes(rows):
    kh = min(NA_ROWS, rows)
    classes, first_steps = [], []
    for s in range(rows // NA_STEP_ROWS):
        slab = min(max(NA_STEP_ROWS * s - kh // 2, 0), rows - kh)
        geom = tuple((r - slab, min(max(r - kh // 2, 0), rows - kh) - slab)
                     for r in range(NA_STEP_ROWS * s, NA_STEP_ROWS * (s + 1)))
        geom = geom + (min(NA_SLAB_ROWS, rows - slab),)
        if not classes or classes[-1] != geom:
            assert geom not in classes
            classes.append(geom)
            first_steps.append(s)
    return classes, first_steps


def _natten_bias_kernel(r_ref, o_ref, *, classes, kh):
    cls = pl.program_id(1)
    kc = lax.broadcasted_iota(jnp.int32, (GRID_W, LANES), 0)
    lane = lax.broadcasted_iota(jnp.int32, (GRID_W, LANES), 1)
    second = lane >= GRID_W
    c = jnp.where(second, lane - GRID_W, lane)
    c0 = jnp.clip(c - NA_COLS // 2, 0, GRID_W - NA_COLS)
    col_ok = jnp.logical_and(kc >= c0, kc < c0 + NA_COLS)
    neg = jnp.full((GRID_W, LANES), NEG_INF, F32)

    def fill(geom):
        for h in range(NA_HEADS):
            for kr in range(NA_SLAB_ROWS):
                halves = []
                for j, (q_off, w_off) in enumerate(geom[:-1]):
                    if w_off <= kr < w_off + kh and kr < geom[-1]:
                        dr = kr - q_off + NA_ROWS - 1
                        row = jnp.broadcast_to(r_ref[0, h, dr:dr + 1, :], (GRID_W, LANES))
                        halves.append(pltpu.roll(row, (j * GRID_W - (NA_COLS - 1)) % LANES, axis=1,
                                                 stride=1, stride_axis=0))
                    else:
                        halves.append(neg)
                val = jnp.where(col_ok, jnp.where(second, halves[1], halves[0]), neg)
                o_ref[0, 0, h // 2, kr * GRID_W:(kr + 1) * GRID_W,
                      (h % 2) * LANES:(h % 2 + 1) * LANES] = val

    for ci, geom in enumerate(classes):
        pl.when(cls == ci)(functools.partial(fill, geom))


def _natten_bias(rpb, rows):
    kh = min(NA_ROWS, rows)
    classes, _ = _natten_step_classes(rows)
    n_dr, n_dc = 2 * NA_ROWS - 1, 2 * NA_COLS - 1
    assert NA_STEP_ROWS == 2 and n_dc <= GRID_W
    r = jnp.pad(rpb[..., ::-1].astype(F32) * LOG2_E,
                ((0, 0), (0, 0), (0, 2 * SUBLANES - n_dr), (0, LANES - n_dc)))
    return pl.pallas_call(
        functools.partial(_natten_bias_kernel, classes=classes, kh=kh),
        grid=(DEPTH, len(classes)),
        in_specs=[pl.BlockSpec((1, NA_HEADS, 2 * SUBLANES, LANES), lambda l, k: (l, 0, 0, 0))],
        out_specs=pl.BlockSpec((1, 1, NA_HEADS // 2, NA_SLAB_ROWS * GRID_W, 2 * LANES),
                               lambda l, k: (l, k, 0, 0, 0)),
        out_shape=jax.ShapeDtypeStruct(
            (DEPTH, len(classes), NA_HEADS // 2, NA_SLAB_ROWS * GRID_W, 2 * LANES), F32),
        compiler_params=_params(2),
        name="natten_bias_tables",
    )(r)


def _neighbourhood_attention(qc, kc, vct, bias, layer, n_lat, n_ctx, with_ctx):
    bsz, n_tok, _ = qc.shape
    rows = n_lat // GRID_W
    kh = min(NA_ROWS, rows)
    n_q = NA_STEP_ROWS * GRID_W
    assert (kh // 2) % NA_STEP_ROWS == 0 and (rows - kh) % NA_STEP_ROWS == 0 and n_q == LANES
    assert (rows - kh + NA_SLAB_ROWS) * GRID_W <= n_tok and kh + NA_STEP_ROWS <= NA_SLAB_ROWS
    assert n_ctx % PV_CHUNK == 0 and (NA_SLAB_ROWS * GRID_W) % LANES == 0
    _, first_steps = _natten_step_classes(rows)
    step_q = NA_GROUPS * n_q
    assert n_lat % step_q == 0 and n_ctx % n_q == 0 and n_ctx <= step_q
    n_steps = pl.cdiv(n_tok if with_ctx else n_lat, step_q)

    full = pl.BlockSpec((1, n_tok, BRANCH_W), lambda b, s: (b, 0, 0))
    full_t = pl.BlockSpec((1, BRANCH_W, n_tok), lambda b, s: (b, 0, 0))
    tile = pl.BlockSpec((1, step_q, BRANCH_W),
                        lambda b, s: (b, _tile_of_step(s, n_steps, with_ctx), 0))
    return pl.pallas_call(
        functools.partial(_natten_kernel, n_lat=n_lat, n_ctx=n_ctx,
                          class_starts=tuple(first_steps[1:]), n_steps=n_steps,
                          ctx_first=with_ctx),
        grid=(bsz, n_steps),
        in_specs=[tile, full, full_t, _resident_layer(bias.shape[1:], layer)],
        out_specs=tile,
        out_shape=jax.ShapeDtypeStruct((bsz, n_tok if with_ctx else n_lat, BRANCH_W), BF16),
        scratch_shapes=[pltpu.VMEM((n_ctx + NA_SLAB_ROWS * GRID_W, 2 * n_q), F32)]
                       * (NA_GROUPS * NA_HEADS // 2),
        compiler_params=_params(2),
        name="neighbourhood_attention",
    )(qc, kc, vct, bias)


def _merge_kernel(x_ref, xc_ref, mod_ref, g_ref, fa_ref, db_ref, nc_ref, wg_ref, bg_ref, wa_ref,
                  wb_ref, wc_ref, wo_ref, o_ref, *, lat_subs, n_subs):
    def sub_tile(s, src_ref=x_ref):
        rows = slice(s * TOK_TILE, (s + 1) * TOK_TILE)
        x = src_ref[0, rows, :]
        a = _norm_modulate(x, g_ref[...], mod_ref[0:1, :], mod_ref[1:2, :]).astype(BF16)

        def gate(j):
            cols = slice(j * D_MODEL, (j + 1) * D_MODEL)
            z = _dot(a, wg_ref[:, cols]) + bg_ref[:, cols]
            return 1.0 / (1.0 + jnp.exp(-z))

        y = (gate(0) * _dot(fa_ref[0, rows, :], wa_ref[...])
             + gate(1) * _dot(db_ref[0, rows, :], wb_ref[...])
             + gate(2) * _dot(nc_ref[0, rows, :], wc_ref[...]))
        y = _dot(y.astype(BF16), wo_ref[...])
        o_ref[0, rows, :] = x + mod_ref[2:3, :] * y

    _for_sub_tiles(pl.program_id(1), lat_subs, n_subs, sub_tile,
                   functools.partial(sub_tile, src_ref=xc_ref))


def _merge(stream, mod, g_mix, fa, db, nc, w_g, b_gate, w_a, w_b, w_c, w_out, layer, n_lat,
           with_ctx):
    bsz, n_tok = fa.shape[0], fa.shape[1]
    n_rows = n_tok if with_ctx else n_lat
    full_steps = n_lat // STEP_ROWS
    tok = lambda w: pl.BlockSpec((1, STEP_ROWS, w), lambda b, t: (b, t, 0))
    return pl.pallas_call(
        functools.partial(_merge_kernel, lat_subs=n_lat // TOK_TILE, n_subs=n_rows // TOK_TILE),
        grid=(bsz, pl.cdiv(n_rows, STEP_ROWS)),
        in_specs=_stream_specs(stream, n_lat) + [
            pl.BlockSpec((None, None, N_MOD, D_MODEL), lambda b, t: (b, t // full_steps, 0, 0)),
            _resident((1, D_MODEL)),
            tok(BRANCH_W), tok(BRANCH_W), tok(BRANCH_W),
            _resident_layer((D_MODEL, N_BRANCH * D_MODEL), layer),
            _resident((1, N_BRANCH * D_MODEL)),
            _resident_layer((BRANCH_W, D_MODEL), layer), _resident_layer((BRANCH_W, D_MODEL), layer),
            _resident_layer((BRANCH_W, D_MODEL), layer), _resident_layer((D_MODEL, D_MODEL), layer),
        ],
        out_specs=tok(D_MODEL),
        out_shape=jax.ShapeDtypeStruct((bsz, n_rows, D_MODEL), F32),
        compiler_params=_params(2),
        name="merge_out_projection",
    )(stream[0], stream[1], mod, g_mix.reshape(1, D_MODEL), fa, db, nc, w_g,
      b_gate.reshape(1, -1), w_a, w_b, w_c, w_out)


def _convffn_kernel(x_ref, prev_ref, next_ref, mod_ref, g_ref, wu_ref, cw_ref, cb_ref, wd_ref,
                    gf_ref, o_ref, *lhs_refs, lat_subs, n_subs, final_norm):
    t = pl.program_id(1)
    g = g_ref[...]
    shift, scale = mod_ref[3:4, :], mod_ref[4:5, :]
    n_rows = TOK_TILE + 2 * SUBLANES

    def conv(u, col):
        w = cw_ref[:, col:col + FF_CHUNK]
        before = pltpu.roll(u, 1, axis=0)
        after = pltpu.roll(u, n_rows - 1, axis=0)
        v = before * w[0:1] + u * w[1:2] + after * w[2:3] + cb_ref[:, col:col + FF_CHUNK]
        return v[SUBLANES:SUBLANES + TOK_TILE]

    def prepare(s):
        idx = t * SUB_TILES + s
        lo = s * TOK_TILE
        x = x_ref[0, lo:lo + TOK_TILE, :]
        has_prev = jnp.logical_and(idx != 0, idx != lat_subs)
        has_next = jnp.logical_and(idx != lat_subs - 1, idx != n_subs - 1)
        prev = prev_ref[0] if s == 0 else x_ref[0, lo - SUBLANES:lo, :]
        nxt = (next_ref[0] if s == SUB_TILES - 1
               else x_ref[0, lo + TOK_TILE:lo + TOK_TILE + SUBLANES, :])
        lhs_ref = lhs_refs[s]
        lhs_ref[0:SUBLANES, :] = jnp.where(has_prev, _norm_modulate(prev, g, shift, scale), 0.0)
        lhs_ref[SUBLANES:SUBLANES + TOK_TILE, :] = _norm_modulate(x, g, shift, scale)
        lhs_ref[SUBLANES + TOK_TILE:, :] = jnp.where(
            has_next, _norm_modulate(nxt, g, shift, scale), 0.0)
        return x, lhs_ref[...].astype(BF16)

    def run(n_sub_tiles):
        tiles = [prepare(s) for s in range(n_sub_tiles)]
        items = [(j, s) for j in range(D_FF // FF_CHUNK) for s in range(n_sub_tiles)]

        def up(item):
            j, s = item
            lhs = tiles[s][1]
            col_a, col_b = j * FF_CHUNK, D_FF + j * FF_CHUNK
            return (_dot(lhs, wu_ref[:, col_a:col_a + FF_CHUNK]),
                    _dot(lhs, wu_ref[:, col_b:col_b + FF_CHUNK]))

        look = FF_LOOKAHEAD * n_sub_tiles
        accs = [jnp.zeros((TOK_TILE, D_MODEL), F32) for _ in range(n_sub_tiles)]
        ahead = [up(item) for item in items[:look]]
        for i, (j, s) in enumerate(items):
            col_a, col_b = j * FF_CHUNK, D_FF + j * FF_CHUNK
            u_a, u_b = ahead.pop(0)
            if i + look < len(items):
                ahead.append(up(items[i + look]))
            a = conv(u_a, col_a)
            b = conv(u_b, col_b)
            act = (a / (1.0 + jnp.exp(-a))) * b
            accs[s] = accs[s] + _dot(act.astype(BF16), wd_ref[col_a:col_a + FF_CHUNK, :])
        for s in range(n_sub_tiles):
            y = tiles[s][0] + mod_ref[5:6, :] * accs[s]
            if final_norm:
                ms = jnp.mean(y * y, axis=-1, keepdims=True)
                y = y * lax.rsqrt(ms + NORM_EPS) * gf_ref[...]
            o_ref[0, s * TOK_TILE:(s + 1) * TOK_TILE, :] = y

    full_steps = lat_subs // SUB_TILES
    pl.when(t < full_steps)(functools.partial(run, SUB_TILES))
    if n_subs > lat_subs:
        pl.when(t >= full_steps)(functools.partial(run, n_subs - lat_subs))


def _conv_ffn(h, mod, g_ffn, w_up, conv_w, conv_b, w_down, layer, g_final, n_lat, with_ctx,
              final_norm):
    bsz, n_rows_in, _ = h.shape
    n_rows = n_rows_in if with_ctx else n_lat
    lat_subs, n_subs = n_lat // TOK_TILE, n_rows // TOK_TILE
    full_steps = n_lat // STEP_ROWS
    blocks_per_step = STEP_ROWS // SUBLANES
    last_block = n_rows_in // SUBLANES - 1
    return pl.pallas_call(
        functools.partial(_convffn_kernel, lat_subs=lat_subs, n_subs=n_subs,
                          final_norm=final_norm),
        grid=(bsz, pl.cdiv(n_rows, STEP_ROWS)),
        in_specs=[
            pl.BlockSpec((1, STEP_ROWS, D_MODEL), lambda b, t: (b, t, 0)),
            pl.BlockSpec((1, SUBLANES, D_MODEL),
                         lambda b, t: (b, jnp.maximum(t * blocks_per_step - 1, 0), 0)),
            pl.BlockSpec((1, SUBLANES, D_MODEL),
                         lambda b, t: (b, jnp.minimum((t + 1) * blocks_per_step, last_block), 0)),
            pl.BlockSpec((None, None, N_MOD, D_MODEL), lambda b, t: (b, t // full_steps, 0, 0)),
            _resident((1, D_MODEL)),
            _resident_layer((D_MODEL, 2 * D_FF), layer),
            _resident((3, 2 * D_FF)),
            _resident((1, 2 * D_FF)),
            _resident_layer((D_FF, D_MODEL), layer),
            _resident((1, D_MODEL)),
        ],
        out_specs=pl.BlockSpec((1, STEP_ROWS, D_MODEL), lambda b, t: (b, t, 0)),
        out_shape=jax.ShapeDtypeStruct((bsz, n_rows, D_MODEL), F32),
        scratch_shapes=[pltpu.VMEM((TOK_TILE + 2 * SUBLANES, D_MODEL), F32)] * SUB_TILES,
        compiler_params=_params(2),
        name="conv_ffn",
    )(h, h, h, mod, g_ffn.reshape(1, D_MODEL), w_up, conv_w, conv_b.reshape(1, -1), w_down,
      g_final.reshape(1, D_MODEL))


def _rope_tables(n_lat, n_ctx):
    t = jnp.arange(n_lat, dtype=jnp.int32)
    pos = jnp.stack([(t // GRID_W).astype(F32), (t % GRID_W).astype(F32)], axis=1)
    n_freq = ROPE_AXIS_DIM // 2
    inv = ROPE_THETA ** (-jnp.arange(n_freq, dtype=F32) / n_freq)
    ang = pos[:, :, None] * inv
    lane = jnp.arange(LANES, dtype=jnp.int32)
    axis = (lane % DA_HEAD_DIM) // ROPE_AXIS_DIM
    freq = lane % n_freq
    second_half = (lane % ROPE_AXIS_DIM) >= n_freq
    ang_l = ang[:, axis, freq]
    cos, sin = jnp.cos(ang_l), jnp.sin(ang_l)
    sin_a = jnp.where(second_half, 0.0, -sin)
    sin_b = jnp.where(second_half, sin, 0.0)
    pad = lambda a, v: jnp.concatenate([a, jnp.full((n_ctx, LANES), v, F32)], axis=0)
    return jnp.concatenate([pad(cos, 1.0), pad(sin_a, 0.0), pad(sin_b, 0.0)], axis=1)


def kernel(x, c, ctx, c_ctx, w_ada, b_ada, g_mix, g_ffn, w_in, b_gate, w_a, lam, subln_g, w_b,
           rpb, w_c, w_out, w_up, conv_w, conv_b, w_down, g_final):
    bsz, n_lat, _ = x.shape
    n_ctx = ctx.shape[1]
    assert n_lat % STEP_ROWS == 0 and n_ctx == TOK_TILE and n_lat % GRID_W == 0

    rope_tabs = _rope_tables(n_lat, n_ctx)
    cl, sl = _dft_tables(n_lat, n_lat // 2, (n_lat * FN_GROUP_DIM) ** -0.5)
    cc, sc = _dft_tables(n_ctx, n_ctx, (n_ctx * FN_GROUP_DIM) ** -0.5)
    cg, sg = _dft_tables(FN_GROUP_DIM, FN_GROUP_DIM, 1.0)
    dft_tabs = (cl, sl, cc, sc, jnp.concatenate([cg, -sg], axis=0), _reversal_matrix())

    n_mod_rows = 2 * SUBLANES * (-(-(bsz + 1) // (2 * SUBLANES)))
    cvec = jnp.zeros((n_mod_rows, D_MODEL), F32).at[:bsz].set(c).at[bsz].set(c_ctx)
    mods = _modulation(cvec, w_ada, b_ada).reshape(DEPTH, n_mod_rows, N_MOD, D_MODEL)

    na_bias = _natten_bias(rpb, n_lat // GRID_W)

    n_branch_cols = w_in.shape[-1] - N_BRANCH * D_MODEL
    w_branch, w_gate, w_a, w_b, w_c, w_out, w_up, w_down = (
        w.astype(BF16) for w in (w_in[..., :n_branch_cols], w_in[..., n_branch_cols:],
                                 w_a, w_b, w_c, w_out, w_up, w_down))

    stream = (x, ctx, 0)
    for l in range(DEPTH):
        with_ctx = l != DEPTH - 1
        lam_init = 0.8 - 0.6 * math.exp(-0.3 * l)
        mod = jnp.stack([mods[l, :bsz],
                         jnp.broadcast_to(mods[l, bsz], (bsz, N_MOD, D_MODEL))], axis=1)
        fa_in, qb, kb, vbt, qc, kc, vct = _in_projection(
            stream, mod, g_mix[l], w_branch, l, rope_tabs, n_lat, n_ctx)
        fa = _fourier_mix(fa_in, dft_tabs, n_lat, n_ctx, with_ctx)
        db = _diff_attention(qb, kb, vbt, lam[l], subln_g[l], lam_init, n_lat, n_ctx, with_ctx)
        nc = _neighbourhood_attention(qc, kc, vct, na_bias, l, n_lat, n_ctx, with_ctx)
        h = _merge(stream, mod, g_mix[l], fa, db, nc, w_gate, b_gate[l], w_a, w_b, w_c, w_out, l,
                   n_lat, with_ctx)
        h = _conv_ffn(h, mod, g_ffn[l], w_up, conv_w[l], conv_b[l], w_down, l, g_final, n_lat,
                      with_ctx, final_norm=not with_ctx)
        stream = (h, h, n_lat // n_ctx)
    return h
```

```python
import functools
import math

import jax
import jax.numpy as jnp
from jax import lax
from jax.experimental import pallas as pl
from jax.experimental.pallas import tpu as pltpu

D_MODEL = 1024
DEPTH = 4
GRID_W = 64
FN_GROUPS = 4
FN_GROUP_DIM = 128
FN_WIDTH = FN_GROUPS * FN_GROUP_DIM
DA_HEADS = 4
DA_HEAD_DIM = 64
DA_V_DIM = 2 * DA_HEAD_DIM
NA_HEADS = 8
NA_HEAD_DIM = 64
NA_ROWS = 8
NA_COLS = 16
BRANCH_W = 512
N_BRANCH = 3
ROPE_THETA = 10000.0
ROPE_AXIS_DIM = DA_HEAD_DIM // 2
D_FF = 2816
N_MOD = 6
NORM_EPS = 1e-6
SUBLN_EPS = 1e-5
NEG_INF = -1e30

LANES = 128
SUBLANES = 8
TOK_TILE = 256
SUB_TILES = 2
STEP_ROWS = SUB_TILES * TOK_TILE
FF_CHUNK = 256
CONV_COLS = 128
DA_LOOKAHEAD = 1
FF_LOOKAHEAD = 2
NA_STEP_ROWS = 2
NA_GROUPS = 2
NA_SLAB_ROWS = 10
VMEM_LIMIT = 56 * 1024 * 1024
ONES_ROWS = 16
PV_CHUNK = 256
LOG2_E = math.log2(math.e)

BF16 = jnp.bfloat16
F32 = jnp.float32


def _params(n_grid_dims):
    return pltpu.CompilerParams(dimension_semantics=("arbitrary",) * n_grid_dims,
                                vmem_limit_bytes=VMEM_LIMIT)


def _resident(shape):
    return pl.BlockSpec(shape, lambda *_: (0,) * len(shape), pipeline_mode=pl.Buffered(1))


def _resident_layer(shape, layer):
    return pl.BlockSpec((None,) + tuple(shape), lambda *_: (layer,) + (0,) * len(shape),
                        pipeline_mode=pl.Buffered(1))


def _for_sub_tiles(t, lat_subs, n_subs, body, ctx_body=None):
    full_steps = lat_subs // SUB_TILES
    if n_subs == lat_subs:
        for s in range(SUB_TILES):
            body(s)
        return

    @pl.when(t < full_steps)
    def _():
        for s in range(SUB_TILES):
            body(s)

    @pl.when(t >= full_steps)
    def _():
        for s in range(n_subs - lat_subs):
            (ctx_body or body)(s)


def _tile_of_step(step, n_steps, ctx_first):
    return (step + n_steps - 1) % n_steps if ctx_first else step


def _nt_dot(a, b):
    return lax.dot_general(a, b, (((1,), (1,)), ((), ())), preferred_element_type=F32)


def _dot(a, b):
    return jnp.dot(a, b, preferred_element_type=F32)


def _norm_modulate(x, g, shift, scale):
    ms = jnp.mean(x * x, axis=-1, keepdims=True)
    return (x * lax.rsqrt(ms + NORM_EPS) * g) * (1.0 + scale) + shift


def _split_bf16(v):
    hi = v.astype(BF16)
    lo = (v - hi.astype(F32)).astype(BF16)
    return hi, lo


def _mod_kernel(c_ref, w_ref, b_ref, o_ref):
    c = c_ref[...]
    s = c / (1.0 + jnp.exp(-c))
    s_hi, s_lo = _split_bf16(s)
    w_hi, w_lo = _split_bf16(w_ref[0])
    acc = _dot(s_hi, w_hi) + (_dot(s_hi, w_lo) + _dot(s_lo, w_hi))
    o_ref[0] = acc + b_ref[0]


def _modulation(cvec, w_ada, b_ada):
    n_rows = cvec.shape[0]
    n_out = N_MOD * D_MODEL
    tn = 1536
    return pl.pallas_call(
        _mod_kernel,
        grid=(DEPTH, n_out // tn),
        in_specs=[
            pl.BlockSpec((n_rows, D_MODEL), lambda l, j: (0, 0)),
            pl.BlockSpec((1, D_MODEL, tn), lambda l, j: (l, 0, j)),
            pl.BlockSpec((1, 1, tn), lambda l, j: (l, 0, j)),
        ],
        out_specs=pl.BlockSpec((1, n_rows, tn), lambda l, j: (l, 0, j)),
        out_shape=jax.ShapeDtypeStruct((DEPTH, n_rows, n_out), F32),
        compiler_params=_params(2),
        name="adaln_modulation",
    )(cvec, w_ada, b_ada.reshape(DEPTH, 1, n_out))


def _rope(p, cos, sin_a, sin_b):
    outs = []
    for k in range(p.shape[1] // LANES):
        xs = p[:, k * LANES:(k + 1) * LANES]
        from_hi = pltpu.roll(xs, LANES - ROPE_AXIS_DIM // 2, axis=1)
        from_lo = pltpu.roll(xs, ROPE_AXIS_DIM // 2, axis=1)
        outs.append(xs * cos + from_hi * sin_a + from_lo * sin_b)
    return jnp.concatenate(outs, axis=1)


def _inproj_kernel(x_ref, xc_ref, mod_ref, g_ref, w_ref, rope_ref,
                   fa_ref, qb_ref, kb_ref, vbt_ref, qc_ref, kc_ref, vct_ref,
                   *, lat_subs, n_subs):
    def sub_tile(s, src_ref=x_ref):
        rows = slice(s * TOK_TILE, (s + 1) * TOK_TILE)
        a = _norm_modulate(src_ref[0, rows, :], g_ref[...], mod_ref[0:1, :],
                           mod_ref[1:2, :]).astype(BF16)
        tok0 = pl.multiple_of(pl.program_id(1) * STEP_ROWS + s * TOK_TILE, TOK_TILE)
        cos, sin_a, sin_b = (rope_ref[pl.ds(tok0, TOK_TILE), k * LANES:(k + 1) * LANES]
                             for k in range(3))

        def proj(seg):
            return _dot(a, w_ref[:, seg * BRANCH_W:(seg + 1) * BRANCH_W])

        fa_ref[0, rows, :] = proj(0).astype(BF16)
        qb_ref[0, rows, :] = (_rope(proj(1), cos, sin_a, sin_b)
                              * (DA_HEAD_DIM ** -0.5 * LOG2_E)).astype(BF16)
        kb_ref[0, rows, :] = _rope(proj(2), cos, sin_a, sin_b).astype(BF16)
        vbt_ref[0, :, rows] = proj(3).T.astype(BF16)
        qc_ref[0, rows, :] = (proj(4) * (NA_HEAD_DIM ** -0.5 * LOG2_E)).astype(BF16)
        kc_ref[0, rows, :] = proj(5).astype(BF16)
        vct_ref[0, :, rows] = proj(6).T.astype(BF16)

    _for_sub_tiles(pl.program_id(1), lat_subs, n_subs, sub_tile,
                   functools.partial(sub_tile, src_ref=xc_ref))


def _stream_specs(stream, n_lat):
    _, _, ctx_block = stream
    last = n_lat // STEP_ROWS - 1
    return [pl.BlockSpec((1, STEP_ROWS, D_MODEL), lambda b, t: (b, jnp.minimum(t, last), 0)),
            pl.BlockSpec((1, TOK_TILE, D_MODEL), lambda b, t: (b, ctx_block, 0))]


def _in_projection(stream, mod, g_mix, w_branch, layer, rope_tabs, n_lat, n_ctx):
    bsz, n_tok = stream[0].shape[0], n_lat + n_ctx
    full_steps = n_lat // STEP_ROWS
    proj_w = w_branch.shape[-1]
    tok = lambda w: pl.BlockSpec((1, STEP_ROWS, w), lambda b, t: (b, t, 0))
    branch = jax.ShapeDtypeStruct((bsz, n_tok, BRANCH_W), BF16)
    tok_t = pl.BlockSpec((1, BRANCH_W, STEP_ROWS), lambda b, t: (b, 0, t))
    branch_t = jax.ShapeDtypeStruct((bsz, BRANCH_W, n_tok), BF16)
    return pl.pallas_call(
        functools.partial(_inproj_kernel, lat_subs=n_lat // TOK_TILE, n_subs=n_tok // TOK_TILE),
        grid=(bsz, pl.cdiv(n_tok, STEP_ROWS)),
        in_specs=_stream_specs(stream, n_lat) + [
            pl.BlockSpec((None, None, N_MOD, D_MODEL), lambda b, t: (b, t // full_steps, 0, 0)),
            _resident((1, D_MODEL)),
            _resident_layer((D_MODEL, proj_w), layer),
            _resident((n_tok, 3 * LANES)),
        ],
        out_specs=[tok(BRANCH_W)] * 3 + [tok_t] + [tok(BRANCH_W)] * 2 + [tok_t],
        out_shape=[branch] * 3 + [branch_t] + [branch] * 2 + [branch_t],
        compiler_params=_params(2),
        name="in_projection",
    )(stream[0], stream[1], mod, g_mix.reshape(1, D_MODEL), w_branch, rope_tabs)


def _dft_kernel(u_ref, cl_ref, sl_ref, cc_ref, sc_ref, csg_ref, rev_ref, o_ref, ue_ref, uo_ref,
                *, n_lat, n_ctx, n_subs, scale, n_steps, ctx_first):
    t = _tile_of_step(pl.program_id(1), n_steps, ctx_first)
    half = n_lat // 2

    def finish(s, p, q):
        rows = slice(s * TOK_TILE, (s + 1) * TOK_TILE)
        for g in range(FN_GROUPS):
            sl = slice(g * FN_GROUP_DIM, (g + 1) * FN_GROUP_DIM)
            pq = jnp.concatenate([p[:, sl], q[:, sl]], axis=1).astype(BF16)
            o_ref[0, rows, sl] = _dot(pq, csg_ref[...]).astype(BF16)

    @pl.when(pl.program_id(1) == 0)
    def _():
        for b in range(half // TOK_TILE):
            hi = n_lat - (b + 1) * TOK_TILE
            first = u_ref[0, hi:hi + TOK_TILE, :]
            wrap = (u_ref[0, hi + TOK_TILE:hi + 2 * TOK_TILE, :] if b > 0
                    else jnp.zeros((TOK_TILE, FN_WIDTH), BF16))
            rev = _dot(rev_ref[...], jnp.concatenate([first, wrap], axis=0))
            lo = u_ref[0, b * TOK_TILE:(b + 1) * TOK_TILE, :].astype(F32)
            ue_ref[b * TOK_TILE:(b + 1) * TOK_TILE, :] = (lo + rev).astype(BF16)
            uo_ref[b * TOK_TILE:(b + 1) * TOK_TILE, :] = (lo - rev).astype(BF16)

    def latent(s):
        k0 = pl.multiple_of((t * SUB_TILES + s) * TOK_TILE, TOK_TILE)
        k = k0 + lax.broadcasted_iota(jnp.int32, (TOK_TILE, 1), 0)
        sign = (1 - 2 * (k & 1)).astype(F32) * scale
        rows = pl.ds(k0, TOK_TILE)
        p = _dot(cl_ref[rows, :], ue_ref[...]) + sign * u_ref[0, half:half + 1, :].astype(F32)
        finish(s, p, _dot(sl_ref[rows, :], uo_ref[...]))

    def context(s):
        u = u_ref[0, n_lat:n_lat + n_ctx, :]
        finish(s, _dot(cc_ref[...], u), _dot(sc_ref[...], u))

    _for_sub_tiles(t, n_lat // TOK_TILE, n_subs, latent, context)


def _fourier_mix(fa_in, tabs, n_lat, n_ctx, with_ctx):
    bsz, n_tok, _ = fa_in.shape
    n_rows = n_tok if with_ctx else n_lat
    half = n_lat // 2
    assert half % TOK_TILE == 0
    cl, sl, cc, sc, csg, rev = tabs
    lat_tab = _resident((n_lat, half))
    n_steps = pl.cdiv(n_rows, STEP_ROWS)
    return pl.pallas_call(
        functools.partial(_dft_kernel, n_lat=n_lat, n_ctx=n_ctx, n_subs=n_rows // TOK_TILE,
                          scale=(n_lat * FN_GROUP_DIM) ** -0.5, n_steps=n_steps,
                          ctx_first=with_ctx),
        grid=(bsz, n_steps),
        in_specs=[
            pl.BlockSpec((1, n_tok, FN_WIDTH), lambda b, t: (b, 0, 0)),
            lat_tab, lat_tab,
            _resident((n_ctx, n_ctx)), _resident((n_ctx, n_ctx)),
            _resident((2 * FN_GROUP_DIM, FN_GROUP_DIM)),
            _resident((TOK_TILE, 2 * TOK_TILE)),
        ],
        out_specs=pl.BlockSpec((1, STEP_ROWS, FN_WIDTH),
                               lambda b, t: (b, _tile_of_step(t, n_steps, with_ctx), 0)),
        out_shape=jax.ShapeDtypeStruct((bsz, n_rows, FN_WIDTH), BF16),
        scratch_shapes=[pltpu.VMEM((half, FN_WIDTH), BF16)] * 2,
        compiler_params=_params(2),
        name="fourier_mix",
    )(fa_in, cl, sl, cc, sc, csg, rev)


def _dft_tables(n, n_cols, scale):
    k = jnp.arange(n, dtype=jnp.int32)
    ang = ((k[:, None] * k[None, :n_cols]) % n).astype(F32) * (2.0 * math.pi / n)
    return (jnp.cos(ang) * scale).astype(BF16), (jnp.sin(ang) * scale).astype(BF16)


def _reversal_matrix():
    i = jnp.arange(TOK_TILE, dtype=jnp.int32)[:, None]
    j = jnp.arange(2 * TOK_TILE, dtype=jnp.int32)[None, :]
    src = jnp.where(i == 0, TOK_TILE, TOK_TILE - i)
    return (j == src).astype(BF16)


def _stack_sub_heads(q, first_half):
    zero = jnp.zeros_like(q)
    return jnp.concatenate([jnp.where(first_half, q, zero), jnp.where(first_half, zero, q)], axis=0)


def _store_scores(s_ref, row, s):
    n = s.shape[0]
    s_ref[row:row + n, :] = s
    return jnp.max(s.reshape(n // SUBLANES, SUBLANES, -1).max(axis=0), axis=0, keepdims=True)


def _softmax_times_values(s_ref, n_rows, m, values_t):
    acc = None
    for r in range(0, n_rows, PV_CHUNK):
        n = min(PV_CHUNK, n_rows - r)
        e = jnp.exp2(s_ref[r:r + n, :] - m).astype(BF16)
        v = values_t(r, n)
        part = _dot(jnp.concatenate([v, jnp.ones((ONES_ROWS, n), BF16)], axis=0), e)
        acc = part if acc is None else acc + part
    n_ch = acc.shape[0] - ONES_ROWS
    return acc[0:n_ch] * (1.0 / acc[n_ch:n_ch + 1])


def _diffattn_kernel(q_ref, k_ref, vt_ref, lam_ref, g_ref, o_ref, *s_refs,
                     n_lat, n_ctx, lam_init, n_steps, ctx_first):
    t = _tile_of_step(pl.program_id(1), n_steps, ctx_first)
    lv = lam_ref[...]
    lam = (jnp.exp(jnp.sum(lv[0:1] * lv[1:2], axis=-1, keepdims=True))
           - jnp.exp(jnp.sum(lv[2:3] * lv[3:4], axis=-1, keepdims=True)) + lam_init)
    first_half = lax.broadcasted_iota(jnp.int32, (1, LANES), 1) < DA_HEAD_DIM

    def attend(n_sub_tiles, k_lo, k_len):
        items = [(s, h) for s in range(n_sub_tiles) for h in range(DA_HEADS)]

        def scores(i):
            s, h = items[i]
            sl = slice(h * DA_V_DIM, (h + 1) * DA_V_DIM)
            q = q_ref[0, s * TOK_TILE:(s + 1) * TOK_TILE, sl]
            sc = _nt_dot(k_ref[0, k_lo:k_lo + k_len, sl], _stack_sub_heads(q, first_half))
            return _store_scores(s_refs[i % len(s_refs)], 0, sc)

        ahead = [scores(i) for i in range(min(DA_LOOKAHEAD, len(items)))]
        for i, (s, h) in enumerate(items):
            sl = slice(h * DA_V_DIM, (h + 1) * DA_V_DIM)
            m = ahead.pop(0)
            if i + DA_LOOKAHEAD < len(items):
                ahead.append(scores(i + DA_LOOKAHEAD))
            o12 = _softmax_times_values(
                s_refs[i % len(s_refs)], k_len, m,
                lambda r, n: vt_ref[0, sl, k_lo + r:k_lo + r + n]).T
            o = o12[0:TOK_TILE] - lam * o12[TOK_TILE:2 * TOK_TILE]
            ms = jnp.mean(o * o, axis=-1, keepdims=True)
            o = o * lax.rsqrt(ms + SUBLN_EPS) * g_ref[...] * (1.0 - lam_init)
            o_ref[0, s * TOK_TILE:(s + 1) * TOK_TILE, sl] = o.astype(BF16)

    @pl.when(t < n_lat // STEP_ROWS)
    def _():
        attend(SUB_TILES, 0, n_lat + n_ctx)

    @pl.when(t >= n_lat // STEP_ROWS)
    def _():
        attend(n_ctx // TOK_TILE, n_lat, n_ctx)


def _diff_attention(qb, kb, vbt, lam_vec, subln_g, lam_init, n_lat, n_ctx, with_ctx):
    bsz, n_tok, _ = qb.shape
    n_rows = n_tok if with_ctx else n_lat
    full = pl.BlockSpec((1, n_tok, BRANCH_W), lambda b, t: (b, 0, 0))
    full_t = pl.BlockSpec((1, BRANCH_W, n_tok), lambda b, t: (b, 0, 0))
    n_steps = pl.cdiv(n_rows, STEP_ROWS)
    tile = pl.BlockSpec((1, STEP_ROWS, BRANCH_W),
                        lambda b, t: (b, _tile_of_step(t, n_steps, with_ctx), 0))
    return pl.pallas_call(
        functools.partial(_diffattn_kernel, n_lat=n_lat, n_ctx=n_ctx, lam_init=lam_init,
                          n_steps=n_steps, ctx_first=with_ctx),
        grid=(bsz, n_steps),
        in_specs=[tile, full, full_t, _resident((4, DA_HEAD_DIM)), _resident((1, DA_V_DIM))],
        out_specs=tile,
        out_shape=jax.ShapeDtypeStruct((bsz, n_rows, BRANCH_W), BF16),
        scratch_shapes=[pltpu.VMEM((n_tok, 2 * TOK_TILE), F32)] * (DA_LOOKAHEAD + 1),
        compiler_params=_params(2),
        name="diff_attention",
    )(qb, kb, vbt, lam_vec, subln_g.reshape(1, DA_V_DIM))


def _natten_kernel(q_ref, k_ref, vt_ref, bias_ref, o_ref, *s_refs, n_lat, n_ctx, class_starts,
                   n_steps, ctx_first):
    s_idx = _tile_of_step(pl.program_id(1), n_steps, ctx_first)
    rows = n_lat // GRID_W
    kh = min(NA_ROWS, rows)
    n_slab = NA_SLAB_ROWS * GRID_W
    n_q = NA_STEP_ROWS * GRID_W
    n_pairs = NA_HEADS // 2
    first_half = lax.broadcasted_iota(jnp.int32, (1, LANES), 1) < NA_HEAD_DIM
    pairs = [slice(p * LANES, (p + 1) * LANES) for p in range(n_pairs)]

    def run(window, n_groups):
        n_keys = n_ctx + (n_slab if window else 0)
        k_los, maxes = [], []
        for g in range(n_groups):
            group = NA_GROUPS * s_idx + g
            first_row = jnp.clip(NA_STEP_ROWS * group - kh // 2, 0, rows - kh)
            k_lo = pl.multiple_of(first_row * GRID_W, LANES)
            k_los.append(k_lo)
            cls = sum((group >= f).astype(jnp.int32) for f in class_starts)
            for p, sl in enumerate(pairs):
                s_ref = s_refs[g * n_pairs + p]
                qs = _stack_sub_heads(q_ref[0, g * n_q:(g + 1) * n_q, sl], first_half)
                m = _store_scores(s_ref, 0, _nt_dot(k_ref[0, n_lat:n_lat + n_ctx, sl], qs))
                if window:
                    s_win = _nt_dot(k_ref[0, pl.ds(k_lo, n_slab), sl], qs) + bias_ref[cls, p]
                    m = jnp.maximum(m, _store_scores(s_ref, n_ctx, s_win))
                maxes.append(m)
        for g in range(n_groups):
            for p, sl in enumerate(pairs):
                def values_t(r, n):
                    if r < n_ctx:
                        return vt_ref[0, sl, n_lat + r:n_lat + r + n]
                    return vt_ref[0, sl, pl.ds(pl.multiple_of(k_los[g] + (r - n_ctx), LANES), n)]

                i = g * n_pairs + p
                o2 = _softmax_times_values(s_refs[i], n_keys, maxes[i], values_t).T
                o_ref[0, g * n_q:(g + 1) * n_q, sl] = jnp.where(
                    first_half, o2[0:n_q], o2[n_q:2 * n_q]).astype(BF16)

    window_steps = rows // (NA_STEP_ROWS * NA_GROUPS)

    @pl.when(s_idx < window_steps)
    def _():
        run(True, NA_GROUPS)

    @pl.when(s_idx >= window_steps)
    def _():
        run(False, min(NA_GROUPS, n_ctx // n_q))


def _natten_step_classes(rows):
    kh = min(NA_ROWS, rows)
    classes, first_steps = [], []
    for s in range(rows // NA_STEP_ROWS):
        slab = min(max(NA_STEP_ROWS * s - kh // 2, 0), rows - kh)
        geom = tuple((r - slab, min(max(r - kh // 2, 0), rows - kh) - slab)
                     for r in range(NA_STEP_ROWS * s, NA_STEP_ROWS * (s + 1)))
        geom = geom + (min(NA_SLAB_ROWS, rows - slab),)
        if not classes or classes[-1] != geom:
            assert geom not in classes
            classes.append(geom)
            first_steps.append(s)
    return classes, first_steps


def _natten_bias_kernel(r_ref, o_ref, *, classes, kh):
    cls = pl.program_id(1)
    kc = lax.broadcasted_iota(jnp.int32, (GRID_W, LANES), 0)
    lane = lax.broadcasted_iota(jnp.int32, (GRID_W, LANES), 1)
    second = lane >= GRID_W
    c = jnp.where(second, lane - GRID_W, lane)
    c0 = jnp.clip(c - NA_COLS // 2, 0, GRID_W - NA_COLS)
    col_ok = jnp.logical_and(kc >= c0, kc < c0 + NA_COLS)
    neg = jnp.full((GRID_W, LANES), NEG_INF, F32)

    def fill(geom):
        for h in range(NA_HEADS):
            for kr in range(NA_SLAB_ROWS):
                halves = []
                for j, (q_off, w_off) in enumerate(geom[:-1]):
                    if w_off <= kr < w_off + kh and kr < geom[-1]:
                        dr = kr - q_off + NA_ROWS - 1
                        row = jnp.broadcast_to(r_ref[0, h, dr:dr + 1, :], (GRID_W, LANES))
                        halves.append(pltpu.roll(row, (j * GRID_W - (NA_COLS - 1)) % LANES, axis=1,
                                                 stride=1, stride_axis=0))
                    else:
                        halves.append(neg)
                val = jnp.where(col_ok, jnp.where(second, halves[1], halves[0]), neg)
                o_ref[0, 0, h // 2, kr * GRID_W:(kr + 1) * GRID_W,
                      (h % 2) * LANES:(h % 2 + 1) * LANES] = val

    for ci, geom in enumerate(classes):
        pl.when(cls == ci)(functools.partial(fill, geom))


def _natten_bias(rpb, rows):
    kh = min(NA_ROWS, rows)
    classes, _ = _natten_step_classes(rows)
    n_dr, n_dc = 2 * NA_ROWS - 1, 2 * NA_COLS - 1
    assert NA_STEP_ROWS == 2 and n_dc <= GRID_W
    r = jnp.pad(rpb[..., ::-1].astype(F32) * LOG2_E,
                ((0, 0), (0, 0), (0, 2 * SUBLANES - n_dr), (0, LANES - n_dc)))
    return pl.pallas_call(
        functools.partial(_natten_bias_kernel, classes=classes, kh=kh),
        grid=(DEPTH, len(classes)),
        in_specs=[pl.BlockSpec((1, NA_HEADS, 2 * SUBLANES, LANES), lambda l, k: (l, 0, 0, 0))],
        out_specs=pl.BlockSpec((1, 1, NA_HEADS // 2, NA_SLAB_ROWS * GRID_W, 2 * LANES),
                               lambda l, k: (l, k, 0, 0, 0)),
        out_shape=jax.ShapeDtypeStruct(
            (DEPTH, len(classes), NA_HEADS // 2, NA_SLAB_ROWS * GRID_W, 2 * LANES), F32),
        compiler_params=_params(2),
        name="natten_bias_tables",
    )(r)


def _neighbourhood_attention(qc, kc, vct, bias, layer, n_lat, n_ctx, with_ctx):
    bsz, n_tok, _ = qc.shape
    rows = n_lat // GRID_W
    kh = min(NA_ROWS, rows)
    n_q = NA_STEP_ROWS * GRID_W
    assert (kh // 2) % NA_STEP_ROWS == 0 and (rows - kh) % NA_STEP_ROWS == 0 and n_q == LANES
    assert (rows - kh + NA_SLAB_ROWS) * GRID_W <= n_tok and kh + NA_STEP_ROWS <= NA_SLAB_ROWS
    assert n_ctx % PV_CHUNK == 0 and (NA_SLAB_ROWS * GRID_W) % LANES == 0
    _, first_steps = _natten_step_classes(rows)
    step_q = NA_GROUPS * n_q
    assert n_lat % step_q == 0 and n_ctx % n_q == 0 and n_ctx <= step_q
    n_steps = pl.cdiv(n_tok if with_ctx else n_lat, step_q)

    full = pl.BlockSpec((1, n_tok, BRANCH_W), lambda b, s: (b, 0, 0))
    full_t = pl.BlockSpec((1, BRANCH_W, n_tok), lambda b, s: (b, 0, 0))
    tile = pl.BlockSpec((1, step_q, BRANCH_W),
                        lambda b, s: (b, _tile_of_step(s, n_steps, with_ctx), 0))
    return pl.pallas_call(
        functools.partial(_natten_kernel, n_lat=n_lat, n_ctx=n_ctx,
                          class_starts=tuple(first_steps[1:]), n_steps=n_steps,
                          ctx_first=with_ctx),
        grid=(bsz, n_steps),
        in_specs=[tile, full, full_t, _resident_layer(bias.shape[1:], layer)],
        out_specs=tile,
        out_shape=jax.ShapeDtypeStruct((bsz, n_tok if with_ctx else n_lat, BRANCH_W), BF16),
        scratch_shapes=[pltpu.VMEM((n_ctx + NA_SLAB_ROWS * GRID_W, 2 * n_q), F32)]
                       * (NA_GROUPS * NA_HEADS // 2),
        compiler_params=_params(2),
        name="neighbourhood_attention",
    )(qc, kc, vct, bias)


def _merge_kernel(x_ref, xc_ref, mod_ref, g_ref, fa_ref, db_ref, nc_ref, wg_ref, bg_ref, wa_ref,
                  wb_ref, wc_ref, wo_ref, o_ref, *, lat_subs, n_subs):
    def sub_tile(s, src_ref=x_ref):
        rows = slice(s * TOK_TILE, (s + 1) * TOK_TILE)
        x = src_ref[0, rows, :]
        a = _norm_modulate(x, g_ref[...], mod_ref[0:1, :], mod_ref[1:2, :]).astype(BF16)

        def gate(j):
            cols = slice(j * D_MODEL, (j + 1) * D_MODEL)
            z = _dot(a, wg_ref[:, cols]) + bg_ref[:, cols]
            return 1.0 / (1.0 + jnp.exp(-z))

        y = (gate(0) * _dot(fa_ref[0, rows, :], wa_ref[...])
             + gate(1) * _dot(db_ref[0, rows, :], wb_ref[...])
             + gate(2) * _dot(nc_ref[0, rows, :], wc_ref[...]))
        y = _dot(y.astype(BF16), wo_ref[...])
        o_ref[0, rows, :] = x + mod_ref[2:3, :] * y

    _for_sub_tiles(pl.program_id(1), lat_subs, n_subs, sub_tile,
                   functools.partial(sub_tile, src_ref=xc_ref))


def _merge(stream, mod, g_mix, fa, db, nc, w_g, b_gate, w_a, w_b, w_c, w_out, layer, n_lat,
           with_ctx):
    bsz, n_tok = fa.shape[0], fa.shape[1]
    n_rows = n_tok if with_ctx else n_lat
    full_steps = n_lat // STEP_ROWS
    tok = lambda w: pl.BlockSpec((1, STEP_ROWS, w), lambda b, t: (b, t, 0))
    return pl.pallas_call(
        functools.partial(_merge_kernel, lat_subs=n_lat // TOK_TILE, n_subs=n_rows // TOK_TILE),
        grid=(bsz, pl.cdiv(n_rows, STEP_ROWS)),
        in_specs=_stream_specs(stream, n_lat) + [
            pl.BlockSpec((None, None, N_MOD, D_MODEL), lambda b, t: (b, t // full_steps, 0, 0)),
            _resident((1, D_MODEL)),
            tok(BRANCH_W), tok(BRANCH_W), tok(BRANCH_W),
            _resident_layer((D_MODEL, N_BRANCH * D_MODEL), layer),
            _resident((1, N_BRANCH * D_MODEL)),
            _resident_layer((BRANCH_W, D_MODEL), layer), _resident_layer((BRANCH_W, D_MODEL), layer),
            _resident_layer((BRANCH_W, D_MODEL), layer), _resident_layer((D_MODEL, D_MODEL), layer),
        ],
        out_specs=tok(D_MODEL),
        out_shape=jax.ShapeDtypeStruct((bsz, n_rows, D_MODEL), F32),
        compiler_params=_params(2),
        name="merge_out_projection",
    )(stream[0], stream[1], mod, g_mix.reshape(1, D_MODEL), fa, db, nc, w_g,
      b_gate.reshape(1, -1), w_a, w_b, w_c, w_out)


def _convffn_kernel(x_ref, prev_ref, next_ref, mod_ref, g_ref, wu_ref, cw_ref, cb_ref, wd_ref,
                    gf_ref, o_ref, *lhs_refs, lat_subs, n_subs, final_norm):
    t = pl.program_id(1)
    g = g_ref[...]
    shift, scale = mod_ref[3:4, :], mod_ref[4:5, :]
    n_rows = TOK_TILE + 2 * SUBLANES

    def conv(u, col):
        width = u.shape[1]
        w = cw_ref[:, col:col + width]
        before = pltpu.roll(u, 1, axis=0)
        after = pltpu.roll(u, n_rows - 1, axis=0)
        v = before * w[0:1] + u * w[1:2] + after * w[2:3] + cb_ref[:, col:col + width]
        return v[SUBLANES:SUBLANES + TOK_TILE]

    def gated(u_a, u_b, col_a, col_b):
        halves = []
        for c in range(0, FF_CHUNK, CONV_COLS):
            a = conv(u_a[:, c:c + CONV_COLS], col_a + c)
            b = conv(u_b[:, c:c + CONV_COLS], col_b + c)
            halves.append(((a / (1.0 + jnp.exp(-a))) * b).astype(BF16))
        return jnp.concatenate(halves, axis=1)

    def prepare(s):
        idx = t * SUB_TILES + s
        lo = s * TOK_TILE
        x = x_ref[0, lo:lo + TOK_TILE, :]
        has_prev = jnp.logical_and(idx != 0, idx != lat_subs)
        has_next = jnp.logical_and(idx != lat_subs - 1, idx != n_subs - 1)
        prev = prev_ref[0] if s == 0 else x_ref[0, lo - SUBLANES:lo, :]
        nxt = (next_ref[0] if s == SUB_TILES - 1
               else x_ref[0, lo + TOK_TILE:lo + TOK_TILE + SUBLANES, :])
        lhs_ref = lhs_refs[s]
        lhs_ref[0:SUBLANES, :] = jnp.where(has_prev, _norm_modulate(prev, g, shift, scale), 0.0)
        lhs_ref[SUBLANES:SUBLANES + TOK_TILE, :] = _norm_modulate(x, g, shift, scale)
        lhs_ref[SUBLANES + TOK_TILE:, :] = jnp.where(
            has_next, _norm_modulate(nxt, g, shift, scale), 0.0)
        return x, lhs_ref[...].astype(BF16)

    def run(n_sub_tiles):
        tiles = [prepare(s) for s in range(n_sub_tiles)]
        items = [(j, s) for j in range(D_FF // FF_CHUNK) for s in range(n_sub_tiles)]

        def up(item):
            j, s = item
            lhs = tiles[s][1]
            col_a, col_b = j * FF_CHUNK, D_FF + j * FF_CHUNK
            return (_dot(lhs, wu_ref[:, col_a:col_a + FF_CHUNK]),
                    _dot(lhs, wu_ref[:, col_b:col_b + FF_CHUNK]))

        look = FF_LOOKAHEAD * n_sub_tiles
        accs = [jnp.zeros((TOK_TILE, D_MODEL), F32) for _ in range(n_sub_tiles)]
        ahead = [up(item) for item in items[:look]]
        for i, (j, s) in enumerate(items):
            col_a, col_b = j * FF_CHUNK, D_FF + j * FF_CHUNK
            u_a, u_b = ahead.pop(0)
            if i + look < len(items):
                ahead.append(up(items[i + look]))
            act = gated(u_a, u_b, col_a, col_b)
            accs[s] = accs[s] + _dot(act, wd_ref[col_a:col_a + FF_CHUNK, :])
        for s in range(n_sub_tiles):
            y = tiles[s][0] + mod_ref[5:6, :] * accs[s]
            if final_norm:
                ms = jnp.mean(y * y, axis=-1, keepdims=True)
                y = y * lax.rsqrt(ms + NORM_EPS) * gf_ref[...]
            o_ref[0, s * TOK_TILE:(s + 1) * TOK_TILE, :] = y

    full_steps = lat_subs // SUB_TILES
    pl.when(t < full_steps)(functools.partial(run, SUB_TILES))
    if n_subs > lat_subs:
        pl.when(t >= full_steps)(functools.partial(run, n_subs - lat_subs))


def _conv_ffn(h, mod, g_ffn, w_up, conv_w, conv_b, w_down, layer, g_final, n_lat, with_ctx,
              final_norm):
    bsz, n_rows_in, _ = h.shape
    n_rows = n_rows_in if with_ctx else n_lat
    lat_subs, n_subs = n_lat // TOK_TILE, n_rows // TOK_TILE
    full_steps = n_lat // STEP_ROWS
    blocks_per_step = STEP_ROWS // SUBLANES
    last_block = n_rows_in // SUBLANES - 1
    return pl.pallas_call(
        functools.partial(_convffn_kernel, lat_subs=lat_subs, n_subs=n_subs,
                          final_norm=final_norm),
        grid=(bsz, pl.cdiv(n_rows, STEP_ROWS)),
        in_specs=[
            pl.BlockSpec((1, STEP_ROWS, D_MODEL), lambda b, t: (b, t, 0)),
            pl.BlockSpec((1, SUBLANES, D_MODEL),
                         lambda b, t: (b, jnp.maximum(t * blocks_per_step - 1, 0), 0)),
            pl.BlockSpec((1, SUBLANES, D_MODEL),
                         lambda b, t: (b, jnp.minimum((t + 1) * blocks_per_step, last_block), 0)),
            pl.BlockSpec((None, None, N_MOD, D_MODEL), lambda b, t: (b, t // full_steps, 0, 0)),
            _resident((1, D_MODEL)),
            _resident_layer((D_MODEL, 2 * D_FF), layer),
            _resident((3, 2 * D_FF)),
            _resident((1, 2 * D_FF)),
            _resident_layer((D_FF, D_MODEL), layer),
            _resident((1, D_MODEL)),
        ],
        out_specs=pl.BlockSpec((1, STEP_ROWS, D_MODEL), lambda b, t: (b, t, 0)),
        out_shape=jax.ShapeDtypeStruct((bsz, n_rows, D_MODEL), F32),
        scratch_shapes=[pltpu.VMEM((TOK_TILE + 2 * SUBLANES, D_MODEL), F32)] * SUB_TILES,
        compiler_params=_params(2),
        name="conv_ffn",
    )(h, h, h, mod, g_ffn.reshape(1, D_MODEL), w_up, conv_w, conv_b.reshape(1, -1), w_down,
      g_final.reshape(1, D_MODEL))


def _rope_tables(n_lat, n_ctx):
    t = jnp.arange(n_lat, dtype=jnp.int32)
    pos = jnp.stack([(t // GRID_W).astype(F32), (t % GRID_W).astype(F32)], axis=1)
    n_freq = ROPE_AXIS_DIM // 2
    inv = ROPE_THETA ** (-jnp.arange(n_freq, dtype=F32) / n_freq)
    ang = pos[:, :, None] * inv
    lane = jnp.arange(LANES, dtype=jnp.int32)
    axis = (lane % DA_HEAD_DIM) // ROPE_AXIS_DIM
    freq = lane % n_freq
    second_half = (lane % ROPE_AXIS_DIM) >= n_freq
    ang_l = ang[:, axis, freq]
    cos, sin = jnp.cos(ang_l), jnp.sin(ang_l)
    sin_a = jnp.where(second_half, 0.0, -sin)
    sin_b = jnp.where(second_half, sin, 0.0)
    pad = lambda a, v: jnp.concatenate([a, jnp.full((n_ctx, LANES), v, F32)], axis=0)
    return jnp.concatenate([pad(cos, 1.0), pad(sin_a, 0.0), pad(sin_b, 0.0)], axis=1)


def kernel(x, c, ctx, c_ctx, w_ada, b_ada, g_mix, g_ffn, w_in, b_gate, w_a, lam, subln_g, w_b,
           rpb, w_c, w_out, w_up, conv_w, conv_b, w_down, g_final):
    bsz, n_lat, _ = x.shape
    n_ctx = ctx.shape[1]
    assert n_lat % STEP_ROWS == 0 and n_ctx == TOK_TILE and n_lat % GRID_W == 0

    rope_tabs = _rope_tables(n_lat, n_ctx)
    cl, sl = _dft_tables(n_lat, n_lat // 2, (n_lat * FN_GROUP_DIM) ** -0.5)
    cc, sc = _dft_tables(n_ctx, n_ctx, (n_ctx * FN_GROUP_DIM) ** -0.5)
    cg, sg = _dft_tables(FN_GROUP_DIM, FN_GROUP_DIM, 1.0)
    dft_tabs = (cl, sl, cc, sc, jnp.concatenate([cg, -sg], axis=0), _reversal_matrix())

    n_mod_rows = 2 * SUBLANES * (-(-(bsz + 1) // (2 * SUBLANES)))
    cvec = jnp.zeros((n_mod_rows, D_MODEL), F32).at[:bsz].set(c).at[bsz].set(c_ctx)
    mods = _modulation(cvec, w_ada, b_ada).reshape(DEPTH, n_mod_rows, N_MOD, D_MODEL)

    na_bias = _natten_bias(rpb, n_lat // GRID_W)

    n_branch_cols = w_in.shape[-1] - N_BRANCH * D_MODEL
    w_branch, w_gate, w_a, w_b, w_c, w_out, w_up, w_down = (
        w.astype(BF16) for w in (w_in[..., :n_branch_cols], w_in[..., n_branch_cols:],
                                 w_a, w_b, w_c, w_out, w_up, w_down))

    stream = (x, ctx, 0)
    for l in range(DEPTH):
        with_ctx = l != DEPTH - 1
        lam_init = 0.8 - 0.6 * math.exp(-0.3 * l)
        mod = jnp.stack([mods[l, :bsz],
                         jnp.broadcast_to(mods[l, bsz], (bsz, N_MOD, D_MODEL))], axis=1)
        fa_in, qb, kb, vbt, qc, kc, vct = _in_projection(
            stream, mod, g_mix[l], w_branch, l, rope_tabs, n_lat, n_ctx)
        fa = _fourier_mix(fa_in, dft_tabs, n_lat, n_ctx, with_ctx)
        db = _diff_attention(qb, kb, vbt, lam[l], subln_g[l], lam_init, n_lat, n_ctx, with_ctx)
        nc = _neighbourhood_attention(qc, kc, vct, na_bias, l, n_lat, n_ctx, with_ctx)
        h = _merge(stream, mod, g_mix[l], fa, db, nc, w_gate, b_gate[l], w_a, w_b, w_c, w_out, l,
                   n_lat, with_ctx)
        h = _conv_ffn(h, mod, g_ffn[l], w_up, conv_w[l], conv_b[l], w_down, l, g_final, n_lat,
                      with_ctx, final_norm=not with_ctx)
        stream = (h, h, n_lat // n_ctx)
    return h
```

```python
import functools
import math

import jax
import jax.numpy as jnp
from jax import lax
from jax.experimental import pallas as pl
from jax.experimental.pallas import tpu as pltpu

D_MODEL = 1024
DEPTH = 4
GRID_W = 64
FN_GROUPS = 4
FN_GROUP_DIM = 128
FN_WIDTH = FN_GROUPS * FN_GROUP_DIM
DA_HEADS = 4
DA_HEAD_DIM = 64
DA_V_DIM = 2 * DA_HEAD_DIM
NA_HEADS = 8
NA_HEAD_DIM = 64
NA_ROWS = 8
NA_COLS = 16
BRANCH_W = 512
N_BRANCH = 3
ROPE_THETA = 10000.0
ROPE_AXIS_DIM = DA_HEAD_DIM // 2
D_FF = 2816
N_MOD = 6
NORM_EPS = 1e-6
SUBLN_EPS = 1e-5
NEG_INF = -1e30

LANES = 128
SUBLANES = 8
TOK_TILE = 256
SUB_TILES = 2
STEP_ROWS = SUB_TILES * TOK_TILE
FF_CHUNK = 256
CONV_COLS = 128
DA_LOOKAHEAD = 2
FF_LOOKAHEAD = 2
NA_STEP_ROWS = 2
NA_GROUPS = 2
NA_SLAB_ROWS = 10
VMEM_LIMIT = 56 * 1024 * 1024
ONES_ROWS = 16
PV_CHUNK = 256
LOG2_E = math.log2(math.e)

BF16 = jnp.bfloat16
F32 = jnp.float32


def _params(n_grid_dims):
    return pltpu.CompilerParams(dimension_semantics=("arbitrary",) * n_grid_dims,
                                vmem_limit_bytes=VMEM_LIMIT)


def _resident(shape):
    return pl.BlockSpec(shape, lambda *_: (0,) * len(shape), pipeline_mode=pl.Buffered(1))


def _resident_layer(shape, layer):
    return pl.BlockSpec((None,) + tuple(shape), lambda *_: (layer,) + (0,) * len(shape),
                        pipeline_mode=pl.Buffered(1))


def _for_sub_tiles(t, lat_subs, n_subs, body, ctx_body=None):
    full_steps = lat_subs // SUB_TILES
    if n_subs == lat_subs:
        for s in range(SUB_TILES):
            body(s)
        return

    @pl.when(t < full_steps)
    def _():
        for s in range(SUB_TILES):
            body(s)

    @pl.when(t >= full_steps)
    def _():
        for s in range(n_subs - lat_subs):
            (ctx_body or body)(s)


def _tile_of_step(step, n_steps, ctx_first):
    return (step + n_steps - 1) % n_steps if ctx_first else step


def _nt_dot(a, b):
    return lax.dot_general(a, b, (((1,), (1,)), ((), ())), preferred_element_type=F32)


def _dot(a, b):
    return jnp.dot(a, b, preferred_element_type=F32)


def _norm_modulate(x, g, shift, scale):
    ms = jnp.mean(x * x, axis=-1, keepdims=True)
    return (x * lax.rsqrt(ms + NORM_EPS) * g) * (1.0 + scale) + shift


def _split_bf16(v):
    hi = v.astype(BF16)
    lo = (v - hi.astype(F32)).astype(BF16)
    return hi, lo


def _mod_kernel(c_ref, w_ref, b_ref, o_ref):
    c = c_ref[...]
    s = c / (1.0 + jnp.exp(-c))
    s_hi, s_lo = _split_bf16(s)
    w_hi, w_lo = _split_bf16(w_ref[0])
    acc = _dot(s_hi, w_hi) + (_dot(s_hi, w_lo) + _dot(s_lo, w_hi))
    o_ref[0] = acc + b_ref[0]


def _modulation(cvec, w_ada, b_ada):
    n_rows = cvec.shape[0]
    n_out = N_MOD * D_MODEL
    tn = 1536
    return pl.pallas_call(
        _mod_kernel,
        grid=(DEPTH, n_out // tn),
        in_specs=[
            pl.BlockSpec((n_rows, D_MODEL), lambda l, j: (0, 0)),
            pl.BlockSpec((1, D_MODEL, tn), lambda l, j: (l, 0, j)),
            pl.BlockSpec((1, 1, tn), lambda l, j: (l, 0, j)),
        ],
        out_specs=pl.BlockSpec((1, n_rows, tn), lambda l, j: (l, 0, j)),
        out_shape=jax.ShapeDtypeStruct((DEPTH, n_rows, n_out), F32),
        compiler_params=_params(2),
        name="adaln_modulation",
    )(cvec, w_ada, b_ada.reshape(DEPTH, 1, n_out))


def _rope(p, cos, sin_a, sin_b):
    outs = []
    for k in range(p.shape[1] // LANES):
        xs = p[:, k * LANES:(k + 1) * LANES]
        from_hi = pltpu.roll(xs, LANES - ROPE_AXIS_DIM // 2, axis=1)
        from_lo = pltpu.roll(xs, ROPE_AXIS_DIM // 2, axis=1)
        outs.append(xs * cos + from_hi * sin_a + from_lo * sin_b)
    return jnp.concatenate(outs, axis=1)


def _inproj_kernel(x_ref, xc_ref, mod_ref, g_ref, w_ref, rope_ref,
                   fa_ref, qb_ref, kb_ref, vbt_ref, qc_ref, kc_ref, vct_ref,
                   *, lat_subs, n_subs):
    def sub_tile(s, src_ref=x_ref):
        rows = slice(s * TOK_TILE, (s + 1) * TOK_TILE)
        a = _norm_modulate(src_ref[0, rows, :], g_ref[...], mod_ref[0:1, :],
                           mod_ref[1:2, :]).astype(BF16)
        tok0 = pl.multiple_of(pl.program_id(1) * STEP_ROWS + s * TOK_TILE, TOK_TILE)
        cos, sin_a, sin_b = (rope_ref[pl.ds(tok0, TOK_TILE), k * LANES:(k + 1) * LANES]
                             for k in range(3))

        def proj(seg):
            return _dot(a, w_ref[:, seg * BRANCH_W:(seg + 1) * BRANCH_W])

        fa_ref[0, rows, :] = proj(0).astype(BF16)
        qb_ref[0, rows, :] = (_rope(proj(1), cos, sin_a, sin_b)
                              * (DA_HEAD_DIM ** -0.5 * LOG2_E)).astype(BF16)
        kb_ref[0, rows, :] = _rope(proj(2), cos, sin_a, sin_b).astype(BF16)
        vbt_ref[0, :, rows] = proj(3).T.astype(BF16)
        qc_ref[0, rows, :] = (proj(4) * (NA_HEAD_DIM ** -0.5 * LOG2_E)).astype(BF16)
        kc_ref[0, rows, :] = proj(5).astype(BF16)
        vct_ref[0, :, rows] = proj(6).T.astype(BF16)

    _for_sub_tiles(pl.program_id(1), lat_subs, n_subs, sub_tile,
                   functools.partial(sub_tile, src_ref=xc_ref))


def _stream_specs(stream, n_lat):
    _, _, ctx_block = stream
    last = n_lat // STEP_ROWS - 1
    return [pl.BlockSpec((1, STEP_ROWS, D_MODEL), lambda b, t: (b, jnp.minimum(t, last), 0)),
            pl.BlockSpec((1, TOK_TILE, D_MODEL), lambda b, t: (b, ctx_block, 0))]


def _in_projection(stream, mod, g_mix, w_branch, layer, rope_tabs, n_lat, n_ctx):
    bsz, n_tok = stream[0].shape[0], n_lat + n_ctx
    full_steps = n_lat // STEP_ROWS
    proj_w = w_branch.shape[-1]
    tok = lambda w: pl.BlockSpec((1, STEP_ROWS, w), lambda b, t: (b, t, 0))
    branch = jax.ShapeDtypeStruct((bsz, n_tok, BRANCH_W), BF16)
    tok_t = pl.BlockSpec((1, BRANCH_W, STEP_ROWS), lambda b, t: (b, 0, t))
    branch_t = jax.ShapeDtypeStruct((bsz, BRANCH_W, n_tok), BF16)
    return pl.pallas_call(
        functools.partial(_inproj_kernel, lat_subs=n_lat // TOK_TILE, n_subs=n_tok // TOK_TILE),
        grid=(bsz, pl.cdiv(n_tok, STEP_ROWS)),
        in_specs=_stream_specs(stream, n_lat) + [
            pl.BlockSpec((None, None, N_MOD, D_MODEL), lambda b, t: (b, t // full_steps, 0, 0)),
            _resident((1, D_MODEL)),
            _resident_layer((D_MODEL, proj_w), layer),
            _resident((n_tok, 3 * LANES)),
        ],
        out_specs=[tok(BRANCH_W)] * 3 + [tok_t] + [tok(BRANCH_W)] * 2 + [tok_t],
        out_shape=[branch] * 3 + [branch_t] + [branch] * 2 + [branch_t],
        compiler_params=_params(2),
        name="in_projection",
    )(stream[0], stream[1], mod, g_mix.reshape(1, D_MODEL), w_branch, rope_tabs)


def _dft_kernel(u_ref, cl_ref, sl_ref, cc_ref, sc_ref, csg_ref, rev_ref, o_ref, ue_ref, uo_ref,
                *, n_lat, n_ctx, n_subs, scale, n_steps, ctx_first):
    t = _tile_of_step(pl.program_id(1), n_steps, ctx_first)
    half = n_lat // 2

    def finish(s, p, q):
        rows = slice(s * TOK_TILE, (s + 1) * TOK_TILE)
        for g in range(FN_GROUPS):
            sl = slice(g * FN_GROUP_DIM, (g + 1) * FN_GROUP_DIM)
            pq = jnp.concatenate([p[:, sl], q[:, sl]], axis=1).astype(BF16)
            o_ref[0, rows, sl] = _dot(pq, csg_ref[...]).astype(BF16)

    @pl.when(pl.program_id(1) == 0)
    def _():
        for b in range(half // TOK_TILE):
            hi = n_lat - (b + 1) * TOK_TILE
            first = u_ref[0, hi:hi + TOK_TILE, :]
            wrap = (u_ref[0, hi + TOK_TILE:hi + 2 * TOK_TILE, :] if b > 0
                    else jnp.zeros((TOK_TILE, FN_WIDTH), BF16))
            rev = _dot(rev_ref[...], jnp.concatenate([first, wrap], axis=0))
            lo = u_ref[0, b * TOK_TILE:(b + 1) * TOK_TILE, :].astype(F32)
            ue_ref[b * TOK_TILE:(b + 1) * TOK_TILE, :] = (lo + rev).astype(BF16)
            uo_ref[b * TOK_TILE:(b + 1) * TOK_TILE, :] = (lo - rev).astype(BF16)

    def latent(s):
        k0 = pl.multiple_of((t * SUB_TILES + s) * TOK_TILE, TOK_TILE)
        k = k0 + lax.broadcasted_iota(jnp.int32, (TOK_TILE, 1), 0)
        sign = (1 - 2 * (k & 1)).astype(F32) * scale
        rows = pl.ds(k0, TOK_TILE)
        p = _dot(cl_ref[rows, :], ue_ref[...]) + sign * u_ref[0, half:half + 1, :].astype(F32)
        finish(s, p, _dot(sl_ref[rows, :], uo_ref[...]))

    def context(s):
        u = u_ref[0, n_lat:n_lat + n_ctx, :]
        finish(s, _dot(cc_ref[...], u), _dot(sc_ref[...], u))

    _for_sub_tiles(t, n_lat // TOK_TILE, n_subs, latent, context)


def _fourier_mix(fa_in, tabs, n_lat, n_ctx, with_ctx):
    bsz, n_tok, _ = fa_in.shape
    n_rows = n_tok if with_ctx else n_lat
    half = n_lat // 2
    assert half % TOK_TILE == 0
    cl, sl, cc, sc, csg, rev = tabs
    lat_tab = _resident((n_lat, half))
    n_steps = pl.cdiv(n_rows, STEP_ROWS)
    return pl.pallas_call(
        functools.partial(_dft_kernel, n_lat=n_lat, n_ctx=n_ctx, n_subs=n_rows // TOK_TILE,
                          scale=(n_lat * FN_GROUP_DIM) ** -0.5, n_steps=n_steps,
                          ctx_first=with_ctx),
        grid=(bsz, n_steps),
        in_specs=[
            pl.BlockSpec((1, n_tok, FN_WIDTH), lambda b, t: (b, 0, 0)),
            lat_tab, lat_tab,
            _resident((n_ctx, n_ctx)), _resident((n_ctx, n_ctx)),
            _resident((2 * FN_GROUP_DIM, FN_GROUP_DIM)),
            _resident((TOK_TILE, 2 * TOK_TILE)),
        ],
        out_specs=pl.BlockSpec((1, STEP_ROWS, FN_WIDTH),
                               lambda b, t: (b, _tile_of_step(t, n_steps, with_ctx), 0)),
        out_shape=jax.ShapeDtypeStruct((bsz, n_rows, FN_WIDTH), BF16),
        scratch_shapes=[pltpu.VMEM((half, FN_WIDTH), BF16)] * 2,
        compiler_params=_params(2),
        name="fourier_mix",
    )(fa_in, cl, sl, cc, sc, csg, rev)


def _dft_tables(n, n_cols, scale):
    k = jnp.arange(n, dtype=jnp.int32)
    ang = ((k[:, None] * k[None, :n_cols]) % n).astype(F32) * (2.0 * math.pi / n)
    return (jnp.cos(ang) * scale).astype(BF16), (jnp.sin(ang) * scale).astype(BF16)


def _reversal_matrix():
    i = jnp.arange(TOK_TILE, dtype=jnp.int32)[:, None]
    j = jnp.arange(2 * TOK_TILE, dtype=jnp.int32)[None, :]
    src = jnp.where(i == 0, TOK_TILE, TOK_TILE - i)
    return (j == src).astype(BF16)


def _stack_sub_heads(q, first_half):
    zero = jnp.zeros_like(q)
    return jnp.concatenate([jnp.where(first_half, q, zero), jnp.where(first_half, zero, q)], axis=0)


def _store_scores(s_ref, row, s):
    n = s.shape[0]
    s_ref[row:row + n, :] = s
    return jnp.max(s.reshape(n // SUBLANES, SUBLANES, -1).max(axis=0), axis=0, keepdims=True)


def _softmax_times_values(s_ref, n_rows, m, values_t):
    acc = None
    for r in range(0, n_rows, PV_CHUNK):
        n = min(PV_CHUNK, n_rows - r)
        e = jnp.exp2(s_ref[r:r + n, :] - m).astype(BF16)
        v = values_t(r, n)
        part = _dot(jnp.concatenate([v, jnp.ones((ONES_ROWS, n), BF16)], axis=0), e)
        acc = part if acc is None else acc + part
    n_ch = acc.shape[0] - ONES_ROWS
    return acc[0:n_ch] * (1.0 / acc[n_ch:n_ch + 1])


def _diffattn_kernel(q_ref, k_ref, vt_ref, lam_ref, g_ref, o_ref, *s_refs,
                     n_lat, n_ctx, lam_init, n_steps, ctx_first):
    t = _tile_of_step(pl.program_id(1), n_steps, ctx_first)
    lv = lam_ref[...]
    lam = (jnp.exp(jnp.sum(lv[0:1] * lv[1:2], axis=-1, keepdims=True))
           - jnp.exp(jnp.sum(lv[2:3] * lv[3:4], axis=-1, keepdims=True)) + lam_init)
    first_half = lax.broadcasted_iota(jnp.int32, (1, LANES), 1) < DA_HEAD_DIM

    def attend(n_sub_tiles, k_lo, k_len):
        items = [(s, h) for s in range(n_sub_tiles) for h in range(DA_HEADS)]

        def scores(i):
            s, h = items[i]
            sl = slice(h * DA_V_DIM, (h + 1) * DA_V_DIM)
            q = q_ref[0, s * TOK_TILE:(s + 1) * TOK_TILE, sl]
            sc = _nt_dot(k_ref[0, k_lo:k_lo + k_len, sl], _stack_sub_heads(q, first_half))
            return _store_scores(s_refs[i % len(s_refs)], 0, sc)

        ahead = [scores(i) for i in range(min(DA_LOOKAHEAD, len(items)))]
        for i, (s, h) in enumerate(items):
            sl = slice(h * DA_V_DIM, (h + 1) * DA_V_DIM)
            m = ahead.pop(0)
            if i + DA_LOOKAHEAD < len(items):
                ahead.append(scores(i + DA_LOOKAHEAD))
            o12 = _softmax_times_values(
                s_refs[i % len(s_refs)], k_len, m,
                lambda r, n: vt_ref[0, sl, k_lo + r:k_lo + r + n]).T
            o = o12[0:TOK_TILE] - lam * o12[TOK_TILE:2 * TOK_TILE]
            ms = jnp.mean(o * o, axis=-1, keepdims=True)
            o = o * lax.rsqrt(ms + SUBLN_EPS) * g_ref[...] * (1.0 - lam_init)
            o_ref[0, s * TOK_TILE:(s + 1) * TOK_TILE, sl] = o.astype(BF16)

    @pl.when(t < n_lat // STEP_ROWS)
    def _():
        attend(SUB_TILES, 0, n_lat + n_ctx)

    @pl.when(t >= n_lat // STEP_ROWS)
    def _():
        attend(n_ctx // TOK_TILE, n_lat, n_ctx)


def _diff_attention(qb, kb, vbt, lam_vec, subln_g, lam_init, n_lat, n_ctx, with_ctx):
    bsz, n_tok, _ = qb.shape
    n_rows = n_tok if with_ctx else n_lat
    full = pl.BlockSpec((1, n_tok, BRANCH_W), lambda b, t: (b, 0, 0))
    full_t = pl.BlockSpec((1, BRANCH_W, n_tok), lambda b, t: (b, 0, 0))
    n_steps = pl.cdiv(n_rows, STEP_ROWS)
    tile = pl.BlockSpec((1, STEP_ROWS, BRANCH_W),
                        lambda b, t: (b, _tile_of_step(t, n_steps, with_ctx), 0))
    return pl.pallas_call(
        functools.partial(_diffattn_kernel, n_lat=n_lat, n_ctx=n_ctx, lam_init=lam_init,
                          n_steps=n_steps, ctx_first=with_ctx),
        grid=(bsz, n_steps),
        in_specs=[tile, full, full_t, _resident((4, DA_HEAD_DIM)), _resident((1, DA_V_DIM))],
        out_specs=tile,
        out_shape=jax.ShapeDtypeStruct((bsz, n_rows, BRANCH_W), BF16),
        scratch_shapes=[pltpu.VMEM((n_tok, 2 * TOK_TILE), F32)] * (DA_LOOKAHEAD + 1),
        compiler_params=_params(2),
        name="diff_attention",
    )(qb, kb, vbt, lam_vec, subln_g.reshape(1, DA_V_DIM))


def _natten_kernel(q_ref, k_ref, vt_ref, bias_ref, o_ref, *s_refs, n_lat, n_ctx, class_starts,
                   n_steps, ctx_first):
    s_idx = _tile_of_step(pl.program_id(1), n_steps, ctx_first)
    rows = n_lat // GRID_W
    kh = min(NA_ROWS, rows)
    n_slab = NA_SLAB_ROWS * GRID_W
    n_q = NA_STEP_ROWS * GRID_W
    n_pairs = NA_HEADS // 2
    first_half = lax.broadcasted_iota(jnp.int32, (1, LANES), 1) < NA_HEAD_DIM
    pairs = [slice(p * LANES, (p + 1) * LANES) for p in range(n_pairs)]

    def run(window, n_groups):
        n_keys = n_ctx + (n_slab if window else 0)
        k_los, maxes = [], []
        for g in range(n_groups):
            group = NA_GROUPS * s_idx + g
            first_row = jnp.clip(NA_STEP_ROWS * group - kh // 2, 0, rows - kh)
            k_lo = pl.multiple_of(first_row * GRID_W, LANES)
            k_los.append(k_lo)
            cls = sum((group >= f).astype(jnp.int32) for f in class_starts)
            for p, sl in enumerate(pairs):
                s_ref = s_refs[g * n_pairs + p]
                qs = _stack_sub_heads(q_ref[0, g * n_q:(g + 1) * n_q, sl], first_half)
                m = _store_scores(s_ref, 0, _nt_dot(k_ref[0, n_lat:n_lat + n_ctx, sl], qs))
                if window:
                    s_win = _nt_dot(k_ref[0, pl.ds(k_lo, n_slab), sl], qs) + bias_ref[cls, p]
                    m = jnp.maximum(m, _store_scores(s_ref, n_ctx, s_win))
                maxes.append(m)
        for g in range(n_groups):
            for p, sl in enumerate(pairs):
                def values_t(r, n):
                    if r < n_ctx:
                        return vt_ref[0, sl, n_lat + r:n_lat + r + n]
                    return vt_ref[0, sl, pl.ds(pl.multiple_of(k_los[g] + (r - n_ctx), LANES), n)]

                i = g * n_pairs + p
                o2 = _softmax_times_values(s_refs[i], n_keys, maxes[i], values_t).T
                o_ref[0, g * n_q:(g + 1) * n_q, sl] = jnp.where(
                    first_half, o2[0:n_q], o2[n_q:2 * n_q]).astype(BF16)

    window_steps = rows // (NA_STEP_ROWS * NA_GROUPS)

    @pl.when(s_idx < window_steps)
    def _():
        run(True, NA_GROUPS)

    @pl.when(s_idx >= window_steps)
    def _():
        run(False, min(NA_GROUPS, n_ctx // n_q))


def _natten_step_classes(rows):
    kh = min(NA_ROWS, rows)
    classes, first_steps = [], []
    for s in range(rows // NA_STEP_ROWS):
        slab = min(max(NA_STEP_ROWS * s - kh // 2, 0), rows - kh)
        geom = tuple((r - slab, min(max(r - kh // 2, 0), rows - kh) - slab)
                     for r in range(NA_STEP_ROWS * s, NA_STEP_ROWS * (s + 1)))
        geom = geom + (min(NA_SLAB_ROWS, rows - slab),)
        if not classes or classes[-1] != geom:
            assert geom not in classes
            classes.append(geom)
            first_steps.append(s)
    return classes, first_steps


def _natten_bias_kernel(r_ref, o_ref, *, classes, kh):
    cls = pl.program_id(1)
    kc = lax.broadcasted_iota(jnp.int32, (GRID_W, LANES), 0)
    lane = lax.broadcasted_iota(jnp.int32, (GRID_W, LANES), 1)
    second = lane >= GRID_W
    c = jnp.where(second, lane - GRID_W, lane)
    c0 = jnp.clip(c - NA_COLS // 2, 0, GRID_W - NA_COLS)
    col_ok = jnp.logical_and(kc >= c0, kc < c0 + NA_COLS)
    neg = jnp.full((GRID_W, LANES), NEG_INF, F32)

    def fill(geom):
        for h in range(NA_HEADS):
            for kr in range(NA_SLAB_ROWS):
                halves = []
                for j, (q_off, w_off) in enumerate(geom[:-1]):
                    if w_off <= kr < w_off + kh and kr < geom[-1]:
                        dr = kr - q_off + NA_ROWS - 1
                        row = jnp.broadcast_to(r_ref[0, h, dr:dr + 1, :], (GRID_W, LANES))
                        halves.append(pltpu.roll(row, (j * GRID_W - (NA_COLS - 1)) % LANES, axis=1,
                                                 stride=1, stride_axis=0))
                    else:
                        halves.append(neg)
                val = jnp.where(col_ok, jnp.where(second, halves[1], halves[0]), neg)
                o_ref[0, 0, h // 2, kr * GRID_W:(kr + 1) * GRID_W,
                      (h % 2) * LANES:(h % 2 + 1) * LANES] = val

    for ci, geom in enumerate(classes):
        pl.when(cls == ci)(functools.partial(fill, geom))


def _natten_bias(rpb, rows):
    kh = min(NA_ROWS, rows)
    classes, _ = _natten_step_classes(rows)
    n_dr, n_dc = 2 * NA_ROWS - 1, 2 * NA_COLS - 1
    assert NA_STEP_ROWS == 2 and n_dc <= GRID_W
    r = jnp.pad(rpb[..., ::-1].astype(F32) * LOG2_E,
                ((0, 0), (0, 0), (0, 2 * SUBLANES - n_dr), (0, LANES - n_dc)))
    return pl.pallas_call(
        functools.partial(_natten_bias_kernel, classes=classes, kh=kh),
        grid=(DEPTH, len(classes)),
        in_specs=[pl.BlockSpec((1, NA_HEADS, 2 * SUBLANES, LANES), lambda l, k: (l, 0, 0, 0))],
        out_specs=pl.BlockSpec((1, 1, NA_HEADS // 2, NA_SLAB_ROWS * GRID_W, 2 * LANES),
                               lambda l, k: (l, k, 0, 0, 0)),
        out_shape=jax.ShapeDtypeStruct(
            (DEPTH, len(classes), NA_HEADS // 2, NA_SLAB_ROWS * GRID_W, 2 * LANES), F32),
        compiler_params=_params(2),
        name="natten_bias_tables",
    )(r)


def _neighbourhood_attention(qc, kc, vct, bias, layer, n_lat, n_ctx, with_ctx):
    bsz, n_tok, _ = qc.shape
    rows = n_lat // GRID_W
    kh = min(NA_ROWS, rows)
    n_q = NA_STEP_ROWS * GRID_W
    assert (kh // 2) % NA_STEP_ROWS == 0 and (rows - kh) % NA_STEP_ROWS == 0 and n_q == LANES
    assert (rows - kh + NA_SLAB_ROWS) * GRID_W <= n_tok and kh + NA_STEP_ROWS <= NA_SLAB_ROWS
    assert n_ctx % PV_CHUNK == 0 and (NA_SLAB_ROWS * GRID_W) % LANES == 0
    _, first_steps = _natten_step_classes(rows)
    step_q = NA_GROUPS * n_q
    assert n_lat % step_q == 0 and n_ctx % n_q == 0 and n_ctx <= step_q
    n_steps = pl.cdiv(n_tok if with_ctx else n_lat, step_q)

    full = pl.BlockSpec((1, n_tok, BRANCH_W), lambda b, s: (b, 0, 0))
    full_t = pl.BlockSpec((1, BRANCH_W, n_tok), lambda b, s: (b, 0, 0))
    tile = pl.BlockSpec((1, step_q, BRANCH_W),
                        lambda b, s: (b, _tile_of_step(s, n_steps, with_ctx), 0))
    return pl.pallas_call(
        functools.partial(_natten_kernel, n_lat=n_lat, n_ctx=n_ctx,
                          class_starts=tuple(first_steps[1:]), n_steps=n_steps,
                          ctx_first=with_ctx),
        grid=(bsz, n_steps),
        in_specs=[tile, full, full_t, _resident_layer(bias.shape[1:], layer)],
        out_specs=tile,
        out_shape=jax.ShapeDtypeStruct((bsz, n_tok if with_ctx else n_lat, BRANCH_W), BF16),
        scratch_shapes=[pltpu.VMEM((n_ctx + NA_SLAB_ROWS * GRID_W, 2 * n_q), F32)]
                       * (NA_GROUPS * NA_HEADS // 2),
        compiler_params=_params(2),
        name="neighbourhood_attention",
    )(qc, kc, vct, bias)


def _merge_kernel(x_ref, xc_ref, mod_ref, g_ref, fa_ref, db_ref, nc_ref, wg_ref, bg_ref, wa_ref,
                  wb_ref, wc_ref, wo_ref, o_ref, *, lat_subs, n_subs):
    def sub_tile(s, src_ref=x_ref):
        rows = slice(s * TOK_TILE, (s + 1) * TOK_TILE)
        x = src_ref[0, rows, :]
        a = _norm_modulate(x, g_ref[...], mod_ref[0:1, :], mod_ref[1:2, :]).astype(BF16)

        def gate(j):
            cols = slice(j * D_MODEL, (j + 1) * D_MODEL)
            z = _dot(a, wg_ref[:, cols]) + bg_ref[:, cols]
            return 1.0 / (1.0 + jnp.exp(-z))

        y = (gate(0) * _dot(fa_ref[0, rows, :], wa_ref[...])
             + gate(1) * _dot(db_ref[0, rows, :], wb_ref[...])
             + gate(2) * _dot(nc_ref[0, rows, :], wc_ref[...]))
        y = _dot(y.astype(BF16), wo_ref[...])
        o_ref[0, rows, :] = x + mod_ref[2:3, :] * y

    _for_sub_tiles(pl.program_id(1), lat_subs, n_subs, sub_tile,
                   functools.partial(sub_tile, src_ref=xc_ref))


def _merge(stream, mod, g_mix, fa, db, nc, w_g, b_gate, w_a, w_b, w_c, w_out, layer, n_lat,
           with_ctx):
    bsz, n_tok = fa.shape[0], fa.shape[1]
    n_rows = n_tok if with_ctx else n_lat
    full_steps = n_lat // STEP_ROWS
    tok = lambda w: pl.BlockSpec((1, STEP_ROWS, w), lambda b, t: (b, t, 0))
    return pl.pallas_call(
        functools.partial(_merge_kernel, lat_subs=n_lat // TOK_TILE, n_subs=n_rows // TOK_TILE),
        grid=(bsz, pl.cdiv(n_rows, STEP_ROWS)),
        in_specs=_stream_specs(stream, n_lat) + [
            pl.BlockSpec((None, None, N_MOD, D_MODEL), lambda b, t: (b, t // full_steps, 0, 0)),
            _resident((1, D_MODEL)),
            tok(BRANCH_W), tok(BRANCH_W), tok(BRANCH_W),
            _resident_layer((D_MODEL, N_BRANCH * D_MODEL), layer),
            _resident((1, N_BRANCH * D_MODEL)),
            _resident_layer((BRANCH_W, D_MODEL), layer), _resident_layer((BRANCH_W, D_MODEL), layer),
            _resident_layer((BRANCH_W, D_MODEL), layer), _resident_layer((D_MODEL, D_MODEL), layer),
        ],
        out_specs=tok(D_MODEL),
        out_shape=jax.ShapeDtypeStruct((bsz, n_rows, D_MODEL), F32),
        compiler_params=_params(2),
        name="merge_out_projection",
    )(stream[0], stream[1], mod, g_mix.reshape(1, D_MODEL), fa, db, nc, w_g,
      b_gate.reshape(1, -1), w_a, w_b, w_c, w_out)


def _convffn_kernel(x_ref, prev_ref, next_ref, mod_ref, g_ref, wu_ref, cw_ref, cb_ref, wd_ref,
                    gf_ref, o_ref, *lhs_refs, lat_subs, n_subs, final_norm):
    t = pl.program_id(1)
    g = g_ref[...]
    shift, scale = mod_ref[3:4, :], mod_ref[4:5, :]
    n_rows = TOK_TILE + 2 * SUBLANES

    def conv(u, col):
        width = u.shape[1]
        w = cw_ref[:, col:col + width]
        before = pltpu.roll(u, 1, axis=0)
        after = pltpu.roll(u, n_rows - 1, axis=0)
        v = before * w[0:1] + u * w[1:2] + after * w[2:3] + cb_ref[:, col:col + width]
        return v[SUBLANES:SUBLANES + TOK_TILE]

    def gated(u_a, u_b, col_a, col_b):
        halves = []
        for c in range(0, FF_CHUNK, CONV_COLS):
            a = conv(u_a[:, c:c + CONV_COLS], col_a + c)
            b = conv(u_b[:, c:c + CONV_COLS], col_b + c)
            halves.append(((a / (1.0 + jnp.exp(-a))) * b).astype(BF16))
        return jnp.concatenate(halves, axis=1)

    def prepare(s):
        idx = t * SUB_TILES + s
        lo = s * TOK_TILE
        x = x_ref[0, lo:lo + TOK_TILE, :]
        has_prev = jnp.logical_and(idx != 0, idx != lat_subs)
        has_next = jnp.logical_and(idx != lat_subs - 1, idx != n_subs - 1)
        prev = prev_ref[0] if s == 0 else x_ref[0, lo - SUBLANES:lo, :]
        nxt = (next_ref[0] if s == SUB_TILES - 1
               else x_ref[0, lo + TOK_TILE:lo + TOK_TILE + SUBLANES, :])
        lhs_ref = lhs_refs[s]
        lhs_ref[0:SUBLANES, :] = jnp.where(has_prev, _norm_modulate(prev, g, shift, scale), 0.0)
        lhs_ref[SUBLANES:SUBLANES + TOK_TILE, :] = _norm_modulate(x, g, shift, scale)
        lhs_ref[SUBLANES + TOK_TILE:, :] = jnp.where(
            has_next, _norm_modulate(nxt, g, shift, scale), 0.0)
        return x, lhs_ref[...].astype(BF16)

    def run(n_sub_tiles):
        tiles = [prepare(s) for s in range(n_sub_tiles)]
        items = [(j, s) for j in range(D_FF // FF_CHUNK) for s in range(n_sub_tiles)]

        def up(item):
            j, s = item
            lhs = tiles[s][1]
            col_a, col_b = j * FF_CHUNK, D_FF + j * FF_CHUNK
            return (_dot(lhs, wu_ref[:, col_a:col_a + FF_CHUNK]),
                    _dot(lhs, wu_ref[:, col_b:col_b + FF_CHUNK]))

        look = FF_LOOKAHEAD * n_sub_tiles
        accs = [jnp.zeros((TOK_TILE, D_MODEL), F32) for _ in range(n_sub_tiles)]
        ahead = [up(item) for item in items[:look]]
        for i, (j, s) in enumerate(items):
            col_a, col_b = j * FF_CHUNK, D_FF + j * FF_CHUNK
            u_a, u_b = ahead.pop(0)
            if i + look < len(items):
                ahead.append(up(items[i + look]))
            act = gated(u_a, u_b, col_a, col_b)
            accs[s] = accs[s] + _dot(act, wd_ref[col_a:col_a + FF_CHUNK, :])
        for s in range(n_sub_tiles):
            y = tiles[s][0] + mod_ref[5:6, :] * accs[s]
            if final_norm:
                ms = jnp.mean(y * y, axis=-1, keepdims=True)
                y = y * lax.rsqrt(ms + NORM_EPS) * gf_ref[...]
            o_ref[0, s * TOK_TILE:(s + 1) * TOK_TILE, :] = y

    full_steps = lat_subs // SUB_TILES
    pl.when(t < full_steps)(functools.partial(run, SUB_TILES))
    if n_subs > lat_subs:
        pl.when(t >= full_steps)(functools.partial(run, n_subs - lat_subs))


def _conv_ffn(h, mod, g_ffn, w_up, conv_w, conv_b, w_down, layer, g_final, n_lat, with_ctx,
              final_norm):
    bsz, n_rows_in, _ = h.shape
    n_rows = n_rows_in if with_ctx else n_lat
    lat_subs, n_subs = n_lat // TOK_TILE, n_rows // TOK_TILE
    full_steps = n_lat // STEP_ROWS
    blocks_per_step = STEP_ROWS // SUBLANES
    last_block = n_rows_in // SUBLANES - 1
    return pl.pallas_call(
        functools.partial(_convffn_kernel, lat_subs=lat_subs, n_subs=n_subs,
                          final_norm=final_norm),
        grid=(bsz, pl.cdiv(n_rows, STEP_ROWS)),
        in_specs=[
            pl.BlockSpec((1, STEP_ROWS, D_MODEL), lambda b, t: (b, t, 0)),
            pl.BlockSpec((1, SUBLANES, D_MODEL),
                         lambda b, t: (b, jnp.maximum(t * blocks_per_step - 1, 0), 0)),
            pl.BlockSpec((1, SUBLANES, D_MODEL),
                         lambda b, t: (b, jnp.minimum((t + 1) * blocks_per_step, last_block), 0)),
            pl.BlockSpec((None, None, N_MOD, D_MODEL), lambda b, t: (b, t // full_steps, 0, 0)),
            _resident((1, D_MODEL)),
            _resident_layer((D_MODEL, 2 * D_FF), layer),
            _resident((3, 2 * D_FF)),
            _resident((1, 2 * D_FF)),
            _resident_layer((D_FF, D_MODEL), layer),
            _resident((1, D_MODEL)),
        ],
        out_specs=pl.BlockSpec((1, STEP_ROWS, D_MODEL), lambda b, t: (b, t, 0)),
        out_shape=jax.ShapeDtypeStruct((bsz, n_rows, D_MODEL), F32),
        scratch_shapes=[pltpu.VMEM((TOK_TILE + 2 * SUBLANES, D_MODEL), F32)] * SUB_TILES,
        compiler_params=_params(2),
        name="conv_ffn",
    )(h, h, h, mod, g_ffn.reshape(1, D_MODEL), w_up, conv_w, conv_b.reshape(1, -1), w_down,
      g_final.reshape(1, D_MODEL))


def _rope_tables(n_lat, n_ctx):
    t = jnp.arange(n_lat, dtype=jnp.int32)
    pos = jnp.stack([(t // GRID_W).astype(F32), (t % GRID_W).astype(F32)], axis=1)
    n_freq = ROPE_AXIS_DIM // 2
    inv = ROPE_THETA ** (-jnp.arange(n_freq, dtype=F32) / n_freq)
    ang = pos[:, :, None] * inv
    lane = jnp.arange(LANES, dtype=jnp.int32)
    axis = (lane % DA_HEAD_DIM) // ROPE_AXIS_DIM
    freq = lane % n_freq
    second_half = (lane % ROPE_AXIS_DIM) >= n_freq
    ang_l = ang[:, axis, freq]
    cos, sin = jnp.cos(ang_l), jnp.sin(ang_l)
    sin_a = jnp.where(second_half, 0.0, -sin)
    sin_b = jnp.where(second_half, sin, 0.0)
    pad = lambda a, v: jnp.concatenate([a, jnp.full((n_ctx, LANES), v, F32)], axis=0)
    return jnp.concatenate([pad(cos, 1.0), pad(sin_a, 0.0), pad(sin_b, 0.0)], axis=1)


def kernel(x, c, ctx, c_ctx, w_ada, b_ada, g_mix, g_ffn, w_in, b_gate, w_a, lam, subln_g, w_b,
           rpb, w_c, w_out, w_up, conv_w, conv_b, w_down, g_final):
    bsz, n_lat, _ = x.shape
    n_ctx = ctx.shape[1]
    assert n_lat % STEP_ROWS == 0 and n_ctx == TOK_TILE and n_lat % GRID_W == 0

    rope_tabs = _rope_tables(n_lat, n_ctx)
    cl, sl = _dft_tables(n_lat, n_lat // 2, (n_lat * FN_GROUP_DIM) ** -0.5)
    cc, sc = _dft_tables(n_ctx, n_ctx, (n_ctx * FN_GROUP_DIM) ** -0.5)
    cg, sg = _dft_tables(FN_GROUP_DIM, FN_GROUP_DIM, 1.0)
    dft_tabs = (cl, sl, cc, sc, jnp.concatenate([cg, -sg], axis=0), _reversal_matrix())

    n_mod_rows = 2 * SUBLANES * (-(-(bsz + 1) // (2 * SUBLANES)))
    cvec = jnp.zeros((n_mod_rows, D_MODEL), F32).at[:bsz].set(c).at[bsz].set(c_ctx)
    mods = _modulation(cvec, w_ada, b_ada).reshape(DEPTH, n_mod_rows, N_MOD, D_MODEL)

    na_bias = _natten_bias(rpb, n_lat // GRID_W)

    n_branch_cols = w_in.shape[-1] - N_BRANCH * D_MODEL
    w_branch, w_gate, w_a, w_b, w_c, w_out, w_up, w_down = (
        w.astype(BF16) for w in (w_in[..., :n_branch_cols], w_in[..., n_branch_cols:],
                                 w_a, w_b, w_c, w_out, w_up, w_down))

    stream = (x, ctx, 0)
    for l in range(DEPTH):
        with_ctx = l != DEPTH - 1
        lam_init = 0.8 - 0.6 * math.exp(-0.3 * l)
        mod = jnp.stack([mods[l, :bsz],
                         jnp.broadcast_to(mods[l, bsz], (bsz, N_MOD, D_MODEL))], axis=1)
        fa_in, qb, kb, vbt, qc, kc, vct = _in_projection(
            stream, mod, g_mix[l], w_branch, l, rope_tabs, n_lat, n_ctx)
        fa = _fourier_mix(fa_in, dft_tabs, n_lat, n_ctx, with_ctx)
        db = _diff_attention(qb, kb, vbt, lam[l], subln_g[l], lam_init, n_lat, n_ctx, with_ctx)
        nc = _neighbourhood_attention(qc, kc, vct, na_bias, l, n_lat, n_ctx, with_ctx)
        h = _merge(stream, mod, g_mix[l], fa, db, nc, w_gate, b_gate[l], w_a, w_b, w_c, w_out, l,
                   n_lat, with_ctx)
        h = _conv_ffn(h, mod, g_ffn[l], w_up, conv_w[l], conv_b[l], w_down, l, g_final, n_lat,
                      with_ctx, final_norm=not with_ctx)
        stream = (h, h, n_lat // n_ctx)
    return h
```
